```python
import jax
import jax.numpy as jnp
from jax import lax
import numpy as np

D_MODEL = 1024
BATCH = 4
SEQ = 4096
DEPTH = 2

HEAD_DIM = 64
A_HEADS = 4
A_KV_HEADS = 2
A_GROUP = A_HEADS // A_KV_HEADS
A_WINDOW = 128
B_HEADS = 4
B_Q_RANK = 256
B_KV_RANK = 128
B_NOPE = 64
B_ROPE = 32
B_V = HEAD_DIM
ROPE_THETA = 10000.0
C_HEADS = 8
C_KV_HEADS = 2
C_GROUP = C_HEADS // C_KV_HEADS
C_CMP_BLOCK = 32
C_CMP_STRIDE = 16
C_CMP_HIDDEN = 128
C_SEL_BLOCK = 64
C_SEL_TOP = 16
C_WINDOW = 512
Q_BLOCK = 128
SEL_Q_BLOCK = 64
D_FF = 4096
N_EXPERTS = 8
TOP_K = 2
D_FF_EXPERT = 3584
NORM_EPS = 1e-6
NEG_INF = -1e30

A_Q_COLS = A_HEADS * HEAD_DIM
A_KV_COLS = A_KV_HEADS * HEAD_DIM
C_Q_COLS = C_HEADS * HEAD_DIM
C_KV_COLS = C_KV_HEADS * HEAD_DIM
C_GATE_COLS = C_HEADS * 3
IN_SPLITS = (A_Q_COLS, A_KV_COLS, A_KV_COLS, B_Q_RANK, B_KV_RANK, B_ROPE, C_Q_COLS,
             C_KV_COLS, C_KV_COLS, C_KV_COLS, C_KV_COLS, C_KV_COLS, C_KV_COLS, C_GATE_COLS)
N_IN = sum(IN_SPLITS)
MIX_WIDTH = A_HEADS * HEAD_DIM + B_HEADS * B_V + C_HEADS * HEAD_DIM

kernel_name = 'hybrid_swa_mla_nsa_moe_block'


def _rmsnorm(x, g):
    xf = x.astype(jnp.float32)
    y = xf * lax.rsqrt(jnp.mean(xf * xf, axis=-1, keepdims=True) + NORM_EPS)
    return (y * g.astype(jnp.float32)).astype(x.dtype)


def _alibi_slopes(n):
    return np.array([2.0 ** (-8.0 * (i + 1) / n) for i in range(n)], dtype=np.float32)


def _rope(x, pos):
    d = x.shape[-1]
    inv = (ROPE_THETA ** (-np.arange(0, d, 2, dtype=np.float32) / d)).astype(np.float32)
    ang = pos.astype(jnp.float32)[..., None] * inv
    cos = jnp.cos(ang)[:, :, None, :]
    sin = jnp.sin(ang)[:, :, None, :]
    xf = x.astype(jnp.float32)
    x1, x2 = xf[..., : d // 2], xf[..., d // 2:]
    return jnp.concatenate([x1 * cos - x2 * sin, x2 * cos + x1 * sin], axis=-1).astype(x.dtype)


def _banded_attention(q, k, v, window, slopes, sinks=None):
    B, S, G, R, dh = q.shape
    nb = S // Q_BLOCK
    span = Q_BLOCK + window
    kp = jnp.pad(k, ((0, 0), (window, 0), (0, 0), (0, 0)))
    vp = jnp.pad(v, ((0, 0), (window, 0), (0, 0), (0, 0)))
    idx = np.arange(nb)[:, None] * Q_BLOCK + np.arange(span)[None, :]
    kb = kp[:, idx]
    vb = vp[:, idx]
    qb = q.reshape(B, nb, Q_BLOCK, G, R, dh)
    s = jnp.einsum('bnqgrd,bnkgd->bngrqk', qb, kb).astype(jnp.float32) * (dh ** -0.5)
    dist = (np.arange(Q_BLOCK)[:, None] + window) - np.arange(span)[None, :]
    kabs = idx - window
    valid = (dist[None] >= 0) & (dist[None] < window) & (kabs[:, None, :] >= 0)
    s = s - slopes[None, None, :, :, None, None] * dist.astype(np.float32)
    s = jnp.where(valid[None, :, None, None], s, NEG_INF)
    if sinks is None:
        p = jax.nn.softmax(s, axis=-1)
    else:
        sk = sinks.astype(jnp.float32)[None, None, :, :, None, None]
        m = jnp.maximum(jnp.max(s, axis=-1, keepdims=True), sk)
        e = jnp.exp(s - m)
        p = e / (jnp.sum(e, axis=-1, keepdims=True) + jnp.exp(sk - m))
    o = jnp.einsum('bngrqk,bnkgd->bnqgrd', p.astype(v.dtype), vb)
    return o.reshape(B, S, G, R, dh)


def _causal_attention_blocks(q, k, v):
    B, S, H, dqk = q.shape
    dv = v.shape[-1]
    nb = S // Q_BLOCK
    scale = dqk ** -0.5
    qb = jnp.moveaxis(q.reshape(B, nb, Q_BLOCK, H, dqk), 1, 0)
    kpos = jnp.arange(S)

    def block(args):
        qi, start = args
        s = jnp.einsum('bqhd,bkhd->bhqk', qi, k).astype(jnp.float32) * scale
        qpos = start + jnp.arange(Q_BLOCK)
        s = jnp.where(kpos[None, :] <= qpos[:, None], s, NEG_INF)
        p = jax.nn.softmax(s, axis=-1).astype(v.dtype)
        return jnp.einsum('bhqk,bkhd->bqhd', p, v)

    o = lax.map(block, (qb, jnp.arange(nb) * Q_BLOCK))
    return jnp.moveaxis(o, 0, 1).reshape(B, S, H, dv)


def _nsa_attention(q, k_cmp, v_cmp, k_sel, v_sel, k_win, v_win, gates,
                   w1k, w2k, pek, w1v, w2v, pev, slopes):
    B, S, G, R, dh = q.shape
    scale = dh ** -0.5
    n_cmp = (S - C_CMP_BLOCK) // C_CMP_STRIDE + 1
    n_sel = S // C_SEL_BLOCK
    n_top = min(C_SEL_TOP, n_sel)
    t = np.arange(S)

    cstart = np.arange(n_cmp) * C_CMP_STRIDE
    cend = cstart + C_CMP_BLOCK - 1
    cidx = cstart[:, None] + np.arange(C_CMP_BLOCK)[None, :]

    def compress(xs, w1, w2, pe):
        blocks = xs[:, cidx] + pe[None, None, :, None, :]
        hid = jax.nn.gelu(jnp.einsum('bnlgd,lde->bnge', blocks, w1))
        return jnp.einsum('bnge,ed->bngd', hid, w2)

    kc = compress(k_cmp, w1k, w2k, pek)
    vc = compress(v_cmp, w1v, w2v, pev)
    cdist = (t[:, None] - cend[None, :]).astype(np.float32)
    cvalid = cdist >= 0
    s = jnp.einsum('bsgrd,bngd->bgrsn', q, kc).astype(jnp.float32) * scale
    s = s - slopes[None, :, :, None, None] * cdist
    s = jnp.where(cvalid, s, NEG_INF)
    has_cmp = (t >= C_CMP_BLOCK - 1).astype(np.float32)[:, None]
    p_cmp = jax.nn.softmax(s, axis=-1) * has_cmp
    o_cmp = jnp.einsum('bgrsn,bngd->bsgrd', p_cmp.astype(vc.dtype), vc)

    sstart = np.arange(n_sel) * C_SEL_BLOCK
    send = sstart + C_SEL_BLOCK - 1
    overlap = np.clip(np.minimum(cend[:, None], send[None, :]) - np.maximum(cstart[:, None], sstart[None, :]) + 1,
                      0, None).astype(np.float32) / C_CMP_STRIDE
    imp = jnp.einsum('bgrsn,nj->bgsj', p_cmp, overlap)
    cur = t // C_SEL_BLOCK
    jj = np.arange(n_sel)
    avail = sstart[None, :] <= t[:, None]
    forced = (jj[None, :] == 0) | (jj[None, :] == cur[:, None]) | (jj[None, :] == cur[:, None] - 1)
    imp = jnp.where(avail, imp, NEG_INF)
    imp = jnp.where(forced, -NEG_INF, imp)
    _, sel_idx = lax.top_k(imp, n_top)

    ksb = jnp.moveaxis(k_sel.reshape(B, n_sel, C_SEL_BLOCK, G, dh), 3, 1)
    vsb = jnp.moveaxis(v_sel.reshape(B, n_sel, C_SEL_BLOCK, G, dh), 3, 1)
    nc = S // SEL_Q_BLOCK
    q_ch = jnp.moveaxis(q.reshape(B, nc, SEL_Q_BLOCK, G, R, dh), 1, 0)
    i_ch = jnp.moveaxis(sel_idx.reshape(B, G, nc, SEL_Q_BLOCK, n_top), 2, 0)
    bi = jnp.arange(B)[:, None, None, None]
    gi = jnp.arange(G)[None, :, None, None]
    n_keys = n_top * C_SEL_BLOCK

    def sel_block(args):
        qi, ii, start = args
        kg = ksb[bi, gi, ii].reshape(B, G, SEL_Q_BLOCK, n_keys, dh)
        vg = vsb[bi, gi, ii].reshape(B, G, SEL_Q_BLOCK, n_keys, dh)
        tq = start + jnp.arange(SEL_Q_BLOCK)
        kpos = (ii[..., None] * C_SEL_BLOCK + jnp.arange(C_SEL_BLOCK)).reshape(B, G, SEL_Q_BLOCK, n_keys)
        dist = (tq[None, None, :, None] - kpos).astype(jnp.float32)[:, :, None]
        sc = jnp.einsum('bqgrd,bgqkd->bgrqk', qi, kg).astype(jnp.float32) * scale
        sc = sc - slopes[None, :, :, None, None] * dist
        sc = jnp.where(dist >= 0, sc, NEG_INF)
        p = jax.nn.softmax(sc, axis=-1).astype(vg.dtype)
        return jnp.einsum('bgrqk,bgqkd->bqgrd', p, vg)

    o_slc = lax.map(sel_block, (q_ch, i_ch, jnp.arange(nc) * SEL_Q_BLOCK))
    o_slc = jnp.moveaxis(o_slc, 0, 1).reshape(B, S, G, R, dh)

    o_win = _banded_attention(q, k_win, v_win, C_WINDOW, slopes)

    return gates[..., 0:1] * o_cmp + gates[..., 1:2] * o_slc + gates[..., 2:3] * o_win


def _hybrid_mixer(h, positions, w_in, a_sinks, g_cq, w_uq, g_ckv, w_ukv,
                  c_w1_k, c_w2_k, c_pe_k, c_w1_v, c_w2_v, c_pe_v, g_oa, g_ob, g_oc, w_o):
    B, S, _ = h.shape
    proj = jnp.einsum('bsd,dn->bsn', h, w_in)
    cuts = np.cumsum(IN_SPLITS)[:-1].tolist()
    (a_q, a_k, a_v, b_cq, b_ckv, b_kr, c_q,
     c_kc, c_vc, c_ks, c_vs, c_kw, c_vw, c_g) = jnp.split(proj, cuts, axis=-1)

    qa = a_q.reshape(B, S, A_KV_HEADS, A_GROUP, HEAD_DIM)
    ka = a_k.reshape(B, S, A_KV_HEADS, HEAD_DIM)
    va = a_v.reshape(B, S, A_KV_HEADS, HEAD_DIM)
    a_slopes = jnp.asarray(_alibi_slopes(A_HEADS).reshape(A_KV_HEADS, A_GROUP))
    o_a = _banded_attention(qa, ka, va, A_WINDOW, a_slopes,
                            a_sinks.reshape(A_KV_HEADS, A_GROUP)).reshape(B, S, A_HEADS * HEAD_DIM)

    q_b = jnp.einsum('bsr,rn->bsn', _rmsnorm(b_cq, g_cq), w_uq).reshape(B, S, B_HEADS, B_NOPE + B_ROPE)
    q_b = jnp.concatenate([q_b[..., :B_NOPE], _rope(q_b[..., B_NOPE:], positions)], axis=-1)
    kv_b = jnp.einsum('bsr,rn->bsn', _rmsnorm(b_ckv, g_ckv), w_ukv).reshape(B, S, B_HEADS, B_NOPE + B_V)
    k_rope = jnp.broadcast_to(_rope(b_kr[:, :, None, :], positions), (B, S, B_HEADS, B_ROPE))
    k_b = jnp.concatenate([kv_b[..., :B_NOPE], k_rope], axis=-1)
    o_b = _causal_attention_blocks(q_b, k_b, kv_b[..., B_NOPE:]).reshape(B, S, B_HEADS * B_V)

    kv_shape = (B, S, C_KV_HEADS, HEAD_DIM)
    gates = jax.nn.sigmoid(c_g.astype(jnp.float32)).astype(h.dtype).reshape(B, S, C_KV_HEADS, C_GROUP, 3)
    c_slopes = jnp.asarray(_alibi_slopes(C_HEADS).reshape(C_KV_HEADS, C_GROUP))
    o_c = _nsa_attention(c_q.reshape(B, S, C_KV_HEADS, C_GROUP, HEAD_DIM),
                         c_kc.reshape(kv_shape), c_vc.reshape(kv_shape),
                         c_ks.reshape(kv_shape), c_vs.reshape(kv_shape),
                         c_kw.reshape(kv_shape), c_vw.reshape(kv_shape), gates,
                         c_w1_k, c_w2_k, c_pe_k, c_w1_v, c_w2_v, c_pe_v, c_slopes).reshape(B, S, C_HEADS * HEAD_DIM)

    mixed = jnp.concatenate([_rmsnorm(o_a, g_oa), _rmsnorm(o_b, g_ob), _rmsnorm(o_c, g_oc)], axis=-1)
    return jnp.einsum('bsm,md->bsd', mixed, w_o)


def _swiglu(h, wg, wu, wd):
    a = jnp.einsum('bsd,df->bsf', h, wg)
    b = jnp.einsum('bsd,df->bsf', h, wu)
    return jnp.einsum('bsf,fd->bsd', jax.nn.silu(a) * b, wd)


def _moe(h, w_router, w_gate, w_up, w_down):
    logits = jnp.einsum('bsd,de->bse', h, w_router).astype(jnp.float32)
    top_v, top_i = lax.top_k(logits, TOP_K)
    top_w = jax.nn.softmax(top_v, axis=-1)
    combine = jnp.sum(jax.nn.one_hot(top_i, N_EXPERTS, dtype=jnp.float32) * top_w[..., None], axis=-2)
    combine = combine.astype(h.dtype)
    y = jnp.zeros_like(h)
    for e in range(N_EXPERTS):
        y = y + combine[..., e:e + 1] * _swiglu(h, w_gate[e], w_up[e], w_down[e])
    return y


def setup_inputs(seed: int = 0) -> dict:
    key = jax.random.key(seed)
    ks = jax.random.split(key, 32)
    n_dense = (DEPTH + 1) // 2
    n_moe = DEPTH // 2
    f32 = jnp.float32

    def nrm(k, shape, scale):
        return jax.random.normal(k, shape, f32) * scale

    def gain(k, shape):
        return 1.0 + 0.1 * jax.random.normal(k, shape, f32)

    x = jax.random.normal(ks[0], (BATCH, SEQ, D_MODEL), f32)
    offset = jax.random.randint(ks[1], (BATCH, 1), 0, 1024, dtype=jnp.int32)
    positions = (offset + jnp.arange(SEQ, dtype=jnp.int32)[None, :]).astype(jnp.int32)
    return {
        'x': x,
        'positions': positions,
        'w_in': nrm(ks[2], (DEPTH, D_MODEL, N_IN), D_MODEL ** -0.5),
        'a_sinks': nrm(ks[3], (DEPTH, A_HEADS), 0.5),
        'g_cq': gain(ks[4], (DEPTH, B_Q_RANK)),
        'w_uq': nrm(ks[5], (DEPTH, B_Q_RANK, B_HEADS * (B_NOPE + B_ROPE)), B_Q_RANK ** -0.5),
        'g_ckv': gain(ks[6], (DEPTH, B_KV_RANK)),
        'w_ukv': nrm(ks[7], (DEPTH, B_KV_RANK, B_HEADS * (B_NOPE + B_V)), B_KV_RANK ** -0.5),
        'c_w1_k': nrm(ks[8], (DEPTH, C_CMP_BLOCK, HEAD_DIM, C_CMP_HIDDEN), (C_CMP_BLOCK * HEAD_DIM) ** -0.5),
        'c_w2_k': nrm(ks[9], (DEPTH, C_CMP_HIDDEN, HEAD_DIM), C_CMP_HIDDEN ** -0.5),
        'c_pe_k': nrm(ks[10], (DEPTH, C_CMP_BLOCK, HEAD_DIM), 0.1),
        'c_w1_v': nrm(ks[11], (DEPTH, C_CMP_BLOCK, HEAD_DIM, C_CMP_HIDDEN), (C_CMP_BLOCK * HEAD_DIM) ** -0.5),
        'c_w2_v': nrm(ks[12], (DEPTH, C_CMP_HIDDEN, HEAD_DIM), C_CMP_HIDDEN ** -0.5),
        'c_pe_v': nrm(ks[13], (DEPTH, C_CMP_BLOCK, HEAD_DIM), 0.1),
        'g_oa': gain(ks[14], (DEPTH, A_HEADS * HEAD_DIM)),
        'g_ob': gain(ks[15], (DEPTH, B_HEADS * B_V)),
        'g_oc': gain(ks[16], (DEPTH, C_HEADS * HEAD_DIM)),
        'w_o': nrm(ks[17], (DEPTH, MIX_WIDTH, D_MODEL), MIX_WIDTH ** -0.5),
        'g_pre_mix': gain(ks[18], (DEPTH, D_MODEL)),
        'g_post_mix': gain(ks[19], (DEPTH, D_MODEL)),
        'g_pre_ffn': gain(ks[20], (DEPTH, D_MODEL)),
        'g_post_ffn': gain(ks[21], (DEPTH, D_MODEL)),
        'ffn_w_gate': nrm(ks[22], (n_dense, D_MODEL, D_FF), D_MODEL ** -0.5),
        'ffn_w_up': nrm(ks[23], (n_dense, D_MODEL, D_FF), D_MODEL ** -0.5),
        'ffn_w_down': nrm(ks[24], (n_dense, D_FF, D_MODEL), D_FF ** -0.5),
        'moe_router': nrm(ks[25], (n_moe, D_MODEL, N_EXPERTS), D_MODEL ** -0.5),
        'moe_w_gate': nrm(ks[26], (n_moe, N_EXPERTS, D_MODEL, D_FF_EXPERT), D_MODEL ** -0.5),
        'moe_w_up': nrm(ks[27], (n_moe, N_EXPERTS, D_MODEL, D_FF_EXPERT), D_MODEL ** -0.5),
        'moe_w_down': nrm(ks[28], (n_moe, N_EXPERTS, D_FF_EXPERT, D_MODEL), D_FF_EXPERT ** -0.5),
    }


def reference(x, positions, w_in, a_sinks, g_cq, w_uq, g_ckv, w_ukv, c_w1_k, c_w2_k, c_pe_k,
              c_w1_v, c_w2_v, c_pe_v, g_oa, g_ob, g_oc, w_o, g_pre_mix, g_post_mix, g_pre_ffn,
              g_post_ffn, ffn_w_gate, ffn_w_up, ffn_w_down, moe_router, moe_w_gate, moe_w_up,
              moe_w_down):
    for l in range(DEPTH):
        h = _rmsnorm(x, g_pre_mix[l])
        m = _hybrid_mixer(h, positions, w_in[l], a_sinks[l], g_cq[l], w_uq[l], g_ckv[l], w_ukv[l],
                          c_w1_k[l], c_w2_k[l], c_pe_k[l], c_w1_v[l], c_w2_v[l], c_pe_v[l],
                          g_oa[l], g_ob[l], g_oc[l], w_o[l])
        x = x + _rmsnorm(m, g_post_mix[l])
        h = _rmsnorm(x, g_pre_ffn[l])
        if l % 2 == 0:
            f = _swiglu(h, ffn_w_gate[l // 2], ffn_w_up[l // 2], ffn_w_down[l // 2])
        else:
            f = _moe(h, moe_router[l // 2], moe_w_gate[l // 2], moe_w_up[l // 2], moe_w_down[l // 2])
        x = x + _rmsnorm(f, g_post_ffn[l])
    return x
```

```python
import functools

import numpy as np
import jax
import jax.numpy as jnp
from jax import lax
from jax.experimental import pallas as pl
from jax.experimental.pallas import tpu as pltpu

F32 = jnp.float32
BF16 = jnp.bfloat16

D_MODEL = 1024
HEAD_DIM = 64
NORM_EPS = 1e-6
NEG_INF = -1e30
ROPE_THETA = 10000.0

A_HEADS, A_KV_HEADS, A_WINDOW = 4, 2, 128
B_HEADS, B_Q_RANK, B_KV_RANK, B_NOPE, B_ROPE = 4, 256, 128, 64, 32
C_HEADS, C_KV_HEADS, C_WINDOW = 8, 2, 512
C_CMP_BLOCK, C_CMP_STRIDE, C_CMP_HIDDEN = 32, 16, 128
C_SEL_BLOCK, C_SEL_TOP = 64, 16
N_EXPERTS = 8

LANES = 128
VMEM_LIMIT = 48 * 1024 * 1024

P_CQ, P_AQ, P_BCQ = 0, 512, 768
P_AK, P_AV, P_BCKV = 1024, 1152, 1280
P_CKC, P_CVC, P_CKS, P_CVS, P_CKW, P_CVW = 1408, 1536, 1664, 1792, 1920, 2048
P_KR, P_KRROT, P_GATE = 2176, 2304, 2432
P_WIDTH = 2560

TM_PROJ = 512
TQ_ATT = 128
TQ_MLA = 256
TK_SEL = 256
TM_FFN = 512
TF_FFN = 512
TS_MOE = 512
TM_MOE = 512
CH_MOE = 128
U_MOE = 24


def _alibi_slopes(n):
    return [float(np.float32(2.0 ** (-8.0 * (i + 1) / n))) for i in range(n)]


def _dot(a, b):
    return jnp.dot(a, b, preferred_element_type=F32)


def _dot_nt(a, b):
    return lax.dot_general(a, b, (((1,), (1,)), ((), ())), preferred_element_type=F32)


def _rms(x, g):
    return x * lax.rsqrt(jnp.mean(x * x, axis=-1, keepdims=True) + NORM_EPS) * g


def _sigmoid(x):
    return 1.0 / (1.0 + jnp.exp(-x))


def _params(*sem):
    return pltpu.CompilerParams(dimension_semantics=sem, vmem_limit_bytes=VMEM_LIMIT)


def _inproj_kernel(x_ref, g_ref, w_ref, o_ref):
    h = _rms(x_ref[...], g_ref[...]).astype(BF16)
    for n in range(P_WIDTH // 512):
        o_ref[:, n * 512:(n + 1) * 512] = _dot(h, w_ref[:, n * 512:(n + 1) * 512]).astype(BF16)


def _inproj(x, g, w):
    T = x.shape[0]
    return pl.pallas_call(
        _inproj_kernel,
        grid=(T // TM_PROJ,),
        in_specs=[pl.BlockSpec((TM_PROJ, D_MODEL), lambda i: (i, 0)),
                  pl.BlockSpec((1, D_MODEL), lambda i: (0, 0)),
                  pl.BlockSpec((D_MODEL, P_WIDTH), lambda i: (0, 0))],
        out_specs=pl.BlockSpec((TM_PROJ, P_WIDTH), lambda i: (i, 0)),
        out_shape=jax.ShapeDtypeStruct((T, P_WIDTH), BF16),
        compiler_params=_params("parallel"),
        name="inproj",
    )(x, g, w)


def _build_w_in(w_in):
    cuts = np.cumsum([0, 256, 128, 128, 256, 128, 32, 512, 128, 128, 128, 128, 128, 128, 24])
    seg = [w_in[:, cuts[i]:cuts[i + 1]] for i in range(14)]
    a_q, a_k, a_v, b_cq, b_ckv, b_kr, c_q, c_kc, c_vc, c_ks, c_vs, c_kw, c_vw, c_g = seg
    z = lambda n: jnp.zeros((w_in.shape[0], n), w_in.dtype)
    half = B_ROPE // 2
    kr_rot = jnp.concatenate([-b_kr[:, half:], b_kr[:, :half]], axis=1)
    cols = [c_q, a_q, b_cq, a_k, a_v, b_ckv, c_kc, c_vc, c_ks, c_vs, c_kw, c_vw,
            z(B_NOPE), b_kr, z(LANES - B_NOPE - B_ROPE),
            z(B_NOPE), kr_rot, z(LANES - B_NOPE - B_ROPE),
            c_g, z(LANES - 24)]
    return jnp.concatenate(cols, axis=1).astype(BF16)


def _rope_tab_kernel(pos_ref, inv_ref, cos_ref, sin_ref):
    ang = pos_ref[...] * inv_ref[...]
    cos_ref[...] = jnp.cos(ang)
    sin_ref[...] = jnp.sin(ang)


def _rope_tables(positions):
    T = positions.size
    inv = (ROPE_THETA ** (-np.arange(0, B_ROPE, 2, dtype=np.float32) / B_ROPE)).astype(np.float32)
    inv128 = np.zeros((1, LANES), np.float32)
    inv128[0, B_NOPE:B_NOPE + B_ROPE // 2] = inv
    inv128[0, B_NOPE + B_ROPE // 2:B_NOPE + B_ROPE] = inv
    pos = positions.reshape(T, 1).astype(F32)
    return pl.pallas_call(
        _rope_tab_kernel,
        grid=(T // TM_PROJ,),
        in_specs=[pl.BlockSpec((TM_PROJ, 1), lambda i: (i, 0)),
                  pl.BlockSpec((1, LANES), lambda i: (0, 0))],
        out_specs=[pl.BlockSpec((TM_PROJ, LANES), lambda i: (i, 0))] * 2,
        out_shape=[jax.ShapeDtypeStruct((T, LANES), F32)] * 2,
        compiler_params=_params("parallel"),
        name="rope_tables",
    )(pos, jnp.asarray(inv128))


def _band_kernel(*refs, TQ, W, G, R, slopes, has_sink, gate_branch):
    refs = list(refs)
    q_ref, k_ref, v_ref = refs[:3]
    rest = refs[3:]
    sink_ref = rest.pop(0) if has_sink else None
    gate_ref = rest.pop(0) if gate_branch is not None else None
    o_ref = rest.pop(0)
    i = pl.program_id(1)
    span = TQ + W
    scale = HEAD_DIM ** -0.5
    qstart = i * TQ
    kstart = pl.multiple_of(jnp.maximum(qstart - W, 0), TQ)
    row = lax.broadcasted_iota(jnp.int32, (TQ, span), 0)
    col = lax.broadcasted_iota(jnp.int32, (TQ, span), 1)
    dist = (qstart + row) - (kstart + col)
    valid = (dist >= 0) & (dist < W)
    distf = dist.astype(F32)
    if gate_ref is not None:
        gate = _sigmoid(gate_ref[...].astype(F32))
    for g in range(G):
        k = k_ref[pl.ds(kstart, span), g * HEAD_DIM:(g + 1) * HEAD_DIM]
        v = v_ref[pl.ds(kstart, span), g * HEAD_DIM:(g + 1) * HEAD_DIM]
        for r in range(R):
            hh = g * R + r
            q = q_ref[:, hh * HEAD_DIM:(hh + 1) * HEAD_DIM]
            s = _dot_nt(q, k) * scale - slopes[hh] * distf
            s = jnp.where(valid, s, NEG_INF)
            m = jnp.max(s, axis=-1, keepdims=True)
            if has_sink:
                m = jnp.maximum(m, sink_ref[hh])
            e = jnp.exp(s - m)
            l = jnp.sum(e, axis=-1, keepdims=True)
            if has_sink:
                l = l + jnp.exp(sink_ref[hh] - m)
            o = _dot(e.astype(BF16), v) / l
            if gate_ref is not None:
                c = 3 * hh + gate_branch
                o = o * gate[:, c:c + 1]
            o_ref[:, hh * HEAD_DIM:(hh + 1) * HEAD_DIM] = o.astype(o_ref.dtype)


def _band_attn(P, B, S, *, q_col, k_col, v_col, W, G, R, slopes, sinks=None, gate_branch=None):
    T = B * S
    TQ = TQ_ATT
    nq = S // TQ
    QW = G * R * HEAD_DIM
    KW = G * HEAD_DIM
    in_specs = [pl.BlockSpec((TQ, QW), lambda b, i: (b * nq + i, q_col // QW)),
                pl.BlockSpec((S, KW), lambda b, i: (b, k_col // KW)),
                pl.BlockSpec((S, KW), lambda b, i: (b, v_col // KW))]
    args = [P, P, P]
    if sinks is not None:
        in_specs.append(pl.BlockSpec(memory_space=pltpu.SMEM))
        args.append(sinks.astype(F32))
    if gate_branch is not None:
        in_specs.append(pl.BlockSpec((TQ, LANES), lambda b, i: (b * nq + i, P_GATE // LANES)))
        args.append(P)
    kern = functools.partial(_band_kernel, TQ=TQ, W=W, G=G, R=R, slopes=slopes,
                             has_sink=sinks is not None, gate_branch=gate_branch)
    return pl.pallas_call(
        kern,
        grid=(B, nq),
        in_specs=in_specs,
        out_specs=pl.BlockSpec((TQ, QW), lambda b, i: (b * nq + i, 0)),
        out_shape=jax.ShapeDtypeStruct((T, QW), BF16),
        compiler_params=_params("parallel", "parallel"),
        name="band_attn_w%d" % W,
    )(*args)


def _mla_prep_kernel(cq_ref, ckv_ref, kr_ref, krr_ref, cos_ref, sin_ref, gq_ref, gkv_ref,
                     wq_ref, wqr_ref, wk_ref, wv_ref, q_out, k_out, v_out):
    cos = cos_ref[...]
    sin = sin_ref[...]
    scale = (B_NOPE + B_ROPE) ** -0.5
    nq = _rms(cq_ref[...].astype(F32), gq_ref[...]).astype(BF16)
    nkv = _rms(ckv_ref[...].astype(F32), gkv_ref[...]).astype(BF16)
    q1 = _dot(nq, wq_ref[...])
    q2 = _dot(nq, wqr_ref[...])
    kk = _dot(nkv, wk_ref[...])
    krope = kr_ref[...].astype(F32) * cos + krr_ref[...].astype(F32) * sin
    for h in range(B_HEADS):
        sl = slice(h * LANES, (h + 1) * LANES)
        q_out[:, sl] = ((q1[:, sl] * cos + q2[:, sl] * sin) * scale).astype(BF16)
        k_out[:, sl] = (kk[:, sl] + krope).astype(BF16)
    v_out[...] = _dot(nkv, wv_ref[...]).astype(BF16)


def _build_mla_weights(w_uq, w_ukv):
    R1, R2 = w_uq.shape[0], w_ukv.shape[0]
    half = B_ROPE // 2
    hq = B_NOPE + B_ROPE
    wq, wqr, wk, wv = [], [], [], []
    for h in range(B_HEADS):
        nope = w_uq[:, h * hq:h * hq + B_NOPE]
        rope = w_uq[:, h * hq + B_NOPE:(h + 1) * hq]
        rot = jnp.concatenate([-rope[:, half:], rope[:, :half]], axis=1)
        pad = jnp.zeros((R1, LANES - hq), w_uq.dtype)
        wq += [nope, rope, pad]
        wqr += [jnp.zeros((R1, B_NOPE), w_uq.dtype), rot, pad]
        wk += [w_ukv[:, h * 128:h * 128 + B_NOPE], jnp.zeros((R2, LANES - B_NOPE), w_ukv.dtype)]
        wv += [w_ukv[:, h * 128 + B_NOPE:(h + 1) * 128]]
    cat = lambda xs: jnp.concatenate(xs, axis=1).astype(BF16)
    return cat(wq), cat(wqr), cat(wk), cat(wv)


def _mla_prep(P, cos, sin, g_cq, g_ckv, w_uq, w_ukv):
    T = P.shape[0]
    TM = TM_PROJ
    wq, wqr, wk, wv = _build_mla_weights(w_uq, w_ukv)
    QW = B_HEADS * LANES
    VW = B_HEADS * HEAD_DIM
    full = lambda shp: pl.BlockSpec(shp, lambda i: (0, 0))
    return pl.pallas_call(
        _mla_prep_kernel,
        grid=(T // TM,),
        in_specs=[pl.BlockSpec((TM, B_Q_RANK), lambda i: (i, P_BCQ // B_Q_RANK)),
                  pl.BlockSpec((TM, B_KV_RANK), lambda i: (i, P_BCKV // B_KV_RANK)),
                  pl.BlockSpec((TM, LANES), lambda i: (i, P_KR // LANES)),
                  pl.BlockSpec((TM, LANES), lambda i: (i, P_KRROT // LANES)),
                  pl.BlockSpec((TM, LANES), lambda i: (i, 0)),
                  pl.BlockSpec((TM, LANES), lambda i: (i, 0)),
                  full((1, B_Q_RANK)), full((1, B_KV_RANK)),
                  full((B_Q_RANK, QW)), full((B_Q_RANK, QW)), full((B_KV_RANK, QW)), full((B_KV_RANK, VW))],
        out_specs=[pl.BlockSpec((TM, QW), lambda i: (i, 0)),
                   pl.BlockSpec((TM, QW), lambda i: (i, 0)),
                   pl.BlockSpec((TM, VW), lambda i: (i, 0))],
        out_shape=[jax.ShapeDtypeStruct((T, QW), BF16), jax.ShapeDtypeStruct((T, QW), BF16),
                   jax.ShapeDtypeStruct((T, VW), BF16)],
        compiler_params=_params("parallel"),
        name="mla_prep",
    )(P, P, P, P, cos, sin, g_cq.reshape(1, -1), g_ckv.reshape(1, -1), wq, wqr, wk, wv)


def _mla_attn_kernel(q_ref, k_ref, v_ref, o_ref, *, TQ):
    i = pl.program_id(1)
    row = lax.broadcasted_iota(jnp.int32, (TQ, TQ), 0)
    col = lax.broadcasted_iota(jnp.int32, (TQ, TQ), 1)
    for h in range(B_HEADS):
        q = q_ref[:, h * LANES:(h + 1) * LANES]

        def body(j, carry, h=h, q=q):
            m, l, acc = carry
            ks = pl.multiple_of(j * TQ, TQ)
            k = k_ref[pl.ds(ks, TQ), h * LANES:(h + 1) * LANES]
            v = v_ref[pl.ds(ks, TQ), h * HEAD_DIM:(h + 1) * HEAD_DIM]
            s = _dot_nt(q, k)
            s = jnp.where((j * TQ + col) <= (i * TQ + row), s, NEG_INF)
            mn = jnp.maximum(m, jnp.max(s, axis=-1, keepdims=True))
            a = jnp.exp(m - mn)
            e = jnp.exp(s - mn)
            l = a * l + jnp.sum(e, axis=-1, keepdims=True)
            acc = a * acc + _dot(e.astype(BF16), v)
            return mn, l, acc

        init = (jnp.full((TQ, 1), NEG_INF, F32), jnp.zeros((TQ, 1), F32), jnp.zeros((TQ, HEAD_DIM), F32))
        m, l, acc = lax.fori_loop(0, i + 1, body, init)
        o_ref[:, h * HEAD_DIM:(h + 1) * HEAD_DIM] = (acc / l).astype(o_ref.dtype)


def _mla_attn(qB, kB, vB, B, S):
    T = B * S
    TQ = TQ_MLA
    nq = S // TQ
    QW = B_HEADS * LANES
    VW = B_HEADS * HEAD_DIM
    return pl.pallas_call(
        functools.partial(_mla_attn_kernel, TQ=TQ),
        grid=(B, nq),
        in_specs=[pl.BlockSpec((TQ, QW), lambda b, i: (b * nq + i, 0)),
                  pl.BlockSpec((S, QW), lambda b, i: (b, 0)),
                  pl.BlockSpec((S, VW), lambda b, i: (b, 0))],
        out_specs=pl.BlockSpec((TQ, VW), lambda b, i: (b * nq + i, 0)),
        out_shape=jax.ShapeDtypeStruct((T, VW), BF16),
        compiler_params=_params("parallel", "parallel"),
        name="mla_attn",
    )(qB, kB, vB)


def _gelu_tanh(x):
    return 0.5 * x * (1.0 + jnp.tanh(np.float32(np.sqrt(2.0 / np.pi)) * (x + 0.044715 * (x * x * x))))


def _compress_kernel(xk_ref, xv_ref, pek_ref, pev_ref, w1k_ref, w1v_ref, w2k_ref, w2v_ref, kc_ref, vc_ref, *, NC):
    for x_ref, pe_ref, w1_ref, w2_ref, o_ref in ((xk_ref, pek_ref, w1k_ref, w2k_ref, kc_ref),
                                                  (xv_ref, pev_ref, w1v_ref, w2v_ref, vc_ref)):
        x = x_ref[...].astype(F32)
        a = _dot((x + pe_ref[0:1, :]).astype(BF16), w1_ref[0])
        b = _dot((x + pe_ref[1:2, :]).astype(BF16), w1_ref[1])
        hid = a + pltpu.roll(b, NC - 1, 0)
        o = _dot(_gelu_tanh(hid).astype(BF16), w2_ref[...])
        n = lax.broadcasted_iota(jnp.int32, o.shape, 0)
        o_ref[...] = jnp.where(n < NC - 1, o, 0.0).astype(o_ref.dtype)


def _build_compress_weights(w1, w2, pe):
    half = C_CMP_BLOCK // 2
    G = C_KV_HEADS
    eye = jnp.eye(G, dtype=w1.dtype)
    w1s = [jnp.einsum('lde,hg->lhdge', w1[c * half:(c + 1) * half], eye).reshape(half * G * HEAD_DIM, G * C_CMP_HIDDEN)
           for c in range(2)]
    w1f = jnp.stack(w1s).astype(BF16)
    w2f = jnp.einsum('ed,hg->hegd', w2, eye).reshape(G * C_CMP_HIDDEN, G * HEAD_DIM).astype(BF16)
    pes = [jnp.broadcast_to(pe[c * half:(c + 1) * half, None, :], (half, G, HEAD_DIM)).reshape(1, -1) for c in range(2)]
    pef = jnp.concatenate(pes, axis=0).astype(F32)
    return w1f, w2f, pef


def _compress(P, B, S, w1k, w2k, pek, w1v, w2v, pev):
    NC = S // C_CMP_STRIDE
    CW = C_CMP_STRIDE * C_KV_HEADS * HEAD_DIM
    xk = P[:, P_CKC:P_CKC + LANES].reshape(B * NC, CW)
    xv = P[:, P_CVC:P_CVC + LANES].reshape(B * NC, CW)
    w1kf, w2kf, pekf = _build_compress_weights(w1k, w2k, pek)
    w1vf, w2vf, pevf = _build_compress_weights(w1v, w2v, pev)
    xspec = pl.BlockSpec((NC, CW), lambda b: (b, 0))
    full2 = lambda a: pl.BlockSpec(a.shape, lambda b: (0, 0))
    full3 = lambda a: pl.BlockSpec(a.shape, lambda b: (0, 0, 0))
    ospec = pl.BlockSpec((NC, LANES), lambda b: (b, 0))
    return pl.pallas_call(
        functools.partial(_compress_kernel, NC=NC),
        grid=(B,),
        in_specs=[xspec, xspec, full2(pekf), full2(pevf), full3(w1kf), full3(w1vf), full2(w2kf), full2(w2vf)],
        out_specs=[ospec, ospec],
        out_shape=[jax.ShapeDtypeStruct((B * NC, LANES), BF16)] * 2,
        compiler_params=_params("parallel"),
        name="nsa_compress",
    )(xk, xv, pekf, pevf, w1kf, w1vf, w2kf, w2vf)


def _cmp_sel_kernel(q_ref, kc_ref, vc_ref, ovt_ref, gate_ref, ocmp_ref, sel_ref, *, TQ, NC, NSEL, NTOP, slopes):
    i = pl.program_id(1)
    t0 = i * TQ
    R = C_HEADS // C_KV_HEADS
    scale = HEAD_DIM ** -0.5
    gate = _sigmoid(gate_ref[...].astype(F32))
    last = C_CMP_BLOCK - 1
    trow = t0 + lax.broadcasted_iota(jnp.int32, (TQ, NC), 0)
    ncol = lax.broadcasted_iota(jnp.int32, (TQ, NC), 1)
    cdist = trow - (ncol * C_CMP_STRIDE + last)
    cvalid = (cdist >= 0) & (ncol < NC - 1)
    cdf = cdist.astype(F32)
    hasc = ((t0 + lax.broadcasted_iota(jnp.int32, (TQ, 1), 0)) >= last).astype(F32)
    tlane = t0 + lax.broadcasted_iota(jnp.int32, (NC, TQ), 1)
    nrow = lax.broadcasted_iota(jnp.int32, (NC, TQ), 0)
    cdist_t = tlane - (nrow * C_CMP_STRIDE + last)
    cvalid_t = (cdist_t >= 0) & (nrow < NC - 1)
    cdf_t = cdist_t.astype(F32)
    hasc_t = ((t0 + lax.broadcasted_iota(jnp.int32, (1, TQ), 1)) >= last).astype(F32)
    jrow = lax.broadcasted_iota(jnp.int32, (NSEL, TQ), 0)
    tl = t0 + lax.broadcasted_iota(jnp.int32, (NSEL, TQ), 1)
    avail = jrow * C_SEL_BLOCK <= tl
    cur = tl // C_SEL_BLOCK
    forced = (jrow == 0) | (jrow == cur) | (jrow == cur - 1)
    sel_parts = []
    for g in range(C_KV_HEADS):
        kc = kc_ref[:, g * HEAD_DIM:(g + 1) * HEAD_DIM]
        vc = vc_ref[:, g * HEAD_DIM:(g + 1) * HEAD_DIM]
        imp_t = jnp.zeros((LANES, TQ), F32)
        for r in range(R):
            hh = g * R + r
            q = q_ref[:, hh * HEAD_DIM:(hh + 1) * HEAD_DIM]
            s = _dot_nt(q, kc) * scale - slopes[hh] * cdf
            s = jnp.where(cvalid, s, NEG_INF)
            e = jnp.exp(s - jnp.max(s, axis=-1, keepdims=True))
            l = jnp.sum(e, axis=-1, keepdims=True)
            o = _dot(e.astype(BF16), vc) / l * (hasc * gate[:, 3 * hh:3 * hh + 1])
            ocmp_ref[:, hh * HEAD_DIM:(hh + 1) * HEAD_DIM] = o.astype(ocmp_ref.dtype)
            st = _dot_nt(kc, q) * scale - slopes[hh] * cdf_t
            st = jnp.where(cvalid_t, st, NEG_INF)
            et = jnp.exp(st - jnp.max(st, axis=0, keepdims=True))
            pt = et / jnp.sum(et, axis=0, keepdims=True) * hasc_t
            imp_t = imp_t + _dot(ovt_ref[...], pt.astype(BF16))
        v = jnp.where(avail, imp_t[0:NSEL, :], NEG_INF)
        v = jnp.where(forced, -NEG_INF, v)
        rank = jnp.zeros((NSEL, TQ), F32)
        for ii in range(NSEL):
            ri = v[ii:ii + 1, :]
            rank = rank + jnp.where(jrow > ii, jnp.where(ri >= v, 1.0, 0.0), jnp.where(ri > v, 1.0, 0.0))
        sel_parts.append(jnp.where(rank < NTOP, 1.0, 0.0))
    pad = LANES - C_KV_HEADS * NSEL
    if pad:
        sel_parts.append(jnp.zeros((pad, TQ), F32))
    sel_t = jnp.concatenate(sel_parts, axis=0)
    sel_ref[...] = sel_t.T.astype(sel_ref.dtype)


def _overlap_t(S):
    n_cmp = (S - C_CMP_BLOCK) // C_CMP_STRIDE + 1
    n_sel = S // C_SEL_BLOCK
    NC = S // C_CMP_STRIDE
    cstart = np.arange(n_cmp) * C_CMP_STRIDE
    cend = cstart + C_CMP_BLOCK - 1
    sstart = np.arange(n_sel) * C_SEL_BLOCK
    send = sstart + C_SEL_BLOCK - 1
    ov = np.clip(np.minimum(cend[:, None], send[None, :]) - np.maximum(cstart[:, None], sstart[None, :]) + 1,
                 0, None).astype(np.float32) / C_CMP_STRIDE
    ovt = np.zeros((LANES, NC), np.float32)
    ovt[:n_sel, :n_cmp] = ov.T
    return jnp.asarray(ovt, dtype=BF16)


def _cmp_sel(P, kc, vc, B, S, slopes):
    T = B * S
    TQ = TQ_ATT
    nq = S // TQ
    NC = S // C_CMP_STRIDE
    NSEL = S // C_SEL_BLOCK
    assert C_KV_HEADS * NSEL <= LANES
    QW = C_HEADS * HEAD_DIM
    kern = functools.partial(_cmp_sel_kernel, TQ=TQ, NC=NC, NSEL=NSEL, NTOP=min(C_SEL_TOP, NSEL), slopes=slopes)
    return pl.pallas_call(
        kern,
        grid=(B, nq),
        in_specs=[pl.BlockSpec((TQ, QW), lambda b, i: (b * nq + i, P_CQ // QW)),
                  pl.BlockSpec((NC, LANES), lambda b, i: (b, 0)),
                  pl.BlockSpec((NC, LANES), lambda b, i: (b, 0)),
                  pl.BlockSpec((LANES, NC), lambda b, i: (0, 0)),
                  pl.BlockSpec((TQ, LANES), lambda b, i: (b * nq + i, P_GATE // LANES))],
        out_specs=[pl.BlockSpec((TQ, QW), lambda b, i: (b * nq + i, 0)),
                   pl.BlockSpec((TQ, LANES), lambda b, i: (b * nq + i, 0))],
        out_shape=[jax.ShapeDtypeStruct((T, QW), BF16), jax.ShapeDtypeStruct((T, LANES), BF16)],
        compiler_params=_params("parallel", "parallel"),
        name="nsa_cmp_select",
    )(P, kc, vc, _overlap_t(S), P)


def _sel_attn_kernel(q_ref, sel_ref, k_ref, v_ref, e_ref, gate_ref, o_ref, *, TQ, TK, NSEL, slopes):
    i = pl.program_id(1)
    t0 = i * TQ
    R = C_HEADS // C_KV_HEADS
    scale = HEAD_DIM ** -0.5
    nkt = (t0 + TQ + TK - 1) // TK
    gate = _sigmoid(gate_ref[...].astype(F32))
    row = lax.broadcasted_iota(jnp.int32, (TQ, TK), 0)
    col = lax.broadcasted_iota(jnp.int32, (TQ, TK), 1)
    for g in range(C_KV_HEADS):
        selg = sel_ref[:, g * NSEL:(g + 1) * NSEL]
        qs = [q_ref[:, (g * R + r) * HEAD_DIM:(g * R + r + 1) * HEAD_DIM] for r in range(R)]

        def body(j, carry, g=g, selg=selg, qs=qs):
            ks = pl.multiple_of(j * TK, TK)
            k = k_ref[pl.ds(ks, TK), g * HEAD_DIM:(g + 1) * HEAD_DIM]
            v = v_ref[pl.ds(ks, TK), g * HEAD_DIM:(g + 1) * HEAD_DIM]
            picked = _dot(selg, e_ref[j])
            dist = (t0 + row) - (ks + col)
            valid = (picked > 0.5) & (dist >= 0)
            distf = dist.astype(F32)
            out = []
            for r in range(R):
                m, l, acc = carry[r]
                s = _dot_nt(qs[r], k) * scale - slopes[g * R + r] * distf
                s = jnp.where(valid, s, NEG_INF)
                mn = jnp.maximum(m, jnp.max(s, axis=-1, keepdims=True))
                a = jnp.exp(m - mn)
                e = jnp.where(valid, jnp.exp(s - mn), 0.0)
                l = a * l + jnp.sum(e, axis=-1, keepdims=True)
                acc = a * acc + _dot(e.astype(BF16), v)
                out.append((mn, l, acc))
            return tuple(out)

        init = tuple((jnp.full((TQ, 1), NEG_INF, F32), jnp.zeros((TQ, 1), F32), jnp.zeros((TQ, HEAD_DIM), F32))
                     for _ in range(R))
        res = lax.fori_loop(0, nkt, body, init)
        for r in range(R):
            hh = g * R + r
            m, l, acc = res[r]
            o = acc / l * gate[:, 3 * hh + 1:3 * hh + 2]
            o_ref[:, hh * HEAD_DIM:(hh + 1) * HEAD_DIM] = o.astype(o_ref.dtype)


def _sel_attn(P, sel, B, S, slopes):
    T = B * S
    TQ = TQ_ATT
    TK = min(TK_SEL, S)
    nq = S // TQ
    NSEL = S // C_SEL_BLOCK
    QW = C_HEADS * HEAD_DIM
    blk = (np.arange(S) // C_SEL_BLOCK).reshape(S // TK, 1, TK)
    e3 = jnp.asarray((blk == np.arange(NSEL)[None, :, None]).astype(np.float32), dtype=BF16)
    kern = functools.partial(_sel_attn_kernel, TQ=TQ, TK=TK, NSEL=NSEL, slopes=slopes)
    return pl.pallas_call(
        kern,
        grid=(B, nq),
        in_specs=[pl.BlockSpec((TQ, QW), lambda b, i: (b * nq + i, P_CQ // QW)),
                  pl.BlockSpec((TQ, LANES), lambda b, i: (b * nq + i, 0)),
                  pl.BlockSpec((S, LANES), lambda b, i: (b, P_CKS // LANES)),
                  pl.BlockSpec((S, LANES), lambda b, i: (b, P_CVS // LANES)),
                  pl.BlockSpec((S // TK, NSEL, TK), lambda b, i: (0, 0, 0)),
                  pl.BlockSpec((TQ, LANES), lambda b, i: (b * nq + i, P_GATE // LANES))],
        out_specs=pl.BlockSpec((TQ, QW), lambda b, i: (b * nq + i, 0)),
        out_shape=jax.ShapeDtypeStruct((T, QW), BF16),
        compiler_params=_params("parallel", "parallel"),
        name="nsa_selected",
    )(P, sel, P, P, e3, P)


def _outproj_kernel(x_ref, oa_ref, ob_ref, oc1_ref, oc2_ref, oc3_ref, ga_ref, gb_ref, gc_ref, wo_ref, gp_ref, o_ref):
    oa = _rms(oa_ref[...].astype(F32), ga_ref[...]).astype(BF16)
    ob = _rms(ob_ref[...].astype(F32), gb_ref[...]).astype(BF16)
    oc = oc1_ref[...].astype(F32) + oc2_ref[...].astype(F32) + oc3_ref[...].astype(F32)
    oc = _rms(oc, gc_ref[...]).astype(BF16)
    m = _dot(oa, wo_ref[0:256, :]) + _dot(ob, wo_ref[256:512, :]) + _dot(oc, wo_ref[512:1024, :])
    o_ref[...] = x_ref[...] + _rms(m, gp_ref[...])


def _outproj(x, o_a, o_b, o_cmp, o_slc, o_win, g_oa, g_ob, g_oc, w_o, g_post):
    T = x.shape[0]
    TM = TM_PROJ
    row = lambda w: pl.BlockSpec((TM, w), lambda i: (i, 0))
    full = lambda r, c: pl.BlockSpec((r, c), lambda i: (0, 0))
    return pl.pallas_call(
        _outproj_kernel,
        grid=(T // TM,),
        in_specs=[row(D_MODEL), row(256), row(256), row(512), row(512), row(512),
                  full(1, 256), full(1, 256), full(1, 512), full(D_MODEL, D_MODEL), full(1, D_MODEL)],
        out_specs=row(D_MODEL),
        out_shape=jax.ShapeDtypeStruct((T, D_MODEL), F32),
        compiler_params=_params("parallel"),
        name="outproj",
    )(x, o_a, o_b, o_cmp, o_slc, o_win, g_oa.reshape(1, -1), g_ob.reshape(1, -1), g_oc.reshape(1, -1),
      w_o.astype(BF16), g_post.reshape(1, -1))


def _swiglu_step(h, wg_ref, wu_ref, wd_ref, acc_ref):
    a = _dot(h, wg_ref[...])
    b = _dot(h, wu_ref[...])
    z = (a * _sigmoid(a) * b).astype(BF16)
    acc_ref[...] += _dot(z, wd_ref[...])


def _ffn_kernel(x_ref, gpre_ref, wg_ref, wu_ref, wd_ref, gpost_ref, o_ref, h_sc, acc_sc, *, nf):
    f = pl.program_id(1)

    @pl.when(f == 0)
    def _():
        h_sc[...] = _rms(x_ref[...], gpre_ref[...]).astype(BF16)
        acc_sc[...] = jnp.zeros_like(acc_sc)

    _swiglu_step(h_sc[...], wg_ref, wu_ref, wd_ref, acc_sc)

    @pl.when(f == nf - 1)
    def _():
        o_ref[...] = x_ref[...] + _rms(acc_sc[...], gpost_ref[...])


def _ffn(x, g_pre, wg, wu, wd, g_post):
    T = x.shape[0]
    TM, TF = TM_FFN, TF_FFN
    FF = wg.shape[1]
    nf = FF // TF
    return pl.pallas_call(
        functools.partial(_ffn_kernel, nf=nf),
        grid=(T // TM, nf),
        in_specs=[pl.BlockSpec((TM, D_MODEL), lambda i, f: (i, 0)),
                  pl.BlockSpec((1, D_MODEL), lambda i, f: (0, 0)),
                  pl.BlockSpec((D_MODEL, TF), lambda i, f: (0, f)),
                  pl.BlockSpec((D_MODEL, TF), lambda i, f: (0, f)),
                  pl.BlockSpec((TF, D_MODEL), lambda i, f: (f, 0)),
                  pl.BlockSpec((1, D_MODEL), lambda i, f: (0, 0))],
        out_specs=pl.BlockSpec((TM, D_MODEL), lambda i, f: (i, 0)),
        out_shape=jax.ShapeDtypeStruct((T, D_MODEL), F32),
        scratch_shapes=[pltpu.VMEM((TM, D_MODEL), BF16), pltpu.VMEM((TM, D_MODEL), F32)],
        compiler_params=_params("parallel", "arbitrary"),
        name="dense_ffn",
    )(x, g_pre.reshape(1, -1), wg.astype(BF16), wu.astype(BF16), wd.astype(BF16), g_post.reshape(1, -1))


def _expert_kernel(te_ref, nt_ref, x_ref, wg_ref, wu_ref, wd_ref, o_ref, acc_sc, *, nf):
    j = pl.program_id(0)
    f = pl.program_id(1)
    live = j < nt_ref[0]

    @pl.when(f == 0)
    def _():
        acc_sc[...] = jnp.zeros_like(acc_sc)

    @pl.when(live)
    def _():
        _swiglu_step(x_ref[...], wg_ref.at[0], wu_ref.at[0], wd_ref.at[0], acc_sc)

    @pl.when(f == nf - 1)
    def _():
        o_ref[...] = acc_sc[...].astype(o_ref.dtype)


def _experts(xs, tile_expert, n_tiles, wg, wu, wd):
    NT = xs.shape[0] // TM_MOE
    TM, TF = TM_MOE, TF_FFN
    FF = wg.shape[2]
    nf = FF // TF
    fidx = lambda j, f, nt: jnp.where(j < nt[0], f, nf - 1)
    grid_spec = pltpu.PrefetchScalarGridSpec(
        num_scalar_prefetch=2,
        grid=(NT, nf),
        in_specs=[pl.BlockSpec((TM, D_MODEL), lambda j, f, te, nt: (j, 0)),
                  pl.BlockSpec((1, D_MODEL, TF), lambda j, f, te, nt: (te[j], 0, fidx(j, f, nt))),
                  pl.BlockSpec((1, D_MODEL, TF), lambda j, f, te, nt: (te[j], 0, fidx(j, f, nt))),
                  pl.BlockSpec((1, TF, D_MODEL), lambda j, f, te, nt: (te[j], fidx(j, f, nt), 0))],
        out_specs=pl.BlockSpec((TM, D_MODEL), lambda j, f, te, nt: (j, 0)),
        scratch_shapes=[pltpu.VMEM((TM, D_MODEL), F32)])
    return pl.pallas_call(
        functools.partial(_expert_kernel, nf=nf),
        grid_spec=grid_spec,
        out_shape=jax.ShapeDtypeStruct(xs.shape, BF16),
        compiler_params=_params("parallel", "arbitrary"),
        name="moe_experts",
    )(tile_expert, n_tiles, xs, wg, wu, wd)


def _router_kernel(x_ref, g_ref, wr_ref, h_out, meta_out, metat_out, start_out, cnt_out, run_sc, *, TS):
    s = pl.program_id(0)

    @pl.when(s == 0)
    def _():
        run_sc[...] = jnp.zeros_like(run_sc)

    h = _rms(x_ref[...], g_ref[...])
    hb = h.astype(BF16)
    h_out[...] = hb
    hl = (h - hb.astype(F32)).astype(BF16)
    w = wr_ref[...]
    wh = w.astype(BF16)
    wl = (w - wh.astype(F32)).astype(BF16)
    logits = _dot(hb, wh) + _dot(hl, wh) + _dot(hb, wl)
    lane = lax.broadcasted_iota(jnp.int32, (TS, LANES), 1).astype(F32)
    logits = jnp.where(lane < N_EXPERTS, logits, NEG_INF)
    m1 = jnp.max(logits, axis=-1, keepdims=True)
    i1 = jnp.min(jnp.where(logits == m1, lane, float(LANES)), axis=-1, keepdims=True)
    rest = jnp.where(lane == i1, NEG_INF, logits)
    m2 = jnp.max(rest, axis=-1, keepdims=True)
    i2 = jnp.min(jnp.where(rest == m2, lane, float(LANES)), axis=-1, keepdims=True)
    ex = jnp.exp(m2 - m1)
    w1 = 1.0 / (1.0 + ex)
    w2 = ex / (1.0 + ex)
    oh1 = lane == i1
    oh2 = lane == i2
    oh = jnp.where(oh1, 1.0, 0.0) + jnp.where(oh2, 1.0, 0.0)
    tri = jnp.where(lax.broadcasted_iota(jnp.int32, (TS, TS), 0) > lax.broadcasted_iota(jnp.int32, (TS, TS), 1),
                    1.0, 0.0).astype(BF16)
    run = run_sc[...]
    before = _dot(tri, oh.astype(BF16)) + run
    r1 = jnp.sum(jnp.where(oh1, before, 0.0), axis=-1, keepdims=True)
    r2 = jnp.sum(jnp.where(oh2, before, 0.0), axis=-1, keepdims=True)
    start_out[0] = run
    run = run + jnp.sum(oh, axis=0, keepdims=True)
    run_sc[...] = run
    cnt_out[...] = run
    meta = jnp.where(lane == 0, i1, jnp.where(lane == 1, i2, jnp.where(lane == 2, r1, jnp.where(
        lane == 3, r2, jnp.where(lane == 4, w1, jnp.where(lane == 5, w2, 0.0))))))
    meta_out[...] = meta
    metat_out[...] = meta.T[0:8, :]


def _router(x, g, w_router):
    T = x.shape[0]
    TS = TS_MOE
    nt = T // TS
    wr = jnp.zeros((D_MODEL, LANES), F32).at[:, :N_EXPERTS].set(w_router.astype(F32))
    return pl.pallas_call(
        functools.partial(_router_kernel, TS=TS),
        grid=(nt,),
        in_specs=[pl.BlockSpec((TS, D_MODEL), lambda s: (s, 0)),
                  pl.BlockSpec((1, D_MODEL), lambda s: (0, 0)),
                  pl.BlockSpec((D_MODEL, LANES), lambda s: (0, 0))],
        out_specs=[pl.BlockSpec((TS, D_MODEL), lambda s: (s, 0)),
                   pl.BlockSpec((TS, LANES), lambda s: (s, 0)),
                   pl.BlockSpec((8, TS), lambda s: (0, s)),
                   pl.BlockSpec((1, 1, LANES), lambda s: (s, 0, 0)),
                   pl.BlockSpec((1, LANES), lambda s: (0, 0))],
        out_shape=[jax.ShapeDtypeStruct((T, D_MODEL), BF16),
                   jax.ShapeDtypeStruct((T, LANES), F32),
                   jax.ShapeDtypeStruct((8, T), F32),
                   jax.ShapeDtypeStruct((nt, 1, LANES), F32),
                   jax.ShapeDtypeStruct((1, LANES), F32)],
        scratch_shapes=[pltpu.VMEM((1, LANES), F32)],
        compiler_params=_params("arbitrary"),
        name="moe_router",
    )(x, g.reshape(1, -1), wr)


def _gather_kernel(te_ref, slo_ref, shi_ref, base_ref, metat_ref, h_ref, xs_out, acc_sc, *, TM, TS):
    j = pl.program_id(0)
    e = te_ref[j]
    ef = e.astype(F32)
    want = (j * TM - base_ref[e] + lax.broadcasted_iota(jnp.int32, (TM, 1), 0)).astype(F32)
    acc_sc[...] = jnp.zeros_like(acc_sc)

    def body(s, _):
        off = pl.multiple_of(s * TS, TS)
        e1 = metat_ref[0:1, pl.ds(off, TS)]
        e2 = metat_ref[1:2, pl.ds(off, TS)]
        r1 = metat_ref[2:3, pl.ds(off, TS)]
        r2 = metat_ref[3:4, pl.ds(off, TS)]
        key = jnp.where(e1 == ef, r1, jnp.where(e2 == ef, r2, -1.0))
        sel = jnp.where(key == want, 1.0, 0.0).astype(BF16)
        acc_sc[...] += _dot(sel, h_ref[pl.ds(off, TS), :])
        return 0

    lax.fori_loop(slo_ref[j], shi_ref[j] + 1, body, 0)
    xs_out[...] = acc_sc[...].astype(xs_out.dtype)


def _gather(hb, metat, tile_expert, s_lo, s_hi, base, NT):
    T = hb.shape[0]
    TM, TS = TM_MOE, TS_MOE
    grid_spec = pltpu.PrefetchScalarGridSpec(
        num_scalar_prefetch=4,
        grid=(NT,),
        in_specs=[pl.BlockSpec(memory_space=pltpu.VMEM), pl.BlockSpec(memory_space=pltpu.VMEM)],
        out_specs=pl.BlockSpec((TM, D_MODEL), lambda j, *_: (j, 0)),
        scratch_shapes=[pltpu.VMEM((TM, D_MODEL), F32)])
    return pl.pallas_call(
        functools.partial(_gather_kernel, TM=TM, TS=TS),
        grid_spec=grid_spec,
        out_shape=jax.ShapeDtypeStruct((NT * TM, D_MODEL), BF16),
        compiler_params=_params("arbitrary"),
        name="moe_gather",
    )(tile_expert, s_lo, s_hi, base, metat, hb)


def _combine_kernel(blk_ref, nch_ref, base_ref, meta_ref, x_ref, gpost_ref, ys_hbm, o_ref, buf, sem, acc_sc,
                    *, TS, CH, U):
    s = pl.program_id(0)
    n = nch_ref[s]
    meta = meta_ref[...]
    e1, e2 = meta[:, 0:1], meta[:, 1:2]
    w1, w2 = meta[:, 4:5], meta[:, 5:6]
    b1 = jnp.zeros((TS, 1), F32)
    b2 = jnp.zeros((TS, 1), F32)
    for e in range(N_EXPERTS):
        be = base_ref[e].astype(F32)
        b1 = jnp.where(e1 == float(e), be, b1)
        b2 = jnp.where(e2 == float(e), be, b2)
    pos1 = b1 + meta[:, 2:3]
    pos2 = b2 + meta[:, 3:4]

    def chunk_copy(u, slot):
        r0 = pl.multiple_of(blk_ref[s * U + u] * CH, CH)
        return pltpu.make_async_copy(ys_hbm.at[pl.ds(r0, CH), :], buf.at[slot], sem.at[slot])

    acc_sc[...] = jnp.zeros_like(acc_sc)

    @pl.when(n > 0)
    def _():
        chunk_copy(0, 0).start()

    def body(u, _):
        slot = u % 2
        chunk_copy(u, slot).wait()

        @pl.when(u + 1 < n)
        def _():
            chunk_copy(u + 1, 1 - slot).start()

        rows = (blk_ref[s * U + u] * CH + lax.broadcasted_iota(jnp.int32, (1, CH), 1)).astype(F32)
        selw = jnp.where(pos1 == rows, w1, 0.0) + jnp.where(pos2 == rows, w2, 0.0)
        acc_sc[...] += _dot(selw.astype(BF16), buf[slot])
        return 0

    lax.fori_loop(0, n, body, 0)
    o_ref[...] = x_ref[...] + _rms(acc_sc[...], gpost_ref[...])


def _combine(x, meta, ys, chunk_blk, n_chunks, base, g_post):
    T = x.shape[0]
    TS, CH, U = TS_MOE, CH_MOE, U_MOE
    grid_spec = pltpu.PrefetchScalarGridSpec(
        num_scalar_prefetch=3,
        grid=(T // TS,),
        in_specs=[pl.BlockSpec((TS, LANES), lambda s, *_: (s, 0)),
                  pl.BlockSpec((TS, D_MODEL), lambda s, *_: (s, 0)),
                  pl.BlockSpec((1, D_MODEL), lambda s, *_: (0, 0)),
                  pl.BlockSpec(memory_space=pl.ANY)],
        out_specs=pl.BlockSpec((TS, D_MODEL), lambda s, *_: (s, 0)),
        scratch_shapes=[pltpu.VMEM((2, CH, D_MODEL), BF16), pltpu.SemaphoreType.DMA((2,)),
                        pltpu.VMEM((TS, D_MODEL), F32)])
    return pl.pallas_call(
        functools.partial(_combine_kernel, TS=TS, CH=CH, U=U),
        grid_spec=grid_spec,
        out_shape=jax.ShapeDtypeStruct((T, D_MODEL), F32),
        compiler_params=_params("arbitrary"),
        name="moe_combine",
    )(chunk_blk, n_chunks, base, meta, x, g_post.reshape(1, -1), ys)


def _moe(x, g_pre, w_router, w_gate, w_up, w_down, g_post):
    T = x.shape[0]
    TS, TM, CH, U = TS_MOE, TM_MOE, CH_MOE, U_MOE
    nt = T // TS
    NT = (2 * T) // TM + N_EXPERTS
    hb, meta, metat, start, cnt = _router(x, g_pre, w_router)

    i32 = jnp.int32
    counts = cnt[0, :N_EXPERTS].astype(i32)
    start = start[:, 0, :N_EXPERTS].astype(i32)
    start_ext = jnp.concatenate([start, counts[None, :]], axis=0)
    tiles_e = (counts + TM - 1) // TM
    tiles_cum = jnp.cumsum(tiles_e)
    base = (tiles_cum - tiles_e) * TM
    n_tiles = tiles_cum[-1:]
    jj = jnp.arange(NT, dtype=i32)
    tile_expert = jnp.minimum(jnp.sum(jj[:, None] >= tiles_cum[None, :], axis=1), N_EXPERTS - 1).astype(i32)
    live = jj < n_tiles[0]
    a = jj * TM - base[tile_expert]
    st_e = start_ext[:, tile_expert]
    s_lo = jnp.sum(st_e[1:] <= a[None, :], axis=0).astype(i32)
    s_hi = (jnp.sum(st_e[:-1] < (a + TM)[None, :], axis=0) - 1).astype(i32)
    s_lo = jnp.where(live, s_lo, 1)
    s_hi = jnp.where(live, s_hi, 0)

    xs = _gather(hb, metat, tile_expert, s_lo, s_hi, base.astype(i32), NT)
    ys = _experts(xs, tile_expert, n_tiles.astype(i32), w_gate.astype(BF16), w_up.astype(BF16), w_down.astype(BF16))

    lo = base[None, :] + start_ext[:-1]
    hi = base[None, :] + start_ext[1:]
    n_e = jnp.where(hi > lo, (hi - 1) // CH - lo // CH + 1, 0)
    cum = jnp.cumsum(n_e, axis=1)
    uu = jnp.arange(U, dtype=i32)
    e_u = jnp.minimum(jnp.sum(uu[None, :, None] >= cum[:, None, :], axis=2), N_EXPERTS - 1)
    first = jnp.take_along_axis(lo // CH, e_u, axis=1)
    skipped = jnp.take_along_axis(cum - n_e, e_u, axis=1)
    chunk_blk = jnp.clip(first + (uu[None, :] - skipped), 0, (NT * TM) // CH - 1).astype(i32).reshape(-1)
    n_chunks = jnp.minimum(cum[:, -1], U).astype(i32)
    return _combine(x, meta, ys, chunk_blk, n_chunks, base.astype(i32), g_post)


def kernel(x, positions, w_in, a_sinks, g_cq, w_uq, g_ckv, w_ukv, c_w1_k, c_w2_k, c_pe_k, c_w1_v, c_w2_v, c_pe_v,
           g_oa, g_ob, g_oc, w_o, g_pre_mix, g_post_mix, g_pre_ffn, g_post_ffn, ffn_w_gate, ffn_w_up, ffn_w_down,
           moe_router, moe_w_gate, moe_w_up, moe_w_down):
    B, S, _ = x.shape
    T = B * S
    depth = w_in.shape[0]
    a_slopes = _alibi_slopes(A_HEADS)
    c_slopes = _alibi_slopes(C_HEADS)
    xf = x.reshape(T, D_MODEL).astype(F32)
    cos, sin = _rope_tables(positions)
    for l in range(depth):
        P = _inproj(xf, g_pre_mix[l].reshape(1, -1), _build_w_in(w_in[l]))
        o_a = _band_attn(P, B, S, q_col=P_AQ, k_col=P_AK, v_col=P_AV, W=A_WINDOW, G=A_KV_HEADS,
                         R=A_HEADS // A_KV_HEADS, slopes=a_slopes, sinks=a_sinks[l])
        qB, kB, vB = _mla_prep(P, cos, sin, g_cq[l], g_ckv[l], w_uq[l], w_ukv[l])
        o_b = _mla_attn(qB, kB, vB, B, S)
        kc, vc = _compress(P, B, S, c_w1_k[l], c_w2_k[l], c_pe_k[l], c_w1_v[l], c_w2_v[l], c_pe_v[l])
        o_cmp, sel = _cmp_sel(P, kc, vc, B, S, c_slopes)
        o_slc = _sel_attn(P, sel, B, S, c_slopes)
        o_win = _band_attn(P, B, S, q_col=P_CQ, k_col=P_CKW, v_col=P_CVW, W=C_WINDOW, G=C_KV_HEADS,
                           R=C_HEADS // C_KV_HEADS, slopes=c_slopes, gate_branch=2)
        xf = _outproj(xf, o_a, o_b, o_cmp, o_slc, o_win, g_oa[l], g_ob[l], g_oc[l], w_o[l], g_post_mix[l])
        if l % 2 == 0:
            xf = _ffn(xf, g_pre_ffn[l], ffn_w_gate[l // 2], ffn_w_up[l // 2], ffn_w_down[l // 2], g_post_ffn[l])
        else:
            xf = _moe(xf, g_pre_ffn[l], moe_router[l // 2], moe_w_gate[l // 2], moe_w_up[l // 2],
                      moe_w_down[l // 2], g_post_ffn[l])
    return xf.reshape(B, S, D_MODEL)
```

```python
import functools

import numpy as np
import jax
import jax.numpy as jnp
from jax import lax
from jax.experimental import pallas as pl
from jax.experimental.pallas import tpu as pltpu

F32 = jnp.float32
BF16 = jnp.bfloat16

D_MODEL = 1024
HEAD_DIM = 64
NORM_EPS = 1e-6
NEG_INF = -1e30
ROPE_THETA = 10000.0

A_HEADS, A_KV_HEADS, A_WINDOW = 4, 2, 128
B_HEADS, B_Q_RANK, B_KV_RANK, B_NOPE, B_ROPE = 4, 256, 128, 64, 32
C_HEADS, C_KV_HEADS, C_WINDOW = 8, 2, 512
C_CMP_BLOCK, C_CMP_STRIDE, C_CMP_HIDDEN = 32, 16, 128
C_SEL_BLOCK, C_SEL_TOP = 64, 16
N_EXPERTS = 8

LANES = 128
VMEM_LIMIT = 48 * 1024 * 1024

P_CQ, P_AQ, P_BCQ = 0, 512, 768
P_BCKV, P_CKC, P_CVC, P_KR, P_KRROT, P_GATE = 1024, 1152, 1280, 1408, 1536, 1664
P_KSEL, KD_SEL = 1792, 256
P_KWIN, KD_WIN = 2304, 128
P_KA, KD_A = 2560, 128
P_WIDTH = 2816
V_ROWS = 80
VT_SEL, VT_WIN, VT_A = 0, 2, 4
VT_SLABS = 6
MASK_BIAS = -1e30
M_INIT = -1e29

TM_PROJ = 512
TQ_ATT = 128
TK_BAND = 128
TQ_MLA = 256
TK_SEL = 256
TM_FFN = 512
TF_FFN = 512
TS_MOE = 512
TM_MOE = 512
CH_MOE = 128
U_MOE = 24


def _alibi_slopes(n):
    return [float(np.float32(2.0 ** (-8.0 * (i + 1) / n))) for i in range(n)]


def _dot(a, b):
    return jnp.dot(a, b, preferred_element_type=F32)


def _dot_nt(a, b):
    return lax.dot_general(a, b, (((1,), (1,)), ((), ())), preferred_element_type=F32)


def _rms(x, g):
    return x * lax.rsqrt(jnp.mean(x * x, axis=-1, keepdims=True) + NORM_EPS) * g


def _sigmoid(x):
    return 1.0 / (1.0 + jnp.exp(-x))


def _params(*sem):
    return pltpu.CompilerParams(dimension_semantics=sem, vmem_limit_bytes=VMEM_LIMIT)


def _inproj_kernel(x_ref, g_ref, w_ref, wvt_ref, vone_ref, kc_ref, o_ref, vt_ref):
    h = _rms(x_ref[...], g_ref[...]).astype(BF16)
    for n in range(P_WIDTH // 256):
        o_ref[:, n * 256:(n + 1) * 256] = _dot(h, w_ref[:, n * 256:(n + 1) * 256]).astype(BF16)
    for g in range(C_KV_HEADS):
        o_ref[:, P_KSEL + g * KD_SEL + 64:P_KSEL + g * KD_SEL + 192] = kc_ref[:, 0:128]
        o_ref[:, P_KWIN + g * KD_WIN + 64:P_KWIN + (g + 1) * KD_WIN] = kc_ref[:, 128:192]
    for g in range(A_KV_HEADS):
        o_ref[:, P_KA + g * KD_A + 64:P_KA + (g + 1) * KD_A] = kc_ref[:, 128:192]
    vt_ref[...] = (_dot_nt(wvt_ref[...], h) + vone_ref[...]).astype(BF16)


def _key_constants(S):
    s = np.arange(S)
    kc = np.zeros((S, 256), np.float32)
    kc[:, 0] = s % TK_SEL
    kc[s, 64 + s // C_SEL_BLOCK] = 1.0
    kc[:, 128] = s % TK_BAND
    return jnp.asarray(kc, dtype=BF16)


def _inproj(x, g, w, wvt, S):
    T = x.shape[0]
    TM = TM_PROJ
    assert S // C_SEL_BLOCK <= 64 and TK_SEL <= 256 and TK_BAND <= 256
    vone = np.zeros((VT_SLABS * V_ROWS, 1), np.float32)
    vone[HEAD_DIM::V_ROWS] = 1.0
    nblk = S // TM
    return pl.pallas_call(
        _inproj_kernel,
        grid=(T // TM,),
        in_specs=[pl.BlockSpec((TM, D_MODEL), lambda i: (i, 0)),
                  pl.BlockSpec((1, D_MODEL), lambda i: (0, 0)),
                  pl.BlockSpec((D_MODEL, P_WIDTH), lambda i: (0, 0)),
                  pl.BlockSpec((VT_SLABS * V_ROWS, D_MODEL), lambda i: (0, 0)),
                  pl.BlockSpec((VT_SLABS * V_ROWS, 1), lambda i: (0, 0)),
                  pl.BlockSpec((TM, 256), lambda i: (i % nblk, 0))],
        out_specs=[pl.BlockSpec((TM, P_WIDTH), lambda i: (i, 0)),
                   pl.BlockSpec((VT_SLABS * V_ROWS, TM), lambda i: (0, i))],
        out_shape=[jax.ShapeDtypeStruct((T, P_WIDTH), BF16),
                   jax.ShapeDtypeStruct((VT_SLABS * V_ROWS, T), BF16)],
        compiler_params=_params("parallel"),
        name="inproj",
    )(x, g, w, wvt, jnp.asarray(vone), _key_constants(S))


def _build_w_in(w_in):
    cuts = np.cumsum([0, 256, 128, 128, 256, 128, 32, 512, 128, 128, 128, 128, 128, 128, 24])
    seg = [w_in[:, cuts[i]:cuts[i + 1]] for i in range(14)]
    a_q, a_k, a_v, b_cq, b_ckv, b_kr, c_q, c_kc, c_vc, c_ks, c_vs, c_kw, c_vw, c_g = seg
    z = lambda n: jnp.zeros((w_in.shape[0], n), w_in.dtype)
    half = B_ROPE // 2
    kr_rot = jnp.concatenate([-b_kr[:, half:], b_kr[:, :half]], axis=1)
    cols = [c_q, a_q, b_cq, b_ckv, c_kc, c_vc,
            z(B_NOPE), b_kr, z(LANES - B_NOPE - B_ROPE),
            z(B_NOPE), kr_rot, z(LANES - B_NOPE - B_ROPE),
            c_g, z(LANES - 24)]
    for k, kd in ((c_ks, KD_SEL), (c_kw, KD_WIN), (a_k, KD_A)):
        for g in range(2):
            cols += [k[:, g * HEAD_DIM:(g + 1) * HEAD_DIM], z(kd - HEAD_DIM)]
    w = jnp.concatenate(cols, axis=1).astype(BF16)
    rows = []
    for v in (c_vs, c_vw, a_v):
        for g in range(2):
            rows += [v[:, g * HEAD_DIM:(g + 1) * HEAD_DIM].T, jnp.zeros((V_ROWS - HEAD_DIM, w_in.shape[0]), w_in.dtype)]
    return w, jnp.concatenate(rows, axis=0).astype(BF16)


def _rope_tab_kernel(pos_ref, inv_ref, cos_ref, sin_ref):
    ang = pos_ref[...] * inv_ref[...]
    cos_ref[...] = jnp.cos(ang)
    sin_ref[...] = jnp.sin(ang)


def _rope_tables(positions):
    T = positions.size
    inv = (ROPE_THETA ** (-np.arange(0, B_ROPE, 2, dtype=np.float32) / B_ROPE)).astype(np.float32)
    inv128 = np.zeros((1, LANES), np.float32)
    inv128[0, B_NOPE:B_NOPE + B_ROPE // 2] = inv
    inv128[0, B_NOPE + B_ROPE // 2:B_NOPE + B_ROPE] = inv
    pos = positions.reshape(T, 1).astype(F32)
    return pl.pallas_call(
        _rope_tab_kernel,
        grid=(T // TM_PROJ,),
        in_specs=[pl.BlockSpec((TM_PROJ, 1), lambda i: (i, 0)),
                  pl.BlockSpec((1, LANES), lambda i: (0, 0))],
        out_specs=[pl.BlockSpec((TM_PROJ, LANES), lambda i: (i, 0))] * 2,
        out_shape=[jax.ShapeDtypeStruct((T, LANES), F32)] * 2,
        compiler_params=_params("parallel"),
        name="rope_tables",
    )(pos, jnp.asarray(inv128))


def _flash_kernel(*refs, TQ, TK, G, R, W, slopes, scale, sel, has_sink, gate_branch):
    refs = list(refs)
    q_ref = refs.pop(0)
    selb_ref = refs.pop(0) if sel else None
    k_refs = [refs.pop(0) for _ in range(G)]
    v_refs = [refs.pop(0) for _ in range(G)]
    sink_ref = refs.pop(0) if has_sink else None
    gate_ref = refs.pop(0) if gate_branch is not None else None
    o_ref = refs.pop(0)
    qa_sc = refs.pop(0) if slopes is not None else None
    m_sc, acc_sc, s_sc = refs
    i = pl.program_id(1)
    t0 = i * TQ
    RQ = R * TQ
    lane = lax.broadcasted_iota(jnp.int32, (1, RQ), 1)
    qpos = t0 + (lane & (TQ - 1))
    hi = (t0 + TQ + TK - 1) // TK
    lo = jnp.maximum(t0 - W, 0) // TK if W else 0
    if gate_ref is not None:
        gate_t = _sigmoid(gate_ref[...].astype(F32)).T
    slope_vecs = []
    if slopes is not None:
        q_t = q_ref[...].astype(F32).T
        feat = lax.broadcasted_iota(jnp.int32, (HEAD_DIM, TQ), 0) == 0
    for g in range(G):
        if slopes is not None:
            slope_vec = jnp.zeros((1, RQ), F32)
            for r in range(R):
                hh = g * R + r
                cols = slice(r * TQ, (r + 1) * TQ)
                slope_vec = jnp.where(lane // TQ == r, slopes[hh], slope_vec)
                qa_sc[g, 0:HEAD_DIM, cols] = (q_t[hh * HEAD_DIM:(hh + 1) * HEAD_DIM, :] * scale).astype(BF16)
                qa_sc[g, HEAD_DIM:2 * HEAD_DIM, cols] = jnp.where(feat, slopes[hh], 0.0).astype(BF16)
                if sel:
                    qa_sc[g, 2 * HEAD_DIM:3 * HEAD_DIM, cols] = selb_ref[g * HEAD_DIM:(g + 1) * HEAD_DIM, :]
                    qa_sc[g, 3 * HEAD_DIM:4 * HEAD_DIM, cols] = jnp.zeros((HEAD_DIM, TQ), BF16)
            slope_vecs.append(slope_vec)
        else:
            slope_vecs.append(None)
        m_sc[g] = jnp.full((1, RQ), M_INIT, F32)
        acc_sc[g] = jnp.zeros((V_ROWS, RQ), F32)

    def scores(j, slot):
        ks = pl.multiple_of(jnp.asarray(j, jnp.int32) * TK, TK)
        for g in range(G):
            qa_t = qa_sc[g] if slopes is not None else q_ref[g * LANES:(g + 1) * LANES, :]
            s_sc[slot, g] = _dot(k_refs[g][pl.ds(ks, TK), :], qa_t)

    def tile(j, slot, mask, live=None):
        ks = pl.multiple_of(jnp.asarray(j, jnp.int32) * TK, TK)
        for g in range(G):
            st = s_sc[slot, g]
            if mask is not None:
                kpos = ks + lax.broadcasted_iota(jnp.int32, (TK, 1), 0)
                ok = kpos <= qpos if mask == "causal" else kpos > qpos - W
                st = jnp.where(ok, st, MASK_BIAS)
            if live is not None:
                st = jnp.where(live, st, MASK_BIAS)
            m = m_sc[g]
            mx = jnp.max(st, axis=0, keepdims=True)
            if slopes is not None:
                c = slope_vecs[g] * ks.astype(F32)
                mn = jnp.maximum(m, mx + c)
                p = jnp.exp(st - (mn - c))
            else:
                mn = jnp.maximum(m, mx)
                p = jnp.exp(st - mn)
            acc_sc[g] = jnp.exp(m - mn) * acc_sc[g] + _dot(v_refs[g][:, pl.ds(ks, TK)], p.astype(BF16))
            m_sc[g] = mn

    if W:
        n_st = W // TK + 1

        def banded(check):
            js = [hi - n_st + t for t in range(n_st)]
            jc = [jnp.maximum(j, 0) for j in js] if check else js
            scores(jc[0], 0)
            for t in range(n_st):
                if t + 1 < n_st:
                    scores(jc[t + 1], (t + 1) & 1)
                mask = "causal" if t == n_st - 1 else ("window" if t == 0 else None)
                tile(jc[t], t & 1, mask, live=(js[t] >= 0) if check and t < n_st - 1 else None)

        @pl.when(hi >= n_st)
        def _():
            banded(False)

        @pl.when(hi < n_st)
        def _():
            banded(True)
    else:
        n = hi
        n_pair = (n - 1) // 2

        def pair(p, carry):
            t = 2 * p
            scores(t + 1, 1)
            tile(t, 0, None)
            scores(t + 2, 0)
            tile(t + 1, 1, None)
            return carry

        scores(0, 0)
        lax.fori_loop(0, n_pair, pair, 0)

        @pl.when(2 * n_pair == n - 1)
        def _():
            tile(n - 1, 0, "causal")

        @pl.when(2 * n_pair == n - 2)
        def _():
            scores(n - 1, 1)
            tile(n - 2, 0, None)
            tile(n - 1, 1, "causal")

    pieces = []
    for g in range(G):
        acc = acc_sc[g]
        num = acc[0:HEAD_DIM, :]
        den = acc[HEAD_DIM:HEAD_DIM + 1, :]
        if has_sink:
            sink_vec = jnp.zeros((1, RQ), F32)
            for r in range(R):
                sink_vec = jnp.where(lane // TQ == r, sink_ref[g * R + r], sink_vec)
            m_true = m_sc[g] - slope_vecs[g] * qpos.astype(F32)
            mf = jnp.maximum(m_true, sink_vec)
            a = jnp.exp(m_true - mf)
            num = num * a
            den = den * a + jnp.exp(sink_vec - mf)
        o_t = num * (1.0 / den)
        for r in range(R):
            piece = o_t[:, r * TQ:(r + 1) * TQ]
            if gate_ref is not None:
                c = 3 * (g * R + r) + gate_branch
                piece = piece * gate_t[c:c + 1, :]
            pieces.append(piece)
    o_ref[...] = jnp.concatenate(pieces, axis=0).T.astype(o_ref.dtype)


def _flash_attn(q, k, vt, B, S, *, name, TQ, TK, G, R, KD, q_col, q_width, k_col, vt_slab, W=0, slopes=None, scale=1.0,
                selb=None, sinks=None, gate=None, gate_branch=None):
    T = B * S
    nq = S // TQ
    RQ = R * TQ
    assert TK % TQ == 0 and (not W or (TK == TQ and W % TK == 0)) and S % TK == 0
    OW = G * R * HEAD_DIM
    if slopes is not None:
        in_specs = [pl.BlockSpec((TQ, q_width), lambda b, i: (b * nq + i, q_col // q_width))]
    else:
        in_specs = [pl.BlockSpec((G * KD, TQ), lambda b, i: (0, b * nq + i))]
    args = [q]
    if selb is not None:
        in_specs.append(pl.BlockSpec((LANES, TQ), lambda b, i: (0, b * nq + i)))
        args.append(selb)
    for g in range(G):
        in_specs.append(pl.BlockSpec((S, KD), lambda b, i, g=g: (b, k_col // KD + g)))
        args.append(k)
    for g in range(G):
        in_specs.append(pl.BlockSpec((V_ROWS, S), lambda b, i, g=g: (vt_slab + g, b)))
        args.append(vt)
    if sinks is not None:
        in_specs.append(pl.BlockSpec(memory_space=pltpu.SMEM))
        args.append(sinks.astype(F32))
    if gate_branch is not None:
        in_specs.append(pl.BlockSpec((TQ, LANES), lambda b, i: (b * nq + i, P_GATE // LANES)))
        args.append(gate)
    scratch = []
    if slopes is not None:
        scratch.append(pltpu.VMEM((G, KD, RQ), BF16))
    scratch += [pltpu.VMEM((G, 1, RQ), F32), pltpu.VMEM((G, V_ROWS, RQ), F32), pltpu.VMEM((2, G, TK, RQ), F32)]
    kern = functools.partial(_flash_kernel, TQ=TQ, TK=TK, G=G, R=R, W=W, slopes=slopes, scale=scale,
                             sel=selb is not None, has_sink=sinks is not None, gate_branch=gate_branch)
    return pl.pallas_call(
        kern,
        grid=(B, nq),
        in_specs=in_specs,
        out_specs=pl.BlockSpec((TQ, OW), lambda b, i: (b * nq + i, 0)),
        out_shape=jax.ShapeDtypeStruct((T, OW), BF16),
        scratch_shapes=scratch,
        compiler_params=_params("parallel", "parallel"),
        name=name,
    )(*args)


def _mla_prep_kernel(cq_ref, ckv_ref, kr_ref, krr_ref, cos_ref, sin_ref, gq_ref, gkv_ref,
                     wq_ref, wqr_ref, wk_ref, wvt_ref, vone_ref, q_out, k_out, vt_out):
    cos = cos_ref[...]
    sin = sin_ref[...]
    scale = (B_NOPE + B_ROPE) ** -0.5
    nq = _rms(cq_ref[...].astype(F32), gq_ref[...]).astype(BF16)
    nkv = _rms(ckv_ref[...].astype(F32), gkv_ref[...]).astype(BF16)
    q1 = _dot_nt(wq_ref[...], nq)
    q2 = _dot_nt(wqr_ref[...], nq)
    kk = _dot(nkv, wk_ref[...])
    krope = kr_ref[...].astype(F32) * cos + krr_ref[...].astype(F32) * sin
    cos_t = cos.T
    sin_t = sin.T
    for h in range(B_HEADS):
        sl = slice(h * LANES, (h + 1) * LANES)
        q_out[sl, :] = ((q1[sl, :] * cos_t + q2[sl, :] * sin_t) * scale).astype(BF16)
        k_out[:, sl] = (kk[:, sl] + krope).astype(BF16)
    vt_out[...] = (_dot_nt(wvt_ref[...], nkv) + vone_ref[...]).astype(BF16)


def _build_mla_weights(w_uq, w_ukv):
    R1, R2 = w_uq.shape[0], w_ukv.shape[0]
    half = B_ROPE // 2
    hq = B_NOPE + B_ROPE
    wq, wqr, wk, wvt = [], [], [], []
    for h in range(B_HEADS):
        nope = w_uq[:, h * hq:h * hq + B_NOPE]
        rope = w_uq[:, h * hq + B_NOPE:(h + 1) * hq]
        rot = jnp.concatenate([-rope[:, half:], rope[:, :half]], axis=1)
        pad = jnp.zeros((R1, LANES - hq), w_uq.dtype)
        wq += [nope, rope, pad]
        wqr += [jnp.zeros((R1, B_NOPE), w_uq.dtype), rot, pad]
        wk += [w_ukv[:, h * 128:h * 128 + B_NOPE], jnp.zeros((R2, LANES - B_NOPE), w_ukv.dtype)]
        wvt += [w_ukv[:, h * 128 + B_NOPE:(h + 1) * 128].T, jnp.zeros((V_ROWS - HEAD_DIM, R2), w_ukv.dtype)]
    cat = lambda xs: jnp.concatenate(xs, axis=1).astype(BF16)
    return cat(wq).T, cat(wqr).T, cat(wk), jnp.concatenate(wvt, axis=0).astype(BF16)


def _mla_prep(P, cos, sin, g_cq, g_ckv, w_uq, w_ukv):
    T = P.shape[0]
    TM = TM_PROJ
    wq, wqr, wk, wvt = _build_mla_weights(w_uq, w_ukv)
    QW = B_HEADS * LANES
    VR = B_HEADS * V_ROWS
    vone = np.zeros((VR, 1), np.float32)
    vone[HEAD_DIM::V_ROWS] = 1.0
    full = lambda shp: pl.BlockSpec(shp, lambda i: (0, 0))
    return pl.pallas_call(
        _mla_prep_kernel,
        grid=(T // TM,),
        in_specs=[pl.BlockSpec((TM, B_Q_RANK), lambda i: (i, P_BCQ // B_Q_RANK)),
                  pl.BlockSpec((TM, B_KV_RANK), lambda i: (i, P_BCKV // B_KV_RANK)),
                  pl.BlockSpec((TM, LANES), lambda i: (i, P_KR // LANES)),
                  pl.BlockSpec((TM, LANES), lambda i: (i, P_KRROT // LANES)),
                  pl.BlockSpec((TM, LANES), lambda i: (i, 0)),
                  pl.BlockSpec((TM, LANES), lambda i: (i, 0)),
                  full((1, B_Q_RANK)), full((1, B_KV_RANK)),
                  full((QW, B_Q_RANK)), full((QW, B_Q_RANK)), full((B_KV_RANK, QW)), full((VR, B_KV_RANK)),
                  full((VR, 1))],
        out_specs=[pl.BlockSpec((QW, TM), lambda i: (0, i)),
                   pl.BlockSpec((TM, QW), lambda i: (i, 0)),
                   pl.BlockSpec((VR, TM), lambda i: (0, i))],
        out_shape=[jax.ShapeDtypeStruct((QW, T), BF16), jax.ShapeDtypeStruct((T, QW), BF16),
                   jax.ShapeDtypeStruct((VR, T), BF16)],
        compiler_params=_params("parallel"),
        name="mla_prep",
    )(P, P, P, P, cos, sin, g_cq.reshape(1, -1), g_ckv.reshape(1, -1), wq, wqr, wk, wvt, jnp.asarray(vone))


def _gelu_tanh(x):
    return 0.5 * x * (1.0 + jnp.tanh(np.float32(np.sqrt(2.0 / np.pi)) * (x + 0.044715 * (x * x * x))))


def _compress_kernel(xk_ref, xv_ref, pek_ref, pev_ref, w1k_ref, w1v_ref, w2k_ref, w2v_ref, kc_ref, vc_ref, *, NC):
    for x_ref, pe_ref, w1_ref, w2_ref, o_ref in ((xk_ref, pek_ref, w1k_ref, w2k_ref, kc_ref),
                                                  (xv_ref, pev_ref, w1v_ref, w2v_ref, vc_ref)):
        x = x_ref[...].astype(F32)
        a = _dot((x + pe_ref[0:1, :]).astype(BF16), w1_ref[0])
        b = _dot((x + pe_ref[1:2, :]).astype(BF16), w1_ref[1])
        hid = a + pltpu.roll(b, NC - 1, 0)
        o = _dot(_gelu_tanh(hid).astype(BF16), w2_ref[...])
        n = lax.broadcasted_iota(jnp.int32, o.shape, 0)
        o_ref[...] = jnp.where(n < NC - 1, o, 0.0).astype(o_ref.dtype)


def _build_compress_weights(w1, w2, pe):
    half = C_CMP_BLOCK // 2
    G = C_KV_HEADS
    eye = jnp.eye(G, dtype=w1.dtype)
    w1s = [jnp.einsum('lde,hg->lhdge', w1[c * half:(c + 1) * half], eye).reshape(half * G * HEAD_DIM, G * C_CMP_HIDDEN)
           for c in range(2)]
    w1f = jnp.stack(w1s).astype(BF16)
    w2f = jnp.einsum('ed,hg->hegd', w2, eye).reshape(G * C_CMP_HIDDEN, G * HEAD_DIM).astype(BF16)
    pes = [jnp.broadcast_to(pe[c * half:(c + 1) * half, None, :], (half, G, HEAD_DIM)).reshape(1, -1) for c in range(2)]
    pef = jnp.concatenate(pes, axis=0).astype(F32)
    return w1f, w2f, pef


def _compress(P, B, S, w1k, w2k, pek, w1v, w2v, pev):
    NC = S // C_CMP_STRIDE
    CW = C_CMP_STRIDE * C_KV_HEADS * HEAD_DIM
    xk = P[:, P_CKC:P_CKC + LANES].reshape(B * NC, CW)
    xv = P[:, P_CVC:P_CVC + LANES].reshape(B * NC, CW)
    w1kf, w2kf, pekf = _build_compress_weights(w1k, w2k, pek)
    w1vf, w2vf, pevf = _build_compress_weights(w1v, w2v, pev)
    xspec = pl.BlockSpec((NC, CW), lambda b: (b, 0))
    full2 = lambda a: pl.BlockSpec(a.shape, lambda b: (0, 0))
    full3 = lambda a: pl.BlockSpec(a.shape, lambda b: (0, 0, 0))
    ospec = pl.BlockSpec((NC, LANES), lambda b: (b, 0))
    return pl.pallas_call(
        functools.partial(_compress_kernel, NC=NC),
        grid=(B,),
        in_specs=[xspec, xspec, full2(pekf), full2(pevf), full3(w1kf), full3(w1vf), full2(w2kf), full2(w2vf)],
        out_specs=[ospec, ospec],
        out_shape=[jax.ShapeDtypeStruct((B * NC, LANES), BF16)] * 2,
        compiler_params=_params("parallel"),
        name="nsa_compress",
    )(xk, xv, pekf, pevf, w1kf, w1vf, w2kf, w2vf)


def _cmp_sel_kernel(q_ref, kc_ref, vc_ref, ovt_ref, gate_ref, ocmp_ref, sel_ref, *, TQ, NC, NSEL, NTOP, slopes):
    i = pl.program_id(1)
    t0 = i * TQ
    R = C_HEADS // C_KV_HEADS
    scale = HEAD_DIM ** -0.5
    gate = _sigmoid(gate_ref[...].astype(F32))
    last = C_CMP_BLOCK - 1
    trow = t0 + lax.broadcasted_iota(jnp.int32, (TQ, NC), 0)
    ncol = lax.broadcasted_iota(jnp.int32, (TQ, NC), 1)
    cdist = trow - (ncol * C_CMP_STRIDE + last)
    cvalid = (cdist >= 0) & (ncol < NC - 1)
    cdf = cdist.astype(F32)
    hasc = ((t0 + lax.broadcasted_iota(jnp.int32, (TQ, 1), 0)) >= last).astype(F32)
    tlane = t0 + lax.broadcasted_iota(jnp.int32, (NC, TQ), 1)
    nrow = lax.broadcasted_iota(jnp.int32, (NC, TQ), 0)
    cdist_t = tlane - (nrow * C_CMP_STRIDE + last)
    cvalid_t = (cdist_t >= 0) & (nrow < NC - 1)
    cdf_t = cdist_t.astype(F32)
    hasc_t = ((t0 + lax.broadcasted_iota(jnp.int32, (1, TQ), 1)) >= last).astype(F32)
    jrow = lax.broadcasted_iota(jnp.int32, (NSEL, TQ), 0)
    tl = t0 + lax.broadcasted_iota(jnp.int32, (NSEL, TQ), 1)
    avail = jrow * C_SEL_BLOCK <= tl
    cur = tl // C_SEL_BLOCK
    forced = (jrow == 0) | (jrow == cur) | (jrow == cur - 1)
    sel_parts = []
    for g in range(C_KV_HEADS):
        kc = kc_ref[:, g * HEAD_DIM:(g + 1) * HEAD_DIM]
        vc = vc_ref[:, g * HEAD_DIM:(g + 1) * HEAD_DIM]
        imp_t = jnp.zeros((LANES, TQ), F32)
        for r in range(R):
            hh = g * R + r
            q = q_ref[:, hh * HEAD_DIM:(hh + 1) * HEAD_DIM]
            s = _dot_nt(q, kc) * scale - slopes[hh] * cdf
            s = jnp.where(cvalid, s, NEG_INF)
            e = jnp.exp(s - jnp.max(s, axis=-1, keepdims=True))
            l = jnp.sum(e, axis=-1, keepdims=True)
            o = _dot(e.astype(BF16), vc) / l * (hasc * gate[:, 3 * hh:3 * hh + 1])
            ocmp_ref[:, hh * HEAD_DIM:(hh + 1) * HEAD_DIM] = o.astype(ocmp_ref.dtype)
            st = _dot_nt(kc, q) * scale - slopes[hh] * cdf_t
            st = jnp.where(cvalid_t, st, NEG_INF)
            et = jnp.exp(st - jnp.max(st, axis=0, keepdims=True))
            pt = et / jnp.sum(et, axis=0, keepdims=True) * hasc_t
            imp_t = imp_t + _dot(ovt_ref[...], pt.astype(BF16))
        v = jnp.where(avail, imp_t[0:NSEL, :], NEG_INF)
        v = jnp.where(forced, -NEG_INF, v)
        rank = jnp.zeros((NSEL, TQ), F32)
        for ii in range(NSEL):
            ri = v[ii:ii + 1, :]
            rank = rank + jnp.where(jrow > ii, jnp.where(ri >= v, 1.0, 0.0), jnp.where(ri > v, 1.0, 0.0))
        sel_parts.append(jnp.where(rank < NTOP, 0.0, MASK_BIAS))
        if NSEL < HEAD_DIM:
            sel_parts.append(jnp.zeros((HEAD_DIM - NSEL, TQ), F32))
    sel_ref[...] = jnp.concatenate(sel_parts, axis=0).astype(sel_ref.dtype)


def _overlap_t(S):
    n_cmp = (S - C_CMP_BLOCK) // C_CMP_STRIDE + 1
    n_sel = S // C_SEL_BLOCK
    NC = S // C_CMP_STRIDE
    cstart = np.arange(n_cmp) * C_CMP_STRIDE
    cend = cstart + C_CMP_BLOCK - 1
    sstart = np.arange(n_sel) * C_SEL_BLOCK
    send = sstart + C_SEL_BLOCK - 1
    ov = np.clip(np.minimum(cend[:, None], send[None, :]) - np.maximum(cstart[:, None], sstart[None, :]) + 1,
                 0, None).astype(np.float32) / C_CMP_STRIDE
    ovt = np.zeros((LANES, NC), np.float32)
    ovt[:n_sel, :n_cmp] = ov.T
    return jnp.asarray(ovt, dtype=BF16)


def _cmp_sel(P, kc, vc, B, S, slopes):
    T = B * S
    TQ = TQ_ATT
    nq = S // TQ
    NC = S // C_CMP_STRIDE
    NSEL = S // C_SEL_BLOCK
    assert NSEL <= HEAD_DIM
    QW = C_HEADS * HEAD_DIM
    kern = functools.partial(_cmp_sel_kernel, TQ=TQ, NC=NC, NSEL=NSEL, NTOP=min(C_SEL_TOP, NSEL), slopes=slopes)
    return pl.pallas_call(
        kern,
        grid=(B, nq),
        in_specs=[pl.BlockSpec((TQ, QW), lambda b, i: (b * nq + i, P_CQ // QW)),
                  pl.BlockSpec((NC, LANES), lambda b, i: (b, 0)),
                  pl.BlockSpec((NC, LANES), lambda b, i: (b, 0)),
                  pl.BlockSpec((LANES, NC), lambda b, i: (0, 0)),
                  pl.BlockSpec((TQ, LANES), lambda b, i: (b * nq + i, P_GATE // LANES))],
        out_specs=[pl.BlockSpec((TQ, QW), lambda b, i: (b * nq + i, 0)),
                   pl.BlockSpec((LANES, TQ), lambda b, i: (0, b * nq + i))],
        out_shape=[jax.ShapeDtypeStruct((T, QW), BF16), jax.ShapeDtypeStruct((LANES, T), BF16)],
        compiler_params=_params("parallel", "parallel"),
        name="nsa_cmp_select",
    )(P, kc, vc, _overlap_t(S), P)


def _outproj_kernel(x_ref, oa_ref, ob_ref, oc1_ref, oc2_ref, oc3_ref, ga_ref, gb_ref, gc_ref, wo_ref, gp_ref, o_ref):
    oa = _rms(oa_ref[...].astype(F32), ga_ref[...]).astype(BF16)
    ob = _rms(ob_ref[...].astype(F32), gb_ref[...]).astype(BF16)
    oc = oc1_ref[...].astype(F32) + oc2_ref[...].astype(F32) + oc3_ref[...].astype(F32)
    oc = _rms(oc, gc_ref[...]).astype(BF16)
    m = _dot(oa, wo_ref[0:256, :]) + _dot(ob, wo_ref[256:512, :]) + _dot(oc, wo_ref[512:1024, :])
    o_ref[...] = x_ref[...] + _rms(m, gp_ref[...])


def _outproj(x, o_a, o_b, o_cmp, o_slc, o_win, g_oa, g_ob, g_oc, w_o, g_post):
    T = x.shape[0]
    TM = TM_PROJ
    row = lambda w: pl.BlockSpec((TM, w), lambda i: (i, 0))
    full = lambda r, c: pl.BlockSpec((r, c), lambda i: (0, 0))
    return pl.pallas_call(
        _outproj_kernel,
        grid=(T // TM,),
        in_specs=[row(D_MODEL), row(256), row(256), row(512), row(512), row(512),
                  full(1, 256), full(1, 256), full(1, 512), full(D_MODEL, D_MODEL), full(1, D_MODEL)],
        out_specs=row(D_MODEL),
        out_shape=jax.ShapeDtypeStruct((T, D_MODEL), F32),
        compiler_params=_params("parallel"),
        name="outproj",
    )(x, o_a, o_b, o_cmp, o_slc, o_win, g_oa.reshape(1, -1), g_ob.reshape(1, -1), g_oc.reshape(1, -1),
      w_o.astype(BF16), g_post.reshape(1, -1))


def _swiglu_step(h, wg_ref, wu_ref, wd_ref, acc_ref):
    a = _dot(h, wg_ref[...])
    b = _dot(h, wu_ref[...])
    z = (a * _sigmoid(a) * b).astype(BF16)
    acc_ref[...] += _dot(z, wd_ref[...])


def _ffn_kernel(x_ref, gpre_ref, wg_ref, wu_ref, wd_ref, gpost_ref, o_ref, h_sc, acc_sc, *, nf):
    f = pl.program_id(1)

    @pl.when(f == 0)
    def _():
        h_sc[...] = _rms(x_ref[...], gpre_ref[...]).astype(BF16)
        acc_sc[...] = jnp.zeros_like(acc_sc)

    _swiglu_step(h_sc[...], wg_ref, wu_ref, wd_ref, acc_sc)

    @pl.when(f == nf - 1)
    def _():
        o_ref[...] = x_ref[...] + _rms(acc_sc[...], gpost_ref[...])


def _ffn(x, g_pre, wg, wu, wd, g_post):
    T = x.shape[0]
    TM, TF = TM_FFN, TF_FFN
    FF = wg.shape[1]
    nf = FF // TF
    return pl.pallas_call(
        functools.partial(_ffn_kernel, nf=nf),
        grid=(T // TM, nf),
        in_specs=[pl.BlockSpec((TM, D_MODEL), lambda i, f: (i, 0)),
                  pl.BlockSpec((1, D_MODEL), lambda i, f: (0, 0)),
                  pl.BlockSpec((D_MODEL, TF), lambda i, f: (0, f)),
                  pl.BlockSpec((D_MODEL, TF), lambda i, f: (0, f)),
                  pl.BlockSpec((TF, D_MODEL), lambda i, f: (f, 0)),
                  pl.BlockSpec((1, D_MODEL), lambda i, f: (0, 0))],
        out_specs=pl.BlockSpec((TM, D_MODEL), lambda i, f: (i, 0)),
        out_shape=jax.ShapeDtypeStruct((T, D_MODEL), F32),
        scratch_shapes=[pltpu.VMEM((TM, D_MODEL), BF16), pltpu.VMEM((TM, D_MODEL), F32)],
        compiler_params=_params("parallel", "arbitrary"),
        name="dense_ffn",
    )(x, g_pre.reshape(1, -1), wg.astype(BF16), wu.astype(BF16), wd.astype(BF16), g_post.reshape(1, -1))


def _expert_kernel(te_ref, nt_ref, x_ref, wg_ref, wu_ref, wd_ref, o_ref, acc_sc, *, nf):
    j = pl.program_id(0)
    f = pl.program_id(1)
    live = j < nt_ref[0]

    @pl.when(f == 0)
    def _():
        acc_sc[...] = jnp.zeros_like(acc_sc)

    @pl.when(live)
    def _():
        _swiglu_step(x_ref[...], wg_ref.at[0], wu_ref.at[0], wd_ref.at[0], acc_sc)

    @pl.when(f == nf - 1)
    def _():
        o_ref[...] = acc_sc[...].astype(o_ref.dtype)


def _experts(xs, tile_expert, n_tiles, wg, wu, wd):
    NT = xs.shape[0] // TM_MOE
    TM, TF = TM_MOE, TF_FFN
    FF = wg.shape[2]
    nf = FF // TF
    fidx = lambda j, f, nt: jnp.where(j < nt[0], f, nf - 1)
    grid_spec = pltpu.PrefetchScalarGridSpec(
        num_scalar_prefetch=2,
        grid=(NT, nf),
        in_specs=[pl.BlockSpec((TM, D_MODEL), lambda j, f, te, nt: (j, 0)),
                  pl.BlockSpec((1, D_MODEL, TF), lambda j, f, te, nt: (te[j], 0, fidx(j, f, nt))),
                  pl.BlockSpec((1, D_MODEL, TF), lambda j, f, te, nt: (te[j], 0, fidx(j, f, nt))),
                  pl.BlockSpec((1, TF, D_MODEL), lambda j, f, te, nt: (te[j], fidx(j, f, nt), 0))],
        out_specs=pl.BlockSpec((TM, D_MODEL), lambda j, f, te, nt: (j, 0)),
        scratch_shapes=[pltpu.VMEM((TM, D_MODEL), F32)])
    return pl.pallas_call(
        functools.partial(_expert_kernel, nf=nf),
        grid_spec=grid_spec,
        out_shape=jax.ShapeDtypeStruct(xs.shape, BF16),
        compiler_params=_params("parallel", "arbitrary"),
        name="moe_experts",
    )(tile_expert, n_tiles, xs, wg, wu, wd)


def _router_kernel(x_ref, g_ref, wr_ref, h_out, meta_out, metat_out, start_out, cnt_out, run_sc, *, TS):
    s = pl.program_id(0)

    @pl.when(s == 0)
    def _():
        run_sc[...] = jnp.zeros_like(run_sc)

    h = _rms(x_ref[...], g_ref[...])
    hb = h.astype(BF16)
    h_out[...] = hb
    hl = (h - hb.astype(F32)).astype(BF16)
    w = wr_ref[...]
    wh = w.astype(BF16)
    wl = (w - wh.astype(F32)).astype(BF16)
    logits = _dot(hb, wh) + _dot(hl, wh) + _dot(hb, wl)
    lane = lax.broadcasted_iota(jnp.int32, (TS, LANES), 1).astype(F32)
    logits = jnp.where(lane < N_EXPERTS, logits, NEG_INF)
    m1 = jnp.max(logits, axis=-1, keepdims=True)
    i1 = jnp.min(jnp.where(logits == m1, lane, float(LANES)), axis=-1, keepdims=True)
    rest = jnp.where(lane == i1, NEG_INF, logits)
    m2 = jnp.max(rest, axis=-1, keepdims=True)
    i2 = jnp.min(jnp.where(rest == m2, lane, float(LANES)), axis=-1, keepdims=True)
    ex = jnp.exp(m2 - m1)
    w1 = 1.0 / (1.0 + ex)
    w2 = ex / (1.0 + ex)
    oh1 = lane == i1
    oh2 = lane == i2
    oh = jnp.where(oh1, 1.0, 0.0) + jnp.where(oh2, 1.0, 0.0)
    tri = jnp.where(lax.broadcasted_iota(jnp.int32, (TS, TS), 0) > lax.broadcasted_iota(jnp.int32, (TS, TS), 1),
                    1.0, 0.0).astype(BF16)
    run = run_sc[...]
    before = _dot(tri, oh.astype(BF16)) + run
    r1 = jnp.sum(jnp.where(oh1, before, 0.0), axis=-1, keepdims=True)
    r2 = jnp.sum(jnp.where(oh2, before, 0.0), axis=-1, keepdims=True)
    start_out[0] = run
    run = run + jnp.sum(oh, axis=0, keepdims=True)
    run_sc[...] = run
    cnt_out[...] = run
    meta = jnp.where(lane == 0, i1, jnp.where(lane == 1, i2, jnp.where(lane == 2, r1, jnp.where(
        lane == 3, r2, jnp.where(lane == 4, w1, jnp.where(lane == 5, w2, 0.0))))))
    meta_out[...] = meta
    metat_out[...] = meta.T[0:8, :]


def _router(x, g, w_router):
    T = x.shape[0]
    TS = TS_MOE
    nt = T // TS
    wr = jnp.zeros((D_MODEL, LANES), F32).at[:, :N_EXPERTS].set(w_router.astype(F32))
    return pl.pallas_call(
        functools.partial(_router_kernel, TS=TS),
        grid=(nt,),
        in_specs=[pl.BlockSpec((TS, D_MODEL), lambda s: (s, 0)),
                  pl.BlockSpec((1, D_MODEL), lambda s: (0, 0)),
                  pl.BlockSpec((D_MODEL, LANES), lambda s: (0, 0))],
        out_specs=[pl.BlockSpec((TS, D_MODEL), lambda s: (s, 0)),
                   pl.BlockSpec((TS, LANES), lambda s: (s, 0)),
                   pl.BlockSpec((8, TS), lambda s: (0, s)),
                   pl.BlockSpec((1, 1, LANES), lambda s: (s, 0, 0)),
                   pl.BlockSpec((1, LANES), lambda s: (0, 0))],
        out_shape=[jax.ShapeDtypeStruct((T, D_MODEL), BF16),
                   jax.ShapeDtypeStruct((T, LANES), F32),
                   jax.ShapeDtypeStruct((8, T), F32),
                   jax.ShapeDtypeStruct((nt, 1, LANES), F32),
                   jax.ShapeDtypeStruct((1, LANES), F32)],
        scratch_shapes=[pltpu.VMEM((1, LANES), F32)],
        compiler_params=_params("arbitrary"),
        name="moe_router",
    )(x, g.reshape(1, -1), wr)


def _gather_kernel(te_ref, slo_ref, shi_ref, base_ref, metat_ref, h_ref, xs_out, acc_sc, *, TM, TS):
    j = pl.program_id(0)
    e = te_ref[j]
    ef = e.astype(F32)
    want = (j * TM - base_ref[e] + lax.broadcasted_iota(jnp.int32, (TM, 1), 0)).astype(F32)
    acc_sc[...] = jnp.zeros_like(acc_sc)

    def body(s, _):
        off = pl.multiple_of(s * TS, TS)
        e1 = metat_ref[0:1, pl.ds(off, TS)]
        e2 = metat_ref[1:2, pl.ds(off, TS)]
        r1 = metat_ref[2:3, pl.ds(off, TS)]
        r2 = metat_ref[3:4, pl.ds(off, TS)]
        key = jnp.where(e1 == ef, r1, jnp.where(e2 == ef, r2, -1.0))
        sel = jnp.where(key == want, 1.0, 0.0).astype(BF16)
        acc_sc[...] += _dot(sel, h_ref[pl.ds(off, TS), :])
        return 0

    lax.fori_loop(slo_ref[j], shi_ref[j] + 1, body, 0)
    xs_out[...] = acc_sc[...].astype(xs_out.dtype)


def _gather(hb, metat, tile_expert, s_lo, s_hi, base, NT):
    TM, TS = TM_MOE, TS_MOE
    assert hb.shape[0] % TS == 0
    grid_spec = pltpu.PrefetchScalarGridSpec(
        num_scalar_prefetch=4,
        grid=(NT,),
        in_specs=[pl.BlockSpec(memory_space=pltpu.VMEM), pl.BlockSpec(memory_space=pltpu.VMEM)],
        out_specs=pl.BlockSpec((TM, D_MODEL), lambda j, *_: (j, 0)),
        scratch_shapes=[pltpu.VMEM((TM, D_MODEL), F32)])
    return pl.pallas_call(
        functools.partial(_gather_kernel, TM=TM, TS=TS),
        grid_spec=grid_spec,
        out_shape=jax.ShapeDtypeStruct((NT * TM, D_MODEL), BF16),
        compiler_params=_params("arbitrary"),
        name="moe_gather",
    )(tile_expert, s_lo, s_hi, base, metat, hb)


def _combine_kernel(blk_ref, nch_ref, base_ref, meta_ref, x_ref, gpost_ref, ys_hbm, o_ref, buf, sem, acc_sc,
                    *, TS, CH, U):
    s = pl.program_id(0)
    n = nch_ref[s]
    meta = meta_ref[...]
    e1, e2 = meta[:, 0:1], meta[:, 1:2]
    w1, w2 = meta[:, 4:5], meta[:, 5:6]
    b1 = jnp.zeros((TS, 1), F32)
    b2 = jnp.zeros((TS, 1), F32)
    for e in range(N_EXPERTS):
        be = base_ref[e].astype(F32)
        b1 = jnp.where(e1 == float(e), be, b1)
        b2 = jnp.where(e2 == float(e), be, b2)
    pos1 = b1 + meta[:, 2:3]
    pos2 = b2 + meta[:, 3:4]

    def chunk_copy(u, slot):
        r0 = pl.multiple_of(blk_ref[s * U + u] * CH, CH)
        return pltpu.make_async_copy(ys_hbm.at[pl.ds(r0, CH), :], buf.at[slot], sem.at[slot])

    acc_sc[...] = jnp.zeros_like(acc_sc)

    @pl.when(n > 0)
    def _():
        chunk_copy(0, 0).start()

    def body(u, _):
        slot = u % 2
        chunk_copy(u, slot).wait()

        @pl.when(u + 1 < n)
        def _():
            chunk_copy(u + 1, 1 - slot).start()

        rows = (blk_ref[s * U + u] * CH + lax.broadcasted_iota(jnp.int32, (1, CH), 1)).astype(F32)
        selw = jnp.where(pos1 == rows, w1, 0.0) + jnp.where(pos2 == rows, w2, 0.0)
        acc_sc[...] += _dot(selw.astype(BF16), buf[slot])
        return 0

    lax.fori_loop(0, n, body, 0)
    o_ref[...] = x_ref[...] + _rms(acc_sc[...], gpost_ref[...])


def _combine(x, meta, ys, chunk_blk, n_chunks, base, g_post):
    T = x.shape[0]
    TS, CH, U = TS_MOE, CH_MOE, U_MOE
    grid_spec = pltpu.PrefetchScalarGridSpec(
        num_scalar_prefetch=3,
        grid=(T // TS,),
        in_specs=[pl.BlockSpec((TS, LANES), lambda s, *_: (s, 0)),
                  pl.BlockSpec((TS, D_MODEL), lambda s, *_: (s, 0)),
                  pl.BlockSpec((1, D_MODEL), lambda s, *_: (0, 0)),
                  pl.BlockSpec(memory_space=pl.ANY)],
        out_specs=pl.BlockSpec((TS, D_MODEL), lambda s, *_: (s, 0)),
        scratch_shapes=[pltpu.VMEM((2, CH, D_MODEL), BF16), pltpu.SemaphoreType.DMA((2,)),
                        pltpu.VMEM((TS, D_MODEL), F32)])
    return pl.pallas_call(
        functools.partial(_combine_kernel, TS=TS, CH=CH, U=U),
        grid_spec=grid_spec,
        out_shape=jax.ShapeDtypeStruct((T, D_MODEL), F32),
        compiler_params=_params("arbitrary"),
        name="moe_combine",
    )(chunk_blk, n_chunks, base, meta, x, g_post.reshape(1, -1), ys)


def _moe(x, g_pre, w_router, w_gate, w_up, w_down, g_post):
    T = x.shape[0]
    TS, TM, CH, U = TS_MOE, TM_MOE, CH_MOE, U_MOE
    nt = T // TS
    NT = (2 * T) // TM + N_EXPERTS
    hb, meta, metat, start, cnt = _router(x, g_pre, w_router)

    i32 = jnp.int32
    counts = cnt[0, :N_EXPERTS].astype(i32)
    start = start[:, 0, :N_EXPERTS].astype(i32)
    start_ext = jnp.concatenate([start, counts[None, :]], axis=0)
    tiles_e = (counts + TM - 1) // TM
    tiles_cum = jnp.cumsum(tiles_e)
    base = (tiles_cum - tiles_e) * TM
    n_tiles = tiles_cum[-1:]
    jj = jnp.arange(NT, dtype=i32)
    tile_expert = jnp.minimum(jnp.sum(jj[:, None] >= tiles_cum[None, :], axis=1), N_EXPERTS - 1).astype(i32)
    live = jj < n_tiles[0]
    a = jj * TM - base[tile_expert]
    st_e = start_ext[:, tile_expert]
    s_lo = jnp.sum(st_e[1:] <= a[None, :], axis=0).astype(i32)
    s_hi = (jnp.sum(st_e[:-1] < (a + TM)[None, :], axis=0) - 1).astype(i32)
    s_lo = jnp.where(live, s_lo, 1)
    s_hi = jnp.where(live, s_hi, 0)

    xs = _gather(hb, metat, tile_expert, s_lo, s_hi, base.astype(i32), NT)
    ys = _experts(xs, tile_expert, n_tiles.astype(i32), w_gate.astype(BF16), w_up.astype(BF16), w_down.astype(BF16))

    lo = base[None, :] + start_ext[:-1]
    hi = base[None, :] + start_ext[1:]
    n_e = jnp.where(hi > lo, (hi - 1) // CH - lo // CH + 1, 0)
    cum = jnp.cumsum(n_e, axis=1)
    uu = jnp.arange(U, dtype=i32)
    e_u = jnp.minimum(jnp.sum(uu[None, :, None] >= cum[:, None, :], axis=2), N_EXPERTS - 1)
    first = jnp.take_along_axis(lo // CH, e_u, axis=1)
    skipped = jnp.take_along_axis(cum - n_e, e_u, axis=1)
    chunk_blk = jnp.clip(first + (uu[None, :] - skipped), 0, (NT * TM) // CH - 1).astype(i32).reshape(-1)
    n_chunks = jnp.minimum(cum[:, -1], U).astype(i32)
    return _combine(x, meta, ys, chunk_blk, n_chunks, base.astype(i32), g_post)


def kernel(x, positions, w_in, a_sinks, g_cq, w_uq, g_ckv, w_ukv, c_w1_k, c_w2_k, c_pe_k, c_w1_v, c_w2_v, c_pe_v,
           g_oa, g_ob, g_oc, w_o, g_pre_mix, g_post_mix, g_pre_ffn, g_post_ffn, ffn_w_gate, ffn_w_up, ffn_w_down,
           moe_router, moe_w_gate, moe_w_up, moe_w_down):
    B, S, _ = x.shape
    T = B * S
    depth = w_in.shape[0]
    a_slopes = _alibi_slopes(A_HEADS)
    c_slopes = _alibi_slopes(C_HEADS)
    xf = x.reshape(T, D_MODEL).astype(F32)
    cos, sin = _rope_tables(positions)
    att_scale = HEAD_DIM ** -0.5
    for l in range(depth):
        w_p, w_vt = _build_w_in(w_in[l])
        P, VT = _inproj(xf, g_pre_mix[l].reshape(1, -1), w_p, w_vt, S)
        o_a = _flash_attn(P, P, VT, B, S, name="swa_attn", TQ=TQ_ATT, TK=TK_BAND, G=A_KV_HEADS,
                          R=A_HEADS // A_KV_HEADS, KD=KD_A, q_col=P_AQ, q_width=A_HEADS * HEAD_DIM, k_col=P_KA,
                          vt_slab=VT_A, W=A_WINDOW, slopes=a_slopes, scale=att_scale, sinks=a_sinks[l])
        qB, kB, vtB = _mla_prep(P, cos, sin, g_cq[l], g_ckv[l], w_uq[l], w_ukv[l])
        o_b = _flash_attn(qB, kB, vtB, B, S, name="mla_attn", TQ=TQ_MLA, TK=TQ_MLA, G=B_HEADS, R=1, KD=LANES,
                          q_col=0, q_width=B_HEADS * LANES, k_col=0, vt_slab=0)
        kc, vc = _compress(P, B, S, c_w1_k[l], c_w2_k[l], c_pe_k[l], c_w1_v[l], c_w2_v[l], c_pe_v[l])
        o_cmp, selb = _cmp_sel(P, kc, vc, B, S, c_slopes)
        c_args = dict(TQ=TQ_ATT, G=C_KV_HEADS, R=C_HEADS // C_KV_HEADS, q_col=P_CQ, q_width=C_HEADS * HEAD_DIM,
                      slopes=c_slopes, scale=att_scale, gate=P)
        o_slc = _flash_attn(P, P, VT, B, S, name="nsa_selected", TK=TK_SEL, KD=KD_SEL, k_col=P_KSEL,
                            vt_slab=VT_SEL, selb=selb, gate_branch=1, **c_args)
        o_win = _flash_attn(P, P, VT, B, S, name="nsa_window", TK=TK_BAND, KD=KD_WIN, k_col=P_KWIN,
                            vt_slab=VT_WIN, W=C_WINDOW, gate_branch=2, **c_args)
        xf = _outproj(xf, o_a, o_b, o_cmp, o_slc, o_win, g_oa[l], g_ob[l], g_oc[l], w_o[l], g_post_mix[l])
        if l % 2 == 0:
            xf = _ffn(xf, g_pre_ffn[l], ffn_w_gate[l // 2], ffn_w_up[l // 2], ffn_w_down[l // 2], g_post_ffn[l])
        else:
            xf = _moe(xf, g_pre_ffn[l], moe_router[l // 2], moe_w_gate[l // 2], moe_w_up[l // 2],
                      moe_w_down[l // 2], g_post_ffn[l])
    return xf.reshape(B, S, D_MODEL)
```

```python
import functools

import numpy as np
import jax
import jax.numpy as jnp
from jax import lax
from jax.experimental import pallas as pl
from jax.experimental.pallas import tpu as pltpu

F32 = jnp.float32
BF16 = jnp.bfloat16

D_MODEL = 1024
HEAD_DIM = 64
NORM_EPS = 1e-6
NEG_INF = -1e30
ROPE_THETA = 10000.0

A_HEADS, A_KV_HEADS, A_WINDOW = 4, 2, 128
B_HEADS, B_Q_RANK, B_KV_RANK, B_NOPE, B_ROPE = 4, 256, 128, 64, 32
C_HEADS, C_KV_HEADS, C_WINDOW = 8, 2, 512
C_CMP_BLOCK, C_CMP_STRIDE, C_CMP_HIDDEN = 32, 16, 128
C_SEL_BLOCK, C_SEL_TOP = 64, 16
N_EXPERTS = 8

LANES = 128
VMEM_LIMIT = 48 * 1024 * 1024

P_CQ, P_AQ, P_BCQ = 0, 512, 768
P_BCKV, P_CKC, P_CVC, P_KR, P_KRROT, P_GATE = 1024, 1152, 1280, 1408, 1536, 1664
P_KSEL, KD_SEL = 1792, 256
P_KWIN, KD_WIN = 2304, 128
P_KA, KD_A = 2560, 128
P_WIDTH = 2816
V_ROWS = 80
VT_SEL, VT_WIN, VT_A = 0, 2, 4
VT_SLABS = 6
MASK_BIAS = -1e30
M_INIT = -1e29

TM_PROJ = 512
TQ_ATT = 128
TK_BAND = 128
TQ_MLA = 256
TK_SEL = 256
TM_FFN = 512
TF_FFN = 1024
TF_MOE = 1792
TS_MOE = 512
TM_MOE = 512
CH_MOE = 128
U_MOE = 24


def _alibi_slopes(n):
    return [float(np.float32(2.0 ** (-8.0 * (i + 1) / n))) for i in range(n)]


def _dot(a, b):
    return jnp.dot(a, b, preferred_element_type=F32)


def _dot_nt(a, b):
    return lax.dot_general(a, b, (((1,), (1,)), ((), ())), preferred_element_type=F32)


def _rms(x, g):
    return x * lax.rsqrt(jnp.mean(x * x, axis=-1, keepdims=True) + NORM_EPS) * g


def _sigmoid(x):
    return 1.0 / (1.0 + jnp.exp(-x))


def _params(*sem):
    return pltpu.CompilerParams(dimension_semantics=sem, vmem_limit_bytes=VMEM_LIMIT)


def _inproj_kernel(x_ref, g_ref, w_ref, wvt_ref, vone_ref, kc_ref, o_ref, vt_ref):
    h = _rms(x_ref[...], g_ref[...]).astype(BF16)
    for n in range(P_WIDTH // 256):
        o_ref[:, n * 256:(n + 1) * 256] = _dot(h, w_ref[:, n * 256:(n + 1) * 256]).astype(BF16)
    for g in range(C_KV_HEADS):
        o_ref[:, P_KSEL + g * KD_SEL + 64:P_KSEL + g * KD_SEL + 192] = kc_ref[:, 0:128]
        o_ref[:, P_KWIN + g * KD_WIN + 64:P_KWIN + (g + 1) * KD_WIN] = kc_ref[:, 128:192]
    for g in range(A_KV_HEADS):
        o_ref[:, P_KA + g * KD_A + 64:P_KA + (g + 1) * KD_A] = kc_ref[:, 128:192]
    vt_ref[...] = (_dot_nt(wvt_ref[...], h) + vone_ref[...]).astype(BF16)


def _key_constants(S):
    s = np.arange(S)
    kc = np.zeros((S, 256), np.float32)
    kc[:, 0] = s % TK_SEL
    kc[s, 64 + s // C_SEL_BLOCK] = 1.0
    kc[:, 128] = s % TK_BAND
    return jnp.asarray(kc, dtype=BF16)


def _inproj(x, g, w, wvt, S):
    T = x.shape[0]
    TM = TM_PROJ
    assert S // C_SEL_BLOCK <= 64 and TK_SEL <= 256 and TK_BAND <= 256
    vone = np.zeros((VT_SLABS * V_ROWS, 1), np.float32)
    vone[HEAD_DIM::V_ROWS] = 1.0
    nblk = S // TM
    return pl.pallas_call(
        _inproj_kernel,
        grid=(T // TM,),
        in_specs=[pl.BlockSpec((TM, D_MODEL), lambda i: (i, 0)),
                  pl.BlockSpec((1, D_MODEL), lambda i: (0, 0)),
                  pl.BlockSpec((D_MODEL, P_WIDTH), lambda i: (0, 0)),
                  pl.BlockSpec((VT_SLABS * V_ROWS, D_MODEL), lambda i: (0, 0)),
                  pl.BlockSpec((VT_SLABS * V_ROWS, 1), lambda i: (0, 0)),
                  pl.BlockSpec((TM, 256), lambda i: (i % nblk, 0))],
        out_specs=[pl.BlockSpec((TM, P_WIDTH), lambda i: (i, 0)),
                   pl.BlockSpec((VT_SLABS * V_ROWS, TM), lambda i: (0, i))],
        out_shape=[jax.ShapeDtypeStruct((T, P_WIDTH), BF16),
                   jax.ShapeDtypeStruct((VT_SLABS * V_ROWS, T), BF16)],
        compiler_params=_params("parallel"),
        name="inproj",
    )(x, g, w, wvt, jnp.asarray(vone), _key_constants(S))


def _build_w_in(w_in):
    cuts = np.cumsum([0, 256, 128, 128, 256, 128, 32, 512, 128, 128, 128, 128, 128, 128, 24])
    seg = [w_in[:, cuts[i]:cuts[i + 1]] for i in range(14)]
    a_q, a_k, a_v, b_cq, b_ckv, b_kr, c_q, c_kc, c_vc, c_ks, c_vs, c_kw, c_vw, c_g = seg
    z = lambda n: jnp.zeros((w_in.shape[0], n), w_in.dtype)
    half = B_ROPE // 2
    kr_rot = jnp.concatenate([-b_kr[:, half:], b_kr[:, :half]], axis=1)
    cols = [c_q, a_q, b_cq, b_ckv, c_kc, c_vc,
            z(B_NOPE), b_kr, z(LANES - B_NOPE - B_ROPE),
            z(B_NOPE), kr_rot, z(LANES - B_NOPE - B_ROPE),
            c_g, z(LANES - 24)]
    for k, kd in ((c_ks, KD_SEL), (c_kw, KD_WIN), (a_k, KD_A)):
        for g in range(2):
            cols += [k[:, g * HEAD_DIM:(g + 1) * HEAD_DIM], z(kd - HEAD_DIM)]
    w = jnp.concatenate(cols, axis=1).astype(BF16)
    rows = []
    for v in (c_vs, c_vw, a_v):
        for g in range(2):
            rows += [v[:, g * HEAD_DIM:(g + 1) * HEAD_DIM].T, jnp.zeros((V_ROWS - HEAD_DIM, w_in.shape[0]), w_in.dtype)]
    return w, jnp.concatenate(rows, axis=0).astype(BF16)


def _rope_tab_kernel(pos_ref, inv_ref, cos_ref, sin_ref):
    ang = pos_ref[...] * inv_ref[...]
    cos_ref[...] = jnp.cos(ang)
    sin_ref[...] = jnp.sin(ang)


def _rope_tables(positions):
    T = positions.size
    inv = (ROPE_THETA ** (-np.arange(0, B_ROPE, 2, dtype=np.float32) / B_ROPE)).astype(np.float32)
    inv128 = np.zeros((1, LANES), np.float32)
    inv128[0, B_NOPE:B_NOPE + B_ROPE // 2] = inv
    inv128[0, B_NOPE + B_ROPE // 2:B_NOPE + B_ROPE] = inv
    pos = positions.reshape(T, 1).astype(F32)
    return pl.pallas_call(
        _rope_tab_kernel,
        grid=(T // TM_PROJ,),
        in_specs=[pl.BlockSpec((TM_PROJ, 1), lambda i: (i, 0)),
                  pl.BlockSpec((1, LANES), lambda i: (0, 0))],
        out_specs=[pl.BlockSpec((TM_PROJ, LANES), lambda i: (i, 0))] * 2,
        out_shape=[jax.ShapeDtypeStruct((T, LANES), F32)] * 2,
        compiler_params=_params("parallel"),
        name="rope_tables",
    )(pos, jnp.asarray(inv128))


def _flash_kernel(*refs, TQ, TK, G, R, W, slopes, scale, sel, has_sink, gate_branch):
    refs = list(refs)
    q_ref = refs.pop(0)
    selb_ref = refs.pop(0) if sel else None
    k_refs = [refs.pop(0) for _ in range(G)]
    v_refs = [refs.pop(0) for _ in range(G)]
    sink_ref = refs.pop(0) if has_sink else None
    gate_ref = refs.pop(0) if gate_branch is not None else None
    o_ref = refs.pop(0)
    qa_sc = refs.pop(0) if slopes is not None else None
    m_sc, acc_sc, s_sc = refs
    i = pl.program_id(1)
    t0 = i * TQ
    RQ = R * TQ
    lane = lax.broadcasted_iota(jnp.int32, (1, RQ), 1)
    qpos = t0 + (lane & (TQ - 1))
    hi = (t0 + TQ + TK - 1) // TK
    lo = jnp.maximum(t0 - W, 0) // TK if W else 0
    if gate_ref is not None:
        gate_t = _sigmoid(gate_ref[...].astype(F32)).T
    slope_vecs = []
    if slopes is not None:
        q_t = q_ref[...].astype(F32).T
        feat = lax.broadcasted_iota(jnp.int32, (HEAD_DIM, TQ), 0) == 0
    for g in range(G):
        if slopes is not None:
            slope_vec = jnp.zeros((1, RQ), F32)
            for r in range(R):
                hh = g * R + r
                cols = slice(r * TQ, (r + 1) * TQ)
                slope_vec = jnp.where(lane // TQ == r, slopes[hh], slope_vec)
                qa_sc[g, 0:HEAD_DIM, cols] = (q_t[hh * HEAD_DIM:(hh + 1) * HEAD_DIM, :] * scale).astype(BF16)
                qa_sc[g, HEAD_DIM:2 * HEAD_DIM, cols] = jnp.where(feat, slopes[hh], 0.0).astype(BF16)
                if sel:
                    qa_sc[g, 2 * HEAD_DIM:3 * HEAD_DIM, cols] = selb_ref[g * HEAD_DIM:(g + 1) * HEAD_DIM, :]
                    qa_sc[g, 3 * HEAD_DIM:4 * HEAD_DIM, cols] = jnp.zeros((HEAD_DIM, TQ), BF16)
            slope_vecs.append(slope_vec)
        else:
            slope_vecs.append(None)
        m_sc[g] = jnp.full((1, RQ), M_INIT, F32)
        acc_sc[g] = jnp.zeros((V_ROWS, RQ), F32)

    def scores(j, slot):
        ks = pl.multiple_of(jnp.asarray(j, jnp.int32) * TK, TK)
        for g in range(G):
            qa_t = qa_sc[g] if slopes is not None else q_ref[g * LANES:(g + 1) * LANES, :]
            s_sc[slot, g] = _dot(k_refs[g][pl.ds(ks, TK), :], qa_t)

    def tile(j, slot, mask, live=None):
        ks = pl.multiple_of(jnp.asarray(j, jnp.int32) * TK, TK)
        for g in range(G):
            st = s_sc[slot, g]
            if mask is not None:
                kpos = ks + lax.broadcasted_iota(jnp.int32, (TK, 1), 0)
                ok = kpos <= qpos if mask == "causal" else kpos > qpos - W
                st = jnp.where(ok, st, MASK_BIAS)
            if live is not None:
                st = jnp.where(live, st, MASK_BIAS)
            m = m_sc[g]
            mx = jnp.max(st, axis=0, keepdims=True)
            if slopes is not None:
                c = slope_vecs[g] * ks.astype(F32)
                mn = jnp.maximum(m, mx + c)
                p = jnp.exp(st - (mn - c))
            else:
                mn = jnp.maximum(m, mx)
                p = jnp.exp(st - mn)
            acc_sc[g] = jnp.exp(m - mn) * acc_sc[g] + _dot(v_refs[g][:, pl.ds(ks, TK)], p.astype(BF16))
            m_sc[g] = mn

    if W:
        n_st = W // TK + 1

        def banded(check):
            js = [hi - n_st + t for t in range(n_st)]
            jc = [jnp.maximum(j, 0) for j in js] if check else js
            scores(jc[0], 0)
            for t in range(n_st):
                if t + 1 < n_st:
                    scores(jc[t + 1], (t + 1) & 1)
                mask = "causal" if t == n_st - 1 else ("window" if t == 0 else None)
                tile(jc[t], t & 1, mask, live=(js[t] >= 0) if check and t < n_st - 1 else None)

        @pl.when(hi >= n_st)
        def _():
            banded(False)

        @pl.when(hi < n_st)
        def _():
            banded(True)
    else:
        n = hi
        n_pair = (n - 1) // 2

        def pair(p, carry):
            t = 2 * p
            scores(t + 1, 1)
            tile(t, 0, None)
            scores(t + 2, 0)
            tile(t + 1, 1, None)
            return carry

        scores(0, 0)
        lax.fori_loop(0, n_pair, pair, 0)

        @pl.when(2 * n_pair == n - 1)
        def _():
            tile(n - 1, 0, "causal")

        @pl.when(2 * n_pair == n - 2)
        def _():
            scores(n - 1, 1)
            tile(n - 2, 0, None)
            tile(n - 1, 1, "causal")

    pieces = []
    for g in range(G):
        acc = acc_sc[g]
        num = acc[0:HEAD_DIM, :]
        den = acc[HEAD_DIM:HEAD_DIM + 1, :]
        if has_sink:
            sink_vec = jnp.zeros((1, RQ), F32)
            for r in range(R):
                sink_vec = jnp.where(lane // TQ == r, sink_ref[g * R + r], sink_vec)
            m_true = m_sc[g] - slope_vecs[g] * qpos.astype(F32)
            mf = jnp.maximum(m_true, sink_vec)
            a = jnp.exp(m_true - mf)
            num = num * a
            den = den * a + jnp.exp(sink_vec - mf)
        o_t = num * (1.0 / den)
        for r in range(R):
            piece = o_t[:, r * TQ:(r + 1) * TQ]
            if gate_ref is not None:
                c = 3 * (g * R + r) + gate_branch
                piece = piece * gate_t[c:c + 1, :]
            pieces.append(piece)
    o_ref[...] = jnp.concatenate(pieces, axis=0).T.astype(o_ref.dtype)


def _flash_attn(q, k, vt, B, S, *, name, TQ, TK, G, R, KD, q_col, q_width, k_col, vt_slab, W=0, slopes=None, scale=1.0,
                selb=None, sinks=None, gate=None, gate_branch=None):
    T = B * S
    nq = S // TQ
    RQ = R * TQ
    assert TK % TQ == 0 and (not W or (TK == TQ and W % TK == 0)) and S % TK == 0
    OW = G * R * HEAD_DIM
    if slopes is not None:
        in_specs = [pl.BlockSpec((TQ, q_width), lambda b, i: (b * nq + i, q_col // q_width))]
    else:
        in_specs = [pl.BlockSpec((G * KD, TQ), lambda b, i: (0, b * nq + i))]
    args = [q]
    if selb is not None:
        in_specs.append(pl.BlockSpec((LANES, TQ), lambda b, i: (0, b * nq + i)))
        args.append(selb)
    for g in range(G):
        in_specs.append(pl.BlockSpec((S, KD), lambda b, i, g=g: (b, k_col // KD + g)))
        args.append(k)
    for g in range(G):
        in_specs.append(pl.BlockSpec((V_ROWS, S), lambda b, i, g=g: (vt_slab + g, b)))
        args.append(vt)
    if sinks is not None:
        in_specs.append(pl.BlockSpec(memory_space=pltpu.SMEM))
        args.append(sinks.astype(F32))
    if gate_branch is not None:
        in_specs.append(pl.BlockSpec((TQ, LANES), lambda b, i: (b * nq + i, P_GATE // LANES)))
        args.append(gate)
    scratch = []
    if slopes is not None:
        scratch.append(pltpu.VMEM((G, KD, RQ), BF16))
    scratch += [pltpu.VMEM((G, 1, RQ), F32), pltpu.VMEM((G, V_ROWS, RQ), F32), pltpu.VMEM((2, G, TK, RQ), F32)]
    kern = functools.partial(_flash_kernel, TQ=TQ, TK=TK, G=G, R=R, W=W, slopes=slopes, scale=scale,
                             sel=selb is not None, has_sink=sinks is not None, gate_branch=gate_branch)
    return pl.pallas_call(
        kern,
        grid=(B, nq),
        in_specs=in_specs,
        out_specs=pl.BlockSpec((TQ, OW), lambda b, i: (b * nq + i, 0)),
        out_shape=jax.ShapeDtypeStruct((T, OW), BF16),
        scratch_shapes=scratch,
        compiler_params=_params("parallel", "parallel"),
        name=name,
    )(*args)


def _mla_prep_kernel(cq_ref, ckv_ref, kr_ref, krr_ref, cos_ref, sin_ref, gq_ref, gkv_ref,
                     wq_ref, wqr_ref, wk_ref, wvt_ref, vone_ref, q_out, k_out, vt_out):
    cos = cos_ref[...]
    sin = sin_ref[...]
    scale = (B_NOPE + B_ROPE) ** -0.5
    nq = _rms(cq_ref[...].astype(F32), gq_ref[...]).astype(BF16)
    nkv = _rms(ckv_ref[...].astype(F32), gkv_ref[...]).astype(BF16)
    q1 = _dot_nt(wq_ref[...], nq)
    q2 = _dot_nt(wqr_ref[...], nq)
    kk = _dot(nkv, wk_ref[...])
    krope = kr_ref[...].astype(F32) * cos + krr_ref[...].astype(F32) * sin
    cos_t = cos.T
    sin_t = sin.T
    for h in range(B_HEADS):
        sl = slice(h * LANES, (h + 1) * LANES)
        q_out[sl, :] = ((q1[sl, :] * cos_t + q2[sl, :] * sin_t) * scale).astype(BF16)
        k_out[:, sl] = (kk[:, sl] + krope).astype(BF16)
    vt_out[...] = (_dot_nt(wvt_ref[...], nkv) + vone_ref[...]).astype(BF16)


def _build_mla_weights(w_uq, w_ukv):
    R1, R2 = w_uq.shape[0], w_ukv.shape[0]
    half = B_ROPE // 2
    hq = B_NOPE + B_ROPE
    wq, wqr, wk, wvt = [], [], [], []
    for h in range(B_HEADS):
        nope = w_uq[:, h * hq:h * hq + B_NOPE]
        rope = w_uq[:, h * hq + B_NOPE:(h + 1) * hq]
        rot = jnp.concatenate([-rope[:, half:], rope[:, :half]], axis=1)
        pad = jnp.zeros((R1, LANES - hq), w_uq.dtype)
        wq += [nope, rope, pad]
        wqr += [jnp.zeros((R1, B_NOPE), w_uq.dtype), rot, pad]
        wk += [w_ukv[:, h * 128:h * 128 + B_NOPE], jnp.zeros((R2, LANES - B_NOPE), w_ukv.dtype)]
        wvt += [w_ukv[:, h * 128 + B_NOPE:(h + 1) * 128].T, jnp.zeros((V_ROWS - HEAD_DIM, R2), w_ukv.dtype)]
    cat = lambda xs: jnp.concatenate(xs, axis=1).astype(BF16)
    return cat(wq).T, cat(wqr).T, cat(wk), jnp.concatenate(wvt, axis=0).astype(BF16)


def _mla_prep(P, cos, sin, g_cq, g_ckv, w_uq, w_ukv):
    T = P.shape[0]
    TM = TM_PROJ
    wq, wqr, wk, wvt = _build_mla_weights(w_uq, w_ukv)
    QW = B_HEADS * LANES
    VR = B_HEADS * V_ROWS
    vone = np.zeros((VR, 1), np.float32)
    vone[HEAD_DIM::V_ROWS] = 1.0
    full = lambda shp: pl.BlockSpec(shp, lambda i: (0, 0))
    return pl.pallas_call(
        _mla_prep_kernel,
        grid=(T // TM,),
        in_specs=[pl.BlockSpec((TM, B_Q_RANK), lambda i: (i, P_BCQ // B_Q_RANK)),
                  pl.BlockSpec((TM, B_KV_RANK), lambda i: (i, P_BCKV // B_KV_RANK)),
                  pl.BlockSpec((TM, LANES), lambda i: (i, P_KR // LANES)),
                  pl.BlockSpec((TM, LANES), lambda i: (i, P_KRROT // LANES)),
                  pl.BlockSpec((TM, LANES), lambda i: (i, 0)),
                  pl.BlockSpec((TM, LANES), lambda i: (i, 0)),
                  full((1, B_Q_RANK)), full((1, B_KV_RANK)),
                  full((QW, B_Q_RANK)), full((QW, B_Q_RANK)), full((B_KV_RANK, QW)), full((VR, B_KV_RANK)),
                  full((VR, 1))],
        out_specs=[pl.BlockSpec((QW, TM), lambda i: (0, i)),
                   pl.BlockSpec((TM, QW), lambda i: (i, 0)),
                   pl.BlockSpec((VR, TM), lambda i: (0, i))],
        out_shape=[jax.ShapeDtypeStruct((QW, T), BF16), jax.ShapeDtypeStruct((T, QW), BF16),
                   jax.ShapeDtypeStruct((VR, T), BF16)],
        compiler_params=_params("parallel"),
        name="mla_prep",
    )(P, P, P, P, cos, sin, g_cq.reshape(1, -1), g_ckv.reshape(1, -1), wq, wqr, wk, wvt, jnp.asarray(vone))


def _gelu_tanh(x):
    return 0.5 * x * (1.0 + jnp.tanh(np.float32(np.sqrt(2.0 / np.pi)) * (x + 0.044715 * (x * x * x))))


def _compress_kernel(xk_ref, xv_ref, pek_ref, pev_ref, w1k_ref, w1v_ref, w2k_ref, w2v_ref, kc_ref, vct_ref, *, NC):
    for x_ref, pe_ref, w1_ref, w2_ref, is_v in ((xk_ref, pek_ref, w1k_ref, w2k_ref, False),
                                                 (xv_ref, pev_ref, w1v_ref, w2v_ref, True)):
        x = x_ref[...].astype(F32)
        a = _dot((x + pe_ref[0:1, :]).astype(BF16), w1_ref[0])
        b = _dot((x + pe_ref[1:2, :]).astype(BF16), w1_ref[1])
        hid = a + pltpu.roll(b, NC - 1, 0)
        o = _dot(_gelu_tanh(hid).astype(BF16), w2_ref[...])
        n = lax.broadcasted_iota(jnp.int32, o.shape, 0)
        o = jnp.where(n < NC - 1, o, 0.0)
        if is_v:
            vct_ref[...] = o.T.astype(vct_ref.dtype)
        else:
            kc_ref[...] = o.astype(kc_ref.dtype)


def _build_compress_weights(w1, w2, pe):
    half = C_CMP_BLOCK // 2
    G = C_KV_HEADS
    eye = jnp.eye(G, dtype=w1.dtype)
    w1s = [jnp.einsum('lde,hg->lhdge', w1[c * half:(c + 1) * half], eye).reshape(half * G * HEAD_DIM, G * C_CMP_HIDDEN)
           for c in range(2)]
    w1f = jnp.stack(w1s).astype(BF16)
    w2f = jnp.einsum('ed,hg->hegd', w2, eye).reshape(G * C_CMP_HIDDEN, G * HEAD_DIM).astype(BF16)
    pes = [jnp.broadcast_to(pe[c * half:(c + 1) * half, None, :], (half, G, HEAD_DIM)).reshape(1, -1) for c in range(2)]
    pef = jnp.concatenate(pes, axis=0).astype(F32)
    return w1f, w2f, pef


def _compress(P, B, S, w1k, w2k, pek, w1v, w2v, pev):
    NC = S // C_CMP_STRIDE
    CW = C_CMP_STRIDE * C_KV_HEADS * HEAD_DIM
    xk = P[:, P_CKC:P_CKC + LANES].reshape(B * NC, CW)
    xv = P[:, P_CVC:P_CVC + LANES].reshape(B * NC, CW)
    w1kf, w2kf, pekf = _build_compress_weights(w1k, w2k, pek)
    w1vf, w2vf, pevf = _build_compress_weights(w1v, w2v, pev)
    xspec = pl.BlockSpec((NC, CW), lambda b: (b, 0))
    full2 = lambda a: pl.BlockSpec(a.shape, lambda b: (0, 0))
    full3 = lambda a: pl.BlockSpec(a.shape, lambda b: (0, 0, 0))
    return pl.pallas_call(
        functools.partial(_compress_kernel, NC=NC),
        grid=(B,),
        in_specs=[xspec, xspec, full2(pekf), full2(pevf), full3(w1kf), full3(w1vf), full2(w2kf), full2(w2vf)],
        out_specs=[pl.BlockSpec((NC, LANES), lambda b: (b, 0)), pl.BlockSpec((LANES, NC), lambda b: (b, 0))],
        out_shape=[jax.ShapeDtypeStruct((B * NC, LANES), BF16), jax.ShapeDtypeStruct((B * LANES, NC), BF16)],
        compiler_params=_params("parallel"),
        name="nsa_compress",
    )(xk, xv, pekf, pevf, w1kf, w1vf, w2kf, w2vf)


def _cmp_sel_kernel(q_ref, kc_ref, vct_ref, cfeat_ref, ovt_ref, gate_ref, ocmp_ref, sel_ref,
                    *, TQ, NC, NSEL, NTOP, slopes):
    i = pl.program_id(1)
    t0 = i * TQ
    R = C_HEADS // C_KV_HEADS
    scale = HEAD_DIM ** -0.5
    gate_t = _sigmoid(gate_ref[...].astype(F32)).T
    last = C_CMP_BLOCK - 1
    tlane = t0 + lax.broadcasted_iota(jnp.int32, (NC, TQ), 1)
    nrow = lax.broadcasted_iota(jnp.int32, (NC, TQ), 0)
    cvalid = (nrow * C_CMP_STRIDE + last <= tlane) & (nrow < NC - 1)
    hasc = ((t0 + lax.broadcasted_iota(jnp.int32, (1, TQ), 1)) >= last).astype(F32)
    jrow = lax.broadcasted_iota(jnp.int32, (NSEL, TQ), 0)
    tl = t0 + lax.broadcasted_iota(jnp.int32, (NSEL, TQ), 1)
    avail = jrow * C_SEL_BLOCK <= tl
    cur = tl // C_SEL_BLOCK
    forced = (jrow == 0) | (jrow == cur) | (jrow == cur - 1)
    q_t = q_ref[...].astype(F32).T
    frow = lax.broadcasted_iota(jnp.int32, (HEAD_DIM, TQ), 0)
    sub8 = lax.broadcasted_iota(jnp.int32, (8, TQ), 0)
    pieces, sel_parts = [], []
    for g in range(C_KV_HEADS):
        kca = jnp.concatenate([kc_ref[:, g * HEAD_DIM:(g + 1) * HEAD_DIM], cfeat_ref[...]], axis=1)
        vo = jnp.concatenate([vct_ref[g * HEAD_DIM:(g + 1) * HEAD_DIM, :], ovt_ref[...]], axis=0)
        imp = jnp.zeros((HEAD_DIM, TQ), F32)
        for r in range(R):
            hh = g * R + r
            feat = jnp.where(frow == 0, slopes[hh] * 256.0, jnp.where(frow == 1, slopes[hh], 0.0))
            qa = jnp.concatenate([q_t[hh * HEAD_DIM:(hh + 1) * HEAD_DIM, :] * scale, feat], axis=0).astype(BF16)
            st = jnp.where(cvalid, _dot(kca, qa), NEG_INF)
            et = jnp.exp(st - jnp.max(st, axis=0, keepdims=True))
            inv = hasc / jnp.sum(et, axis=0, keepdims=True)
            res = _dot(vo, et.astype(BF16))
            pieces.append(res[0:HEAD_DIM, :] * (inv * gate_t[3 * hh:3 * hh + 1, :]))
            imp = imp + res[HEAD_DIM:2 * HEAD_DIM, :] * inv
        v = jnp.where(avail, imp[0:NSEL, :], NEG_INF)
        v = jnp.where(forced, -NEG_INF, v)
        ranks = [jnp.zeros((8, TQ), F32) for _ in range(NSEL // 8)]
        for ii in range(NSEL):
            ri = v[ii:ii + 1, :]
            for k in range(NSEL // 8):
                vk = v[8 * k:8 * k + 8, :]
                if 8 * k > ii:
                    inc = jnp.where(ri >= vk, 1.0, 0.0)
                elif 8 * k + 7 < ii:
                    inc = jnp.where(ri > vk, 1.0, 0.0)
                else:
                    inc = jnp.where(sub8 + 8 * k > ii, jnp.where(ri >= vk, 1.0, 0.0), jnp.where(ri > vk, 1.0, 0.0))
                ranks[k] = ranks[k] + inc
        rank = jnp.concatenate(ranks, axis=0)
        sel_parts.append(jnp.where(rank < NTOP, 0.0, MASK_BIAS))
        if NSEL < HEAD_DIM:
            sel_parts.append(jnp.zeros((HEAD_DIM - NSEL, TQ), F32))
    sel_ref[...] = jnp.concatenate(sel_parts, axis=0).astype(sel_ref.dtype)
    ocmp_ref[...] = jnp.concatenate(pieces, axis=0).T.astype(ocmp_ref.dtype)


def _overlap_t(S):
    n_cmp = (S - C_CMP_BLOCK) // C_CMP_STRIDE + 1
    n_sel = S // C_SEL_BLOCK
    NC = S // C_CMP_STRIDE
    cstart = np.arange(n_cmp) * C_CMP_STRIDE
    cend = cstart + C_CMP_BLOCK - 1
    sstart = np.arange(n_sel) * C_SEL_BLOCK
    send = sstart + C_SEL_BLOCK - 1
    ov = np.clip(np.minimum(cend[:, None], send[None, :]) - np.maximum(cstart[:, None], sstart[None, :]) + 1,
                 0, None).astype(np.float32) / C_CMP_STRIDE
    ovt = np.zeros((HEAD_DIM, NC), np.float32)
    ovt[:n_sel, :n_cmp] = ov.T
    cend_all = np.arange(NC) * C_CMP_STRIDE + C_CMP_BLOCK - 1
    cfeat = np.zeros((NC, HEAD_DIM), np.float32)
    cfeat[:, 0] = cend_all // 256
    cfeat[:, 1] = cend_all % 256
    return jnp.asarray(ovt, dtype=BF16), jnp.asarray(cfeat, dtype=BF16)


def _cmp_sel(P, kc, vct, B, S, slopes):
    T = B * S
    TQ = TQ_ATT
    nq = S // TQ
    NC = S // C_CMP_STRIDE
    NSEL = S // C_SEL_BLOCK
    assert NSEL <= HEAD_DIM and NSEL % 8 == 0 and S <= 256 * 256
    QW = C_HEADS * HEAD_DIM
    ovt, cfeat = _overlap_t(S)
    kern = functools.partial(_cmp_sel_kernel, TQ=TQ, NC=NC, NSEL=NSEL, NTOP=min(C_SEL_TOP, NSEL), slopes=slopes)
    return pl.pallas_call(
        kern,
        grid=(B, nq),
        in_specs=[pl.BlockSpec((TQ, QW), lambda b, i: (b * nq + i, P_CQ // QW)),
                  pl.BlockSpec((NC, LANES), lambda b, i: (b, 0)),
                  pl.BlockSpec((LANES, NC), lambda b, i: (b, 0)),
                  pl.BlockSpec((NC, HEAD_DIM), lambda b, i: (0, 0)),
                  pl.BlockSpec((HEAD_DIM, NC), lambda b, i: (0, 0)),
                  pl.BlockSpec((TQ, LANES), lambda b, i: (b * nq + i, P_GATE // LANES))],
        out_specs=[pl.BlockSpec((TQ, QW), lambda b, i: (b * nq + i, 0)),
                   pl.BlockSpec((LANES, TQ), lambda b, i: (0, b * nq + i))],
        out_shape=[jax.ShapeDtypeStruct((T, QW), BF16), jax.ShapeDtypeStruct((LANES, T), BF16)],
        compiler_params=_params("parallel", "parallel"),
        name="nsa_cmp_select",
    )(P, kc, vct, cfeat, ovt, P)


def _outproj_kernel(x_ref, oa_ref, ob_ref, oc1_ref, oc2_ref, oc3_ref, ga_ref, gb_ref, gc_ref, wo_ref, gp_ref, o_ref):
    oa = _rms(oa_ref[...].astype(F32), ga_ref[...]).astype(BF16)
    ob = _rms(ob_ref[...].astype(F32), gb_ref[...]).astype(BF16)
    oc = oc1_ref[...].astype(F32) + oc2_ref[...].astype(F32) + oc3_ref[...].astype(F32)
    oc = _rms(oc, gc_ref[...]).astype(BF16)
    m = _dot(oa, wo_ref[0:256, :]) + _dot(ob, wo_ref[256:512, :]) + _dot(oc, wo_ref[512:1024, :])
    o_ref[...] = x_ref[...] + _rms(m, gp_ref[...])


def _outproj(x, o_a, o_b, o_cmp, o_slc, o_win, g_oa, g_ob, g_oc, w_o, g_post):
    T = x.shape[0]
    TM = TM_PROJ
    row = lambda w: pl.BlockSpec((TM, w), lambda i: (i, 0))
    full = lambda r, c: pl.BlockSpec((r, c), lambda i: (0, 0))
    return pl.pallas_call(
        _outproj_kernel,
        grid=(T // TM,),
        in_specs=[row(D_MODEL), row(256), row(256), row(512), row(512), row(512),
                  full(1, 256), full(1, 256), full(1, 512), full(D_MODEL, D_MODEL), full(1, D_MODEL)],
        out_specs=row(D_MODEL),
        out_shape=jax.ShapeDtypeStruct((T, D_MODEL), F32),
        compiler_params=_params("parallel"),
        name="outproj",
    )(x, o_a, o_b, o_cmp, o_slc, o_win, g_oa.reshape(1, -1), g_ob.reshape(1, -1), g_oc.reshape(1, -1),
      w_o.astype(BF16), g_post.reshape(1, -1))


def _swiglu_step(h, wg_ref, wu_ref, wd_ref, acc_ref):
    a = _dot(h, wg_ref[...])
    b = _dot(h, wu_ref[...])
    z = (a * _sigmoid(a) * b).astype(BF16)
    acc_ref[...] += _dot(z, wd_ref[...])


def _ffn_kernel(x_ref, gpre_ref, wg_ref, wu_ref, wd_ref, gpost_ref, o_ref, h_sc, acc_sc, *, nf):
    f = pl.program_id(1)

    @pl.when(f == 0)
    def _():
        h_sc[...] = _rms(x_ref[...], gpre_ref[...]).astype(BF16)
        acc_sc[...] = jnp.zeros_like(acc_sc)

    _swiglu_step(h_sc[...], wg_ref, wu_ref, wd_ref, acc_sc)

    @pl.when(f == nf - 1)
    def _():
        o_ref[...] = x_ref[...] + _rms(acc_sc[...], gpost_ref[...])


def _ffn(x, g_pre, wg, wu, wd, g_post):
    T = x.shape[0]
    TM, TF = TM_FFN, TF_FFN
    FF = wg.shape[1]
    nf = FF // TF
    return pl.pallas_call(
        functools.partial(_ffn_kernel, nf=nf),
        grid=(T // TM, nf),
        in_specs=[pl.BlockSpec((TM, D_MODEL), lambda i, f: (i, 0)),
                  pl.BlockSpec((1, D_MODEL), lambda i, f: (0, 0)),
                  pl.BlockSpec((D_MODEL, TF), lambda i, f: (0, f)),
                  pl.BlockSpec((D_MODEL, TF), lambda i, f: (0, f)),
                  pl.BlockSpec((TF, D_MODEL), lambda i, f: (f, 0)),
                  pl.BlockSpec((1, D_MODEL), lambda i, f: (0, 0))],
        out_specs=pl.BlockSpec((TM, D_MODEL), lambda i, f: (i, 0)),
        out_shape=jax.ShapeDtypeStruct((T, D_MODEL), F32),
        scratch_shapes=[pltpu.VMEM((TM, D_MODEL), BF16), pltpu.VMEM((TM, D_MODEL), F32)],
        compiler_params=_params("parallel", "arbitrary"),
        name="dense_ffn",
    )(x, g_pre.reshape(1, -1), wg.astype(BF16), wu.astype(BF16), wd.astype(BF16), g_post.reshape(1, -1))


def _expert_kernel(te_ref, nt_ref, x_ref, wg_ref, wu_ref, wd_ref, o_ref, acc_sc, *, nf):
    j = pl.program_id(0)
    f = pl.program_id(1)
    live = j < nt_ref[0]

    @pl.when(f == 0)
    def _():
        acc_sc[...] = jnp.zeros_like(acc_sc)

    @pl.when(live)
    def _():
        _swiglu_step(x_ref[...], wg_ref.at[0], wu_ref.at[0], wd_ref.at[0], acc_sc)

    @pl.when(f == nf - 1)
    def _():
        o_ref[...] = acc_sc[...].astype(o_ref.dtype)


def _experts(xs, tile_expert, n_tiles, wg, wu, wd):
    NT = xs.shape[0] // TM_MOE
    TM, TF = TM_MOE, TF_MOE
    assert wg.shape[2] % TF == 0
    FF = wg.shape[2]
    nf = FF // TF
    fidx = lambda j, f, nt: jnp.where(j < nt[0], f, nf - 1)
    grid_spec = pltpu.PrefetchScalarGridSpec(
        num_scalar_prefetch=2,
        grid=(NT, nf),
        in_specs=[pl.BlockSpec((TM, D_MODEL), lambda j, f, te, nt: (j, 0)),
                  pl.BlockSpec((1, D_MODEL, TF), lambda j, f, te, nt: (te[j], 0, fidx(j, f, nt))),
                  pl.BlockSpec((1, D_MODEL, TF), lambda j, f, te, nt: (te[j], 0, fidx(j, f, nt))),
                  pl.BlockSpec((1, TF, D_MODEL), lambda j, f, te, nt: (te[j], fidx(j, f, nt), 0))],
        out_specs=pl.BlockSpec((TM, D_MODEL), lambda j, f, te, nt: (j, 0)),
        scratch_shapes=[pltpu.VMEM((TM, D_MODEL), F32)])
    return pl.pallas_call(
        functools.partial(_expert_kernel, nf=nf),
        grid_spec=grid_spec,
        out_shape=jax.ShapeDtypeStruct(xs.shape, BF16),
        compiler_params=_params("parallel", "arbitrary"),
        name="moe_experts",
    )(tile_expert, n_tiles, xs, wg, wu, wd)


def _router_kernel(x_ref, g_ref, wr_ref, h_out, meta_out, metat_out, start_out, cnt_out, run_sc, *, TS):
    s = pl.program_id(0)

    @pl.when(s == 0)
    def _():
        run_sc[...] = jnp.zeros_like(run_sc)

    h = _rms(x_ref[...], g_ref[...])
    hb = h.astype(BF16)
    h_out[...] = hb
    hl = (h - hb.astype(F32)).astype(BF16)
    w = wr_ref[...]
    wh = w.astype(BF16)
    wl = (w - wh.astype(F32)).astype(BF16)
    logits = _dot(hb, wh) + _dot(hl, wh) + _dot(hb, wl)
    lane = lax.broadcasted_iota(jnp.int32, (TS, LANES), 1).astype(F32)
    logits = jnp.where(lane < N_EXPERTS, logits, NEG_INF)
    m1 = jnp.max(logits, axis=-1, keepdims=True)
    i1 = jnp.min(jnp.where(logits == m1, lane, float(LANES)), axis=-1, keepdims=True)
    rest = jnp.where(lane == i1, NEG_INF, logits)
    m2 = jnp.max(rest, axis=-1, keepdims=True)
    i2 = jnp.min(jnp.where(rest == m2, lane, float(LANES)), axis=-1, keepdims=True)
    ex = jnp.exp(m2 - m1)
    w1 = 1.0 / (1.0 + ex)
    w2 = ex / (1.0 + ex)
    oh1 = lane == i1
    oh2 = lane == i2
    oh = jnp.where(oh1, 1.0, 0.0) + jnp.where(oh2, 1.0, 0.0)
    tri = jnp.where(lax.broadcasted_iota(jnp.int32, (TS, TS), 0) > lax.broadcasted_iota(jnp.int32, (TS, TS), 1),
                    1.0, 0.0).astype(BF16)
    run = run_sc[...]
    before = _dot(tri, oh.astype(BF16)) + run
    r1 = jnp.sum(jnp.where(oh1, before, 0.0), axis=-1, keepdims=True)
    r2 = jnp.sum(jnp.where(oh2, before, 0.0), axis=-1, keepdims=True)
    start_out[0] = run
    run = run + jnp.sum(oh, axis=0, keepdims=True)
    run_sc[...] = run
    cnt_out[...] = run
    meta = jnp.where(lane == 0, i1, jnp.where(lane == 1, i2, jnp.where(lane == 2, r1, jnp.where(
        lane == 3, r2, jnp.where(lane == 4, w1, jnp.where(lane == 5, w2, 0.0))))))
    meta_out[...] = meta
    metat_out[...] = meta.T[0:8, :]


def _router(x, g, w_router):
    T = x.shape[0]
    TS = TS_MOE
    nt = T // TS
    wr = jnp.zeros((D_MODEL, LANES), F32).at[:, :N_EXPERTS].set(w_router.astype(F32))
    return pl.pallas_call(
        functools.partial(_router_kernel, TS=TS),
        grid=(nt,),
        in_specs=[pl.BlockSpec((TS, D_MODEL), lambda s: (s, 0)),
                  pl.BlockSpec((1, D_MODEL), lambda s: (0, 0)),
                  pl.BlockSpec((D_MODEL, LANES), lambda s: (0, 0))],
        out_specs=[pl.BlockSpec((TS, D_MODEL), lambda s: (s, 0)),
                   pl.BlockSpec((TS, LANES), lambda s: (s, 0)),
                   pl.BlockSpec((8, TS), lambda s: (0, s)),
                   pl.BlockSpec((1, 1, LANES), lambda s: (s, 0, 0)),
                   pl.BlockSpec((1, LANES), lambda s: (0, 0))],
        out_shape=[jax.ShapeDtypeStruct((T, D_MODEL), BF16),
                   jax.ShapeDtypeStruct((T, LANES), F32),
                   jax.ShapeDtypeStruct((8, T), F32),
                   jax.ShapeDtypeStruct((nt, 1, LANES), F32),
                   jax.ShapeDtypeStruct((1, LANES), F32)],
        scratch_shapes=[pltpu.VMEM((1, LANES), F32)],
        compiler_params=_params("arbitrary"),
        name="moe_router",
    )(x, g.reshape(1, -1), wr)


def _gather_kernel(te_ref, slo_ref, shi_ref, base_ref, metat_ref, h_ref, xs_out, acc_sc, *, TM, TS, CH):
    j = pl.program_id(0)
    e = te_ref[j]
    ef = e.astype(F32)
    nsub = TM // CH
    for sub in range(nsub):
        first = j * TM + sub * CH - base_ref[e]
        want = (first + lax.broadcasted_iota(jnp.int32, (CH, 1), 0)).astype(F32)
        acc_sc[...] = jnp.zeros_like(acc_sc)

        def body(s, _, want=want):
            off = pl.multiple_of(s * TS, TS)
            e1 = metat_ref[0:1, pl.ds(off, TS)]
            e2 = metat_ref[1:2, pl.ds(off, TS)]
            r1 = metat_ref[2:3, pl.ds(off, TS)]
            r2 = metat_ref[3:4, pl.ds(off, TS)]
            key = jnp.where(e1 == ef, r1, jnp.where(e2 == ef, r2, -1.0))
            sel = jnp.where(key == want, 1.0, 0.0).astype(BF16)
            acc_sc[...] += _dot(sel, h_ref[pl.ds(off, TS), :])
            return 0

        q = j * nsub + sub
        lax.fori_loop(slo_ref[q], shi_ref[q] + 1, body, 0)
        xs_out[sub * CH:(sub + 1) * CH, :] = acc_sc[...].astype(xs_out.dtype)


def _gather(hb, metat, tile_expert, s_lo, s_hi, base, NT):
    TM, TS = TM_MOE, TS_MOE
    assert hb.shape[0] % TS == 0
    grid_spec = pltpu.PrefetchScalarGridSpec(
        num_scalar_prefetch=4,
        grid=(NT,),
        in_specs=[pl.BlockSpec(memory_space=pltpu.VMEM), pl.BlockSpec(memory_space=pltpu.VMEM)],
        out_specs=pl.BlockSpec((TM, D_MODEL), lambda j, *_: (j, 0)),
        scratch_shapes=[pltpu.VMEM((CH_MOE, D_MODEL), F32)])
    return pl.pallas_call(
        functools.partial(_gather_kernel, TM=TM, TS=TS, CH=CH_MOE),
        grid_spec=grid_spec,
        out_shape=jax.ShapeDtypeStruct((NT * TM, D_MODEL), BF16),
        compiler_params=_params("arbitrary"),
        name="moe_gather",
    )(tile_expert, s_lo, s_hi, base, metat, hb)


def _combine_kernel(blk_ref, row0_ref, base_ref, meta_ref, x_ref, gpost_ref, ys_hbm, o_ref, buf, sem, selw_sc,
                    *, TS, CH, U, nt):
    s = pl.program_id(0)
    slot = s % 2

    def chunk_copies(step, slot_):
        out = []
        for u in range(U):
            r0 = pl.multiple_of(blk_ref[step * U + u] * CH, CH)
            out.append(pltpu.make_async_copy(ys_hbm.at[pl.ds(r0, CH), :], buf.at[slot_, pl.ds(u * CH, CH), :],
                                             sem.at[slot_]))
        return out

    @pl.when(s == 0)
    def _():
        for c in chunk_copies(0, 0):
            c.start()

    @pl.when(s + 1 < nt)
    def _():
        for c in chunk_copies(s + 1, 1 - slot):
            c.start()

    meta = meta_ref[...]
    e1, e2 = meta[:, 0:1], meta[:, 1:2]
    w1, w2 = meta[:, 4:5], meta[:, 5:6]
    b1 = jnp.zeros((TS, 1), F32)
    b2 = jnp.zeros((TS, 1), F32)
    for e in range(N_EXPERTS):
        be = base_ref[e].astype(F32)
        b1 = jnp.where(e1 == float(e), be, b1)
        b2 = jnp.where(e2 == float(e), be, b2)
    pos1 = b1 + meta[:, 2:3]
    pos2 = b2 + meta[:, 3:4]

    lane = lax.broadcasted_iota(jnp.int32, (1, CH), 1)
    for u in range(U):
        rows = (row0_ref[s * U + u] + lane).astype(F32)
        selw = jnp.where(pos1 == rows, w1, 0.0) + jnp.where(pos2 == rows, w2, 0.0)
        selw_sc[:, u * CH:(u + 1) * CH] = selw.astype(BF16)
    for c in chunk_copies(s, slot):
        c.wait()
    y = _dot(selw_sc[...], buf[slot])
    o_ref[...] = x_ref[...] + _rms(y, gpost_ref[...])


def _combine(x, meta, ys, chunk_blk, chunk_row0, base, g_post):
    T = x.shape[0]
    TS, CH, U = TS_MOE, CH_MOE, U_MOE
    nt = T // TS
    grid_spec = pltpu.PrefetchScalarGridSpec(
        num_scalar_prefetch=3,
        grid=(nt,),
        in_specs=[pl.BlockSpec((TS, LANES), lambda s, *_: (s, 0)),
                  pl.BlockSpec((TS, D_MODEL), lambda s, *_: (s, 0)),
                  pl.BlockSpec((1, D_MODEL), lambda s, *_: (0, 0)),
                  pl.BlockSpec(memory_space=pl.ANY)],
        out_specs=pl.BlockSpec((TS, D_MODEL), lambda s, *_: (s, 0)),
        scratch_shapes=[pltpu.VMEM((2, U * CH, D_MODEL), BF16), pltpu.SemaphoreType.DMA((2,)),
                        pltpu.VMEM((TS, U * CH), BF16)])
    return pl.pallas_call(
        functools.partial(_combine_kernel, TS=TS, CH=CH, U=U, nt=nt),
        grid_spec=grid_spec,
        out_shape=jax.ShapeDtypeStruct((T, D_MODEL), F32),
        compiler_params=_params("arbitrary"),
        name="moe_combine",
    )(chunk_blk, chunk_row0, base, meta, x, g_post.reshape(1, -1), ys)


def _moe(x, g_pre, w_router, w_gate, w_up, w_down, g_post):
    T = x.shape[0]
    TS, TM, CH, U = TS_MOE, TM_MOE, CH_MOE, U_MOE
    nt = T // TS
    NT = (2 * T) // TM + N_EXPERTS
    hb, meta, metat, start, cnt = _router(x, g_pre, w_router)

    i32 = jnp.int32
    counts = cnt[0, :N_EXPERTS].astype(i32)
    start = start[:, 0, :N_EXPERTS].astype(i32)
    start_ext = jnp.concatenate([start, counts[None, :]], axis=0)
    tiles_e = (counts + TM - 1) // TM
    tiles_cum = jnp.cumsum(tiles_e)
    base = (tiles_cum - tiles_e) * TM
    n_tiles = tiles_cum[-1:]
    jj = jnp.arange(NT, dtype=i32)
    tile_expert = jnp.minimum(jnp.sum(jj[:, None] >= tiles_cum[None, :], axis=1), N_EXPERTS - 1).astype(i32)
    live = jj < n_tiles[0]
    qq = jnp.arange(NT * (TM // CH), dtype=i32)
    q_exp = tile_expert[qq // (TM // CH)]
    a = qq * CH - base[q_exp]
    st_e = start_ext[:, q_exp]
    s_lo = jnp.sum(st_e[1:] <= a[None, :], axis=0).astype(i32)
    s_hi = (jnp.sum(st_e[:-1] < (a + CH)[None, :], axis=0) - 1).astype(i32)
    q_live = live[qq // (TM // CH)]
    s_lo = jnp.where(q_live, s_lo, 1)
    s_hi = jnp.where(q_live, s_hi, 0)

    xs = _gather(hb, metat, tile_expert, s_lo, s_hi, base.astype(i32), NT)
    ys = _experts(xs, tile_expert, n_tiles.astype(i32), w_gate.astype(BF16), w_up.astype(BF16), w_down.astype(BF16))

    lo = base[None, :] + start_ext[:-1]
    hi = base[None, :] + start_ext[1:]
    n_e = jnp.where(hi > lo, (hi - 1) // CH - lo // CH + 1, 0)
    cum = jnp.cumsum(n_e, axis=1)
    uu = jnp.arange(U, dtype=i32)
    e_u = jnp.minimum(jnp.sum(uu[None, :, None] >= cum[:, None, :], axis=2), N_EXPERTS - 1)
    first = jnp.take_along_axis(lo // CH, e_u, axis=1)
    skipped = jnp.take_along_axis(cum - n_e, e_u, axis=1)
    chunk_blk = jnp.clip(first + (uu[None, :] - skipped), 0, (NT * TM) // CH - 1).astype(i32)
    used = uu[None, :] < cum[:, -1:]
    chunk_blk = jnp.where(used, chunk_blk, chunk_blk[:, :1])
    chunk_row0 = jnp.where(used, chunk_blk * CH, -(NT * TM)).astype(i32)
    return _combine(x, meta, ys, chunk_blk.reshape(-1), chunk_row0.reshape(-1), base.astype(i32), g_post)


def kernel(x, positions, w_in, a_sinks, g_cq, w_uq, g_ckv, w_ukv, c_w1_k, c_w2_k, c_pe_k, c_w1_v, c_w2_v, c_pe_v,
           g_oa, g_ob, g_oc, w_o, g_pre_mix, g_post_mix, g_pre_ffn, g_post_ffn, ffn_w_gate, ffn_w_up, ffn_w_down,
           moe_router, moe_w_gate, moe_w_up, moe_w_down):
    B, S, _ = x.shape
    T = B * S
    depth = w_in.shape[0]
    a_slopes = _alibi_slopes(A_HEADS)
    c_slopes = _alibi_slopes(C_HEADS)
    xf = x.reshape(T, D_MODEL).astype(F32)
    cos, sin = _rope_tables(positions)
    att_scale = HEAD_DIM ** -0.5
    for l in range(depth):
        w_p, w_vt = _build_w_in(w_in[l])
        P, VT = _inproj(xf, g_pre_mix[l].reshape(1, -1), w_p, w_vt, S)
        o_a = _flash_attn(P, P, VT, B, S, name="swa_attn", TQ=TQ_ATT, TK=TK_BAND, G=A_KV_HEADS,
                          R=A_HEADS // A_KV_HEADS, KD=KD_A, q_col=P_AQ, q_width=A_HEADS * HEAD_DIM, k_col=P_KA,
                          vt_slab=VT_A, W=A_WINDOW, slopes=a_slopes, scale=att_scale, sinks=a_sinks[l])
        qB, kB, vtB = _mla_prep(P, cos, sin, g_cq[l], g_ckv[l], w_uq[l], w_ukv[l])
        o_b = _flash_attn(qB, kB, vtB, B, S, name="mla_attn", TQ=TQ_MLA, TK=TQ_MLA, G=B_HEADS, R=1, KD=LANES,
                          q_col=0, q_width=B_HEADS * LANES, k_col=0, vt_slab=0)
        kc, vct = _compress(P, B, S, c_w1_k[l], c_w2_k[l], c_pe_k[l], c_w1_v[l], c_w2_v[l], c_pe_v[l])
        o_cmp, selb = _cmp_sel(P, kc, vct, B, S, c_slopes)
        c_args = dict(TQ=TQ_ATT, G=C_KV_HEADS, R=C_HEADS // C_KV_HEADS, q_col=P_CQ, q_width=C_HEADS * HEAD_DIM,
                      slopes=c_slopes, scale=att_scale, gate=P)
        o_slc = _flash_attn(P, P, VT, B, S, name="nsa_selected", TK=TK_SEL, KD=KD_SEL, k_col=P_KSEL,
                            vt_slab=VT_SEL, selb=selb, gate_branch=1, **c_args)
        o_win = _flash_attn(P, P, VT, B, S, name="nsa_window", TK=TK_BAND, KD=KD_WIN, k_col=P_KWIN,
                            vt_slab=VT_WIN, W=C_WINDOW, gate_branch=2, **c_args)
        xf = _outproj(xf, o_a, o_b, o_cmp, o_slc, o_win, g_oa[l], g_ob[l], g_oc[l], w_o[l], g_post_mix[l])
        if l % 2 == 0:
            xf = _ffn(xf, g_pre_ffn[l], ffn_w_gate[l // 2], ffn_w_up[l // 2], ffn_w_down[l // 2], g_post_ffn[l])
        else:
            xf = _moe(xf, g_pre_ffn[l], moe_router[l // 2], moe_w_gate[l // 2], moe_w_up[l // 2],
                      moe_w_down[l // 2], g_post_ffn[l])
    return xf.reshape(B, S, D_MODEL)
```

```python
import functools

import numpy as np
import jax
import jax.numpy as jnp
from jax import lax
from jax.experimental import pallas as pl
from jax.experimental.pallas import tpu as pltpu

F32 = jnp.float32
BF16 = jnp.bfloat16

D_MODEL = 1024
HEAD_DIM = 64
NORM_EPS = 1e-6
NEG_INF = -1e30
ROPE_THETA = 10000.0
LOG2E = 1.4426950408889634

A_HEADS, A_KV_HEADS, A_WINDOW = 4, 2, 128
B_HEADS, B_Q_RANK, B_KV_RANK, B_NOPE, B_ROPE = 4, 256, 128, 64, 32
C_HEADS, C_KV_HEADS, C_WINDOW = 8, 2, 512
C_CMP_BLOCK, C_CMP_STRIDE, C_CMP_HIDDEN = 32, 16, 128
C_SEL_BLOCK, C_SEL_TOP = 64, 16
N_EXPERTS = 8

LANES = 128
VMEM_LIMIT = 48 * 1024 * 1024

P_CQ, P_AQ, P_BCQ = 0, 512, 768
P_BCKV, P_CKC, P_CVC, P_KR, P_KRROT, P_GATE = 1024, 1152, 1280, 1408, 1536, 1664
P_KSEL, KD_SEL = 1792, 256
P_KWIN, KD_WIN = 2304, 128
P_KA, KD_A = 2560, 128
P_WIDTH = 2816
V_ROWS = 80
VT_SEL, VT_WIN, VT_A = 0, 2, 4
VT_SLABS = 6
MASK_BIAS = -1e30
M_INIT = -1e29

TM_PROJ = 512
TQ_ATT = 128
TK_BAND = 128
TQ_MLA = 256
TK_SEL = 256
TM_FFN = 512
TF_FFN = 2048
TF_MOE = 1792
TS_MOE = 512
TM_MOE = 512
CH_MOE = 128
U_MOE = 24


def _alibi_slopes(n):
    return [float(np.float32(2.0 ** (-8.0 * (i + 1) / n))) for i in range(n)]


def _split_bf16(x):
    parts = []
    for _ in range(3):
        p = float(np.asarray(x, np.float32).astype(jnp.bfloat16))
        parts.append(p)
        x = x - p
    return parts


def _dot(a, b):
    return jnp.dot(a, b, preferred_element_type=F32)


def _dot_nt(a, b):
    return lax.dot_general(a, b, (((1,), (1,)), ((), ())), preferred_element_type=F32)


def _rms(x, g):
    return x * lax.rsqrt(jnp.mean(x * x, axis=-1, keepdims=True) + NORM_EPS) * g


def _sigmoid(x):
    return 1.0 / (1.0 + jnp.exp(-x))


def _params(*sem):
    return pltpu.CompilerParams(dimension_semantics=sem, vmem_limit_bytes=VMEM_LIMIT)


def _inproj_kernel(x_ref, g_ref, w_ref, wvt_ref, vone_ref, kc_ref, o_ref, vt_ref):
    h = _rms(x_ref[...], g_ref[...]).astype(BF16)
    for n in range(P_WIDTH // 256):
        o_ref[:, n * 256:(n + 1) * 256] = _dot(h, w_ref[:, n * 256:(n + 1) * 256]).astype(BF16)
    for g in range(C_KV_HEADS):
        o_ref[:, P_KSEL + g * KD_SEL + 64:P_KSEL + g * KD_SEL + 192] = kc_ref[:, 0:128]
        o_ref[:, P_KWIN + g * KD_WIN + 64:P_KWIN + (g + 1) * KD_WIN] = kc_ref[:, 128:192]
    for g in range(A_KV_HEADS):
        o_ref[:, P_KA + g * KD_A + 64:P_KA + (g + 1) * KD_A] = kc_ref[:, 128:192]
    vt_ref[...] = (_dot_nt(wvt_ref[...], h) + vone_ref[...]).astype(BF16)


def _key_constants(S):
    s = np.arange(S)
    kc = np.zeros((S, 256), np.float32)
    kc[:, 0:3] = (s % TK_SEL)[:, None]
    kc[s, 64 + s // C_SEL_BLOCK] = 1.0
    kc[:, 128:131] = (s % TK_BAND)[:, None]
    return jnp.asarray(kc, dtype=BF16)


def _inproj(x, g, w, wvt, S):
    T = x.shape[0]
    TM = TM_PROJ
    assert S // C_SEL_BLOCK <= 64 and TK_SEL <= 256 and TK_BAND <= 256
    vone = np.zeros((VT_SLABS * V_ROWS, 1), np.float32)
    vone[HEAD_DIM::V_ROWS] = 1.0
    nblk = S // TM
    return pl.pallas_call(
        _inproj_kernel,
        grid=(T // TM,),
        in_specs=[pl.BlockSpec((TM, D_MODEL), lambda i: (i, 0)),
                  pl.BlockSpec((1, D_MODEL), lambda i: (0, 0)),
                  pl.BlockSpec((D_MODEL, P_WIDTH), lambda i: (0, 0)),
                  pl.BlockSpec((VT_SLABS * V_ROWS, D_MODEL), lambda i: (0, 0)),
                  pl.BlockSpec((VT_SLABS * V_ROWS, 1), lambda i: (0, 0)),
                  pl.BlockSpec((TM, 256), lambda i: (i % nblk, 0))],
        out_specs=[pl.BlockSpec((TM, P_WIDTH), lambda i: (i, 0)),
                   pl.BlockSpec((VT_SLABS * V_ROWS, TM), lambda i: (0, i))],
        out_shape=[jax.ShapeDtypeStruct((T, P_WIDTH), BF16),
                   jax.ShapeDtypeStruct((VT_SLABS * V_ROWS, T), BF16)],
        compiler_params=_params("parallel"),
        name="inproj",
    )(x, g, w, wvt, jnp.asarray(vone), _key_constants(S))


def _build_w_in(w_in):
    cuts = np.cumsum([0, 256, 128, 128, 256, 128, 32, 512, 128, 128, 128, 128, 128, 128, 24])
    seg = [w_in[:, cuts[i]:cuts[i + 1]] for i in range(14)]
    a_q, a_k, a_v, b_cq, b_ckv, b_kr, c_q, c_kc, c_vc, c_ks, c_vs, c_kw, c_vw, c_g = seg
    z = lambda n: jnp.zeros((w_in.shape[0], n), w_in.dtype)
    half = B_ROPE // 2
    kr_rot = jnp.concatenate([-b_kr[:, half:], b_kr[:, :half]], axis=1)
    cols = [c_q, a_q, b_cq, b_ckv, c_kc, c_vc,
            z(B_NOPE), b_kr, z(LANES - B_NOPE - B_ROPE),
            z(B_NOPE), kr_rot, z(LANES - B_NOPE - B_ROPE),
            c_g, z(LANES - 24)]
    for k, kd in ((c_ks, KD_SEL), (c_kw, KD_WIN), (a_k, KD_A)):
        for g in range(2):
            cols += [k[:, g * HEAD_DIM:(g + 1) * HEAD_DIM], z(kd - HEAD_DIM)]
    w = jnp.concatenate(cols, axis=1).astype(BF16)
    rows = []
    for v in (c_vs, c_vw, a_v):
        for g in range(2):
            rows += [v[:, g * HEAD_DIM:(g + 1) * HEAD_DIM].T, jnp.zeros((V_ROWS - HEAD_DIM, w_in.shape[0]), w_in.dtype)]
    return w, jnp.concatenate(rows, axis=0).astype(BF16)


def _rope_tab_kernel(pos_ref, inv_ref, cos_ref, sin_ref):
    ang = pos_ref[...] * inv_ref[...]
    cos_ref[...] = jnp.cos(ang)
    sin_ref[...] = jnp.sin(ang)


def _rope_tables(positions):
    T = positions.size
    inv = (ROPE_THETA ** (-np.arange(0, B_ROPE, 2, dtype=np.float32) / B_ROPE)).astype(np.float32)
    inv128 = np.zeros((1, LANES), np.float32)
    inv128[0, B_NOPE:B_NOPE + B_ROPE // 2] = inv
    inv128[0, B_NOPE + B_ROPE // 2:B_NOPE + B_ROPE] = inv
    pos = positions.reshape(T, 1).astype(F32)
    return pl.pallas_call(
        _rope_tab_kernel,
        grid=(T // TM_PROJ,),
        in_specs=[pl.BlockSpec((TM_PROJ, 1), lambda i: (i, 0)),
                  pl.BlockSpec((1, LANES), lambda i: (0, 0))],
        out_specs=[pl.BlockSpec((TM_PROJ, LANES), lambda i: (i, 0))] * 2,
        out_shape=[jax.ShapeDtypeStruct((T, LANES), F32)] * 2,
        compiler_params=_params("parallel"),
        name="rope_tables",
    )(pos, jnp.asarray(inv128))


def _flash_kernel(*refs, TQ, TK, G, R, W, slopes, scale, sel, has_sink, gate_branch):
    refs = list(refs)
    q_ref = refs.pop(0)
    selb_ref = refs.pop(0) if sel else None
    k_refs = [refs.pop(0) for _ in range(G)]
    v_refs = [refs.pop(0) for _ in range(G)]
    sink_ref = refs.pop(0) if has_sink else None
    gate_ref = refs.pop(0) if gate_branch is not None else None
    o_ref = refs.pop(0)
    qa_sc = refs.pop(0) if slopes is not None else None
    m_sc, acc_sc, s_sc, mx_sc = refs
    i = pl.program_id(1)
    t0 = i * TQ
    RQ = R * TQ
    lane = lax.broadcasted_iota(jnp.int32, (1, RQ), 1)
    qpos = t0 + (lane & (TQ - 1))
    hi = (t0 + TQ + TK - 1) // TK
    lo = jnp.maximum(t0 - W, 0) // TK if W else 0
    if gate_ref is not None:
        gate_t = _sigmoid(gate_ref[...].astype(F32)).T
    slope_vecs = []
    if slopes is not None:
        q_t = q_ref[...].astype(F32).T
        frow = lax.broadcasted_iota(jnp.int32, (HEAD_DIM, TQ), 0)
    for g in range(G):
        if slopes is not None:
            slope_vec = jnp.zeros((1, RQ), F32)
            for r in range(R):
                hh = g * R + r
                cols = slice(r * TQ, (r + 1) * TQ)
                s3 = _split_bf16(slopes[hh] * LOG2E)
                slope_vec = jnp.where(lane // TQ == r, sum(s3), slope_vec)
                qa_sc[g, 0:HEAD_DIM, cols] = (q_t[hh * HEAD_DIM:(hh + 1) * HEAD_DIM, :] * (scale * LOG2E)).astype(BF16)
                qa_sc[g, HEAD_DIM:2 * HEAD_DIM, cols] = jnp.where(
                    frow == 0, s3[0], jnp.where(frow == 1, s3[1], jnp.where(frow == 2, s3[2], 0.0))).astype(BF16)
                if sel:
                    qa_sc[g, 2 * HEAD_DIM:3 * HEAD_DIM, cols] = selb_ref[g * HEAD_DIM:(g + 1) * HEAD_DIM, :]
                    qa_sc[g, 3 * HEAD_DIM:4 * HEAD_DIM, cols] = jnp.zeros((HEAD_DIM, TQ), BF16)
            slope_vecs.append(slope_vec)
        else:
            slope_vecs.append(None)
        m_sc[g] = jnp.full((1, RQ), M_INIT, F32)
        acc_sc[g] = jnp.zeros((V_ROWS, RQ), F32)

    def scores(j, slot):
        ks = pl.multiple_of(jnp.asarray(j, jnp.int32) * TK, TK)
        for g in range(G):
            qa_t = qa_sc[g] if slopes is not None else q_ref[g * LANES:(g + 1) * LANES, :]
            st = _dot(k_refs[g][pl.ds(ks, TK), :], qa_t)
            s_sc[slot, g] = st
            mx_sc[slot, g] = jnp.max(st, axis=0, keepdims=True)

    def tile(j, slot, mask, live=None):
        ks = pl.multiple_of(jnp.asarray(j, jnp.int32) * TK, TK)
        for g in range(G):
            st = s_sc[slot, g]
            if mask is not None:
                kpos = ks + lax.broadcasted_iota(jnp.int32, (TK, 1), 0)
                ok = kpos <= qpos if mask == "causal" else kpos > qpos - W
                st = jnp.where(ok, st, MASK_BIAS)
            if live is not None:
                st = jnp.where(live, st, MASK_BIAS)
            m = m_sc[g]
            mx = mx_sc[slot, g] if mask is None and live is None else jnp.max(st, axis=0, keepdims=True)
            if slopes is not None:
                c = slope_vecs[g] * ks.astype(F32)
                mn = jnp.maximum(m, mx + c)
                p = jnp.exp2(st - (mn - c))
            else:
                mn = jnp.maximum(m, mx)
                p = jnp.exp2(st - mn)
            acc_sc[g] = jnp.exp2(m - mn) * acc_sc[g] + _dot(v_refs[g][:, pl.ds(ks, TK)], p.astype(BF16))
            m_sc[g] = mn

    if W:
        n_st = W // TK + 1

        def banded(check):
            js = [hi - n_st + t for t in range(n_st)]
            jc = [jnp.maximum(j, 0) for j in js] if check else js
            scores(jc[0], 0)
            for t in range(n_st):
                if t + 1 < n_st:
                    scores(jc[t + 1], (t + 1) & 1)
                mask = "causal" if t == n_st - 1 else ("window" if t == 0 else None)
                tile(jc[t], t & 1, mask, live=(js[t] >= 0) if check and t < n_st - 1 else None)

        @pl.when(hi >= n_st)
        def _():
            banded(False)

        @pl.when(hi < n_st)
        def _():
            banded(True)
    else:
        n = hi
        n_pair = (n - 1) // 2

        def pair(p, carry):
            t = 2 * p
            scores(t + 1, 1)
            tile(t, 0, None)
            scores(t + 2, 0)
            tile(t + 1, 1, None)
            return carry

        scores(0, 0)
        lax.fori_loop(0, n_pair, pair, 0)

        @pl.when(2 * n_pair == n - 1)
        def _():
            tile(n - 1, 0, "causal")

        @pl.when(2 * n_pair == n - 2)
        def _():
            scores(n - 1, 1)
            tile(n - 2, 0, None)
            tile(n - 1, 1, "causal")

    pieces = []
    for g in range(G):
        acc = acc_sc[g]
        num = acc[0:HEAD_DIM, :]
        den = acc[HEAD_DIM:HEAD_DIM + 1, :]
        if has_sink:
            sink_vec = jnp.zeros((1, RQ), F32)
            for r in range(R):
                sink_vec = jnp.where(lane // TQ == r, sink_ref[g * R + r] * LOG2E, sink_vec)
            m_true = m_sc[g] - slope_vecs[g] * qpos.astype(F32)
            mf = jnp.maximum(m_true, sink_vec)
            a = jnp.exp2(m_true - mf)
            num = num * a
            den = den * a + jnp.exp2(sink_vec - mf)
        o_t = num * (1.0 / den)
        for r in range(R):
            piece = o_t[:, r * TQ:(r + 1) * TQ]
            if gate_ref is not None:
                c = 3 * (g * R + r) + gate_branch
                piece = piece * gate_t[c:c + 1, :]
            pieces.append(piece)
    o_ref[...] = jnp.concatenate(pieces, axis=0).T.astype(o_ref.dtype)


def _flash_attn(q, k, vt, B, S, *, name, TQ, TK, G, R, KD, q_col, q_width, k_col, vt_slab, W=0, slopes=None, scale=1.0,
                selb=None, sinks=None, gate=None, gate_branch=None):
    T = B * S
    nq = S // TQ
    RQ = R * TQ
    assert TK % TQ == 0 and (not W or (TK == TQ and W % TK == 0)) and S % TK == 0
    OW = G * R * HEAD_DIM
    if slopes is not None:
        in_specs = [pl.BlockSpec((TQ, q_width), lambda b, i: (b * nq + i, q_col // q_width))]
    else:
        in_specs = [pl.BlockSpec((G * KD, TQ), lambda b, i: (0, b * nq + i))]
    args = [q]
    if selb is not None:
        in_specs.append(pl.BlockSpec((LANES, TQ), lambda b, i: (0, b * nq + i)))
        args.append(selb)
    for g in range(G):
        in_specs.append(pl.BlockSpec((S, KD), lambda b, i, g=g: (b, k_col // KD + g)))
        args.append(k)
    for g in range(G):
        in_specs.append(pl.BlockSpec((V_ROWS, S), lambda b, i, g=g: (vt_slab + g, b)))
        args.append(vt)
    if sinks is not None:
        in_specs.append(pl.BlockSpec(memory_space=pltpu.SMEM))
        args.append(sinks.astype(F32))
    if gate_branch is not None:
        in_specs.append(pl.BlockSpec((TQ, LANES), lambda b, i: (b * nq + i, P_GATE // LANES)))
        args.append(gate)
    scratch = []
    if slopes is not None:
        scratch.append(pltpu.VMEM((G, KD, RQ), BF16))
    scratch += [pltpu.VMEM((G, 1, RQ), F32), pltpu.VMEM((G, V_ROWS, RQ), F32), pltpu.VMEM((2, G, TK, RQ), F32),
                pltpu.VMEM((2, G, 1, RQ), F32)]
    kern = functools.partial(_flash_kernel, TQ=TQ, TK=TK, G=G, R=R, W=W, slopes=slopes, scale=scale,
                             sel=selb is not None, has_sink=sinks is not None, gate_branch=gate_branch)
    return pl.pallas_call(
        kern,
        grid=(B, nq),
        in_specs=in_specs,
        out_specs=pl.BlockSpec((TQ, OW), lambda b, i: (b * nq + i, 0)),
        out_shape=jax.ShapeDtypeStruct((T, OW), BF16),
        scratch_shapes=scratch,
        compiler_params=_params("parallel", "parallel"),
        name=name,
    )(*args)


def _mla_prep_kernel(cq_ref, ckv_ref, kr_ref, krr_ref, cos_ref, sin_ref, gq_ref, gkv_ref,
                     wq_ref, wqr_ref, wk_ref, wvt_ref, vone_ref, q_out, k_out, vt_out):
    cos = cos_ref[...]
    sin = sin_ref[...]
    scale = (B_NOPE + B_ROPE) ** -0.5 * LOG2E
    nq = _rms(cq_ref[...].astype(F32), gq_ref[...]).astype(BF16)
    nkv = _rms(ckv_ref[...].astype(F32), gkv_ref[...]).astype(BF16)
    q1 = _dot_nt(wq_ref[...], nq)
    q2 = _dot_nt(wqr_ref[...], nq)
    kk = _dot(nkv, wk_ref[...])
    krope = kr_ref[...].astype(F32) * cos + krr_ref[...].astype(F32) * sin
    cos_t = cos.T
    sin_t = sin.T
    for h in range(B_HEADS):
        sl = slice(h * LANES, (h + 1) * LANES)
        q_out[sl, :] = ((q1[sl, :] * cos_t + q2[sl, :] * sin_t) * scale).astype(BF16)
        k_out[:, sl] = (kk[:, sl] + krope).astype(BF16)
    vt_out[...] = (_dot_nt(wvt_ref[...], nkv) + vone_ref[...]).astype(BF16)


def _build_mla_weights(w_uq, w_ukv):
    R1, R2 = w_uq.shape[0], w_ukv.shape[0]
    half = B_ROPE // 2
    hq = B_NOPE + B_ROPE
    wq, wqr, wk, wvt = [], [], [], []
    for h in range(B_HEADS):
        nope = w_uq[:, h * hq:h * hq + B_NOPE]
        rope = w_uq[:, h * hq + B_NOPE:(h + 1) * hq]
        rot = jnp.concatenate([-rope[:, half:], rope[:, :half]], axis=1)
        pad = jnp.zeros((R1, LANES - hq), w_uq.dtype)
        wq += [nope, rope, pad]
        wqr += [jnp.zeros((R1, B_NOPE), w_uq.dtype), rot, pad]
        wk += [w_ukv[:, h * 128:h * 128 + B_NOPE], jnp.zeros((R2, LANES - B_NOPE), w_ukv.dtype)]
        wvt += [w_ukv[:, h * 128 + B_NOPE:(h + 1) * 128].T, jnp.zeros((V_ROWS - HEAD_DIM, R2), w_ukv.dtype)]
    cat = lambda xs: jnp.concatenate(xs, axis=1).astype(BF16)
    return cat(wq).T, cat(wqr).T, cat(wk), jnp.concatenate(wvt, axis=0).astype(BF16)


def _mla_prep(P, cos, sin, g_cq, g_ckv, w_uq, w_ukv):
    T = P.shape[0]
    TM = TM_PROJ
    wq, wqr, wk, wvt = _build_mla_weights(w_uq, w_ukv)
    QW = B_HEADS * LANES
    VR = B_HEADS * V_ROWS
    vone = np.zeros((VR, 1), np.float32)
    vone[HEAD_DIM::V_ROWS] = 1.0
    full = lambda shp: pl.BlockSpec(shp, lambda i: (0, 0))
    return pl.pallas_call(
        _mla_prep_kernel,
        grid=(T // TM,),
        in_specs=[pl.BlockSpec((TM, B_Q_RANK), lambda i: (i, P_BCQ // B_Q_RANK)),
                  pl.BlockSpec((TM, B_KV_RANK), lambda i: (i, P_BCKV // B_KV_RANK)),
                  pl.BlockSpec((TM, LANES), lambda i: (i, P_KR // LANES)),
                  pl.BlockSpec((TM, LANES), lambda i: (i, P_KRROT // LANES)),
                  pl.BlockSpec((TM, LANES), lambda i: (i, 0)),
                  pl.BlockSpec((TM, LANES), lambda i: (i, 0)),
                  full((1, B_Q_RANK)), full((1, B_KV_RANK)),
                  full((QW, B_Q_RANK)), full((QW, B_Q_RANK)), full((B_KV_RANK, QW)), full((VR, B_KV_RANK)),
                  full((VR, 1))],
        out_specs=[pl.BlockSpec((QW, TM), lambda i: (0, i)),
                   pl.BlockSpec((TM, QW), lambda i: (i, 0)),
                   pl.BlockSpec((VR, TM), lambda i: (0, i))],
        out_shape=[jax.ShapeDtypeStruct((QW, T), BF16), jax.ShapeDtypeStruct((T, QW), BF16),
                   jax.ShapeDtypeStruct((VR, T), BF16)],
        compiler_params=_params("parallel"),
        name="mla_prep",
    )(P, P, P, P, cos, sin, g_cq.reshape(1, -1), g_ckv.reshape(1, -1), wq, wqr, wk, wvt, jnp.asarray(vone))


def _gelu_tanh(x):
    return 0.5 * x * (1.0 + jnp.tanh(np.float32(np.sqrt(2.0 / np.pi)) * (x + 0.044715 * (x * x * x))))


def _compress_kernel(xk_ref, xv_ref, pek_ref, pev_ref, w1k_ref, w1v_ref, w2k_ref, w2v_ref, kc_ref, vct_ref, *, NC):
    for x_ref, pe_ref, w1_ref, w2_ref, is_v in ((xk_ref, pek_ref, w1k_ref, w2k_ref, False),
                                                 (xv_ref, pev_ref, w1v_ref, w2v_ref, True)):
        x = x_ref[...].astype(F32)
        a = _dot((x + pe_ref[0:1, :]).astype(BF16), w1_ref[0])
        b = _dot((x + pe_ref[1:2, :]).astype(BF16), w1_ref[1])
        hid = a + pltpu.roll(b, NC - 1, 0)
        o = _dot(_gelu_tanh(hid).astype(BF16), w2_ref[...])
        n = lax.broadcasted_iota(jnp.int32, o.shape, 0)
        o = jnp.where(n < NC - 1, o, 0.0)
        if is_v:
            vct_ref[...] = o.T.astype(vct_ref.dtype)
        else:
            kc_ref[...] = o.astype(kc_ref.dtype)


def _build_compress_weights(w1, w2, pe):
    half = C_CMP_BLOCK // 2
    G = C_KV_HEADS
    eye = jnp.eye(G, dtype=w1.dtype)
    w1s = [jnp.einsum('lde,hg->lhdge', w1[c * half:(c + 1) * half], eye).reshape(half * G * HEAD_DIM, G * C_CMP_HIDDEN)
           for c in range(2)]
    w1f = jnp.stack(w1s).astype(BF16)
    w2f = jnp.einsum('ed,hg->hegd', w2, eye).reshape(G * C_CMP_HIDDEN, G * HEAD_DIM).astype(BF16)
    pes = [jnp.broadcast_to(pe[c * half:(c + 1) * half, None, :], (half, G, HEAD_DIM)).reshape(1, -1) for c in range(2)]
    pef = jnp.concatenate(pes, axis=0).astype(F32)
    return w1f, w2f, pef


def _compress(P, B, S, w1k, w2k, pek, w1v, w2v, pev):
    NC = S // C_CMP_STRIDE
    CW = C_CMP_STRIDE * C_KV_HEADS * HEAD_DIM
    xk = P[:, P_CKC:P_CKC + LANES].reshape(B * NC, CW)
    xv = P[:, P_CVC:P_CVC + LANES].reshape(B * NC, CW)
    w1kf, w2kf, pekf = _build_compress_weights(w1k, w2k, pek)
    w1vf, w2vf, pevf = _build_compress_weights(w1v, w2v, pev)
    xspec = pl.BlockSpec((NC, CW), lambda b: (b, 0))
    full2 = lambda a: pl.BlockSpec(a.shape, lambda b: (0, 0))
    full3 = lambda a: pl.BlockSpec(a.shape, lambda b: (0, 0, 0))
    return pl.pallas_call(
        functools.partial(_compress_kernel, NC=NC),
        grid=(B,),
        in_specs=[xspec, xspec, full2(pekf), full2(pevf), full3(w1kf), full3(w1vf), full2(w2kf), full2(w2vf)],
        out_specs=[pl.BlockSpec((NC, LANES), lambda b: (b, 0)), pl.BlockSpec((LANES, NC), lambda b: (b, 0))],
        out_shape=[jax.ShapeDtypeStruct((B * NC, LANES), BF16), jax.ShapeDtypeStruct((B * LANES, NC), BF16)],
        compiler_params=_params("parallel"),
        name="nsa_compress",
    )(xk, xv, pekf, pevf, w1kf, w1vf, w2kf, w2vf)


def _cmp_sel_kernel(q_ref, kc_ref, vct_ref, cfeat_ref, ovt_ref, gate_ref, ocmp_ref, sel_ref,
                    *, TQ, NC, NSEL, NTOP, slopes):
    i = pl.program_id(1)
    t0 = i * TQ
    R = C_HEADS // C_KV_HEADS
    scale = HEAD_DIM ** -0.5
    gate_t = _sigmoid(gate_ref[...].astype(F32)).T
    last = C_CMP_BLOCK - 1
    tlane = t0 + lax.broadcasted_iota(jnp.int32, (NC, TQ), 1)
    nrow = lax.broadcasted_iota(jnp.int32, (NC, TQ), 0)
    cvalid = (nrow * C_CMP_STRIDE + last <= tlane) & (nrow < NC - 1)
    hasc = ((t0 + lax.broadcasted_iota(jnp.int32, (1, TQ), 1)) >= last).astype(F32)
    jrow = lax.broadcasted_iota(jnp.int32, (NSEL, TQ), 0)
    tl = t0 + lax.broadcasted_iota(jnp.int32, (NSEL, TQ), 1)
    avail = jrow * C_SEL_BLOCK <= tl
    cur = tl // C_SEL_BLOCK
    forced = (jrow == 0) | (jrow == cur) | (jrow == cur - 1)
    q_t = q_ref[...].astype(F32).T
    frow = lax.broadcasted_iota(jnp.int32, (HEAD_DIM, TQ), 0)
    sub8 = lax.broadcasted_iota(jnp.int32, (8, TQ), 0)
    pieces, sel_parts = [], []
    for g in range(C_KV_HEADS):
        kca = jnp.concatenate([kc_ref[:, g * HEAD_DIM:(g + 1) * HEAD_DIM], cfeat_ref[...]], axis=1)
        vo = jnp.concatenate([vct_ref[g * HEAD_DIM:(g + 1) * HEAD_DIM, :], ovt_ref[...]], axis=0)
        imp = jnp.zeros((HEAD_DIM, TQ), F32)
        for r in range(R):
            hh = g * R + r
            s3 = _split_bf16(slopes[hh] * LOG2E)
            feat = jnp.zeros((HEAD_DIM, TQ), F32)
            for k in range(3):
                feat = jnp.where(frow == k, s3[k] * 256.0, jnp.where(frow == 3 + k, s3[k], feat))
            qa = jnp.concatenate([q_t[hh * HEAD_DIM:(hh + 1) * HEAD_DIM, :] * (scale * LOG2E), feat],
                                 axis=0).astype(BF16)
            st = jnp.where(cvalid, _dot(kca, qa), NEG_INF)
            et = jnp.exp2(st - jnp.max(st, axis=0, keepdims=True))
            inv = hasc / jnp.sum(et, axis=0, keepdims=True)
            res = _dot(vo, et.astype(BF16))
            pieces.append(res[0:HEAD_DIM, :] * (inv * gate_t[3 * hh:3 * hh + 1, :]))
            imp = imp + res[HEAD_DIM:2 * HEAD_DIM, :] * inv
        v = jnp.where(avail, imp[0:NSEL, :], NEG_INF)
        v = jnp.where(forced, -NEG_INF, v)
        ranks = [jnp.zeros((8, TQ), F32) for _ in range(NSEL // 8)]
        for ii in range(NSEL):
            ri = v[ii:ii + 1, :]
            for k in range(NSEL // 8):
                vk = v[8 * k:8 * k + 8, :]
                if 8 * k > ii:
                    inc = jnp.where(ri >= vk, 1.0, 0.0)
                elif 8 * k + 7 < ii:
                    inc = jnp.where(ri > vk, 1.0, 0.0)
                else:
                    inc = jnp.where(sub8 + 8 * k > ii, jnp.where(ri >= vk, 1.0, 0.0), jnp.where(ri > vk, 1.0, 0.0))
                ranks[k] = ranks[k] + inc
        rank = jnp.concatenate(ranks, axis=0)
        sel_parts.append(jnp.where(rank < NTOP, 0.0, MASK_BIAS))
        if NSEL < HEAD_DIM:
            sel_parts.append(jnp.zeros((HEAD_DIM - NSEL, TQ), F32))
    sel_ref[...] = jnp.concatenate(sel_parts, axis=0).astype(sel_ref.dtype)
    ocmp_ref[...] = jnp.concatenate(pieces, axis=0).T.astype(ocmp_ref.dtype)


def _overlap_t(S):
    n_cmp = (S - C_CMP_BLOCK) // C_CMP_STRIDE + 1
    n_sel = S // C_SEL_BLOCK
    NC = S // C_CMP_STRIDE
    cstart = np.arange(n_cmp) * C_CMP_STRIDE
    cend = cstart + C_CMP_BLOCK - 1
    sstart = np.arange(n_sel) * C_SEL_BLOCK
    send = sstart + C_SEL_BLOCK - 1
    ov = np.clip(np.minimum(cend[:, None], send[None, :]) - np.maximum(cstart[:, None], sstart[None, :]) + 1,
                 0, None).astype(np.float32) / C_CMP_STRIDE
    ovt = np.zeros((HEAD_DIM, NC), np.float32)
    ovt[:n_sel, :n_cmp] = ov.T
    cend_all = np.arange(NC) * C_CMP_STRIDE + C_CMP_BLOCK - 1
    cfeat = np.zeros((NC, HEAD_DIM), np.float32)
    cfeat[:, 0:3] = (cend_all // 256)[:, None]
    cfeat[:, 3:6] = (cend_all % 256)[:, None]
    return jnp.asarray(ovt, dtype=BF16), jnp.asarray(cfeat, dtype=BF16)


def _cmp_sel(P, kc, vct, B, S, slopes):
    T = B * S
    TQ = TQ_ATT
    nq = S // TQ
    NC = S // C_CMP_STRIDE
    NSEL = S // C_SEL_BLOCK
    assert NSEL <= HEAD_DIM and NSEL % 8 == 0 and S <= 256 * 256
    QW = C_HEADS * HEAD_DIM
    ovt, cfeat = _overlap_t(S)
    kern = functools.partial(_cmp_sel_kernel, TQ=TQ, NC=NC, NSEL=NSEL, NTOP=min(C_SEL_TOP, NSEL), slopes=slopes)
    return pl.pallas_call(
        kern,
        grid=(B, nq),
        in_specs=[pl.BlockSpec((TQ, QW), lambda b, i: (b * nq + i, P_CQ // QW)),
                  pl.BlockSpec((NC, LANES), lambda b, i: (b, 0)),
                  pl.BlockSpec((LANES, NC), lambda b, i: (b, 0)),
                  pl.BlockSpec((NC, HEAD_DIM), lambda b, i: (0, 0)),
                  pl.BlockSpec((HEAD_DIM, NC), lambda b, i: (0, 0)),
                  pl.BlockSpec((TQ, LANES), lambda b, i: (b * nq + i, P_GATE // LANES))],
        out_specs=[pl.BlockSpec((TQ, QW), lambda b, i: (b * nq + i, 0)),
                   pl.BlockSpec((LANES, TQ), lambda b, i: (0, b * nq + i))],
        out_shape=[jax.ShapeDtypeStruct((T, QW), BF16), jax.ShapeDtypeStruct((LANES, T), BF16)],
        compiler_params=_params("parallel", "parallel"),
        name="nsa_cmp_select",
    )(P, kc, vct, cfeat, ovt, P)


def _outproj_kernel(x_ref, oa_ref, ob_ref, oc1_ref, oc2_ref, oc3_ref, ga_ref, gb_ref, gc_ref, wo_ref, gp_ref, o_ref):
    oa = _rms(oa_ref[...].astype(F32), ga_ref[...]).astype(BF16)
    ob = _rms(ob_ref[...].astype(F32), gb_ref[...]).astype(BF16)
    oc = oc1_ref[...].astype(F32) + oc2_ref[...].astype(F32) + oc3_ref[...].astype(F32)
    oc = _rms(oc, gc_ref[...]).astype(BF16)
    m = _dot(oa, wo_ref[0:256, :]) + _dot(ob, wo_ref[256:512, :]) + _dot(oc, wo_ref[512:1024, :])
    o_ref[...] = x_ref[...] + _rms(m, gp_ref[...])


def _outproj(x, o_a, o_b, o_cmp, o_slc, o_win, g_oa, g_ob, g_oc, w_o, g_post):
    T = x.shape[0]
    TM = TM_PROJ
    row = lambda w: pl.BlockSpec((TM, w), lambda i: (i, 0))
    full = lambda r, c: pl.BlockSpec((r, c), lambda i: (0, 0))
    return pl.pallas_call(
        _outproj_kernel,
        grid=(T // TM,),
        in_specs=[row(D_MODEL), row(256), row(256), row(512), row(512), row(512),
                  full(1, 256), full(1, 256), full(1, 512), full(D_MODEL, D_MODEL), full(1, D_MODEL)],
        out_specs=row(D_MODEL),
        out_shape=jax.ShapeDtypeStruct((T, D_MODEL), F32),
        compiler_params=_params("parallel"),
        name="outproj",
    )(x, o_a, o_b, o_cmp, o_slc, o_win, g_oa.reshape(1, -1), g_ob.reshape(1, -1), g_oc.reshape(1, -1),
      w_o.astype(BF16), g_post.reshape(1, -1))


def _swiglu_step(h, wg_ref, wu_ref, wd_ref, acc_ref):
    a = _dot(h, wg_ref[...])
    b = _dot(h, wu_ref[...])
    z = (a * _sigmoid(a) * b).astype(BF16)
    acc_ref[...] += _dot(z, wd_ref[...])


def _ffn_kernel(x_ref, gpre_ref, wg_ref, wu_ref, wd_ref, gpost_ref, o_ref, h_sc, acc_sc, *, nf):
    f = pl.program_id(1)

    @pl.when(f == 0)
    def _():
        h_sc[...] = _rms(x_ref[...], gpre_ref[...]).astype(BF16)
        acc_sc[...] = jnp.zeros_like(acc_sc)

    _swiglu_step(h_sc[...], wg_ref, wu_ref, wd_ref, acc_sc)

    @pl.when(f == nf - 1)
    def _():
        o_ref[...] = x_ref[...] + _rms(acc_sc[...], gpost_ref[...])


def _ffn(x, g_pre, wg, wu, wd, g_post):
    T = x.shape[0]
    TM, TF = TM_FFN, TF_FFN
    FF = wg.shape[1]
    nf = FF // TF
    return pl.pallas_call(
        functools.partial(_ffn_kernel, nf=nf),
        grid=(T // TM, nf),
        in_specs=[pl.BlockSpec((TM, D_MODEL), lambda i, f: (i, 0)),
                  pl.BlockSpec((1, D_MODEL), lambda i, f: (0, 0)),
                  pl.BlockSpec((D_MODEL, TF), lambda i, f: (0, f)),
                  pl.BlockSpec((D_MODEL, TF), lambda i, f: (0, f)),
                  pl.BlockSpec((TF, D_MODEL), lambda i, f: (f, 0)),
                  pl.BlockSpec((1, D_MODEL), lambda i, f: (0, 0))],
        out_specs=pl.BlockSpec((TM, D_MODEL), lambda i, f: (i, 0)),
        out_shape=jax.ShapeDtypeStruct((T, D_MODEL), F32),
        scratch_shapes=[pltpu.VMEM((TM, D_MODEL), BF16), pltpu.VMEM((TM, D_MODEL), F32)],
        compiler_params=_params("parallel", "arbitrary"),
        name="dense_ffn",
    )(x, g_pre.reshape(1, -1), wg.astype(BF16), wu.astype(BF16), wd.astype(BF16), g_post.reshape(1, -1))


def _expert_kernel(te_ref, nt_ref, x_ref, wg_ref, wu_ref, wd_ref, o_ref, acc_sc, *, nf):
    j = pl.program_id(0)
    f = pl.program_id(1)
    live = j < nt_ref[0]

    @pl.when(f == 0)
    def _():
        acc_sc[...] = jnp.zeros_like(acc_sc)

    @pl.when(live)
    def _():
        _swiglu_step(x_ref[...], wg_ref.at[0], wu_ref.at[0], wd_ref.at[0], acc_sc)

    @pl.when(f == nf - 1)
    def _():
        o_ref[...] = acc_sc[...].astype(o_ref.dtype)


def _experts(xs, tile_expert, n_tiles, wg, wu, wd):
    NT = xs.shape[0] // TM_MOE
    TM, TF = TM_MOE, TF_MOE
    assert wg.shape[2] % TF == 0
    FF = wg.shape[2]
    nf = FF // TF
    fidx = lambda j, f, nt: jnp.where(j < nt[0], f, nf - 1)
    grid_spec = pltpu.PrefetchScalarGridSpec(
        num_scalar_prefetch=2,
        grid=(NT, nf),
        in_specs=[pl.BlockSpec((TM, D_MODEL), lambda j, f, te, nt: (j, 0)),
                  pl.BlockSpec((1, D_MODEL, TF), lambda j, f, te, nt: (te[j], 0, fidx(j, f, nt))),
                  pl.BlockSpec((1, D_MODEL, TF), lambda j, f, te, nt: (te[j], 0, fidx(j, f, nt))),
                  pl.BlockSpec((1, TF, D_MODEL), lambda j, f, te, nt: (te[j], fidx(j, f, nt), 0))],
        out_specs=pl.BlockSpec((TM, D_MODEL), lambda j, f, te, nt: (j, 0)),
        scratch_shapes=[pltpu.VMEM((TM, D_MODEL), F32)])
    return pl.pallas_call(
        functools.partial(_expert_kernel, nf=nf),
        grid_spec=grid_spec,
        out_shape=jax.ShapeDtypeStruct(xs.shape, BF16),
        compiler_params=_params("parallel", "arbitrary"),
        name="moe_experts",
    )(tile_expert, n_tiles, xs, wg, wu, wd)


def _router_kernel(x_ref, g_ref, wr_ref, h_out, meta_out, metat_out, start_out, cnt_out, run_sc, *, TS):
    s = pl.program_id(0)

    @pl.when(s == 0)
    def _():
        run_sc[...] = jnp.zeros_like(run_sc)

    h = _rms(x_ref[...], g_ref[...])
    hb = h.astype(BF16)
    h_out[...] = hb
    hl = (h - hb.astype(F32)).astype(BF16)
    w = wr_ref[...]
    wh = w.astype(BF16)
    wl = (w - wh.astype(F32)).astype(BF16)
    logits = _dot(hb, wh) + _dot(hl, wh) + _dot(hb, wl)
    lane = lax.broadcasted_iota(jnp.int32, (TS, LANES), 1).astype(F32)
    logits = jnp.where(lane < N_EXPERTS, logits, NEG_INF)
    m1 = jnp.max(logits, axis=-1, keepdims=True)
    i1 = jnp.min(jnp.where(logits == m1, lane, float(LANES)), axis=-1, keepdims=True)
    rest = jnp.where(lane == i1, NEG_INF, logits)
    m2 = jnp.max(rest, axis=-1, keepdims=True)
    i2 = jnp.min(jnp.where(rest == m2, lane, float(LANES)), axis=-1, keepdims=True)
    ex = jnp.exp(m2 - m1)
    w1 = 1.0 / (1.0 + ex)
    w2 = ex / (1.0 + ex)
    oh1 = lane == i1
    oh2 = lane == i2
    oh = jnp.where(oh1, 1.0, 0.0) + jnp.where(oh2, 1.0, 0.0)
    tri = jnp.where(lax.broadcasted_iota(jnp.int32, (TS, TS), 0) > lax.broadcasted_iota(jnp.int32, (TS, TS), 1),
                    1.0, 0.0).astype(BF16)
    run = run_sc[...]
    before = _dot(tri, oh.astype(BF16)) + run
    r1 = jnp.sum(jnp.where(oh1, before, 0.0), axis=-1, keepdims=True)
    r2 = jnp.sum(jnp.where(oh2, before, 0.0), axis=-1, keepdims=True)
    start_out[0] = run
    run = run + jnp.sum(oh, axis=0, keepdims=True)
    run_sc[...] = run
    cnt_out[...] = run
    meta = jnp.where(lane == 0, i1, jnp.where(lane == 1, i2, jnp.where(lane == 2, r1, jnp.where(
        lane == 3, r2, jnp.where(lane == 4, w1, jnp.where(lane == 5, w2, 0.0))))))
    meta_out[...] = meta
    metat_out[...] = meta.T[0:8, :]


def _router(x, g, w_router):
    T = x.shape[0]
    TS = TS_MOE
    nt = T // TS
    wr = jnp.zeros((D_MODEL, LANES), F32).at[:, :N_EXPERTS].set(w_router.astype(F32))
    return pl.pallas_call(
        functools.partial(_router_kernel, TS=TS),
        grid=(nt,),
        in_specs=[pl.BlockSpec((TS, D_MODEL), lambda s: (s, 0)),
                  pl.BlockSpec((1, D_MODEL), lambda s: (0, 0)),
                  pl.BlockSpec((D_MODEL, LANES), lambda s: (0, 0))],
        out_specs=[pl.BlockSpec((TS, D_MODEL), lambda s: (s, 0)),
                   pl.BlockSpec((TS, LANES), lambda s: (s, 0)),
                   pl.BlockSpec((8, TS), lambda s: (0, s)),
                   pl.BlockSpec((1, 1, LANES), lambda s: (s, 0, 0)),
                   pl.BlockSpec((1, LANES), lambda s: (0, 0))],
        out_shape=[jax.ShapeDtypeStruct((T, D_MODEL), BF16),
                   jax.ShapeDtypeStruct((T, LANES), F32),
                   jax.ShapeDtypeStruct((8, T), F32),
                   jax.ShapeDtypeStruct((nt, 1, LANES), F32),
                   jax.ShapeDtypeStruct((1, LANES), F32)],
        scratch_shapes=[pltpu.VMEM((1, LANES), F32)],
        compiler_params=_params("arbitrary"),
        name="moe_router",
    )(x, g.reshape(1, -1), wr)


def _gather_kernel(te_ref, slo_ref, shi_ref, base_ref, metat_ref, h_ref, xs_out, acc_sc, *, TM, TS, CH):
    j = pl.program_id(0)
    e = te_ref[j]
    ef = e.astype(F32)
    nsub = TM // CH
    for sub in range(nsub):
        first = j * TM + sub * CH - base_ref[e]
        want = (first + lax.broadcasted_iota(jnp.int32, (CH, 1), 0)).astype(F32)
        acc_sc[...] = jnp.zeros_like(acc_sc)

        def body(s, _, want=want):
            off = pl.multiple_of(s * TS, TS)
            e1 = metat_ref[0:1, pl.ds(off, TS)]
            e2 = metat_ref[1:2, pl.ds(off, TS)]
            r1 = metat_ref[2:3, pl.ds(off, TS)]
            r2 = metat_ref[3:4, pl.ds(off, TS)]
            key = jnp.where(e1 == ef, r1, jnp.where(e2 == ef, r2, -1.0))
            sel = jnp.where(key == want, 1.0, 0.0).astype(BF16)
            acc_sc[...] += _dot(sel, h_ref[pl.ds(off, TS), :])
            return 0

        q = j * nsub + sub
        lax.fori_loop(slo_ref[q], shi_ref[q] + 1, body, 0)
        xs_out[sub * CH:(sub + 1) * CH, :] = acc_sc[...].astype(xs_out.dtype)


def _gather(hb, metat, tile_expert, s_lo, s_hi, base, NT):
    TM, TS = TM_MOE, TS_MOE
    assert hb.shape[0] % TS == 0
    grid_spec = pltpu.PrefetchScalarGridSpec(
        num_scalar_prefetch=4,
        grid=(NT,),
        in_specs=[pl.BlockSpec(memory_space=pltpu.VMEM), pl.BlockSpec(memory_space=pltpu.VMEM)],
        out_specs=pl.BlockSpec((TM, D_MODEL), lambda j, *_: (j, 0)),
        scratch_shapes=[pltpu.VMEM((CH_MOE, D_MODEL), F32)])
    return pl.pallas_call(
        functools.partial(_gather_kernel, TM=TM, TS=TS, CH=CH_MOE),
        grid_spec=grid_spec,
        out_shape=jax.ShapeDtypeStruct((NT * TM, D_MODEL), BF16),
        compiler_params=_params("arbitrary"),
        name="moe_gather",
    )(tile_expert, s_lo, s_hi, base, metat, hb)


def _combine_kernel(blk_ref, row0_ref, base_ref, meta_ref, x_ref, gpost_ref, ys_hbm, o_ref, buf, sem, selw_sc,
                    *, TS, CH, U, nt):
    s = pl.program_id(0)
    slot = s % 2

    def chunk_copies(step, slot_):
        out = []
        for u in range(U):
            r0 = pl.multiple_of(blk_ref[step * U + u] * CH, CH)
            out.append(pltpu.make_async_copy(ys_hbm.at[pl.ds(r0, CH), :], buf.at[slot_, pl.ds(u * CH, CH), :],
                                             sem.at[slot_]))
        return out

    @pl.when(s == 0)
    def _():
        for c in chunk_copies(0, 0):
            c.start()

    @pl.when(s + 1 < nt)
    def _():
        for c in chunk_copies(s + 1, 1 - slot):
            c.start()

    meta = meta_ref[...]
    e1, e2 = meta[:, 0:1], meta[:, 1:2]
    w1, w2 = meta[:, 4:5], meta[:, 5:6]
    b1 = jnp.zeros((TS, 1), F32)
    b2 = jnp.zeros((TS, 1), F32)
    for e in range(N_EXPERTS):
        be = base_ref[e].astype(F32)
        b1 = jnp.where(e1 == float(e), be, b1)
        b2 = jnp.where(e2 == float(e), be, b2)
    pos1 = b1 + meta[:, 2:3]
    pos2 = b2 + meta[:, 3:4]

    lane = lax.broadcasted_iota(jnp.int32, (1, CH), 1)
    for u in range(U):
        rows = (row0_ref[s * U + u] + lane).astype(F32)
        selw = jnp.where(pos1 == rows, w1, 0.0) + jnp.where(pos2 == rows, w2, 0.0)
        selw_sc[:, u * CH:(u + 1) * CH] = selw.astype(BF16)
    for c in chunk_copies(s, slot):
        c.wait()
    y = _dot(selw_sc[...], buf[slot])
    o_ref[...] = x_ref[...] + _rms(y, gpost_ref[...])


def _combine(x, meta, ys, chunk_blk, chunk_row0, base, g_post):
    T = x.shape[0]
    TS, CH, U = TS_MOE, CH_MOE, U_MOE
    nt = T // TS
    grid_spec = pltpu.PrefetchScalarGridSpec(
        num_scalar_prefetch=3,
        grid=(nt,),
        in_specs=[pl.BlockSpec((TS, LANES), lambda s, *_: (s, 0)),
                  pl.BlockSpec((TS, D_MODEL), lambda s, *_: (s, 0)),
                  pl.BlockSpec((1, D_MODEL), lambda s, *_: (0, 0)),
                  pl.BlockSpec(memory_space=pl.ANY)],
        out_specs=pl.BlockSpec((TS, D_MODEL), lambda s, *_: (s, 0)),
        scratch_shapes=[pltpu.VMEM((2, U * CH, D_MODEL), BF16), pltpu.SemaphoreType.DMA((2,)),
                        pltpu.VMEM((TS, U * CH), BF16)])
    return pl.pallas_call(
        functools.partial(_combine_kernel, TS=TS, CH=CH, U=U, nt=nt),
        grid_spec=grid_spec,
        out_shape=jax.ShapeDtypeStruct((T, D_MODEL), F32),
        compiler_params=_params("arbitrary"),
        name="moe_combine",
    )(chunk_blk, chunk_row0, base, meta, x, g_post.reshape(1, -1), ys)


def _moe(x, g_pre, w_router, w_gate, w_up, w_down, g_post):
    T = x.shape[0]
    TS, TM, CH, U = TS_MOE, TM_MOE, CH_MOE, U_MOE
    nt = T // TS
    NT = (2 * T) // TM + N_EXPERTS
    hb, meta, metat, start, cnt = _router(x, g_pre, w_router)

    i32 = jnp.int32
    counts = cnt[0, :N_EXPERTS].astype(i32)
    start = start[:, 0, :N_EXPERTS].astype(i32)
    start_ext = jnp.concatenate([start, counts[None, :]], axis=0)
    tiles_e = (counts + TM - 1) // TM
    tiles_cum = jnp.cumsum(tiles_e)
    base = (tiles_cum - tiles_e) * TM
    n_tiles = tiles_cum[-1:]
    jj = jnp.arange(NT, dtype=i32)
    tile_expert = jnp.minimum(jnp.sum(jj[:, None] >= tiles_cum[None, :], axis=1), N_EXPERTS - 1).astype(i32)
    live = jj < n_tiles[0]
    qq = jnp.arange(NT * (TM // CH), dtype=i32)
    q_exp = tile_expert[qq // (TM // CH)]
    a = qq * CH - base[q_exp]
    st_e = start_ext[:, q_exp]
    s_lo = jnp.sum(st_e[1:] <= a[None, :], axis=0).astype(i32)
    s_hi = (jnp.sum(st_e[:-1] < (a + CH)[None, :], axis=0) - 1).astype(i32)
    q_live = live[qq // (TM // CH)]
    s_lo = jnp.where(q_live, s_lo, 1)
    s_hi = jnp.where(q_live, s_hi, 0)

    xs = _gather(hb, metat, tile_expert, s_lo, s_hi, base.astype(i32), NT)
    ys = _experts(xs, tile_expert, n_tiles.astype(i32), w_gate.astype(BF16), w_up.astype(BF16), w_down.astype(BF16))

    lo = base[None, :] + start_ext[:-1]
    hi = base[None, :] + start_ext[1:]
    n_e = jnp.where(hi > lo, (hi - 1) // CH - lo // CH + 1, 0)
    cum = jnp.cumsum(n_e, axis=1)
    uu = jnp.arange(U, dtype=i32)
    e_u = jnp.minimum(jnp.sum(uu[None, :, None] >= cum[:, None, :], axis=2), N_EXPERTS - 1)
    first = jnp.take_along_axis(lo // CH, e_u, axis=1)
    skipped = jnp.take_along_axis(cum - n_e, e_u, axis=1)
    chunk_blk = jnp.clip(first + (uu[None, :] - skipped), 0, (NT * TM) // CH - 1).astype(i32)
    used = uu[None, :] < cum[:, -1:]
    chunk_blk = jnp.where(used, chunk_blk, chunk_blk[:, :1])
    chunk_row0 = jnp.where(used, chunk_blk * CH, -(NT * TM)).astype(i32)
    return _combine(x, meta, ys, chunk_blk.reshape(-1), chunk_row0.reshape(-1), base.astype(i32), g_post)


def kernel(x, positions, w_in, a_sinks, g_cq, w_uq, g_ckv, w_ukv, c_w1_k, c_w2_k, c_pe_k, c_w1_v, c_w2_v, c_pe_v,
           g_oa, g_ob, g_oc, w_o, g_pre_mix, g_post_mix, g_pre_ffn, g_post_ffn, ffn_w_gate, ffn_w_up, ffn_w_down,
           moe_router, moe_w_gate, moe_w_up, moe_w_down):
    B, S, _ = x.shape
    T = B * S
    depth = w_in.shape[0]
    a_slopes = _alibi_slopes(A_HEADS)
    c_slopes = _alibi_slopes(C_HEADS)
    xf = x.reshape(T, D_MODEL).astype(F32)
    cos, sin = _rope_tables(positions)
    att_scale = HEAD_DIM ** -0.5
    for l in range(depth):
        w_p, w_vt = _build_w_in(w_in[l])
        P, VT = _inproj(xf, g_pre_mix[l].reshape(1, -1), w_p, w_vt, S)
        o_a = _flash_attn(P, P, VT, B, S, name="swa_attn", TQ=TQ_ATT, TK=TK_BAND, G=A_KV_HEADS,
                          R=A_HEADS // A_KV_HEADS, KD=KD_A, q_col=P_AQ, q_width=A_HEADS * HEAD_DIM, k_col=P_KA,
                          vt_slab=VT_A, W=A_WINDOW, slopes=a_slopes, scale=att_scale, sinks=a_sinks[l])
        qB, kB, vtB = _mla_prep(P, cos, sin, g_cq[l], g_ckv[l], w_uq[l], w_ukv[l])
        o_b = _flash_attn(qB, kB, vtB, B, S, name="mla_attn", TQ=TQ_MLA, TK=TQ_MLA, G=B_HEADS, R=1, KD=LANES,
                          q_col=0, q_width=B_HEADS * LANES, k_col=0, vt_slab=0)
        kc, vct = _compress(P, B, S, c_w1_k[l], c_w2_k[l], c_pe_k[l], c_w1_v[l], c_w2_v[l], c_pe_v[l])
        o_cmp, selb = _cmp_sel(P, kc, vct, B, S, c_slopes)
        c_args = dict(TQ=TQ_ATT, G=C_KV_HEADS, R=C_HEADS // C_KV_HEADS, q_col=P_CQ, q_width=C_HEADS * HEAD_DIM,
                      slopes=c_slopes, scale=att_scale, gate=P)
        o_slc = _flash_attn(P, P, VT, B, S, name="nsa_selected", TK=TK_SEL, KD=KD_SEL, k_col=P_KSEL,
                            vt_slab=VT_SEL, selb=selb, gate_branch=1, **c_args)
        o_win = _flash_attn(P, P, VT, B, S, name="nsa_window", TK=TK_BAND, KD=KD_WIN, k_col=P_KWIN,
                            vt_slab=VT_WIN, W=C_WINDOW, gate_branch=2, **c_args)
        xf = _outproj(xf, o_a, o_b, o_cmp, o_slc, o_win, g_oa[l], g_ob[l], g_oc[l], w_o[l], g_post_mix[l])
        if l % 2 == 0:
            xf = _ffn(xf, g_pre_ffn[l], ffn_w_gate[l // 2], ffn_w_up[l // 2], ffn_w_down[l // 2], g_post_ffn[l])
        else:
            xf = _moe(xf, g_pre_ffn[l], moe_router[l // 2], moe_w_gate[l // 2], moe_w_up[l // 2],
                      moe_w_down[l // 2], g_post_ffn[l])
    return xf.reshape(B, S, D_MODEL)
```

```python
import functools

import numpy as np
import jax
import jax.numpy as jnp
from jax import lax
from jax.experimental import pallas as pl
from jax.experimental.pallas import tpu as pltpu

F32 = jnp.float32
BF16 = jnp.bfloat16

D_MODEL = 1024
HEAD_DIM = 64
NORM_EPS = 1e-6
NEG_INF = -1e30
ROPE_THETA = 10000.0
LOG2E = 1.4426950408889634

A_HEADS, A_KV_HEADS, A_WINDOW = 4, 2, 128
B_HEADS, B_Q_RANK, B_KV_RANK, B_NOPE, B_ROPE = 4, 256, 128, 64, 32
C_HEADS, C_KV_HEADS, C_WINDOW = 8, 2, 512
C_CMP_BLOCK, C_CMP_STRIDE, C_CMP_HIDDEN = 32, 16, 128
C_SEL_BLOCK, C_SEL_TOP = 64, 16
N_EXPERTS = 8

LANES = 128
VMEM_LIMIT = 48 * 1024 * 1024

P_CQ, P_AQ, P_BCQ = 0, 512, 768
P_BCKV, P_CKC, P_CVC, P_KR, P_KRROT, P_GATE = 1024, 1152, 1280, 1408, 1536, 1664
P_KSEL, KD_SEL = 1792, 256
P_KWIN, KD_WIN = 2304, 128
P_KA, KD_A = 2560, 128
P_WIDTH = 2816
V_ROWS = 80
VT_SEL, VT_WIN, VT_A = 0, 2, 4
VT_SLABS = 6
MASK_BIAS = -1e30
M_INIT = -1e29

TM_PROJ = 512
TQ_ATT = 128
TK_BAND = 128
NSUB_A = 4
NSUB_WIN = 2
TQ_MLA = 256
TK_SEL = 256
TM_FFN = 512
TF_FFN = 2048
TF_MOE = 1792
TS_MOE = 512
TM_MOE = 512
CH_MOE = 128
U_MOE = 24


def _alibi_slopes(n):
    return [float(np.float32(2.0 ** (-8.0 * (i + 1) / n))) for i in range(n)]


def _split_bf16(x):
    parts = []
    for _ in range(3):
        p = float(np.asarray(x, np.float32).astype(jnp.bfloat16))
        parts.append(p)
        x = x - p
    return parts


def _dot(a, b):
    return jnp.dot(a, b, preferred_element_type=F32)


def _dot_nt(a, b):
    return lax.dot_general(a, b, (((1,), (1,)), ((), ())), preferred_element_type=F32)


def _rms(x, g):
    return x * lax.rsqrt(jnp.mean(x * x, axis=-1, keepdims=True) + NORM_EPS) * g


def _sigmoid(x):
    return 1.0 / (1.0 + jnp.exp(-x))


def _params(*sem):
    return pltpu.CompilerParams(dimension_semantics=sem, vmem_limit_bytes=VMEM_LIMIT)


def _inproj_kernel(x_ref, g_ref, w_ref, wvt_ref, vone_ref, kc_ref, o_ref, vt_ref):
    h = _rms(x_ref[...], g_ref[...]).astype(BF16)
    for n in range(P_WIDTH // 256):
        o_ref[:, n * 256:(n + 1) * 256] = _dot(h, w_ref[:, n * 256:(n + 1) * 256]).astype(BF16)
    for g in range(C_KV_HEADS):
        o_ref[:, P_KSEL + g * KD_SEL + 64:P_KSEL + g * KD_SEL + 192] = kc_ref[:, 0:128]
        o_ref[:, P_KWIN + g * KD_WIN + 64:P_KWIN + (g + 1) * KD_WIN] = kc_ref[:, 128:192]
    for g in range(A_KV_HEADS):
        o_ref[:, P_KA + g * KD_A + 64:P_KA + (g + 1) * KD_A] = kc_ref[:, 128:192]
    vt_ref[...] = (_dot_nt(wvt_ref[...], h) + vone_ref[...]).astype(BF16)


def _key_constants(S):
    s = np.arange(S)
    kc = np.zeros((S, 256), np.float32)
    kc[:, 0:3] = (s % TK_SEL)[:, None]
    kc[s, 64 + s // C_SEL_BLOCK] = 1.0
    kc[:, 128:131] = (s % TK_BAND)[:, None]
    return jnp.asarray(kc, dtype=BF16)


def _inproj(x, g, w, wvt, S):
    T = x.shape[0]
    TM = TM_PROJ
    assert S // C_SEL_BLOCK <= 64 and TK_SEL <= 256 and TK_BAND <= 256
    vone = np.zeros((VT_SLABS * V_ROWS, 1), np.float32)
    vone[HEAD_DIM::V_ROWS] = 1.0
    nblk = S // TM
    return pl.pallas_call(
        _inproj_kernel,
        grid=(T // TM,),
        in_specs=[pl.BlockSpec((TM, D_MODEL), lambda i: (i, 0)),
                  pl.BlockSpec((1, D_MODEL), lambda i: (0, 0)),
                  pl.BlockSpec((D_MODEL, P_WIDTH), lambda i: (0, 0)),
                  pl.BlockSpec((VT_SLABS * V_ROWS, D_MODEL), lambda i: (0, 0)),
                  pl.BlockSpec((VT_SLABS * V_ROWS, 1), lambda i: (0, 0)),
                  pl.BlockSpec((TM, 256), lambda i: (i % nblk, 0))],
        out_specs=[pl.BlockSpec((TM, P_WIDTH), lambda i: (i, 0)),
                   pl.BlockSpec((VT_SLABS * V_ROWS, TM), lambda i: (0, i))],
        out_shape=[jax.ShapeDtypeStruct((T, P_WIDTH), BF16),
                   jax.ShapeDtypeStruct((VT_SLABS * V_ROWS, T), BF16)],
        compiler_params=_params("parallel"),
        name="inproj",
    )(x, g, w, wvt, jnp.asarray(vone), _key_constants(S))


def _build_w_in(w_in):
    cuts = np.cumsum([0, 256, 128, 128, 256, 128, 32, 512, 128, 128, 128, 128, 128, 128, 24])
    seg = [w_in[:, cuts[i]:cuts[i + 1]] for i in range(14)]
    a_q, a_k, a_v, b_cq, b_ckv, b_kr, c_q, c_kc, c_vc, c_ks, c_vs, c_kw, c_vw, c_g = seg
    z = lambda n: jnp.zeros((w_in.shape[0], n), w_in.dtype)
    half = B_ROPE // 2
    kr_rot = jnp.concatenate([-b_kr[:, half:], b_kr[:, :half]], axis=1)
    cols = [c_q, a_q, b_cq, b_ckv, c_kc, c_vc,
            z(B_NOPE), b_kr, z(LANES - B_NOPE - B_ROPE),
            z(B_NOPE), kr_rot, z(LANES - B_NOPE - B_ROPE),
            c_g, z(LANES - 24)]
    for k, kd in ((c_ks, KD_SEL), (c_kw, KD_WIN), (a_k, KD_A)):
        for g in range(2):
            cols += [k[:, g * HEAD_DIM:(g + 1) * HEAD_DIM], z(kd - HEAD_DIM)]
    w = jnp.concatenate(cols, axis=1).astype(BF16)
    rows = []
    for v in (c_vs, c_vw, a_v):
        for g in range(2):
            rows += [v[:, g * HEAD_DIM:(g + 1) * HEAD_DIM].T, jnp.zeros((V_ROWS - HEAD_DIM, w_in.shape[0]), w_in.dtype)]
    return w, jnp.concatenate(rows, axis=0).astype(BF16)


def _rope_tab_kernel(pos_ref, inv_ref, cos_ref, sin_ref):
    ang = pos_ref[...] * inv_ref[...]
    cos_ref[...] = jnp.cos(ang)
    sin_ref[...] = jnp.sin(ang)


def _rope_tables(positions):
    T = positions.size
    inv = (ROPE_THETA ** (-np.arange(0, B_ROPE, 2, dtype=np.float32) / B_ROPE)).astype(np.float32)
    inv128 = np.zeros((1, LANES), np.float32)
    inv128[0, B_NOPE:B_NOPE + B_ROPE // 2] = inv
    inv128[0, B_NOPE + B_ROPE // 2:B_NOPE + B_ROPE] = inv
    pos = positions.reshape(T, 1).astype(F32)
    return pl.pallas_call(
        _rope_tab_kernel,
        grid=(T // TM_PROJ,),
        in_specs=[pl.BlockSpec((TM_PROJ, 1), lambda i: (i, 0)),
                  pl.BlockSpec((1, LANES), lambda i: (0, 0))],
        out_specs=[pl.BlockSpec((TM_PROJ, LANES), lambda i: (i, 0))] * 2,
        out_shape=[jax.ShapeDtypeStruct((T, LANES), F32)] * 2,
        compiler_params=_params("parallel"),
        name="rope_tables",
    )(pos, jnp.asarray(inv128))


def _flash_kernel(*refs, TQ, TK, G, R, NKT, slopes, scale, sel, gate_branch):
    refs = list(refs)
    lst_ref, cnt_ref = (refs.pop(0), refs.pop(0)) if sel else (None, None)
    q_ref = refs.pop(0)
    selb_ref = refs.pop(0) if sel else None
    k_refs = [refs.pop(0) for _ in range(G)]
    v_refs = [refs.pop(0) for _ in range(G)]
    gate_ref = refs.pop(0) if gate_branch is not None else None
    o_ref = refs.pop(0)
    qa_sc = refs.pop(0) if slopes is not None else None
    m_sc, acc_sc, s_sc, mx_sc = refs
    i = pl.program_id(1)
    step = pl.program_id(0) * pl.num_programs(1) + i
    t0 = i * TQ
    RQ = R * TQ
    lane = lax.broadcasted_iota(jnp.int32, (1, RQ), 1)
    qpos = t0 + (lane & (TQ - 1))
    hi = (t0 + TQ + TK - 1) // TK
    if gate_ref is not None:
        gate_t = _sigmoid(gate_ref[...].astype(F32)).T
    slope_vecs = []
    if slopes is not None:
        q_t = q_ref[...].astype(F32).T
        frow = lax.broadcasted_iota(jnp.int32, (HEAD_DIM, TQ), 0)
    for g in range(G):
        if slopes is not None:
            slope_vec = jnp.zeros((1, RQ), F32)
            for r in range(R):
                hh = g * R + r
                cols = slice(r * TQ, (r + 1) * TQ)
                s3 = _split_bf16(slopes[hh] * LOG2E)
                slope_vec = jnp.where(lane // TQ == r, sum(s3), slope_vec)
                qa_sc[g, 0:HEAD_DIM, cols] = (q_t[hh * HEAD_DIM:(hh + 1) * HEAD_DIM, :] * (scale * LOG2E)).astype(BF16)
                qa_sc[g, HEAD_DIM:2 * HEAD_DIM, cols] = jnp.where(
                    frow == 0, s3[0], jnp.where(frow == 1, s3[1], jnp.where(frow == 2, s3[2], 0.0))).astype(BF16)
                if sel:
                    qa_sc[g, 2 * HEAD_DIM:3 * HEAD_DIM, cols] = selb_ref[g * HEAD_DIM:(g + 1) * HEAD_DIM, :]
                    qa_sc[g, 3 * HEAD_DIM:4 * HEAD_DIM, cols] = jnp.zeros((HEAD_DIM, TQ), BF16)
            slope_vecs.append(slope_vec)
        else:
            slope_vecs.append(None)
        m_sc[g] = jnp.full((1, RQ), M_INIT, F32)
        acc_sc[g] = jnp.zeros((V_ROWS, RQ), F32)

    def scores(j, slot):
        ks = pl.multiple_of(jnp.asarray(j, jnp.int32) * TK, TK)
        for g in range(G):
            qa_t = qa_sc[g] if slopes is not None else q_ref[g * LANES:(g + 1) * LANES, :]
            st = _dot(k_refs[g][pl.ds(ks, TK), :], qa_t)
            s_sc[slot, g] = st
            mx_sc[slot, g] = jnp.max(st, axis=0, keepdims=True)

    def tile(j, slot, causal):
        ks = pl.multiple_of(jnp.asarray(j, jnp.int32) * TK, TK)
        for g in range(G):
            st = s_sc[slot, g]
            if causal:
                kpos = ks + lax.broadcasted_iota(jnp.int32, (TK, 1), 0)
                st = jnp.where(kpos <= qpos, st, MASK_BIAS)
            m = m_sc[g]
            mx = jnp.max(st, axis=0, keepdims=True) if causal else mx_sc[slot, g]
            if slopes is not None:
                c = slope_vecs[g] * ks.astype(F32)
                mn = jnp.maximum(m, mx + c)
                p = jnp.exp2(st - (mn - c))
            else:
                mn = jnp.maximum(m, mx)
                p = jnp.exp2(st - mn)
            acc_sc[g] = jnp.exp2(m - mn) * acc_sc[g] + _dot(v_refs[g][:, pl.ds(ks, TK)], p.astype(BF16))
            m_sc[g] = mn

    if sel:
        n = cnt_ref[step]
        jt = lambda t: lst_ref[step * NKT + t]
    else:
        n = hi
        jt = lambda t: t
    n_pair = (n - 1) // 2

    def pair(p, carry):
        t = 2 * p
        scores(jt(t + 1), 1)
        tile(jt(t), 0, False)
        scores(jt(t + 2), 0)
        tile(jt(t + 1), 1, False)
        return carry

    scores(jt(0), 0)
    lax.fori_loop(0, n_pair, pair, 0)

    @pl.when(2 * n_pair == n - 1)
    def _():
        tile(jt(n - 1), 0, True)

    @pl.when(2 * n_pair == n - 2)
    def _():
        scores(jt(n - 1), 1)
        tile(jt(n - 2), 0, False)
        tile(jt(n - 1), 1, True)

    pieces = []
    for g in range(G):
        acc = acc_sc[g]
        num = acc[0:HEAD_DIM, :]
        den = acc[HEAD_DIM:HEAD_DIM + 1, :]
        o_t = num * (1.0 / den)
        for r in range(R):
            piece = o_t[:, r * TQ:(r + 1) * TQ]
            if gate_ref is not None:
                c = 3 * (g * R + r) + gate_branch
                piece = piece * gate_t[c:c + 1, :]
            pieces.append(piece)
    o_ref[...] = jnp.concatenate(pieces, axis=0).T.astype(o_ref.dtype)


def _banded_kernel(*refs, TQ, NSUB, G, R, W, slopes, scale, has_sink, gate_branch):
    refs = list(refs)
    q_ref = refs.pop(0)
    k_refs = [refs.pop(0) for _ in range(G)]
    v_refs = [refs.pop(0) for _ in range(G)]
    sink_ref = refs.pop(0) if has_sink else None
    gate_ref = refs.pop(0) if gate_branch is not None else None
    o_ref = refs.pop(0)
    qa_sc, m_sc, acc_sc, s_sc = refs
    TK = TQ
    RQ = R * TQ
    n_st = W // TK + 1
    i = pl.program_id(1)
    lane = lax.broadcasted_iota(jnp.int32, (1, RQ), 1)
    frow = lax.broadcasted_iota(jnp.int32, (HEAD_DIM, TQ), 0)

    def process(sub, check):
        qt = i * NSUB + sub
        t0 = qt * TQ
        rows = slice(sub * TQ, (sub + 1) * TQ)
        qpos = t0 + (lane & (TQ - 1))
        if gate_ref is not None:
            gate_t = _sigmoid(gate_ref[rows, :].astype(F32)).T
        q_t = q_ref[rows, :].astype(F32).T
        slope_vecs = []
        for g in range(G):
            slope_vec = jnp.zeros((1, RQ), F32)
            for r in range(R):
                hh = g * R + r
                cols = slice(r * TQ, (r + 1) * TQ)
                s3 = _split_bf16(slopes[hh] * LOG2E)
                slope_vec = jnp.where(lane // TQ == r, sum(s3), slope_vec)
                qa_sc[sub, g, 0:HEAD_DIM, cols] = (
                    q_t[hh * HEAD_DIM:(hh + 1) * HEAD_DIM, :] * (scale * LOG2E)).astype(BF16)
                qa_sc[sub, g, HEAD_DIM:2 * HEAD_DIM, cols] = jnp.where(
                    frow == 0, s3[0], jnp.where(frow == 1, s3[1], jnp.where(frow == 2, s3[2], 0.0))).astype(BF16)
            slope_vecs.append(slope_vec)
            m_sc[sub, g] = jnp.full((1, RQ), M_INIT, F32)
            acc_sc[sub, g] = jnp.zeros((V_ROWS, RQ), F32)

        def scores(j, slot):
            ks = pl.multiple_of(j * TK, TK)
            for g in range(G):
                s_sc[sub, slot, g] = _dot(k_refs[g][pl.ds(ks, TK), :], qa_sc[sub, g])

        def tile(j, slot, mask, live):
            ks = pl.multiple_of(j * TK, TK)
            for g in range(G):
                st = s_sc[sub, slot, g]
                if mask is not None:
                    kpos = ks + lax.broadcasted_iota(jnp.int32, (TK, 1), 0)
                    ok = kpos <= qpos if mask == "causal" else kpos > qpos - W
                    st = jnp.where(ok, st, MASK_BIAS)
                if live is not None:
                    st = jnp.where(live, st, MASK_BIAS)
                m = m_sc[sub, g]
                c = slope_vecs[g] * ks.astype(F32)
                mn = jnp.maximum(m, jnp.max(st, axis=0, keepdims=True) + c)
                p = jnp.exp2(st - (mn - c))
                acc_sc[sub, g] = (jnp.exp2(m - mn) * acc_sc[sub, g]
                                  + _dot(v_refs[g][:, pl.ds(ks, TK)], p.astype(BF16)))
                m_sc[sub, g] = mn

        js = [qt - n_st + 1 + t for t in range(n_st)]
        jc = [jnp.maximum(j, 0) for j in js] if check else js
        scores(jc[0], 0)
        for t in range(n_st):
            if t + 1 < n_st:
                scores(jc[t + 1], (t + 1) & 1)
            mask = "causal" if t == n_st - 1 else ("window" if t == 0 else None)
            tile(jc[t], t & 1, mask, (js[t] >= 0) if check and t < n_st - 1 else None)

        pieces = []
        for g in range(G):
            acc = acc_sc[sub, g]
            num = acc[0:HEAD_DIM, :]
            den = acc[HEAD_DIM:HEAD_DIM + 1, :]
            if has_sink:
                sink_vec = jnp.zeros((1, RQ), F32)
                for r in range(R):
                    sink_vec = jnp.where(lane // TQ == r, sink_ref[g * R + r] * LOG2E, sink_vec)
                m_true = m_sc[sub, g] - slope_vecs[g] * qpos.astype(F32)
                mf = jnp.maximum(m_true, sink_vec)
                a = jnp.exp2(m_true - mf)
                num = num * a
                den = den * a + jnp.exp2(sink_vec - mf)
            o_t = num * (1.0 / den)
            for r in range(R):
                piece = o_t[:, r * TQ:(r + 1) * TQ]
                if gate_ref is not None:
                    c = 3 * (g * R + r) + gate_branch
                    piece = piece * gate_t[c:c + 1, :]
                pieces.append(piece)
        o_ref[rows, :] = jnp.concatenate(pieces, axis=0).T.astype(o_ref.dtype)

    all_tiles_exist = i * NSUB >= n_st - 1

    @pl.when(all_tiles_exist)
    def _():
        for sub in range(NSUB):
            process(sub, False)

    @pl.when(jnp.logical_not(all_tiles_exist))
    def _():
        for sub in range(NSUB):
            process(sub, True)


def _banded_attn(P, VT, B, S, *, name, NSUB, G, R, KD, q_col, k_col, vt_slab, W, slopes, scale, sinks=None,
                 gate_branch=None):
    T = B * S
    TQ = TQ_ATT
    TB = NSUB * TQ
    nq = S // TB
    RQ = R * TQ
    assert TK_BAND == TQ and W % TQ == 0 and S % TB == 0 and KD == 2 * HEAD_DIM
    QW = G * R * HEAD_DIM
    in_specs = [pl.BlockSpec((TB, QW), lambda b, i: (b * nq + i, q_col // QW))]
    args = [P]
    for g in range(G):
        in_specs.append(pl.BlockSpec((S, KD), lambda b, i, g=g: (b, k_col // KD + g)))
        args.append(P)
    for g in range(G):
        in_specs.append(pl.BlockSpec((V_ROWS, S), lambda b, i, g=g: (vt_slab + g, b)))
        args.append(VT)
    if sinks is not None:
        in_specs.append(pl.BlockSpec(memory_space=pltpu.SMEM))
        args.append(sinks.astype(F32))
    if gate_branch is not None:
        in_specs.append(pl.BlockSpec((TB, LANES), lambda b, i: (b * nq + i, P_GATE // LANES)))
        args.append(P)
    kern = functools.partial(_banded_kernel, TQ=TQ, NSUB=NSUB, G=G, R=R, W=W, slopes=slopes, scale=scale,
                             has_sink=sinks is not None, gate_branch=gate_branch)
    return pl.pallas_call(
        kern,
        grid=(B, nq),
        in_specs=in_specs,
        out_specs=pl.BlockSpec((TB, QW), lambda b, i: (b * nq + i, 0)),
        out_shape=jax.ShapeDtypeStruct((T, QW), BF16),
        scratch_shapes=[pltpu.VMEM((NSUB, G, KD, RQ), BF16), pltpu.VMEM((NSUB, G, 1, RQ), F32),
                        pltpu.VMEM((NSUB, G, V_ROWS, RQ), F32), pltpu.VMEM((NSUB, 2, G, TQ, RQ), F32)],
        compiler_params=_params("parallel", "parallel"),
        name=name,
    )(*args)


def _flash_attn(q, k, vt, B, S, *, name, TQ, TK, G, R, KD, q_col, q_width, k_col, vt_slab, slopes=None, scale=1.0,
                selb=None, tiles=None, gate=None, gate_branch=None):
    T = B * S
    nq = S // TQ
    RQ = R * TQ
    assert TK % TQ == 0 and S % TK == 0
    OW = G * R * HEAD_DIM
    if slopes is not None:
        in_specs = [pl.BlockSpec((TQ, q_width), lambda b, i, *_: (b * nq + i, q_col // q_width))]
    else:
        in_specs = [pl.BlockSpec((G * KD, TQ), lambda b, i, *_: (0, b * nq + i))]
    args = [q]
    if selb is not None:
        in_specs.append(pl.BlockSpec((LANES, TQ), lambda b, i, *_: (0, b * nq + i)))
        args.append(selb)
    for g in range(G):
        in_specs.append(pl.BlockSpec((S, KD), lambda b, i, *_, g=g: (b, k_col // KD + g)))
        args.append(k)
    for g in range(G):
        in_specs.append(pl.BlockSpec((V_ROWS, S), lambda b, i, *_, g=g: (vt_slab + g, b)))
        args.append(vt)
    if gate_branch is not None:
        in_specs.append(pl.BlockSpec((TQ, LANES), lambda b, i, *_: (b * nq + i, P_GATE // LANES)))
        args.append(gate)
    scratch = []
    if slopes is not None:
        scratch.append(pltpu.VMEM((G, KD, RQ), BF16))
    scratch += [pltpu.VMEM((G, 1, RQ), F32), pltpu.VMEM((G, V_ROWS, RQ), F32), pltpu.VMEM((2, G, TK, RQ), F32),
                pltpu.VMEM((2, G, 1, RQ), F32)]
    kern = functools.partial(_flash_kernel, TQ=TQ, TK=TK, G=G, R=R, NKT=S // TK, slopes=slopes, scale=scale,
                             sel=selb is not None, gate_branch=gate_branch)
    prefetch = list(tiles) if selb is not None else []
    grid_spec = pltpu.PrefetchScalarGridSpec(
        num_scalar_prefetch=len(prefetch),
        grid=(B, nq),
        in_specs=in_specs,
        out_specs=pl.BlockSpec((TQ, OW), lambda b, i, *_: (b * nq + i, 0)),
        scratch_shapes=scratch)
    return pl.pallas_call(
        kern,
        grid_spec=grid_spec,
        out_shape=jax.ShapeDtypeStruct((T, OW), BF16),
        compiler_params=_params("parallel", "parallel"),
        name=name,
    )(*prefetch, *args)


def _mla_prep_kernel(cq_ref, ckv_ref, kr_ref, krr_ref, cos_ref, sin_ref, gq_ref, gkv_ref,
                     wq_ref, wqr_ref, wk_ref, wvt_ref, vone_ref, q_out, k_out, vt_out):
    cos = cos_ref[...]
    sin = sin_ref[...]
    scale = (B_NOPE + B_ROPE) ** -0.5 * LOG2E
    nq = _rms(cq_ref[...].astype(F32), gq_ref[...]).astype(BF16)
    nkv = _rms(ckv_ref[...].astype(F32), gkv_ref[...]).astype(BF16)
    q1 = _dot_nt(wq_ref[...], nq)
    q2 = _dot_nt(wqr_ref[...], nq)
    kk = _dot(nkv, wk_ref[...])
    krope = kr_ref[...].astype(F32) * cos + krr_ref[...].astype(F32) * sin
    cos_t = cos.T
    sin_t = sin.T
    for h in range(B_HEADS):
        sl = slice(h * LANES, (h + 1) * LANES)
        q_out[sl, :] = ((q1[sl, :] * cos_t + q2[sl, :] * sin_t) * scale).astype(BF16)
        k_out[:, sl] = (kk[:, sl] + krope).astype(BF16)
    vt_out[...] = (_dot_nt(wvt_ref[...], nkv) + vone_ref[...]).astype(BF16)


def _build_mla_weights(w_uq, w_ukv):
    R1, R2 = w_uq.shape[0], w_ukv.shape[0]
    half = B_ROPE // 2
    hq = B_NOPE + B_ROPE
    wq, wqr, wk, wvt = [], [], [], []
    for h in range(B_HEADS):
        nope = w_uq[:, h * hq:h * hq + B_NOPE]
        rope = w_uq[:, h * hq + B_NOPE:(h + 1) * hq]
        rot = jnp.concatenate([-rope[:, half:], rope[:, :half]], axis=1)
        pad = jnp.zeros((R1, LANES - hq), w_uq.dtype)
        wq += [nope, rope, pad]
        wqr += [jnp.zeros((R1, B_NOPE), w_uq.dtype), rot, pad]
        wk += [w_ukv[:, h * 128:h * 128 + B_NOPE], jnp.zeros((R2, LANES - B_NOPE), w_ukv.dtype)]
        wvt += [w_ukv[:, h * 128 + B_NOPE:(h + 1) * 128].T, jnp.zeros((V_ROWS - HEAD_DIM, R2), w_ukv.dtype)]
    cat = lambda xs: jnp.concatenate(xs, axis=1).astype(BF16)
    return cat(wq).T, cat(wqr).T, cat(wk), jnp.concatenate(wvt, axis=0).astype(BF16)


def _mla_prep(P, cos, sin, g_cq, g_ckv, w_uq, w_ukv):
    T = P.shape[0]
    TM = TM_PROJ
    wq, wqr, wk, wvt = _build_mla_weights(w_uq, w_ukv)
    QW = B_HEADS * LANES
    VR = B_HEADS * V_ROWS
    vone = np.zeros((VR, 1), np.float32)
    vone[HEAD_DIM::V_ROWS] = 1.0
    full = lambda shp: pl.BlockSpec(shp, lambda i: (0, 0))
    return pl.pallas_call(
        _mla_prep_kernel,
        grid=(T // TM,),
        in_specs=[pl.BlockSpec((TM, B_Q_RANK), lambda i: (i, P_BCQ // B_Q_RANK)),
                  pl.BlockSpec((TM, B_KV_RANK), lambda i: (i, P_BCKV // B_KV_RANK)),
                  pl.BlockSpec((TM, LANES), lambda i: (i, P_KR // LANES)),
                  pl.BlockSpec((TM, LANES), lambda i: (i, P_KRROT // LANES)),
                  pl.BlockSpec((TM, LANES), lambda i: (i, 0)),
                  pl.BlockSpec((TM, LANES), lambda i: (i, 0)),
                  full((1, B_Q_RANK)), full((1, B_KV_RANK)),
                  full((QW, B_Q_RANK)), full((QW, B_Q_RANK)), full((B_KV_RANK, QW)), full((VR, B_KV_RANK)),
                  full((VR, 1))],
        out_specs=[pl.BlockSpec((QW, TM), lambda i: (0, i)),
                   pl.BlockSpec((TM, QW), lambda i: (i, 0)),
                   pl.BlockSpec((VR, TM), lambda i: (0, i))],
        out_shape=[jax.ShapeDtypeStruct((QW, T), BF16), jax.ShapeDtypeStruct((T, QW), BF16),
                   jax.ShapeDtypeStruct((VR, T), BF16)],
        compiler_params=_params("parallel"),
        name="mla_prep",
    )(P, P, P, P, cos, sin, g_cq.reshape(1, -1), g_ckv.reshape(1, -1), wq, wqr, wk, wvt, jnp.asarray(vone))


def _gelu_tanh(x):
    return 0.5 * x * (1.0 + jnp.tanh(np.float32(np.sqrt(2.0 / np.pi)) * (x + 0.044715 * (x * x * x))))


def _compress_kernel(xk_ref, xv_ref, pek_ref, pev_ref, w1k_ref, w1v_ref, w2k_ref, w2v_ref, kc_ref, vct_ref, *, NC):
    for x_ref, pe_ref, w1_ref, w2_ref, is_v in ((xk_ref, pek_ref, w1k_ref, w2k_ref, False),
                                                 (xv_ref, pev_ref, w1v_ref, w2v_ref, True)):
        x = x_ref[...].astype(F32)
        a = _dot((x + pe_ref[0:1, :]).astype(BF16), w1_ref[0])
        b = _dot((x + pe_ref[1:2, :]).astype(BF16), w1_ref[1])
        hid = a + pltpu.roll(b, NC - 1, 0)
        o = _dot(_gelu_tanh(hid).astype(BF16), w2_ref[...])
        n = lax.broadcasted_iota(jnp.int32, o.shape, 0)
        o = jnp.where(n < NC - 1, o, 0.0)
        if is_v:
            vct_ref[...] = o.T.astype(vct_ref.dtype)
        else:
            kc_ref[...] = o.astype(kc_ref.dtype)


def _build_compress_weights(w1, w2, pe):
    half = C_CMP_BLOCK // 2
    G = C_KV_HEADS
    eye = jnp.eye(G, dtype=w1.dtype)
    w1s = [jnp.einsum('lde,hg->lhdge', w1[c * half:(c + 1) * half], eye).reshape(half * G * HEAD_DIM, G * C_CMP_HIDDEN)
           for c in range(2)]
    w1f = jnp.stack(w1s).astype(BF16)
    w2f = jnp.einsum('ed,hg->hegd', w2, eye).reshape(G * C_CMP_HIDDEN, G * HEAD_DIM).astype(BF16)
    pes = [jnp.broadcast_to(pe[c * half:(c + 1) * half, None, :], (half, G, HEAD_DIM)).reshape(1, -1) for c in range(2)]
    pef = jnp.concatenate(pes, axis=0).astype(F32)
    return w1f, w2f, pef


def _compress(P, B, S, w1k, w2k, pek, w1v, w2v, pev):
    NC = S // C_CMP_STRIDE
    CW = C_CMP_STRIDE * C_KV_HEADS * HEAD_DIM
    xk = P[:, P_CKC:P_CKC + LANES].reshape(B * NC, CW)
    xv = P[:, P_CVC:P_CVC + LANES].reshape(B * NC, CW)
    w1kf, w2kf, pekf = _build_compress_weights(w1k, w2k, pek)
    w1vf, w2vf, pevf = _build_compress_weights(w1v, w2v, pev)
    xspec = pl.BlockSpec((NC, CW), lambda b: (b, 0))
    full2 = lambda a: pl.BlockSpec(a.shape, lambda b: (0, 0))
    full3 = lambda a: pl.BlockSpec(a.shape, lambda b: (0, 0, 0))
    return pl.pallas_call(
        functools.partial(_compress_kernel, NC=NC),
        grid=(B,),
        in_specs=[xspec, xspec, full2(pekf), full2(pevf), full3(w1kf), full3(w1vf), full2(w2kf), full2(w2vf)],
        out_specs=[pl.BlockSpec((NC, LANES), lambda b: (b, 0)), pl.BlockSpec((LANES, NC), lambda b: (b, 0))],
        out_shape=[jax.ShapeDtypeStruct((B * NC, LANES), BF16), jax.ShapeDtypeStruct((B * LANES, NC), BF16)],
        compiler_params=_params("parallel"),
        name="nsa_compress",
    )(xk, xv, pekf, pevf, w1kf, w1vf, w2kf, w2vf)


def _cmp_sel_kernel(q_ref, kc_ref, vct_ref, cfeat_ref, ovt_ref, gate_ref, ocmp_ref, sel_ref, cnt_ref,
                    *, TQ, NC, NSEL, NTOP, slopes):
    i = pl.program_id(1)
    t0 = i * TQ
    R = C_HEADS // C_KV_HEADS
    scale = HEAD_DIM ** -0.5
    gate_t = _sigmoid(gate_ref[...].astype(F32)).T
    last = C_CMP_BLOCK - 1
    tlane = t0 + lax.broadcasted_iota(jnp.int32, (NC, TQ), 1)
    nrow = lax.broadcasted_iota(jnp.int32, (NC, TQ), 0)
    cvalid = (nrow * C_CMP_STRIDE + last <= tlane) & (nrow < NC - 1)
    hasc = ((t0 + lax.broadcasted_iota(jnp.int32, (1, TQ), 1)) >= last).astype(F32)
    jrow = lax.broadcasted_iota(jnp.int32, (NSEL, TQ), 0)
    tl = t0 + lax.broadcasted_iota(jnp.int32, (NSEL, TQ), 1)
    avail = jrow * C_SEL_BLOCK <= tl
    cur = tl // C_SEL_BLOCK
    forced = (jrow == 0) | (jrow == cur) | (jrow == cur - 1)
    q_t = q_ref[...].astype(F32).T
    frow = lax.broadcasted_iota(jnp.int32, (HEAD_DIM, TQ), 0)
    sub8 = lax.broadcasted_iota(jnp.int32, (8, TQ), 0)
    pieces, sel_parts = [], []
    picked = jnp.zeros((NSEL, TQ), F32)
    for g in range(C_KV_HEADS):
        kca = jnp.concatenate([kc_ref[:, g * HEAD_DIM:(g + 1) * HEAD_DIM], cfeat_ref[...]], axis=1)
        vo = jnp.concatenate([vct_ref[g * HEAD_DIM:(g + 1) * HEAD_DIM, :], ovt_ref[...]], axis=0)
        imp = jnp.zeros((HEAD_DIM, TQ), F32)
        for r in range(R):
            hh = g * R + r
            s3 = _split_bf16(slopes[hh] * LOG2E)
            feat = jnp.zeros((HEAD_DIM, TQ), F32)
            for k in range(3):
                feat = jnp.where(frow == k, s3[k] * 256.0, jnp.where(frow == 3 + k, s3[k], feat))
            qa = jnp.concatenate([q_t[hh * HEAD_DIM:(hh + 1) * HEAD_DIM, :] * (scale * LOG2E), feat],
                                 axis=0).astype(BF16)
            st = jnp.where(cvalid, _dot(kca, qa), NEG_INF)
            et = jnp.exp2(st - jnp.max(st, axis=0, keepdims=True))
            inv = hasc / jnp.sum(et, axis=0, keepdims=True)
            res = _dot(vo, et.astype(BF16))
            pieces.append(res[0:HEAD_DIM, :] * (inv * gate_t[3 * hh:3 * hh + 1, :]))
            imp = imp + res[HEAD_DIM:2 * HEAD_DIM, :] * inv
        v = jnp.where(avail, imp[0:NSEL, :], NEG_INF)
        v = jnp.where(forced, -NEG_INF, v)
        ranks = [jnp.zeros((8, TQ), F32) for _ in range(NSEL // 8)]
        for ii in range(NSEL):
            ri = v[ii:ii + 1, :]
            for k in range(NSEL // 8):
                vk = v[8 * k:8 * k + 8, :]
                if 8 * k > ii:
                    inc = jnp.where(ri >= vk, 1.0, 0.0)
                elif 8 * k + 7 < ii:
                    inc = jnp.where(ri > vk, 1.0, 0.0)
                else:
                    inc = jnp.where(sub8 + 8 * k > ii, jnp.where(ri >= vk, 1.0, 0.0), jnp.where(ri > vk, 1.0, 0.0))
                ranks[k] = ranks[k] + inc
        rank = jnp.concatenate(ranks, axis=0)
        sel_parts.append(jnp.where(rank < NTOP, 0.0, MASK_BIAS))
        if NSEL < HEAD_DIM:
            sel_parts.append(jnp.zeros((HEAD_DIM - NSEL, TQ), F32))
        picked = picked + jnp.where(rank < NTOP, 1.0, 0.0)
    sel_ref[...] = jnp.concatenate(sel_parts, axis=0).astype(sel_ref.dtype)
    cnt_ref[...] = _dot_nt(jnp.ones((8, TQ), BF16), picked.astype(BF16))
    ocmp_ref[...] = jnp.concatenate(pieces, axis=0).T.astype(ocmp_ref.dtype)


def _overlap_t(S):
    n_cmp = (S - C_CMP_BLOCK) // C_CMP_STRIDE + 1
    n_sel = S // C_SEL_BLOCK
    NC = S // C_CMP_STRIDE
    cstart = np.arange(n_cmp) * C_CMP_STRIDE
    cend = cstart + C_CMP_BLOCK - 1
    sstart = np.arange(n_sel) * C_SEL_BLOCK
    send = sstart + C_SEL_BLOCK - 1
    ov = np.clip(np.minimum(cend[:, None], send[None, :]) - np.maximum(cstart[:, None], sstart[None, :]) + 1,
                 0, None).astype(np.float32) / C_CMP_STRIDE
    ovt = np.zeros((HEAD_DIM, NC), np.float32)
    ovt[:n_sel, :n_cmp] = ov.T
    cend_all = np.arange(NC) * C_CMP_STRIDE + C_CMP_BLOCK - 1
    cfeat = np.zeros((NC, HEAD_DIM), np.float32)
    cfeat[:, 0:3] = (cend_all // 256)[:, None]
    cfeat[:, 3:6] = (cend_all % 256)[:, None]
    return jnp.asarray(ovt, dtype=BF16), jnp.asarray(cfeat, dtype=BF16)


def _cmp_sel(P, kc, vct, B, S, slopes):
    T = B * S
    TQ = TQ_ATT
    nq = S // TQ
    NC = S // C_CMP_STRIDE
    NSEL = S // C_SEL_BLOCK
    assert NSEL <= HEAD_DIM and NSEL % 8 == 0 and S <= 256 * 256
    QW = C_HEADS * HEAD_DIM
    ovt, cfeat = _overlap_t(S)
    kern = functools.partial(_cmp_sel_kernel, TQ=TQ, NC=NC, NSEL=NSEL, NTOP=min(C_SEL_TOP, NSEL), slopes=slopes)
    return pl.pallas_call(
        kern,
        grid=(B, nq),
        in_specs=[pl.BlockSpec((TQ, QW), lambda b, i: (b * nq + i, P_CQ // QW)),
                  pl.BlockSpec((NC, LANES), lambda b, i: (b, 0)),
                  pl.BlockSpec((LANES, NC), lambda b, i: (b, 0)),
                  pl.BlockSpec((NC, HEAD_DIM), lambda b, i: (0, 0)),
                  pl.BlockSpec((HEAD_DIM, NC), lambda b, i: (0, 0)),
                  pl.BlockSpec((TQ, LANES), lambda b, i: (b * nq + i, P_GATE // LANES))],
        out_specs=[pl.BlockSpec((TQ, QW), lambda b, i: (b * nq + i, 0)),
                   pl.BlockSpec((LANES, TQ), lambda b, i: (0, b * nq + i)),
                   pl.BlockSpec((8, NSEL), lambda b, i: (b * nq + i, 0))],
        out_shape=[jax.ShapeDtypeStruct((T, QW), BF16), jax.ShapeDtypeStruct((LANES, T), BF16),
                   jax.ShapeDtypeStruct((B * nq * 8, NSEL), F32)],
        compiler_params=_params("parallel", "parallel"),
        name="nsa_cmp_select",
    )(P, kc, vct, cfeat, ovt, P)


def _selected_tile_lists(cnt, B, S):
    nq = S // TQ_ATT
    nkt = S // TK_SEL
    per = TK_SEL // C_SEL_BLOCK
    blk = cnt.reshape(B * nq, 8, -1)[:, 0, :] > 0.5
    need = jnp.any(blk.reshape(B * nq, nkt, per), axis=-1)
    own = ((jnp.arange(B * nq, dtype=jnp.int32) % nq) * TQ_ATT + TQ_ATT - 1) // TK_SEL
    jj = jnp.arange(nkt, dtype=jnp.int32)[None, :]
    need = (need | (jj == 0) | (jj == own[:, None])) & (jj <= own[:, None])
    order = jnp.argsort(jnp.where(need, jj, nkt + jj), axis=-1).astype(jnp.int32)
    return order.reshape(-1), jnp.sum(need, axis=-1).astype(jnp.int32)


def _outproj_kernel(x_ref, oa_ref, ob_ref, oc1_ref, oc2_ref, oc3_ref, ga_ref, gb_ref, gc_ref, wo_ref, gp_ref, o_ref):
    oa = _rms(oa_ref[...].astype(F32), ga_ref[...]).astype(BF16)
    ob = _rms(ob_ref[...].astype(F32), gb_ref[...]).astype(BF16)
    oc = oc1_ref[...].astype(F32) + oc2_ref[...].astype(F32) + oc3_ref[...].astype(F32)
    oc = _rms(oc, gc_ref[...]).astype(BF16)
    m = _dot(oa, wo_ref[0:256, :]) + _dot(ob, wo_ref[256:512, :]) + _dot(oc, wo_ref[512:1024, :])
    o_ref[...] = x_ref[...] + _rms(m, gp_ref[...])


def _outproj(x, o_a, o_b, o_cmp, o_slc, o_win, g_oa, g_ob, g_oc, w_o, g_post):
    T = x.shape[0]
    TM = TM_PROJ
    row = lambda w: pl.BlockSpec((TM, w), lambda i: (i, 0))
    full = lambda r, c: pl.BlockSpec((r, c), lambda i: (0, 0))
    return pl.pallas_call(
        _outproj_kernel,
        grid=(T // TM,),
        in_specs=[row(D_MODEL), row(256), row(256), row(512), row(512), row(512),
                  full(1, 256), full(1, 256), full(1, 512), full(D_MODEL, D_MODEL), full(1, D_MODEL)],
        out_specs=row(D_MODEL),
        out_shape=jax.ShapeDtypeStruct((T, D_MODEL), F32),
        compiler_params=_params("parallel"),
        name="outproj",
    )(x, o_a, o_b, o_cmp, o_slc, o_win, g_oa.reshape(1, -1), g_ob.reshape(1, -1), g_oc.reshape(1, -1),
      w_o.astype(BF16), g_post.reshape(1, -1))


def _swiglu_step(h, wg_ref, wu_ref, wd_ref, acc_ref):
    a = _dot(h, wg_ref[...])
    b = _dot(h, wu_ref[...])
    z = (a * _sigmoid(a) * b).astype(BF16)
    acc_ref[...] += _dot(z, wd_ref[...])


def _ffn_kernel(x_ref, gpre_ref, wg_ref, wu_ref, wd_ref, gpost_ref, o_ref, h_sc, acc_sc, *, nf):
    f = pl.program_id(1)

    @pl.when(f == 0)
    def _():
        h_sc[...] = _rms(x_ref[...], gpre_ref[...]).astype(BF16)
        acc_sc[...] = jnp.zeros_like(acc_sc)

    _swiglu_step(h_sc[...], wg_ref, wu_ref, wd_ref, acc_sc)

    @pl.when(f == nf - 1)
    def _():
        o_ref[...] = x_ref[...] + _rms(acc_sc[...], gpost_ref[...])


def _ffn(x, g_pre, wg, wu, wd, g_post):
    T = x.shape[0]
    TM, TF = TM_FFN, TF_FFN
    FF = wg.shape[1]
    nf = FF // TF
    return pl.pallas_call(
        functools.partial(_ffn_kernel, nf=nf),
        grid=(T // TM, nf),
        in_specs=[pl.BlockSpec((TM, D_MODEL), lambda i, f: (i, 0)),
                  pl.BlockSpec((1, D_MODEL), lambda i, f: (0, 0)),
                  pl.BlockSpec((D_MODEL, TF), lambda i, f: (0, f)),
                  pl.BlockSpec((D_MODEL, TF), lambda i, f: (0, f)),
                  pl.BlockSpec((TF, D_MODEL), lambda i, f: (f, 0)),
                  pl.BlockSpec((1, D_MODEL), lambda i, f: (0, 0))],
        out_specs=pl.BlockSpec((TM, D_MODEL), lambda i, f: (i, 0)),
        out_shape=jax.ShapeDtypeStruct((T, D_MODEL), F32),
        scratch_shapes=[pltpu.VMEM((TM, D_MODEL), BF16), pltpu.VMEM((TM, D_MODEL), F32)],
        compiler_params=_params("parallel", "arbitrary"),
        name="dense_ffn",
    )(x, g_pre.reshape(1, -1), wg.astype(BF16), wu.astype(BF16), wd.astype(BF16), g_post.reshape(1, -1))


def _expert_kernel(te_ref, nt_ref, x_ref, wg_ref, wu_ref, wd_ref, o_ref, acc_sc, *, nf):
    j = pl.program_id(0)
    f = pl.program_id(1)
    live = j < nt_ref[0]

    @pl.when(f == 0)
    def _():
        acc_sc[...] = jnp.zeros_like(acc_sc)

    @pl.when(live)
    def _():
        _swiglu_step(x_ref[...], wg_ref.at[0], wu_ref.at[0], wd_ref.at[0], acc_sc)

    @pl.when(f == nf - 1)
    def _():
        o_ref[...] = acc_sc[...].astype(o_ref.dtype)


def _experts(xs, tile_expert, n_tiles, wg, wu, wd):
    NT = xs.shape[0] // TM_MOE
    TM, TF = TM_MOE, TF_MOE
    assert wg.shape[2] % TF == 0
    FF = wg.shape[2]
    nf = FF // TF
    fidx = lambda j, f, nt: jnp.where(j < nt[0], f, nf - 1)
    grid_spec = pltpu.PrefetchScalarGridSpec(
        num_scalar_prefetch=2,
        grid=(NT, nf),
        in_specs=[pl.BlockSpec((TM, D_MODEL), lambda j, f, te, nt: (j, 0)),
                  pl.BlockSpec((1, D_MODEL, TF), lambda j, f, te, nt: (te[j], 0, fidx(j, f, nt))),
                  pl.BlockSpec((1, D_MODEL, TF), lambda j, f, te, nt: (te[j], 0, fidx(j, f, nt))),
                  pl.BlockSpec((1, TF, D_MODEL), lambda j, f, te, nt: (te[j], fidx(j, f, nt), 0))],
        out_specs=pl.BlockSpec((TM, D_MODEL), lambda j, f, te, nt: (j, 0)),
        scratch_shapes=[pltpu.VMEM((TM, D_MODEL), F32)])
    return pl.pallas_call(
        functools.partial(_expert_kernel, nf=nf),
        grid_spec=grid_spec,
        out_shape=jax.ShapeDtypeStruct(xs.shape, BF16),
        compiler_params=_params("parallel", "arbitrary"),
        name="moe_experts",
    )(tile_expert, n_tiles, xs, wg, wu, wd)


def _router_kernel(x_ref, g_ref, wr_ref, h_out, meta_out, metat_out, start_out, cnt_out, run_sc, *, TS):
    s = pl.program_id(0)

    @pl.when(s == 0)
    def _():
        run_sc[...] = jnp.zeros_like(run_sc)

    h = _rms(x_ref[...], g_ref[...])
    hb = h.astype(BF16)
    h_out[...] = hb
    hl = (h - hb.astype(F32)).astype(BF16)
    w = wr_ref[...]
    wh = w.astype(BF16)
    wl = (w - wh.astype(F32)).astype(BF16)
    logits = _dot(hb, wh) + _dot(hl, wh) + _dot(hb, wl)
    lane = lax.broadcasted_iota(jnp.int32, (TS, LANES), 1).astype(F32)
    logits = jnp.where(lane < N_EXPERTS, logits, NEG_INF)
    m1 = jnp.max(logits, axis=-1, keepdims=True)
    i1 = jnp.min(jnp.where(logits == m1, lane, float(LANES)), axis=-1, keepdims=True)
    rest = jnp.where(lane == i1, NEG_INF, logits)
    m2 = jnp.max(rest, axis=-1, keepdims=True)
    i2 = jnp.min(jnp.where(rest == m2, lane, float(LANES)), axis=-1, keepdims=True)
    ex = jnp.exp(m2 - m1)
    w1 = 1.0 / (1.0 + ex)
    w2 = ex / (1.0 + ex)
    oh1 = lane == i1
    oh2 = lane == i2
    oh = jnp.where(oh1, 1.0, 0.0) + jnp.where(oh2, 1.0, 0.0)
    tri = jnp.where(lax.broadcasted_iota(jnp.int32, (TS, TS), 0) > lax.broadcasted_iota(jnp.int32, (TS, TS), 1),
                    1.0, 0.0).astype(BF16)
    run = run_sc[...]
    before = _dot(tri, oh.astype(BF16)) + run
    r1 = jnp.sum(jnp.where(oh1, before, 0.0), axis=-1, keepdims=True)
    r2 = jnp.sum(jnp.where(oh2, before, 0.0), axis=-1, keepdims=True)
    start_out[0] = run
    run = run + jnp.sum(oh, axis=0, keepdims=True)
    run_sc[...] = run
    cnt_out[...] = run
    meta = jnp.where(lane == 0, i1, jnp.where(lane == 1, i2, jnp.where(lane == 2, r1, jnp.where(
        lane == 3, r2, jnp.where(lane == 4, w1, jnp.where(lane == 5, w2, 0.0))))))
    meta_out[...] = meta
    metat_out[...] = meta.T[0:8, :]


def _router(x, g, w_router):
    T = x.shape[0]
    TS = TS_MOE
    nt = T // TS
    wr = jnp.zeros((D_MODEL, LANES), F32).at[:, :N_EXPERTS].set(w_router.astype(F32))
    return pl.pallas_call(
        functools.partial(_router_kernel, TS=TS),
        grid=(nt,),
        in_specs=[pl.BlockSpec((TS, D_MODEL), lambda s: (s, 0)),
                  pl.BlockSpec((1, D_MODEL), lambda s: (0, 0)),
                  pl.BlockSpec((D_MODEL, LANES), lambda s: (0, 0))],
        out_specs=[pl.BlockSpec((TS, D_MODEL), lambda s: (s, 0)),
                   pl.BlockSpec((TS, LANES), lambda s: (s, 0)),
                   pl.BlockSpec((8, TS), lambda s: (0, s)),
                   pl.BlockSpec((1, 1, LANES), lambda s: (s, 0, 0)),
                   pl.BlockSpec((1, LANES), lambda s: (0, 0))],
        out_shape=[jax.ShapeDtypeStruct((T, D_MODEL), BF16),
                   jax.ShapeDtypeStruct((T, LANES), F32),
                   jax.ShapeDtypeStruct((8, T), F32),
                   jax.ShapeDtypeStruct((nt, 1, LANES), F32),
                   jax.ShapeDtypeStruct((1, LANES), F32)],
        scratch_shapes=[pltpu.VMEM((1, LANES), F32)],
        compiler_params=_params("arbitrary"),
        name="moe_router",
    )(x, g.reshape(1, -1), wr)


def _gather_kernel(te_ref, slo_ref, shi_ref, base_ref, metat_ref, h_ref, xs_out, acc_sc, *, TM, TS, CH):
    j = pl.program_id(0)
    e = te_ref[j]
    ef = e.astype(F32)
    nsub = TM // CH
    for sub in range(nsub):
        first = j * TM + sub * CH - base_ref[e]
        want = (first + lax.broadcasted_iota(jnp.int32, (CH, 1), 0)).astype(F32)
        acc_sc[...] = jnp.zeros_like(acc_sc)

        def body(s, _, want=want):
            off = pl.multiple_of(s * TS, TS)
            e1 = metat_ref[0:1, pl.ds(off, TS)]
            e2 = metat_ref[1:2, pl.ds(off, TS)]
            r1 = metat_ref[2:3, pl.ds(off, TS)]
            r2 = metat_ref[3:4, pl.ds(off, TS)]
            key = jnp.where(e1 == ef, r1, jnp.where(e2 == ef, r2, -1.0))
            sel = jnp.where(key == want, 1.0, 0.0).astype(BF16)
            acc_sc[...] += _dot(sel, h_ref[pl.ds(off, TS), :])
            return 0

        q = j * nsub + sub
        lax.fori_loop(slo_ref[q], shi_ref[q] + 1, body, 0)
        xs_out[sub * CH:(sub + 1) * CH, :] = acc_sc[...].astype(xs_out.dtype)


def _gather(hb, metat, tile_expert, s_lo, s_hi, base, NT):
    TM, TS = TM_MOE, TS_MOE
    assert hb.shape[0] % TS == 0
    grid_spec = pltpu.PrefetchScalarGridSpec(
        num_scalar_prefetch=4,
        grid=(NT,),
        in_specs=[pl.BlockSpec(memory_space=pltpu.VMEM), pl.BlockSpec(memory_space=pltpu.VMEM)],
        out_specs=pl.BlockSpec((TM, D_MODEL), lambda j, *_: (j, 0)),
        scratch_shapes=[pltpu.VMEM((CH_MOE, D_MODEL), F32)])
    return pl.pallas_call(
        functools.partial(_gather_kernel, TM=TM, TS=TS, CH=CH_MOE),
        grid_spec=grid_spec,
        out_shape=jax.ShapeDtypeStruct((NT * TM, D_MODEL), BF16),
        compiler_params=_params("arbitrary"),
        name="moe_gather",
    )(tile_expert, s_lo, s_hi, base, metat, hb)


def _combine_kernel(blk_ref, row0_ref, base_ref, meta_ref, x_ref, gpost_ref, ys_hbm, o_ref, buf, sem, selw_sc,
                    *, TS, CH, U, nt):
    s = pl.program_id(0)
    slot = s % 2

    def chunk_copies(step, slot_):
        out = []
        for u in range(U):
            r0 = pl.multiple_of(blk_ref[step * U + u] * CH, CH)
            out.append(pltpu.make_async_copy(ys_hbm.at[pl.ds(r0, CH), :], buf.at[slot_, pl.ds(u * CH, CH), :],
                                             sem.at[slot_]))
        return out

    @pl.when(s == 0)
    def _():
        for c in chunk_copies(0, 0):
            c.start()

    @pl.when(s + 1 < nt)
    def _():
        for c in chunk_copies(s + 1, 1 - slot):
            c.start()

    meta = meta_ref[...]
    e1, e2 = meta[:, 0:1], meta[:, 1:2]
    w1, w2 = meta[:, 4:5], meta[:, 5:6]
    b1 = jnp.zeros((TS, 1), F32)
    b2 = jnp.zeros((TS, 1), F32)
    for e in range(N_EXPERTS):
        be = base_ref[e].astype(F32)
        b1 = jnp.where(e1 == float(e), be, b1)
        b2 = jnp.where(e2 == float(e), be, b2)
    pos1 = b1 + meta[:, 2:3]
    pos2 = b2 + meta[:, 3:4]

    lane = lax.broadcasted_iota(jnp.int32, (1, CH), 1)
    for u in range(U):
        rows = (row0_ref[s * U + u] + lane).astype(F32)
        selw = jnp.where(pos1 == rows, w1, 0.0) + jnp.where(pos2 == rows, w2, 0.0)
        selw_sc[:, u * CH:(u + 1) * CH] = selw.astype(BF16)
    for c in chunk_copies(s, slot):
        c.wait()
    y = _dot(selw_sc[...], buf[slot])
    o_ref[...] = x_ref[...] + _rms(y, gpost_ref[...])


def _combine(x, meta, ys, chunk_blk, chunk_row0, base, g_post):
    T = x.shape[0]
    TS, CH, U = TS_MOE, CH_MOE, U_MOE
    nt = T // TS
    grid_spec = pltpu.PrefetchScalarGridSpec(
        num_scalar_prefetch=3,
        grid=(nt,),
        in_specs=[pl.BlockSpec((TS, LANES), lambda s, *_: (s, 0)),
                  pl.BlockSpec((TS, D_MODEL), lambda s, *_: (s, 0)),
                  pl.BlockSpec((1, D_MODEL), lambda s, *_: (0, 0)),
                  pl.BlockSpec(memory_space=pl.ANY)],
        out_specs=pl.BlockSpec((TS, D_MODEL), lambda s, *_: (s, 0)),
        scratch_shapes=[pltpu.VMEM((2, U * CH, D_MODEL), BF16), pltpu.SemaphoreType.DMA((2,)),
                        pltpu.VMEM((TS, U * CH), BF16)])
    return pl.pallas_call(
        functools.partial(_combine_kernel, TS=TS, CH=CH, U=U, nt=nt),
        grid_spec=grid_spec,
        out_shape=jax.ShapeDtypeStruct((T, D_MODEL), F32),
        compiler_params=_params("arbitrary"),
        name="moe_combine",
    )(chunk_blk, chunk_row0, base, meta, x, g_post.reshape(1, -1), ys)


def _moe(x, g_pre, w_router, w_gate, w_up, w_down, g_post):
    T = x.shape[0]
    TS, TM, CH, U = TS_MOE, TM_MOE, CH_MOE, U_MOE
    nt = T // TS
    NT = (2 * T) // TM + N_EXPERTS
    hb, meta, metat, start, cnt = _router(x, g_pre, w_router)

    i32 = jnp.int32
    counts = cnt[0, :N_EXPERTS].astype(i32)
    start = start[:, 0, :N_EXPERTS].astype(i32)
    start_ext = jnp.concatenate([start, counts[None, :]], axis=0)
    tiles_e = (counts + TM - 1) // TM
    tiles_cum = jnp.cumsum(tiles_e)
    base = (tiles_cum - tiles_e) * TM
    n_tiles = tiles_cum[-1:]
    jj = jnp.arange(NT, dtype=i32)
    tile_expert = jnp.minimum(jnp.sum(jj[:, None] >= tiles_cum[None, :], axis=1), N_EXPERTS - 1).astype(i32)
    live = jj < n_tiles[0]
    qq = jnp.arange(NT * (TM // CH), dtype=i32)
    q_exp = tile_expert[qq // (TM // CH)]
    a = qq * CH - base[q_exp]
    st_e = start_ext[:, q_exp]
    s_lo = jnp.sum(st_e[1:] <= a[None, :], axis=0).astype(i32)
    s_hi = (jnp.sum(st_e[:-1] < (a + CH)[None, :], axis=0) - 1).astype(i32)
    q_live = live[qq // (TM // CH)]
    s_lo = jnp.where(q_live, s_lo, 1)
    s_hi = jnp.where(q_live, s_hi, 0)

    xs = _gather(hb, metat, tile_expert, s_lo, s_hi, base.astype(i32), NT)
    ys = _experts(xs, tile_expert, n_tiles.astype(i32), w_gate.astype(BF16), w_up.astype(BF16), w_down.astype(BF16))

    lo = base[None, :] + start_ext[:-1]
    hi = base[None, :] + start_ext[1:]
    n_e = jnp.where(hi > lo, (hi - 1) // CH - lo // CH + 1, 0)
    cum = jnp.cumsum(n_e, axis=1)
    uu = jnp.arange(U, dtype=i32)
    e_u = jnp.minimum(jnp.sum(uu[None, :, None] >= cum[:, None, :], axis=2), N_EXPERTS - 1)
    first = jnp.take_along_axis(lo // CH, e_u, axis=1)
    skipped = jnp.take_along_axis(cum - n_e, e_u, axis=1)
    chunk_blk = jnp.clip(first + (uu[None, :] - skipped), 0, (NT * TM) // CH - 1).astype(i32)
    used = uu[None, :] < cum[:, -1:]
    chunk_blk = jnp.where(used, chunk_blk, chunk_blk[:, :1])
    chunk_row0 = jnp.where(used, chunk_blk * CH, -(NT * TM)).astype(i32)
    return _combine(x, meta, ys, chunk_blk.reshape(-1), chunk_row0.reshape(-1), base.astype(i32), g_post)


def kernel(x, positions, w_in, a_sinks, g_cq, w_uq, g_ckv, w_ukv, c_w1_k, c_w2_k, c_pe_k, c_w1_v, c_w2_v, c_pe_v,
           g_oa, g_ob, g_oc, w_o, g_pre_mix, g_post_mix, g_pre_ffn, g_post_ffn, ffn_w_gate, ffn_w_up, ffn_w_down,
           moe_router, moe_w_gate, moe_w_up, moe_w_down):
    B, S, _ = x.shape
    T = B * S
    depth = w_in.shape[0]
    a_slopes = _alibi_slopes(A_HEADS)
    c_slopes = _alibi_slopes(C_HEADS)
    xf = x.reshape(T, D_MODEL).astype(F32)
    cos, sin = _rope_tables(positions)
    att_scale = HEAD_DIM ** -0.5
    for l in range(depth):
        w_p, w_vt = _build_w_in(w_in[l])
        P, VT = _inproj(xf, g_pre_mix[l].reshape(1, -1), w_p, w_vt, S)
        o_a = _banded_attn(P, VT, B, S, name="swa_attn", NSUB=NSUB_A, G=A_KV_HEADS, R=A_HEADS // A_KV_HEADS,
                           KD=KD_A, q_col=P_AQ, k_col=P_KA, vt_slab=VT_A, W=A_WINDOW, slopes=a_slopes,
                           scale=att_scale, sinks=a_sinks[l])
        qB, kB, vtB = _mla_prep(P, cos, sin, g_cq[l], g_ckv[l], w_uq[l], w_ukv[l])
        o_b = _flash_attn(qB, kB, vtB, B, S, name="mla_attn", TQ=TQ_MLA, TK=TQ_MLA, G=B_HEADS, R=1, KD=LANES,
                          q_col=0, q_width=B_HEADS * LANES, k_col=0, vt_slab=0)
        kc, vct = _compress(P, B, S, c_w1_k[l], c_w2_k[l], c_pe_k[l], c_w1_v[l], c_w2_v[l], c_pe_v[l])
        o_cmp, selb, pick_cnt = _cmp_sel(P, kc, vct, B, S, c_slopes)
        c_args = dict(TQ=TQ_ATT, G=C_KV_HEADS, R=C_HEADS // C_KV_HEADS, q_col=P_CQ, q_width=C_HEADS * HEAD_DIM,
                      slopes=c_slopes, scale=att_scale, gate=P)
        o_slc = _flash_attn(P, P, VT, B, S, name="nsa_selected", TK=TK_SEL, KD=KD_SEL, k_col=P_KSEL,
                            vt_slab=VT_SEL, selb=selb, tiles=_selected_tile_lists(pick_cnt, B, S), gate_branch=1,
                            **c_args)
        o_win = _banded_attn(P, VT, B, S, name="nsa_window", NSUB=NSUB_WIN, G=C_KV_HEADS, R=C_HEADS // C_KV_HEADS,
                             KD=KD_WIN, q_col=P_CQ, k_col=P_KWIN, vt_slab=VT_WIN, W=C_WINDOW, slopes=c_slopes,
                             scale=att_scale, gate_branch=2)
        xf = _outproj(xf, o_a, o_b, o_cmp, o_slc, o_win, g_oa[l], g_ob[l], g_oc[l], w_o[l], g_post_mix[l])
        if l % 2 == 0:
            xf = _ffn(xf, g_pre_ffn[l], ffn_w_gate[l // 2], ffn_w_up[l // 2], ffn_w_down[l // 2], g_post_ffn[l])
        else:
            xf = _moe(xf, g_pre_ffn[l], moe_router[l // 2], moe_w_gate[l // 2], moe_w_up[l // 2],
                      moe_w_down[l // 2], g_post_ffn[l])
    return xf.reshape(B, S, D_MODEL)
```

```python
import functools

import numpy as np
import jax
import jax.numpy as jnp
from jax import lax
from jax.experimental import pallas as pl
from jax.experimental.pallas import tpu as pltpu

F32 = jnp.float32
BF16 = jnp.bfloat16

D_MODEL = 1024
HEAD_DIM = 64
NORM_EPS = 1e-6
NEG_INF = -1e30
ROPE_THETA = 10000.0
LOG2E = 1.4426950408889634

A_HEADS, A_KV_HEADS, A_WINDOW = 4, 2, 128
B_HEADS, B_Q_RANK, B_KV_RANK, B_NOPE, B_ROPE = 4, 256, 128, 64, 32
C_HEADS, C_KV_HEADS, C_WINDOW = 8, 2, 512
C_CMP_BLOCK, C_CMP_STRIDE, C_CMP_HIDDEN = 32, 16, 128
C_SEL_BLOCK, C_SEL_TOP = 64, 16
N_EXPERTS = 8

LANES = 128
VMEM_LIMIT = 48 * 1024 * 1024

P_CQ, P_AQ, P_BCQ = 0, 512, 768
P_BCKV, P_CKC, P_CVC, P_KR, P_KRROT, P_GATE = 1024, 1152, 1280, 1408, 1536, 1664
P_KSEL, KD_SEL = 1792, 256
P_KWIN, KD_WIN = 2304, 128
P_KA, KD_A = 2560, 128
P_WIDTH = 2816
V_ROWS = 80
VT_SEL, VT_WIN, VT_A = 0, 2, 4
VT_SLABS = 6
MASK_BIAS = -1e30
M_INIT = -1e29

TM_PROJ = 512
TQ_ATT = 128
TK_BAND = 128
NSUB_A = 4
NSUB_WIN = 4
TQ_MLA = 256
TK_SEL = 256
TM_FFN = 512
TF_FFN = 2048
TF_MOE = 1792
TS_MOE = 512
TM_MOE = 512
CH_MOE = 128
U_MOE = 24


def _alibi_slopes(n):
    return [float(np.float32(2.0 ** (-8.0 * (i + 1) / n))) for i in range(n)]


def _split_bf16(x):
    parts = []
    for _ in range(3):
        p = float(np.asarray(x, np.float32).astype(jnp.bfloat16))
        parts.append(p)
        x = x - p
    return parts


def _dot(a, b):
    return jnp.dot(a, b, preferred_element_type=F32)


def _dot_nt(a, b):
    return lax.dot_general(a, b, (((1,), (1,)), ((), ())), preferred_element_type=F32)


def _rms(x, g):
    return x * lax.rsqrt(jnp.mean(x * x, axis=-1, keepdims=True) + NORM_EPS) * g


def _sigmoid(x):
    return 1.0 / (1.0 + jnp.exp(-x))


def _params(*sem):
    return pltpu.CompilerParams(dimension_semantics=sem, vmem_limit_bytes=VMEM_LIMIT)


def _inproj_kernel(x_ref, g_ref, w_ref, wvt_ref, vone_ref, kc_ref, o_ref, vt_ref):
    h = _rms(x_ref[...], g_ref[...]).astype(BF16)
    for n in range(P_WIDTH // 256):
        o_ref[:, n * 256:(n + 1) * 256] = _dot(h, w_ref[:, n * 256:(n + 1) * 256]).astype(BF16)
    for g in range(C_KV_HEADS):
        o_ref[:, P_KSEL + g * KD_SEL + 64:P_KSEL + g * KD_SEL + 192] = kc_ref[:, 0:128]
        o_ref[:, P_KWIN + g * KD_WIN + 64:P_KWIN + (g + 1) * KD_WIN] = kc_ref[:, 128:192]
    for g in range(A_KV_HEADS):
        o_ref[:, P_KA + g * KD_A + 64:P_KA + (g + 1) * KD_A] = kc_ref[:, 128:192]
    vt_ref[...] = (_dot_nt(wvt_ref[...], h) + vone_ref[...]).astype(BF16)


def _key_constants(S):
    s = np.arange(S)
    kc = np.zeros((S, 256), np.float32)
    kc[:, 0:3] = (s % TK_SEL)[:, None]
    kc[s, 64 + s // C_SEL_BLOCK] = 1.0
    kc[:, 128:131] = (s % TK_BAND)[:, None]
    return jnp.asarray(kc, dtype=BF16)


def _inproj(x, g, w, wvt, S):
    T = x.shape[0]
    TM = TM_PROJ
    assert S // C_SEL_BLOCK <= 64 and TK_SEL <= 256 and TK_BAND <= 256
    vone = np.zeros((VT_SLABS * V_ROWS, 1), np.float32)
    vone[HEAD_DIM::V_ROWS] = 1.0
    nblk = S // TM
    return pl.pallas_call(
        _inproj_kernel,
        grid=(T // TM,),
        in_specs=[pl.BlockSpec((TM, D_MODEL), lambda i: (i, 0)),
                  pl.BlockSpec((1, D_MODEL), lambda i: (0, 0)),
                  pl.BlockSpec((D_MODEL, P_WIDTH), lambda i: (0, 0)),
                  pl.BlockSpec((VT_SLABS * V_ROWS, D_MODEL), lambda i: (0, 0)),
                  pl.BlockSpec((VT_SLABS * V_ROWS, 1), lambda i: (0, 0)),
                  pl.BlockSpec((TM, 256), lambda i: (i % nblk, 0))],
        out_specs=[pl.BlockSpec((TM, P_WIDTH), lambda i: (i, 0)),
                   pl.BlockSpec((VT_SLABS * V_ROWS, TM), lambda i: (0, i))],
        out_shape=[jax.ShapeDtypeStruct((T, P_WIDTH), BF16),
                   jax.ShapeDtypeStruct((VT_SLABS * V_ROWS, T), BF16)],
        compiler_params=_params("parallel"),
        name="inproj",
    )(x, g, w, wvt, jnp.asarray(vone), _key_constants(S))


def _build_w_in(w_in):
    cuts = np.cumsum([0, 256, 128, 128, 256, 128, 32, 512, 128, 128, 128, 128, 128, 128, 24])
    seg = [w_in[:, cuts[i]:cuts[i + 1]] for i in range(14)]
    a_q, a_k, a_v, b_cq, b_ckv, b_kr, c_q, c_kc, c_vc, c_ks, c_vs, c_kw, c_vw, c_g = seg
    z = lambda n: jnp.zeros((w_in.shape[0], n), w_in.dtype)
    half = B_ROPE // 2
    kr_rot = jnp.concatenate([-b_kr[:, half:], b_kr[:, :half]], axis=1)
    cols = [c_q, a_q, b_cq, b_ckv, c_kc, c_vc,
            z(B_NOPE), b_kr, z(LANES - B_NOPE - B_ROPE),
            z(B_NOPE), kr_rot, z(LANES - B_NOPE - B_ROPE),
            c_g, z(LANES - 24)]
    for k, kd in ((c_ks, KD_SEL), (c_kw, KD_WIN), (a_k, KD_A)):
        for g in range(2):
            cols += [k[:, g * HEAD_DIM:(g + 1) * HEAD_DIM], z(kd - HEAD_DIM)]
    w = jnp.concatenate(cols, axis=1).astype(BF16)
    rows = []
    for v in (c_vs, c_vw, a_v):
        for g in range(2):
            rows += [v[:, g * HEAD_DIM:(g + 1) * HEAD_DIM].T, jnp.zeros((V_ROWS - HEAD_DIM, w_in.shape[0]), w_in.dtype)]
    return w, jnp.concatenate(rows, axis=0).astype(BF16)


def _rope_tab_kernel(pos_ref, inv_ref, cos_ref, sin_ref):
    ang = pos_ref[...] * inv_ref[...]
    cos_ref[...] = jnp.cos(ang)
    sin_ref[...] = jnp.sin(ang)


def _rope_tables(positions):
    T = positions.size
    inv = (ROPE_THETA ** (-np.arange(0, B_ROPE, 2, dtype=np.float32) / B_ROPE)).astype(np.float32)
    inv128 = np.zeros((1, LANES), np.float32)
    inv128[0, B_NOPE:B_NOPE + B_ROPE // 2] = inv
    inv128[0, B_NOPE + B_ROPE // 2:B_NOPE + B_ROPE] = inv
    pos = positions.reshape(T, 1).astype(F32)
    return pl.pallas_call(
        _rope_tab_kernel,
        grid=(T // TM_PROJ,),
        in_specs=[pl.BlockSpec((TM_PROJ, 1), lambda i: (i, 0)),
                  pl.BlockSpec((1, LANES), lambda i: (0, 0))],
        out_specs=[pl.BlockSpec((TM_PROJ, LANES), lambda i: (i, 0))] * 2,
        out_shape=[jax.ShapeDtypeStruct((T, LANES), F32)] * 2,
        compiler_params=_params("parallel"),
        name="rope_tables",
    )(pos, jnp.asarray(inv128))


def _flash_kernel(*refs, TQ, TK, G, R, NKT, slopes, scale, sel, gate_branch, n_side):
    refs = list(refs)
    lst_ref, cnt_ref = (refs.pop(0), refs.pop(0)) if sel else (None, None)
    q_ref = refs.pop(0)
    selb_ref = refs.pop(0) if sel else None
    k_refs = [refs.pop(0) for _ in range(G)]
    v_refs = [refs.pop(0) for _ in range(G)]
    gate_ref = refs.pop(0) if gate_branch is not None else None
    side_in = [refs.pop(0) for _ in range(n_side)]
    o_ref = refs.pop(0)
    side_out = [refs.pop(0) for _ in range(n_side)]
    qa_sc = refs.pop(0) if slopes is not None else None
    m_sc, acc_sc, s_sc, mx_sc = refs
    for src, dst in zip(side_in, side_out):
        dst[...] = src[...].astype(BF16)
    i = pl.program_id(1)
    step = pl.program_id(0) * pl.num_programs(1) + i
    t0 = i * TQ
    RQ = R * TQ
    lane = lax.broadcasted_iota(jnp.int32, (1, RQ), 1)
    qpos = t0 + (lane & (TQ - 1))
    hi = (t0 + TQ + TK - 1) // TK
    if gate_ref is not None:
        gate_t = _sigmoid(gate_ref[...].astype(F32)).T
    slope_vecs = []
    if slopes is not None:
        q_t = q_ref[...].astype(F32).T
        frow = lax.broadcasted_iota(jnp.int32, (HEAD_DIM, TQ), 0)
    for g in range(G):
        if slopes is not None:
            slope_vec = jnp.zeros((1, RQ), F32)
            for r in range(R):
                hh = g * R + r
                cols = slice(r * TQ, (r + 1) * TQ)
                s3 = _split_bf16(slopes[hh] * LOG2E)
                slope_vec = jnp.where(lane // TQ == r, sum(s3), slope_vec)
                qa_sc[g, 0:HEAD_DIM, cols] = (q_t[hh * HEAD_DIM:(hh + 1) * HEAD_DIM, :] * (scale * LOG2E)).astype(BF16)
                qa_sc[g, HEAD_DIM:2 * HEAD_DIM, cols] = jnp.where(
                    frow == 0, s3[0], jnp.where(frow == 1, s3[1], jnp.where(frow == 2, s3[2], 0.0))).astype(BF16)
                if sel:
                    qa_sc[g, 2 * HEAD_DIM:3 * HEAD_DIM, cols] = selb_ref[g * HEAD_DIM:(g + 1) * HEAD_DIM, :]
                    qa_sc[g, 3 * HEAD_DIM:4 * HEAD_DIM, cols] = jnp.zeros((HEAD_DIM, TQ), BF16)
            slope_vecs.append(slope_vec)
        else:
            slope_vecs.append(None)
        m_sc[g] = jnp.full((1, RQ), M_INIT, F32)
        acc_sc[g] = jnp.zeros((V_ROWS, RQ), F32)

    def scores(j, slot):
        ks = pl.multiple_of(jnp.asarray(j, jnp.int32) * TK, TK)
        for g in range(G):
            qa_t = qa_sc[g] if slopes is not None else q_ref[g * LANES:(g + 1) * LANES, :]
            st = _dot(k_refs[g][pl.ds(ks, TK), :], qa_t)
            s_sc[slot, g] = st
            mx_sc[slot, g] = jnp.max(st, axis=0, keepdims=True)

    def tile(j, slot, causal):
        ks = pl.multiple_of(jnp.asarray(j, jnp.int32) * TK, TK)
        for g in range(G):
            st = s_sc[slot, g]
            if causal:
                kpos = ks + lax.broadcasted_iota(jnp.int32, (TK, 1), 0)
                st = jnp.where(kpos <= qpos, st, MASK_BIAS)
            m = m_sc[g]
            mx = jnp.max(st, axis=0, keepdims=True) if causal else mx_sc[slot, g]
            if slopes is not None:
                c = slope_vecs[g] * ks.astype(F32)
                mn = jnp.maximum(m, mx + c)
                p = jnp.exp2(st - (mn - c))
            else:
                mn = jnp.maximum(m, mx)
                p = jnp.exp2(st - mn)
            acc_sc[g] = jnp.exp2(m - mn) * acc_sc[g] + _dot(v_refs[g][:, pl.ds(ks, TK)], p.astype(BF16))
            m_sc[g] = mn

    if sel:
        n = cnt_ref[step]
        jt = lambda t: lst_ref[step * NKT + t]
    else:
        n = hi
        jt = lambda t: t
    n_pair = (n - 1) // 2

    def pair(p, carry):
        t = 2 * p
        scores(jt(t + 1), 1)
        tile(jt(t), 0, False)
        scores(jt(t + 2), 0)
        tile(jt(t + 1), 1, False)
        return carry

    scores(jt(0), 0)
    lax.fori_loop(0, n_pair, pair, 0)

    @pl.when(2 * n_pair == n - 1)
    def _():
        tile(jt(n - 1), 0, True)

    @pl.when(2 * n_pair == n - 2)
    def _():
        scores(jt(n - 1), 1)
        tile(jt(n - 2), 0, False)
        tile(jt(n - 1), 1, True)

    pieces = []
    for g in range(G):
        acc = acc_sc[g]
        num = acc[0:HEAD_DIM, :]
        den = acc[HEAD_DIM:HEAD_DIM + 1, :]
        o_t = num * (1.0 / den)
        for r in range(R):
            piece = o_t[:, r * TQ:(r + 1) * TQ]
            if gate_ref is not None:
                c = 3 * (g * R + r) + gate_branch
                piece = piece * gate_t[c:c + 1, :]
            pieces.append(piece)
    o_ref[...] = jnp.concatenate(pieces, axis=0).T.astype(o_ref.dtype)


def _banded_kernel(*refs, TQ, NSUB, G, R, W, slopes, scale, has_sink, gate_branch):
    refs = list(refs)
    q_ref = refs.pop(0)
    k_refs = [refs.pop(0) for _ in range(G)]
    v_refs = [refs.pop(0) for _ in range(G)]
    sink_ref = refs.pop(0) if has_sink else None
    gate_ref = refs.pop(0) if gate_branch is not None else None
    o_ref = refs.pop(0)
    qa_sc, m_sc, acc_sc, s_sc = refs
    TK = TQ
    RQ = R * TQ
    n_st = W // TK + 1
    i = pl.program_id(1)
    lane = lax.broadcasted_iota(jnp.int32, (1, RQ), 1)
    frow = lax.broadcasted_iota(jnp.int32, (HEAD_DIM, TQ), 0)

    def process(sub, check):
        qt = i * NSUB + sub
        t0 = qt * TQ
        rows = slice(sub * TQ, (sub + 1) * TQ)
        qpos = t0 + (lane & (TQ - 1))
        if gate_ref is not None:
            gate_t = _sigmoid(gate_ref[rows, :].astype(F32)).T
        q_t = q_ref[rows, :].astype(F32).T
        slope_vecs = []
        for g in range(G):
            slope_vec = jnp.zeros((1, RQ), F32)
            for r in range(R):
                hh = g * R + r
                cols = slice(r * TQ, (r + 1) * TQ)
                s3 = _split_bf16(slopes[hh] * LOG2E)
                slope_vec = jnp.where(lane // TQ == r, sum(s3), slope_vec)
                qa_sc[sub, g, 0:HEAD_DIM, cols] = (
                    q_t[hh * HEAD_DIM:(hh + 1) * HEAD_DIM, :] * (scale * LOG2E)).astype(BF16)
                qa_sc[sub, g, HEAD_DIM:2 * HEAD_DIM, cols] = jnp.where(
                    frow == 0, s3[0], jnp.where(frow == 1, s3[1], jnp.where(frow == 2, s3[2], 0.0))).astype(BF16)
            slope_vecs.append(slope_vec)
            m_sc[sub, g] = jnp.full((1, RQ), M_INIT, F32)
            acc_sc[sub, g] = jnp.zeros((V_ROWS, RQ), F32)

        def scores(j, slot):
            ks = pl.multiple_of(j * TK, TK)
            for g in range(G):
                s_sc[sub, slot, g] = _dot(k_refs[g][pl.ds(ks, TK), :], qa_sc[sub, g])

        def tile(j, slot, mask, live):
            ks = pl.multiple_of(j * TK, TK)
            for g in range(G):
                st = s_sc[sub, slot, g]
                if mask is not None:
                    kpos = ks + lax.broadcasted_iota(jnp.int32, (TK, 1), 0)
                    ok = kpos <= qpos if mask == "causal" else kpos > qpos - W
                    st = jnp.where(ok, st, MASK_BIAS)
                if live is not None:
                    st = jnp.where(live, st, MASK_BIAS)
                m = m_sc[sub, g]
                c = slope_vecs[g] * ks.astype(F32)
                mn = jnp.maximum(m, jnp.max(st, axis=0, keepdims=True) + c)
                p = jnp.exp2(st - (mn - c))
                acc_sc[sub, g] = (jnp.exp2(m - mn) * acc_sc[sub, g]
                                  + _dot(v_refs[g][:, pl.ds(ks, TK)], p.astype(BF16)))
                m_sc[sub, g] = mn

        js = [qt - n_st + 1 + t for t in range(n_st)]
        jc = [jnp.maximum(j, 0) for j in js] if check else js
        scores(jc[0], 0)
        for t in range(n_st):
            if t + 1 < n_st:
                scores(jc[t + 1], (t + 1) & 1)
            mask = "causal" if t == n_st - 1 else ("window" if t == 0 else None)
            tile(jc[t], t & 1, mask, (js[t] >= 0) if check and t < n_st - 1 else None)

        pieces = []
        for g in range(G):
            acc = acc_sc[sub, g]
            num = acc[0:HEAD_DIM, :]
            den = acc[HEAD_DIM:HEAD_DIM + 1, :]
            if has_sink:
                sink_vec = jnp.zeros((1, RQ), F32)
                for r in range(R):
                    sink_vec = jnp.where(lane // TQ == r, sink_ref[g * R + r] * LOG2E, sink_vec)
                m_true = m_sc[sub, g] - slope_vecs[g] * qpos.astype(F32)
                mf = jnp.maximum(m_true, sink_vec)
                a = jnp.exp2(m_true - mf)
                num = num * a
                den = den * a + jnp.exp2(sink_vec - mf)
            o_t = num * (1.0 / den)
            for r in range(R):
                piece = o_t[:, r * TQ:(r + 1) * TQ]
                if gate_ref is not None:
                    c = 3 * (g * R + r) + gate_branch
                    piece = piece * gate_t[c:c + 1, :]
                pieces.append(piece)
        o_ref[rows, :] = jnp.concatenate(pieces, axis=0).T.astype(o_ref.dtype)

    all_tiles_exist = i * NSUB >= n_st - 1

    @pl.when(all_tiles_exist)
    def _():
        for sub in range(NSUB):
            process(sub, False)

    @pl.when(jnp.logical_not(all_tiles_exist))
    def _():
        for sub in range(NSUB):
            process(sub, True)


def _banded_attn(P, VT, B, S, *, name, NSUB, G, R, KD, q_col, k_col, vt_slab, W, slopes, scale, sinks=None,
                 gate_branch=None):
    T = B * S
    TQ = TQ_ATT
    TB = NSUB * TQ
    nq = S // TB
    RQ = R * TQ
    assert TK_BAND == TQ and W % TQ == 0 and S % TB == 0 and KD == 2 * HEAD_DIM
    QW = G * R * HEAD_DIM
    in_specs = [pl.BlockSpec((TB, QW), lambda b, i: (b * nq + i, q_col // QW))]
    args = [P]
    for g in range(G):
        in_specs.append(pl.BlockSpec((S, KD), lambda b, i, g=g: (b, k_col // KD + g)))
        args.append(P)
    for g in range(G):
        in_specs.append(pl.BlockSpec((V_ROWS, S), lambda b, i, g=g: (vt_slab + g, b)))
        args.append(VT)
    if sinks is not None:
        in_specs.append(pl.BlockSpec(memory_space=pltpu.SMEM))
        args.append(sinks.astype(F32))
    if gate_branch is not None:
        in_specs.append(pl.BlockSpec((TB, LANES), lambda b, i: (b * nq + i, P_GATE // LANES)))
        args.append(P)
    kern = functools.partial(_banded_kernel, TQ=TQ, NSUB=NSUB, G=G, R=R, W=W, slopes=slopes, scale=scale,
                             has_sink=sinks is not None, gate_branch=gate_branch)
    return pl.pallas_call(
        kern,
        grid=(B, nq),
        in_specs=in_specs,
        out_specs=pl.BlockSpec((TB, QW), lambda b, i: (b * nq + i, 0)),
        out_shape=jax.ShapeDtypeStruct((T, QW), BF16),
        scratch_shapes=[pltpu.VMEM((NSUB, G, KD, RQ), BF16), pltpu.VMEM((NSUB, G, 1, RQ), F32),
                        pltpu.VMEM((NSUB, G, V_ROWS, RQ), F32), pltpu.VMEM((NSUB, 2, G, TQ, RQ), F32)],
        compiler_params=_params("parallel", "parallel"),
        name=name,
    )(*args)


def _flash_attn(q, k, vt, B, S, *, name, TQ, TK, G, R, KD, q_col, q_width, k_col, vt_slab, slopes=None, scale=1.0,
                selb=None, tiles=None, gate=None, gate_branch=None, side=()):
    T = B * S
    nq = S // TQ
    RQ = R * TQ
    assert TK % TQ == 0 and S % TK == 0
    OW = G * R * HEAD_DIM
    if slopes is not None:
        in_specs = [pl.BlockSpec((TQ, q_width), lambda b, i, *_: (b * nq + i, q_col // q_width))]
    else:
        in_specs = [pl.BlockSpec((G * KD, TQ), lambda b, i, *_: (0, b * nq + i))]
    args = [q]
    if selb is not None:
        in_specs.append(pl.BlockSpec((LANES, TQ), lambda b, i, *_: (0, b * nq + i)))
        args.append(selb)
    for g in range(G):
        in_specs.append(pl.BlockSpec((S, KD), lambda b, i, *_, g=g: (b, k_col // KD + g)))
        args.append(k)
    for g in range(G):
        in_specs.append(pl.BlockSpec((V_ROWS, S), lambda b, i, *_, g=g: (vt_slab + g, b)))
        args.append(vt)
    if gate_branch is not None:
        in_specs.append(pl.BlockSpec((TQ, LANES), lambda b, i, *_: (b * nq + i, P_GATE // LANES)))
        args.append(gate)
    out_specs = [pl.BlockSpec((TQ, OW), lambda b, i, *_: (b * nq + i, 0))]
    out_shape = [jax.ShapeDtypeStruct((T, OW), BF16)]
    for w in side:
        rb = w.shape[0] // (B * nq)
        assert w.shape[0] % (B * nq) == 0 and rb % 16 == 0 and w.shape[1] % LANES == 0
        spec = pl.BlockSpec((rb, w.shape[1]), lambda b, i, *_: (b * nq + i, 0))
        in_specs.append(spec)
        args.append(w)
        out_specs.append(spec)
        out_shape.append(jax.ShapeDtypeStruct(w.shape, BF16))
    scratch = []
    if slopes is not None:
        scratch.append(pltpu.VMEM((G, KD, RQ), BF16))
    scratch += [pltpu.VMEM((G, 1, RQ), F32), pltpu.VMEM((G, V_ROWS, RQ), F32), pltpu.VMEM((2, G, TK, RQ), F32),
                pltpu.VMEM((2, G, 1, RQ), F32)]
    kern = functools.partial(_flash_kernel, TQ=TQ, TK=TK, G=G, R=R, NKT=S // TK, slopes=slopes, scale=scale,
                             sel=selb is not None, gate_branch=gate_branch, n_side=len(side))
    prefetch = list(tiles) if selb is not None else []
    grid_spec = pltpu.PrefetchScalarGridSpec(
        num_scalar_prefetch=len(prefetch),
        grid=(B, nq),
        in_specs=in_specs,
        out_specs=out_specs,
        scratch_shapes=scratch)
    outs = pl.pallas_call(
        kern,
        grid_spec=grid_spec,
        out_shape=out_shape,
        compiler_params=_params("parallel", "parallel"),
        name=name,
    )(*prefetch, *args)
    return list(outs)


def _mla_prep_kernel(cq_ref, ckv_ref, kr_ref, krr_ref, cos_ref, sin_ref, gq_ref, gkv_ref,
                     wq_ref, wqr_ref, wk_ref, wvt_ref, vone_ref, q_out, k_out, vt_out):
    cos = cos_ref[...]
    sin = sin_ref[...]
    scale = (B_NOPE + B_ROPE) ** -0.5 * LOG2E
    nq = _rms(cq_ref[...].astype(F32), gq_ref[...]).astype(BF16)
    nkv = _rms(ckv_ref[...].astype(F32), gkv_ref[...]).astype(BF16)
    q1 = _dot_nt(wq_ref[...], nq)
    q2 = _dot_nt(wqr_ref[...], nq)
    kk = _dot(nkv, wk_ref[...])
    krope = kr_ref[...].astype(F32) * cos + krr_ref[...].astype(F32) * sin
    cos_t = cos.T
    sin_t = sin.T
    for h in range(B_HEADS):
        sl = slice(h * LANES, (h + 1) * LANES)
        q_out[sl, :] = ((q1[sl, :] * cos_t + q2[sl, :] * sin_t) * scale).astype(BF16)
        k_out[:, sl] = (kk[:, sl] + krope).astype(BF16)
    vt_out[...] = (_dot_nt(wvt_ref[...], nkv) + vone_ref[...]).astype(BF16)


def _build_mla_weights(w_uq, w_ukv):
    R1, R2 = w_uq.shape[0], w_ukv.shape[0]
    half = B_ROPE // 2
    hq = B_NOPE + B_ROPE
    wq, wqr, wk, wvt = [], [], [], []
    for h in range(B_HEADS):
        nope = w_uq[:, h * hq:h * hq + B_NOPE]
        rope = w_uq[:, h * hq + B_NOPE:(h + 1) * hq]
        rot = jnp.concatenate([-rope[:, half:], rope[:, :half]], axis=1)
        pad = jnp.zeros((R1, LANES - hq), w_uq.dtype)
        wq += [nope, rope, pad]
        wqr += [jnp.zeros((R1, B_NOPE), w_uq.dtype), rot, pad]
        wk += [w_ukv[:, h * 128:h * 128 + B_NOPE], jnp.zeros((R2, LANES - B_NOPE), w_ukv.dtype)]
        wvt += [w_ukv[:, h * 128 + B_NOPE:(h + 1) * 128].T, jnp.zeros((V_ROWS - HEAD_DIM, R2), w_ukv.dtype)]
    cat = lambda xs: jnp.concatenate(xs, axis=1).astype(BF16)
    return cat(wq).T, cat(wqr).T, cat(wk), jnp.concatenate(wvt, axis=0).astype(BF16)


def _mla_prep(P, cos, sin, g_cq, g_ckv, w_uq, w_ukv):
    T = P.shape[0]
    TM = TM_PROJ
    wq, wqr, wk, wvt = _build_mla_weights(w_uq, w_ukv)
    QW = B_HEADS * LANES
    VR = B_HEADS * V_ROWS
    vone = np.zeros((VR, 1), np.float32)
    vone[HEAD_DIM::V_ROWS] = 1.0
    full = lambda shp: pl.BlockSpec(shp, lambda i: (0, 0))
    return pl.pallas_call(
        _mla_prep_kernel,
        grid=(T // TM,),
        in_specs=[pl.BlockSpec((TM, B_Q_RANK), lambda i: (i, P_BCQ // B_Q_RANK)),
                  pl.BlockSpec((TM, B_KV_RANK), lambda i: (i, P_BCKV // B_KV_RANK)),
                  pl.BlockSpec((TM, LANES), lambda i: (i, P_KR // LANES)),
                  pl.BlockSpec((TM, LANES), lambda i: (i, P_KRROT // LANES)),
                  pl.BlockSpec((TM, LANES), lambda i: (i, 0)),
                  pl.BlockSpec((TM, LANES), lambda i: (i, 0)),
                  full((1, B_Q_RANK)), full((1, B_KV_RANK)),
                  full((QW, B_Q_RANK)), full((QW, B_Q_RANK)), full((B_KV_RANK, QW)), full((VR, B_KV_RANK)),
                  full((VR, 1))],
        out_specs=[pl.BlockSpec((QW, TM), lambda i: (0, i)),
                   pl.BlockSpec((TM, QW), lambda i: (i, 0)),
                   pl.BlockSpec((VR, TM), lambda i: (0, i))],
        out_shape=[jax.ShapeDtypeStruct((QW, T), BF16), jax.ShapeDtypeStruct((T, QW), BF16),
                   jax.ShapeDtypeStruct((VR, T), BF16)],
        compiler_params=_params("parallel"),
        name="mla_prep",
    )(P, P, P, P, cos, sin, g_cq.reshape(1, -1), g_ckv.reshape(1, -1), wq, wqr, wk, wvt, jnp.asarray(vone))


def _gelu_tanh(x):
    return 0.5 * x * (1.0 + jnp.tanh(np.float32(np.sqrt(2.0 / np.pi)) * (x + 0.044715 * (x * x * x))))


def _compress_kernel(xk_ref, xv_ref, pek_ref, pev_ref, w1k_ref, w1v_ref, w2k_ref, w2v_ref, kc_ref, vct_ref, *, NC):
    for x_ref, pe_ref, w1_ref, w2_ref, is_v in ((xk_ref, pek_ref, w1k_ref, w2k_ref, False),
                                                 (xv_ref, pev_ref, w1v_ref, w2v_ref, True)):
        x = x_ref[...].astype(F32)
        a = _dot((x + pe_ref[0:1, :]).astype(BF16), w1_ref[0])
        b = _dot((x + pe_ref[1:2, :]).astype(BF16), w1_ref[1])
        hid = a + pltpu.roll(b, NC - 1, 0)
        o = _dot(_gelu_tanh(hid).astype(BF16), w2_ref[...])
        n = lax.broadcasted_iota(jnp.int32, o.shape, 0)
        o = jnp.where(n < NC - 1, o, 0.0)
        if is_v:
            vct_ref[...] = o.T.astype(vct_ref.dtype)
        else:
            kc_ref[...] = o.astype(kc_ref.dtype)


def _build_compress_weights(w1, w2, pe):
    half = C_CMP_BLOCK // 2
    G = C_KV_HEADS
    eye = jnp.eye(G, dtype=w1.dtype)
    w1s = [jnp.einsum('lde,hg->lhdge', w1[c * half:(c + 1) * half], eye).reshape(half * G * HEAD_DIM, G * C_CMP_HIDDEN)
           for c in range(2)]
    w1f = jnp.stack(w1s).astype(BF16)
    w2f = jnp.einsum('ed,hg->hegd', w2, eye).reshape(G * C_CMP_HIDDEN, G * HEAD_DIM).astype(BF16)
    pes = [jnp.broadcast_to(pe[c * half:(c + 1) * half, None, :], (half, G, HEAD_DIM)).reshape(1, -1) for c in range(2)]
    pef = jnp.concatenate(pes, axis=0).astype(F32)
    return w1f, w2f, pef


def _compress(P, B, S, w1k, w2k, pek, w1v, w2v, pev):
    NC = S // C_CMP_STRIDE
    CW = C_CMP_STRIDE * C_KV_HEADS * HEAD_DIM
    xk = P[:, P_CKC:P_CKC + LANES].reshape(B * NC, CW)
    xv = P[:, P_CVC:P_CVC + LANES].reshape(B * NC, CW)
    w1kf, w2kf, pekf = _build_compress_weights(w1k, w2k, pek)
    w1vf, w2vf, pevf = _build_compress_weights(w1v, w2v, pev)
    xspec = pl.BlockSpec((NC, CW), lambda b: (b, 0))
    full2 = lambda a: pl.BlockSpec(a.shape, lambda b: (0, 0))
    full3 = lambda a: pl.BlockSpec(a.shape, lambda b: (0, 0, 0))
    return pl.pallas_call(
        functools.partial(_compress_kernel, NC=NC),
        grid=(B,),
        in_specs=[xspec, xspec, full2(pekf), full2(pevf), full3(w1kf), full3(w1vf), full2(w2kf), full2(w2vf)],
        out_specs=[pl.BlockSpec((NC, LANES), lambda b: (b, 0)), pl.BlockSpec((LANES, NC), lambda b: (b, 0))],
        out_shape=[jax.ShapeDtypeStruct((B * NC, LANES), BF16), jax.ShapeDtypeStruct((B * LANES, NC), BF16)],
        compiler_params=_params("parallel"),
        name="nsa_compress",
    )(xk, xv, pekf, pevf, w1kf, w1vf, w2kf, w2vf)


def _cmp_sel_kernel(q_ref, kc_ref, vct_ref, cfeat_ref, ovt_ref, gate_ref, ocmp_ref, sel_ref, cnt_ref,
                    *, TQ, NC, NSEL, NTOP, slopes):
    i = pl.program_id(1)
    t0 = i * TQ
    R = C_HEADS // C_KV_HEADS
    scale = HEAD_DIM ** -0.5
    gate_t = _sigmoid(gate_ref[...].astype(F32)).T
    last = C_CMP_BLOCK - 1
    tlane = t0 + lax.broadcasted_iota(jnp.int32, (NC, TQ), 1)
    nrow = lax.broadcasted_iota(jnp.int32, (NC, TQ), 0)
    cvalid = (nrow * C_CMP_STRIDE + last <= tlane) & (nrow < NC - 1)
    hasc = ((t0 + lax.broadcasted_iota(jnp.int32, (1, TQ), 1)) >= last).astype(F32)
    jrow = lax.broadcasted_iota(jnp.int32, (NSEL, TQ), 0)
    tl = t0 + lax.broadcasted_iota(jnp.int32, (NSEL, TQ), 1)
    avail = jrow * C_SEL_BLOCK <= tl
    cur = tl // C_SEL_BLOCK
    forced = (jrow == 0) | (jrow == cur) | (jrow == cur - 1)
    q_t = q_ref[...].astype(F32).T
    frow = lax.broadcasted_iota(jnp.int32, (HEAD_DIM, TQ), 0)
    sub8 = lax.broadcasted_iota(jnp.int32, (8, TQ), 0)
    pieces, sel_parts = [], []
    picked = jnp.zeros((NSEL, TQ), F32)
    for g in range(C_KV_HEADS):
        kca = jnp.concatenate([kc_ref[:, g * HEAD_DIM:(g + 1) * HEAD_DIM], cfeat_ref[...]], axis=1)
        vo = jnp.concatenate([vct_ref[g * HEAD_DIM:(g + 1) * HEAD_DIM, :], ovt_ref[...]], axis=0)
        imp = jnp.zeros((HEAD_DIM, TQ), F32)
        for r in range(R):
            hh = g * R + r
            s3 = _split_bf16(slopes[hh] * LOG2E)
            feat = jnp.zeros((HEAD_DIM, TQ), F32)
            for k in range(3):
                feat = jnp.where(frow == k, s3[k] * 256.0, jnp.where(frow == 3 + k, s3[k], feat))
            qa = jnp.concatenate([q_t[hh * HEAD_DIM:(hh + 1) * HEAD_DIM, :] * (scale * LOG2E), feat],
                                 axis=0).astype(BF16)
            st = jnp.where(cvalid, _dot(kca, qa), NEG_INF)
            et = jnp.exp2(st - jnp.max(st, axis=0, keepdims=True))
            inv = hasc / jnp.sum(et, axis=0, keepdims=True)
            res = _dot(vo, et.astype(BF16))
            pieces.append(res[0:HEAD_DIM, :] * (inv * gate_t[3 * hh:3 * hh + 1, :]))
            imp = imp + res[HEAD_DIM:2 * HEAD_DIM, :] * inv
        v = jnp.where(avail, imp[0:NSEL, :], NEG_INF)
        v = jnp.where(forced, -NEG_INF, v)
        ranks = [jnp.zeros((8, TQ), F32) for _ in range(NSEL // 8)]
        for ii in range(NSEL):
            ri = v[ii:ii + 1, :]
            for k in range(NSEL // 8):
                vk = v[8 * k:8 * k + 8, :]
                if 8 * k > ii:
                    inc = jnp.where(ri >= vk, 1.0, 0.0)
                elif 8 * k + 7 < ii:
                    inc = jnp.where(ri > vk, 1.0, 0.0)
                else:
                    inc = jnp.where(sub8 + 8 * k > ii, jnp.where(ri >= vk, 1.0, 0.0), jnp.where(ri > vk, 1.0, 0.0))
                ranks[k] = ranks[k] + inc
        rank = jnp.concatenate(ranks, axis=0)
        sel_parts.append(jnp.where(rank < NTOP, 0.0, MASK_BIAS))
        if NSEL < HEAD_DIM:
            sel_parts.append(jnp.zeros((HEAD_DIM - NSEL, TQ), F32))
        picked = picked + jnp.where(rank < NTOP, 1.0, 0.0)
    sel_ref[...] = jnp.concatenate(sel_parts, axis=0).astype(sel_ref.dtype)
    cnt_ref[...] = _dot_nt(jnp.ones((8, TQ), BF16), picked.astype(BF16))
    ocmp_ref[...] = jnp.concatenate(pieces, axis=0).T.astype(ocmp_ref.dtype)


def _overlap_t(S):
    n_cmp = (S - C_CMP_BLOCK) // C_CMP_STRIDE + 1
    n_sel = S // C_SEL_BLOCK
    NC = S // C_CMP_STRIDE
    cstart = np.arange(n_cmp) * C_CMP_STRIDE
    cend = cstart + C_CMP_BLOCK - 1
    sstart = np.arange(n_sel) * C_SEL_BLOCK
    send = sstart + C_SEL_BLOCK - 1
    ov = np.clip(np.minimum(cend[:, None], send[None, :]) - np.maximum(cstart[:, None], sstart[None, :]) + 1,
                 0, None).astype(np.float32) / C_CMP_STRIDE
    ovt = np.zeros((HEAD_DIM, NC), np.float32)
    ovt[:n_sel, :n_cmp] = ov.T
    cend_all = np.arange(NC) * C_CMP_STRIDE + C_CMP_BLOCK - 1
    cfeat = np.zeros((NC, HEAD_DIM), np.float32)
    cfeat[:, 0:3] = (cend_all // 256)[:, None]
    cfeat[:, 3:6] = (cend_all % 256)[:, None]
    return jnp.asarray(ovt, dtype=BF16), jnp.asarray(cfeat, dtype=BF16)


def _cmp_sel(P, kc, vct, B, S, slopes):
    T = B * S
    TQ = TQ_ATT
    nq = S // TQ
    NC = S // C_CMP_STRIDE
    NSEL = S // C_SEL_BLOCK
    assert NSEL <= HEAD_DIM and NSEL % 8 == 0 and S <= 256 * 256
    QW = C_HEADS * HEAD_DIM
    ovt, cfeat = _overlap_t(S)
    kern = functools.partial(_cmp_sel_kernel, TQ=TQ, NC=NC, NSEL=NSEL, NTOP=min(C_SEL_TOP, NSEL), slopes=slopes)
    return pl.pallas_call(
        kern,
        grid=(B, nq),
        in_specs=[pl.BlockSpec((TQ, QW), lambda b, i: (b * nq + i, P_CQ // QW)),
                  pl.BlockSpec((NC, LANES), lambda b, i: (b, 0)),
                  pl.BlockSpec((LANES, NC), lambda b, i: (b, 0)),
                  pl.BlockSpec((NC, HEAD_DIM), lambda b, i: (0, 0)),
                  pl.BlockSpec((HEAD_DIM, NC), lambda b, i: (0, 0)),
                  pl.BlockSpec((TQ, LANES), lambda b, i: (b * nq + i, P_GATE // LANES))],
        out_specs=[pl.BlockSpec((TQ, QW), lambda b, i: (b * nq + i, 0)),
                   pl.BlockSpec((LANES, TQ), lambda b, i: (0, b * nq + i)),
                   pl.BlockSpec((8, NSEL), lambda b, i: (b * nq + i, 0))],
        out_shape=[jax.ShapeDtypeStruct((T, QW), BF16), jax.ShapeDtypeStruct((LANES, T), BF16),
                   jax.ShapeDtypeStruct((B * nq * 8, NSEL), F32)],
        compiler_params=_params("parallel", "parallel"),
        name="nsa_cmp_select",
    )(P, kc, vct, cfeat, ovt, P)


def _selected_tile_lists(cnt, B, S):
    nq = S // TQ_ATT
    nkt = S // TK_SEL
    per = TK_SEL // C_SEL_BLOCK
    blk = cnt.reshape(B * nq, 8, -1)[:, 0, :] > 0.5
    need = jnp.any(blk.reshape(B * nq, nkt, per), axis=-1)
    own = ((jnp.arange(B * nq, dtype=jnp.int32) % nq) * TQ_ATT + TQ_ATT - 1) // TK_SEL
    jj = jnp.arange(nkt, dtype=jnp.int32)[None, :]
    need = (need | (jj == 0) | (jj == own[:, None])) & (jj <= own[:, None])
    order = jnp.argsort(jnp.where(need, jj, nkt + jj), axis=-1).astype(jnp.int32)
    return order.reshape(-1), jnp.sum(need, axis=-1).astype(jnp.int32)


def _outproj_kernel(x_ref, oa_ref, ob_ref, oc1_ref, oc2_ref, oc3_ref, ga_ref, gb_ref, gc_ref, wo_ref, gp_ref, o_ref):
    oa = _rms(oa_ref[...].astype(F32), ga_ref[...]).astype(BF16)
    ob = _rms(ob_ref[...].astype(F32), gb_ref[...]).astype(BF16)
    oc = oc1_ref[...].astype(F32) + oc2_ref[...].astype(F32) + oc3_ref[...].astype(F32)
    oc = _rms(oc, gc_ref[...]).astype(BF16)
    m = _dot(oa, wo_ref[0:256, :]) + _dot(ob, wo_ref[256:512, :]) + _dot(oc, wo_ref[512:1024, :])
    o_ref[...] = x_ref[...] + _rms(m, gp_ref[...])


def _outproj(x, o_a, o_b, o_cmp, o_slc, o_win, g_oa, g_ob, g_oc, w_o, g_post):
    T = x.shape[0]
    TM = TM_PROJ
    row = lambda w: pl.BlockSpec((TM, w), lambda i: (i, 0))
    full = lambda r, c: pl.BlockSpec((r, c), lambda i: (0, 0))
    return pl.pallas_call(
        _outproj_kernel,
        grid=(T // TM,),
        in_specs=[row(D_MODEL), row(256), row(256), row(512), row(512), row(512),
                  full(1, 256), full(1, 256), full(1, 512), full(D_MODEL, D_MODEL), full(1, D_MODEL)],
        out_specs=row(D_MODEL),
        out_shape=jax.ShapeDtypeStruct((T, D_MODEL), F32),
        compiler_params=_params("parallel"),
        name="outproj",
    )(x, o_a, o_b, o_cmp, o_slc, o_win, g_oa.reshape(1, -1), g_ob.reshape(1, -1), g_oc.reshape(1, -1),
      w_o.astype(BF16), g_post.reshape(1, -1))


def _swiglu_step(h, wg_ref, wu_ref, wd_ref, acc_ref):
    a = _dot(h, wg_ref[...])
    b = _dot(h, wu_ref[...])
    z = (a * _sigmoid(a) * b).astype(BF16)
    acc_ref[...] += _dot(z, wd_ref[...])


def _ffn_kernel(x_ref, gpre_ref, wg_ref, wu_ref, wd_ref, gpost_ref, o_ref, h_sc, acc_sc, *, nf):
    f = pl.program_id(1)

    @pl.when(f == 0)
    def _():
        h_sc[...] = _rms(x_ref[...], gpre_ref[...]).astype(BF16)
        acc_sc[...] = jnp.zeros_like(acc_sc)

    _swiglu_step(h_sc[...], wg_ref, wu_ref, wd_ref, acc_sc)

    @pl.when(f == nf - 1)
    def _():
        o_ref[...] = x_ref[...] + _rms(acc_sc[...], gpost_ref[...])


def _ffn(x, g_pre, wg, wu, wd, g_post):
    T = x.shape[0]
    TM, TF = TM_FFN, TF_FFN
    FF = wg.shape[1]
    nf = FF // TF
    return pl.pallas_call(
        functools.partial(_ffn_kernel, nf=nf),
        grid=(T // TM, nf),
        in_specs=[pl.BlockSpec((TM, D_MODEL), lambda i, f: (i, 0)),
                  pl.BlockSpec((1, D_MODEL), lambda i, f: (0, 0)),
                  pl.BlockSpec((D_MODEL, TF), lambda i, f: (0, f)),
                  pl.BlockSpec((D_MODEL, TF), lambda i, f: (0, f)),
                  pl.BlockSpec((TF, D_MODEL), lambda i, f: (f, 0)),
                  pl.BlockSpec((1, D_MODEL), lambda i, f: (0, 0))],
        out_specs=pl.BlockSpec((TM, D_MODEL), lambda i, f: (i, 0)),
        out_shape=jax.ShapeDtypeStruct((T, D_MODEL), F32),
        scratch_shapes=[pltpu.VMEM((TM, D_MODEL), BF16), pltpu.VMEM((TM, D_MODEL), F32)],
        compiler_params=_params("parallel", "arbitrary"),
        name="dense_ffn",
    )(x, g_pre.reshape(1, -1), wg.astype(BF16), wu.astype(BF16), wd.astype(BF16), g_post.reshape(1, -1))


def _expert_kernel(te_ref, nt_ref, x_ref, wg_ref, wu_ref, wd_ref, o_ref, acc_sc, *, nf):
    j = pl.program_id(0)
    f = pl.program_id(1)
    live = j < nt_ref[0]

    @pl.when(f == 0)
    def _():
        acc_sc[...] = jnp.zeros_like(acc_sc)

    @pl.when(live)
    def _():
        _swiglu_step(x_ref[...], wg_ref.at[0], wu_ref.at[0], wd_ref.at[0], acc_sc)

    @pl.when(f == nf - 1)
    def _():
        o_ref[...] = acc_sc[...].astype(o_ref.dtype)


def _experts(xs, tile_expert, n_tiles, wg, wu, wd):
    NT = xs.shape[0] // TM_MOE
    TM, TF = TM_MOE, TF_MOE
    assert wg.shape[2] % TF == 0
    FF = wg.shape[2]
    nf = FF // TF
    fidx = lambda j, f, nt: jnp.where(j < nt[0], f, nf - 1)
    grid_spec = pltpu.PrefetchScalarGridSpec(
        num_scalar_prefetch=2,
        grid=(NT, nf),
        in_specs=[pl.BlockSpec((TM, D_MODEL), lambda j, f, te, nt: (j, 0)),
                  pl.BlockSpec((1, D_MODEL, TF), lambda j, f, te, nt: (te[j], 0, fidx(j, f, nt))),
                  pl.BlockSpec((1, D_MODEL, TF), lambda j, f, te, nt: (te[j], 0, fidx(j, f, nt))),
                  pl.BlockSpec((1, TF, D_MODEL), lambda j, f, te, nt: (te[j], fidx(j, f, nt), 0))],
        out_specs=pl.BlockSpec((TM, D_MODEL), lambda j, f, te, nt: (j, 0)),
        scratch_shapes=[pltpu.VMEM((TM, D_MODEL), F32)])
    return pl.pallas_call(
        functools.partial(_expert_kernel, nf=nf),
        grid_spec=grid_spec,
        out_shape=jax.ShapeDtypeStruct(xs.shape, BF16),
        compiler_params=_params("parallel", "arbitrary"),
        name="moe_experts",
    )(tile_expert, n_tiles, xs, wg, wu, wd)


def _router_kernel(x_ref, g_ref, wr_ref, h_out, meta_out, metat_out, start_out, cnt_out, run_sc, *, TS):
    s = pl.program_id(0)

    @pl.when(s == 0)
    def _():
        run_sc[...] = jnp.zeros_like(run_sc)

    h = _rms(x_ref[...], g_ref[...])
    hb = h.astype(BF16)
    h_out[...] = hb
    hl = (h - hb.astype(F32)).astype(BF16)
    w = wr_ref[...]
    wh = w.astype(BF16)
    wl = (w - wh.astype(F32)).astype(BF16)
    logits = _dot(hb, wh) + _dot(hl, wh) + _dot(hb, wl)
    lane = lax.broadcasted_iota(jnp.int32, (TS, LANES), 1).astype(F32)
    logits = jnp.where(lane < N_EXPERTS, logits, NEG_INF)
    m1 = jnp.max(logits, axis=-1, keepdims=True)
    i1 = jnp.min(jnp.where(logits == m1, lane, float(LANES)), axis=-1, keepdims=True)
    rest = jnp.where(lane == i1, NEG_INF, logits)
    m2 = jnp.max(rest, axis=-1, keepdims=True)
    i2 = jnp.min(jnp.where(rest == m2, lane, float(LANES)), axis=-1, keepdims=True)
    ex = jnp.exp(m2 - m1)
    w1 = 1.0 / (1.0 + ex)
    w2 = ex / (1.0 + ex)
    oh1 = lane == i1
    oh2 = lane == i2
    oh = jnp.where(oh1, 1.0, 0.0) + jnp.where(oh2, 1.0, 0.0)
    tri = jnp.where(lax.broadcasted_iota(jnp.int32, (TS, TS), 0) > lax.broadcasted_iota(jnp.int32, (TS, TS), 1),
                    1.0, 0.0).astype(BF16)
    run = run_sc[...]
    before = _dot(tri, oh.astype(BF16)) + run
    r1 = jnp.sum(jnp.where(oh1, before, 0.0), axis=-1, keepdims=True)
    r2 = jnp.sum(jnp.where(oh2, before, 0.0), axis=-1, keepdims=True)
    start_out[0] = run
    run = run + jnp.sum(oh, axis=0, keepdims=True)
    run_sc[...] = run
    cnt_out[...] = run
    meta = jnp.where(lane == 0, i1, jnp.where(lane == 1, i2, jnp.where(lane == 2, r1, jnp.where(
        lane == 3, r2, jnp.where(lane == 4, w1, jnp.where(lane == 5, w2, 0.0))))))
    meta_out[...] = meta
    metat_out[...] = meta.T[0:8, :]


def _router(x, g, w_router):
    T = x.shape[0]
    TS = TS_MOE
    nt = T // TS
    wr = jnp.zeros((D_MODEL, LANES), F32).at[:, :N_EXPERTS].set(w_router.astype(F32))
    return pl.pallas_call(
        functools.partial(_router_kernel, TS=TS),
        grid=(nt,),
        in_specs=[pl.BlockSpec((TS, D_MODEL), lambda s: (s, 0)),
                  pl.BlockSpec((1, D_MODEL), lambda s: (0, 0)),
                  pl.BlockSpec((D_MODEL, LANES), lambda s: (0, 0))],
        out_specs=[pl.BlockSpec((TS, D_MODEL), lambda s: (s, 0)),
                   pl.BlockSpec((TS, LANES), lambda s: (s, 0)),
                   pl.BlockSpec((8, TS), lambda s: (0, s)),
                   pl.BlockSpec((1, 1, LANES), lambda s: (s, 0, 0)),
                   pl.BlockSpec((1, LANES), lambda s: (0, 0))],
        out_shape=[jax.ShapeDtypeStruct((T, D_MODEL), BF16),
                   jax.ShapeDtypeStruct((T, LANES), F32),
                   jax.ShapeDtypeStruct((8, T), F32),
                   jax.ShapeDtypeStruct((nt, 1, LANES), F32),
                   jax.ShapeDtypeStruct((1, LANES), F32)],
        scratch_shapes=[pltpu.VMEM((1, LANES), F32)],
        compiler_params=_params("arbitrary"),
        name="moe_router",
    )(x, g.reshape(1, -1), wr)


def _gather_kernel(te_ref, slo_ref, shi_ref, base_ref, metat_ref, h_ref, xs_out, acc_sc, *, TM, TS, CH):
    j = pl.program_id(0)
    e = te_ref[j]
    ef = e.astype(F32)
    nsub = TM // CH
    nt = h_ref.shape[0] // TS

    def picked_rows(s, want):
        off = pl.multiple_of(s * TS, TS)
        e1 = metat_ref[0:1, pl.ds(off, TS)]
        e2 = metat_ref[1:2, pl.ds(off, TS)]
        r1 = metat_ref[2:3, pl.ds(off, TS)]
        r2 = metat_ref[3:4, pl.ds(off, TS)]
        key = jnp.where(e1 == ef, r1, jnp.where(e2 == ef, r2, -1.0))
        sel = jnp.where(key == want, 1.0, 0.0).astype(BF16)
        return _dot(sel, h_ref[pl.ds(off, TS), :])

    wants = []
    for sub in range(nsub):
        q = j * nsub + sub
        first = j * TM + sub * CH - base_ref[e]
        want = (first + lax.broadcasted_iota(jnp.int32, (CH, 1), 0)).astype(F32)
        lo = jnp.minimum(slo_ref[q], nt - 1)
        second = jnp.where(lo + 1 < nt, want, -2.0)
        acc_sc[sub] = picked_rows(lo, want) + picked_rows(jnp.minimum(lo + 1, nt - 1), second)
        wants.append(want)
    for sub in range(nsub):
        q = j * nsub + sub

        def body(s, _, sub=sub):
            acc_sc[sub] += picked_rows(s, wants[sub])
            return 0

        lax.fori_loop(slo_ref[q] + 2, shi_ref[q] + 1, body, 0)
    for sub in range(nsub):
        xs_out[sub * CH:(sub + 1) * CH, :] = acc_sc[sub].astype(xs_out.dtype)


def _gather(hb, metat, tile_expert, s_lo, s_hi, base, NT):
    TM, TS = TM_MOE, TS_MOE
    assert hb.shape[0] % TS == 0
    grid_spec = pltpu.PrefetchScalarGridSpec(
        num_scalar_prefetch=4,
        grid=(NT,),
        in_specs=[pl.BlockSpec(memory_space=pltpu.VMEM), pl.BlockSpec(memory_space=pltpu.VMEM)],
        out_specs=pl.BlockSpec((TM, D_MODEL), lambda j, *_: (j, 0)),
        scratch_shapes=[pltpu.VMEM((TM // CH_MOE, CH_MOE, D_MODEL), F32)])
    return pl.pallas_call(
        functools.partial(_gather_kernel, TM=TM, TS=TS, CH=CH_MOE),
        grid_spec=grid_spec,
        out_shape=jax.ShapeDtypeStruct((NT * TM, D_MODEL), BF16),
        compiler_params=_params("arbitrary"),
        name="moe_gather",
    )(tile_expert, s_lo, s_hi, base, metat, hb)


def _combine_kernel(blk_ref, row0_ref, base_ref, meta_ref, x_ref, gpost_ref, ys_hbm, o_ref, buf, sem, selw_sc,
                    *, TS, CH, U, nt):
    s = pl.program_id(0)
    slot = s % 2

    def chunk_copies(step, slot_):
        out = []
        for u in range(U):
            r0 = pl.multiple_of(blk_ref[step * U + u] * CH, CH)
            out.append(pltpu.make_async_copy(ys_hbm.at[pl.ds(r0, CH), :], buf.at[slot_, pl.ds(u * CH, CH), :],
                                             sem.at[slot_]))
        return out

    @pl.when(s == 0)
    def _():
        for c in chunk_copies(0, 0):
            c.start()

    @pl.when(s + 1 < nt)
    def _():
        for c in chunk_copies(s + 1, 1 - slot):
            c.start()

    meta = meta_ref[...]
    e1, e2 = meta[:, 0:1], meta[:, 1:2]
    w1, w2 = meta[:, 4:5], meta[:, 5:6]
    b1 = jnp.zeros((TS, 1), F32)
    b2 = jnp.zeros((TS, 1), F32)
    for e in range(N_EXPERTS):
        be = base_ref[e].astype(F32)
        b1 = jnp.where(e1 == float(e), be, b1)
        b2 = jnp.where(e2 == float(e), be, b2)
    pos1 = b1 + meta[:, 2:3]
    pos2 = b2 + meta[:, 3:4]

    lane = lax.broadcasted_iota(jnp.int32, (1, CH), 1)
    for u in range(U):
        rows = (row0_ref[s * U + u] + lane).astype(F32)
        selw = jnp.where(pos1 == rows, w1, 0.0) + jnp.where(pos2 == rows, w2, 0.0)
        selw_sc[:, u * CH:(u + 1) * CH] = selw.astype(BF16)
    for c in chunk_copies(s, slot):
        c.wait()
    y = _dot(selw_sc[...], buf[slot])
    o_ref[...] = x_ref[...] + _rms(y, gpost_ref[...])


def _combine(x, meta, ys, chunk_blk, chunk_row0, base, g_post):
    T = x.shape[0]
    TS, CH, U = TS_MOE, CH_MOE, U_MOE
    nt = T // TS
    grid_spec = pltpu.PrefetchScalarGridSpec(
        num_scalar_prefetch=3,
        grid=(nt,),
        in_specs=[pl.BlockSpec((TS, LANES), lambda s, *_: (s, 0)),
                  pl.BlockSpec((TS, D_MODEL), lambda s, *_: (s, 0)),
                  pl.BlockSpec((1, D_MODEL), lambda s, *_: (0, 0)),
                  pl.BlockSpec(memory_space=pl.ANY)],
        out_specs=pl.BlockSpec((TS, D_MODEL), lambda s, *_: (s, 0)),
        scratch_shapes=[pltpu.VMEM((2, U * CH, D_MODEL), BF16), pltpu.SemaphoreType.DMA((2,)),
                        pltpu.VMEM((TS, U * CH), BF16)])
    return pl.pallas_call(
        functools.partial(_combine_kernel, TS=TS, CH=CH, U=U, nt=nt),
        grid_spec=grid_spec,
        out_shape=jax.ShapeDtypeStruct((T, D_MODEL), F32),
        compiler_params=_params("arbitrary"),
        name="moe_combine",
    )(chunk_blk, chunk_row0, base, meta, x, g_post.reshape(1, -1), ys)


def _moe(x, g_pre, w_router, w_gate, w_up, w_down, g_post):
    T = x.shape[0]
    TS, TM, CH, U = TS_MOE, TM_MOE, CH_MOE, U_MOE
    nt = T // TS
    NT = (2 * T) // TM + N_EXPERTS
    hb, meta, metat, start, cnt = _router(x, g_pre, w_router)

    i32 = jnp.int32
    counts = cnt[0, :N_EXPERTS].astype(i32)
    start = start[:, 0, :N_EXPERTS].astype(i32)
    start_ext = jnp.concatenate([start, counts[None, :]], axis=0)
    tiles_e = (counts + TM - 1) // TM
    tiles_cum = jnp.cumsum(tiles_e)
    base = (tiles_cum - tiles_e) * TM
    n_tiles = tiles_cum[-1:]
    jj = jnp.arange(NT, dtype=i32)
    tile_expert = jnp.minimum(jnp.sum(jj[:, None] >= tiles_cum[None, :], axis=1), N_EXPERTS - 1).astype(i32)
    live = jj < n_tiles[0]
    qq = jnp.arange(NT * (TM // CH), dtype=i32)
    q_exp = tile_expert[qq // (TM // CH)]
    a = qq * CH - base[q_exp]
    st_e = start_ext[:, q_exp]
    s_lo = jnp.sum(st_e[1:] <= a[None, :], axis=0).astype(i32)
    s_hi = (jnp.sum(st_e[:-1] < (a + CH)[None, :], axis=0) - 1).astype(i32)
    q_live = live[qq // (TM // CH)]
    s_lo = jnp.where(q_live, s_lo, 1)
    s_hi = jnp.where(q_live, s_hi, 0)

    xs = _gather(hb, metat, tile_expert, s_lo, s_hi, base.astype(i32), NT)
    ys = _experts(xs, tile_expert, n_tiles.astype(i32), w_gate.astype(BF16), w_up.astype(BF16), w_down.astype(BF16))

    lo = base[None, :] + start_ext[:-1]
    hi = base[None, :] + start_ext[1:]
    n_e = jnp.where(hi > lo, (hi - 1) // CH - lo // CH + 1, 0)
    cum = jnp.cumsum(n_e, axis=1)
    uu = jnp.arange(U, dtype=i32)
    e_u = jnp.minimum(jnp.sum(uu[None, :, None] >= cum[:, None, :], axis=2), N_EXPERTS - 1)
    first = jnp.take_along_axis(lo // CH, e_u, axis=1)
    skipped = jnp.take_along_axis(cum - n_e, e_u, axis=1)
    chunk_blk = jnp.clip(first + (uu[None, :] - skipped), 0, (NT * TM) // CH - 1).astype(i32)
    used = uu[None, :] < cum[:, -1:]
    chunk_blk = jnp.where(used, chunk_blk, chunk_blk[:, :1])
    chunk_row0 = jnp.where(used, chunk_blk * CH, -(NT * TM)).astype(i32)
    return _combine(x, meta, ys, chunk_blk.reshape(-1), chunk_row0.reshape(-1), base.astype(i32), g_post)


def kernel(x, positions, w_in, a_sinks, g_cq, w_uq, g_ckv, w_ukv, c_w1_k, c_w2_k, c_pe_k, c_w1_v, c_w2_v, c_pe_v,
           g_oa, g_ob, g_oc, w_o, g_pre_mix, g_post_mix, g_pre_ffn, g_post_ffn, ffn_w_gate, ffn_w_up, ffn_w_down,
           moe_router, moe_w_gate, moe_w_up, moe_w_down):
    B, S, _ = x.shape
    T = B * S
    depth = w_in.shape[0]
    a_slopes = _alibi_slopes(A_HEADS)
    c_slopes = _alibi_slopes(C_HEADS)
    xf = x.reshape(T, D_MODEL).astype(F32)
    cos, sin = _rope_tables(positions)
    att_scale = HEAD_DIM ** -0.5
    flat = lambda w: w.reshape(-1, w.shape[-1])
    bf16_w = {}
    for l in range(depth):
        w_p, w_vt = _build_w_in(w_in[l])
        P, VT = _inproj(xf, g_pre_mix[l].reshape(1, -1), w_p, w_vt, S)
        o_a = _banded_attn(P, VT, B, S, name="swa_attn", NSUB=NSUB_A, G=A_KV_HEADS, R=A_HEADS // A_KV_HEADS,
                           KD=KD_A, q_col=P_AQ, k_col=P_KA, vt_slab=VT_A, W=A_WINDOW, slopes=a_slopes,
                           scale=att_scale, sinks=a_sinks[l])
        qB, kB, vtB = _mla_prep(P, cos, sin, g_cq[l], g_ckv[l], w_uq[l], w_ukv[l])
        if l % 2 == 0:
            side = (ffn_w_gate[l // 2], ffn_w_up[l // 2], ffn_w_down[l // 2])
        else:
            side = (flat(moe_w_down[l // 2]),)
        o_b, *cast = _flash_attn(qB, kB, vtB, B, S, name="mla_attn", TQ=TQ_MLA, TK=TQ_MLA, G=B_HEADS, R=1, KD=LANES,
                                 q_col=0, q_width=B_HEADS * LANES, k_col=0, vt_slab=0, side=side)
        if l % 2 == 0:
            bf16_w["ffn", l] = cast
        else:
            bf16_w["down", l] = cast[0]
        kc, vct = _compress(P, B, S, c_w1_k[l], c_w2_k[l], c_pe_k[l], c_w1_v[l], c_w2_v[l], c_pe_v[l])
        o_cmp, selb, pick_cnt = _cmp_sel(P, kc, vct, B, S, c_slopes)
        c_args = dict(TQ=TQ_ATT, G=C_KV_HEADS, R=C_HEADS // C_KV_HEADS, q_col=P_CQ, q_width=C_HEADS * HEAD_DIM,
                      slopes=c_slopes, scale=att_scale, gate=P)
        side = ()
        if l % 2 == 0 and l + 1 < depth:
            side = (flat(moe_w_gate[(l + 1) // 2]),)
        elif l % 2 == 1:
            side = (flat(moe_w_up[l // 2]),)
        o_slc, *cast = _flash_attn(P, P, VT, B, S, name="nsa_selected", TK=TK_SEL, KD=KD_SEL, k_col=P_KSEL,
                                   vt_slab=VT_SEL, selb=selb, tiles=_selected_tile_lists(pick_cnt, B, S),
                                   gate_branch=1, side=side, **c_args)
        if l % 2 == 0 and l + 1 < depth:
            bf16_w["gate", l + 1] = cast[0]
        elif l % 2 == 1:
            bf16_w["up", l] = cast[0]
        o_win = _banded_attn(P, VT, B, S, name="nsa_window", NSUB=NSUB_WIN, G=C_KV_HEADS, R=C_HEADS // C_KV_HEADS,
                             KD=KD_WIN, q_col=P_CQ, k_col=P_KWIN, vt_slab=VT_WIN, W=C_WINDOW, slopes=c_slopes,
                             scale=att_scale, gate_branch=2)
        xf = _outproj(xf, o_a, o_b, o_cmp, o_slc, o_win, g_oa[l], g_ob[l], g_oc[l], w_o[l], g_post_mix[l])
        if l % 2 == 0:
            wg, wu, wd = bf16_w["ffn", l]
            xf = _ffn(xf, g_pre_ffn[l], wg, wu, wd, g_post_ffn[l])
        else:
            e = l // 2
            wg = bf16_w["gate", l].reshape(moe_w_gate[e].shape)
            wu = bf16_w["up", l].reshape(moe_w_up[e].shape)
            wd = bf16_w["down", l].reshape(moe_w_down[e].shape)
            xf = _moe(xf, g_pre_ffn[l], moe_router[e], wg, wu, wd, g_post_ffn[l])
    return xf.reshape(B, S, D_MODEL)
```

```python
import functools

import numpy as np
import jax
import jax.numpy as jnp
from jax import lax
from jax.experimental import pallas as pl
from jax.experimental.pallas import tpu as pltpu

F32 = jnp.float32
BF16 = jnp.bfloat16

D_MODEL = 1024
HEAD_DIM = 64
NORM_EPS = 1e-6
NEG_INF = -1e30
ROPE_THETA = 10000.0
LOG2E = 1.4426950408889634

A_HEADS, A_KV_HEADS, A_WINDOW = 4, 2, 128
B_HEADS, B_Q_RANK, B_KV_RANK, B_NOPE, B_ROPE = 4, 256, 128, 64, 32
C_HEADS, C_KV_HEADS, C_WINDOW = 8, 2, 512
C_CMP_BLOCK, C_CMP_STRIDE, C_CMP_HIDDEN = 32, 16, 128
C_SEL_BLOCK, C_SEL_TOP = 64, 16
N_EXPERTS = 8

LANES = 128
VMEM_LIMIT = 48 * 1024 * 1024

P_CQ, P_AQ, P_BCQ = 0, 512, 768
P_BCKV, P_CKC, P_CVC, P_KR, P_KRROT, P_GATE = 1024, 1152, 1280, 1408, 1536, 1664
P_KSEL, KD_SEL = 1792, 256
P_KWIN, KD_WIN = 2304, 128
P_KA, KD_A = 2560, 128
P_WIDTH = 2816
V_ROWS = 80
VT_SEL, VT_WIN, VT_A = 0, 2, 4
VT_SLABS = 6
MASK_BIAS = -1e30
M_INIT = -1e29

TM_PROJ = 512
TM_OUT = 1024
TQ_ATT = 128
TK_BAND = 128
NSUB_A = 4
NSUB_WIN = 4
TQ_MLA = 256
TK_SEL = 256
TM_FFN = 512
TF_FFN = 2048
TF_MOE = 1792
TS_MOE = 512
TM_MOE = 512
CH_MOE = 128
U_MOE = 24


def _alibi_slopes(n):
    return [float(np.float32(2.0 ** (-8.0 * (i + 1) / n))) for i in range(n)]


def _split_bf16(x):
    parts = []
    for _ in range(3):
        p = float(np.asarray(x, np.float32).astype(jnp.bfloat16))
        parts.append(p)
        x = x - p
    return parts


def _dot(a, b):
    return jnp.dot(a, b, preferred_element_type=F32)


def _dot_nt(a, b):
    return lax.dot_general(a, b, (((1,), (1,)), ((), ())), preferred_element_type=F32)


def _rms(x, g):
    return x * lax.rsqrt(jnp.mean(x * x, axis=-1, keepdims=True) + NORM_EPS) * g


def _sigmoid(x):
    return 1.0 / (1.0 + jnp.exp(-x))


def _params(*sem):
    return pltpu.CompilerParams(dimension_semantics=sem, vmem_limit_bytes=VMEM_LIMIT)


def _inproj_kernel(x_ref, g_ref, w_ref, wvt_ref, vone_ref, kc_ref, cos_ref, sin_ref, gq_ref, gkv_ref,
                   wq_ref, wqr_ref, wk_ref, wvtb_ref, voneb_ref, o_ref, vt_ref, qb_out, kb_out, vtb_out):
    h = _rms(x_ref[...], g_ref[...]).astype(BF16)
    for n in range(P_WIDTH // 256):
        o_ref[:, n * 256:(n + 1) * 256] = _dot(h, w_ref[:, n * 256:(n + 1) * 256]).astype(BF16)
    _mla_prep(o_ref[:, P_BCQ:P_BCQ + B_Q_RANK], o_ref[:, P_BCKV:P_BCKV + B_KV_RANK], o_ref[:, P_KR:P_KR + LANES],
              o_ref[:, P_KRROT:P_KRROT + LANES], cos_ref[...], sin_ref[...], gq_ref, gkv_ref,
              wq_ref, wqr_ref, wk_ref, wvtb_ref, voneb_ref, qb_out, kb_out, vtb_out)
    for g in range(C_KV_HEADS):
        o_ref[:, P_KSEL + g * KD_SEL + 64:P_KSEL + g * KD_SEL + 192] = kc_ref[:, 0:128]
        o_ref[:, P_KWIN + g * KD_WIN + 64:P_KWIN + (g + 1) * KD_WIN] = kc_ref[:, 128:192]
    for g in range(A_KV_HEADS):
        o_ref[:, P_KA + g * KD_A + 64:P_KA + (g + 1) * KD_A] = kc_ref[:, 128:192]
    vt_ref[...] = (_dot_nt(wvt_ref[...], h) + vone_ref[...]).astype(BF16)


def _key_constants(S):
    s = np.arange(S)
    kc = np.zeros((S, 256), np.float32)
    kc[:, 0:3] = (s % TK_SEL)[:, None]
    kc[s, 64 + s // C_SEL_BLOCK] = 1.0
    kc[:, 128:131] = (s % TK_BAND)[:, None]
    return jnp.asarray(kc, dtype=BF16)


def _inproj(x, g, w, wvt, S, cos, sin, g_cq, g_ckv, mla_w):
    T = x.shape[0]
    TM = TM_PROJ
    assert S // C_SEL_BLOCK <= 64 and TK_SEL <= 256 and TK_BAND <= 256
    vone = np.zeros((VT_SLABS * V_ROWS, 1), np.float32)
    vone[HEAD_DIM::V_ROWS] = 1.0
    QW = B_HEADS * LANES
    VR = B_HEADS * V_ROWS
    voneb = np.zeros((VR, 1), np.float32)
    voneb[HEAD_DIM::V_ROWS] = 1.0
    wq, wqr, wk, wvtb = mla_w
    nblk = S // TM
    full = lambda a: pl.BlockSpec(a.shape, lambda i: (0, 0))
    rows = lambda w_: pl.BlockSpec((TM, w_), lambda i: (i, 0))
    cols = lambda r: pl.BlockSpec((r, TM), lambda i: (0, i))
    consts = [g, w, wvt, jnp.asarray(vone)]
    mla_consts = [g_cq.reshape(1, -1), g_ckv.reshape(1, -1), wq, wqr, wk, wvtb, jnp.asarray(voneb)]
    return pl.pallas_call(
        _inproj_kernel,
        grid=(T // TM,),
        in_specs=[rows(D_MODEL)] + [full(a) for a in consts]
                 + [pl.BlockSpec((TM, 256), lambda i: (i % nblk, 0)), rows(LANES), rows(LANES)]
                 + [full(a) for a in mla_consts],
        out_specs=[rows(P_WIDTH), cols(VT_SLABS * V_ROWS), cols(QW), rows(QW), cols(VR)],
        out_shape=[jax.ShapeDtypeStruct((T, P_WIDTH), BF16),
                   jax.ShapeDtypeStruct((VT_SLABS * V_ROWS, T), BF16),
                   jax.ShapeDtypeStruct((QW, T), BF16), jax.ShapeDtypeStruct((T, QW), BF16),
                   jax.ShapeDtypeStruct((VR, T), BF16)],
        compiler_params=_params("parallel"),
        name="inproj",
    )(x, *consts, _key_constants(S), cos, sin, *mla_consts)


def _build_w_in(w_in):
    cuts = np.cumsum([0, 256, 128, 128, 256, 128, 32, 512, 128, 128, 128, 128, 128, 128, 24])
    seg = [w_in[:, cuts[i]:cuts[i + 1]] for i in range(14)]
    a_q, a_k, a_v, b_cq, b_ckv, b_kr, c_q, c_kc, c_vc, c_ks, c_vs, c_kw, c_vw, c_g = seg
    z = lambda n: jnp.zeros((w_in.shape[0], n), w_in.dtype)
    half = B_ROPE // 2
    kr_rot = jnp.concatenate([-b_kr[:, half:], b_kr[:, :half]], axis=1)
    cols = [c_q, a_q, b_cq, b_ckv, c_kc, c_vc,
            z(B_NOPE), b_kr, z(LANES - B_NOPE - B_ROPE),
            z(B_NOPE), kr_rot, z(LANES - B_NOPE - B_ROPE),
            c_g, z(LANES - 24)]
    for k, kd in ((c_ks, KD_SEL), (c_kw, KD_WIN), (a_k, KD_A)):
        for g in range(2):
            cols += [k[:, g * HEAD_DIM:(g + 1) * HEAD_DIM], z(kd - HEAD_DIM)]
    w = jnp.concatenate(cols, axis=1).astype(BF16)
    rows = []
    for v in (c_vs, c_vw, a_v):
        for g in range(2):
            rows += [v[:, g * HEAD_DIM:(g + 1) * HEAD_DIM].T, jnp.zeros((V_ROWS - HEAD_DIM, w_in.shape[0]), w_in.dtype)]
    return w, jnp.concatenate(rows, axis=0).astype(BF16)


def _rope_tab_kernel(pos_ref, inv_ref, cos_ref, sin_ref):
    ang = pos_ref[...] * inv_ref[...]
    cos_ref[...] = jnp.cos(ang)
    sin_ref[...] = jnp.sin(ang)


def _rope_tables(positions):
    T = positions.size
    inv = (ROPE_THETA ** (-np.arange(0, B_ROPE, 2, dtype=np.float32) / B_ROPE)).astype(np.float32)
    inv128 = np.zeros((1, LANES), np.float32)
    inv128[0, B_NOPE:B_NOPE + B_ROPE // 2] = inv
    inv128[0, B_NOPE + B_ROPE // 2:B_NOPE + B_ROPE] = inv
    pos = positions.reshape(T, 1).astype(F32)
    return pl.pallas_call(
        _rope_tab_kernel,
        grid=(T // TM_PROJ,),
        in_specs=[pl.BlockSpec((TM_PROJ, 1), lambda i: (i, 0)),
                  pl.BlockSpec((1, LANES), lambda i: (0, 0))],
        out_specs=[pl.BlockSpec((TM_PROJ, LANES), lambda i: (i, 0))] * 2,
        out_shape=[jax.ShapeDtypeStruct((T, LANES), F32)] * 2,
        compiler_params=_params("parallel"),
        name="rope_tables",
    )(pos, jnp.asarray(inv128))


def _flash_kernel(*refs, TQ, TK, G, R, NKT, slopes, scale, sel, gate_branch, n_side):
    refs = list(refs)
    lst_ref, cnt_ref = (refs.pop(0), refs.pop(0)) if sel else (None, None)
    q_ref = refs.pop(0)
    selb_ref = refs.pop(0) if sel else None
    k_refs = [refs.pop(0) for _ in range(G)]
    v_refs = [refs.pop(0) for _ in range(G)]
    gate_ref = refs.pop(0) if gate_branch is not None else None
    side_in = [refs.pop(0) for _ in range(n_side)]
    o_ref = refs.pop(0)
    side_out = [refs.pop(0) for _ in range(n_side)]
    qa_sc = refs.pop(0) if slopes is not None else None
    m_sc, acc_sc, s_sc, mx_sc = refs
    for src, dst in zip(side_in, side_out):
        dst[...] = src[...].astype(BF16)
    i = pl.program_id(1)
    step = pl.program_id(0) * pl.num_programs(1) + i
    t0 = i * TQ
    RQ = R * TQ
    lane = lax.broadcasted_iota(jnp.int32, (1, RQ), 1)
    qpos = t0 + (lane & (TQ - 1))
    hi = (t0 + TQ + TK - 1) // TK
    if gate_ref is not None:
        gate_t = _sigmoid(gate_ref[...].astype(F32)).T
    slope_vecs = []
    if slopes is not None:
        q_t = q_ref[...].astype(F32).T
        frow = lax.broadcasted_iota(jnp.int32, (HEAD_DIM, TQ), 0)
    for g in range(G):
        if slopes is not None:
            slope_vec = jnp.zeros((1, RQ), F32)
            for r in range(R):
                hh = g * R + r
                cols = slice(r * TQ, (r + 1) * TQ)
                s3 = _split_bf16(slopes[hh] * LOG2E)
                slope_vec = jnp.where(lane // TQ == r, sum(s3), slope_vec)
                qa_sc[g, 0:HEAD_DIM, cols] = (q_t[hh * HEAD_DIM:(hh + 1) * HEAD_DIM, :] * (scale * LOG2E)).astype(BF16)
                qa_sc[g, HEAD_DIM:2 * HEAD_DIM, cols] = jnp.where(
                    frow == 0, s3[0], jnp.where(frow == 1, s3[1], jnp.where(frow == 2, s3[2], 0.0))).astype(BF16)
                if sel:
                    qa_sc[g, 2 * HEAD_DIM:3 * HEAD_DIM, cols] = selb_ref[g * HEAD_DIM:(g + 1) * HEAD_DIM, :]
                    qa_sc[g, 3 * HEAD_DIM:4 * HEAD_DIM, cols] = jnp.zeros((HEAD_DIM, TQ), BF16)
            slope_vecs.append(slope_vec)
        else:
            slope_vecs.append(None)
        m_sc[g] = jnp.full((1, RQ), M_INIT, F32)
        acc_sc[g] = jnp.zeros((V_ROWS, RQ), F32)

    def scores(j, slot):
        ks = pl.multiple_of(jnp.asarray(j, jnp.int32) * TK, TK)
        for g in range(G):
            qa_t = qa_sc[g] if slopes is not None else q_ref[g * LANES:(g + 1) * LANES, :]
            st = _dot(k_refs[g][pl.ds(ks, TK), :], qa_t)
            s_sc[slot, g] = st
            mx_sc[slot, g] = jnp.max(st, axis=0, keepdims=True)

    def tile(j, slot, causal):
        ks = pl.multiple_of(jnp.asarray(j, jnp.int32) * TK, TK)
        for g in range(G):
            st = s_sc[slot, g]
            if causal:
                kpos = ks + lax.broadcasted_iota(jnp.int32, (TK, 1), 0)
                st = jnp.where(kpos <= qpos, st, MASK_BIAS)
            m = m_sc[g]
            mx = jnp.max(st, axis=0, keepdims=True) if causal else mx_sc[slot, g]
            if slopes is not None:
                c = slope_vecs[g] * ks.astype(F32)
                mn = jnp.maximum(m, mx + c)
                p = jnp.exp2(st - (mn - c))
            else:
                mn = jnp.maximum(m, mx)
                p = jnp.exp2(st - mn)
            acc_sc[g] = jnp.exp2(m - mn) * acc_sc[g] + _dot(v_refs[g][:, pl.ds(ks, TK)], p.astype(BF16))
            m_sc[g] = mn

    if sel:
        n = cnt_ref[step]
        jt = lambda t: lst_ref[step * NKT + t]
    else:
        n = hi
        jt = lambda t: t
    n_pair = (n - 1) // 2

    def pair(p, carry):
        t = 2 * p
        scores(jt(t + 1), 1)
        tile(jt(t), 0, False)
        scores(jt(t + 2), 0)
        tile(jt(t + 1), 1, False)
        return carry

    scores(jt(0), 0)
    lax.fori_loop(0, n_pair, pair, 0)

    @pl.when(2 * n_pair == n - 1)
    def _():
        tile(jt(n - 1), 0, True)

    @pl.when(2 * n_pair == n - 2)
    def _():
        scores(jt(n - 1), 1)
        tile(jt(n - 2), 0, False)
        tile(jt(n - 1), 1, True)

    pieces = []
    for g in range(G):
        acc = acc_sc[g]
        num = acc[0:HEAD_DIM, :]
        den = acc[HEAD_DIM:HEAD_DIM + 1, :]
        o_t = num * (1.0 / den)
        for r in range(R):
            piece = o_t[:, r * TQ:(r + 1) * TQ]
            if gate_ref is not None:
                c = 3 * (g * R + r) + gate_branch
                piece = piece * gate_t[c:c + 1, :]
            pieces.append(piece)
    o_ref[...] = jnp.concatenate(pieces, axis=0).T.astype(o_ref.dtype)


def _banded_kernel(*refs, TQ, NSUB, G, R, W, slopes, scale, has_sink, gate_branch):
    refs = list(refs)
    q_ref = refs.pop(0)
    k_refs = [refs.pop(0) for _ in range(G)]
    v_refs = [refs.pop(0) for _ in range(G)]
    sink_ref = refs.pop(0) if has_sink else None
    gate_ref = refs.pop(0) if gate_branch is not None else None
    o_ref = refs.pop(0)
    qa_sc, m_sc, acc_sc, s_sc = refs
    TK = TQ
    RQ = R * TQ
    n_st = W // TK + 1
    i = pl.program_id(1)
    lane = lax.broadcasted_iota(jnp.int32, (1, RQ), 1)
    frow = lax.broadcasted_iota(jnp.int32, (HEAD_DIM, TQ), 0)

    def process(sub, check):
        qt = i * NSUB + sub
        t0 = qt * TQ
        rows = slice(sub * TQ, (sub + 1) * TQ)
        qpos = t0 + (lane & (TQ - 1))
        if gate_ref is not None:
            gate_t = _sigmoid(gate_ref[rows, :].astype(F32)).T
        q_t = q_ref[rows, :].astype(F32).T
        slope_vecs = []
        for g in range(G):
            slope_vec = jnp.zeros((1, RQ), F32)
            for r in range(R):
                hh = g * R + r
                cols = slice(r * TQ, (r + 1) * TQ)
                s3 = _split_bf16(slopes[hh] * LOG2E)
                slope_vec = jnp.where(lane // TQ == r, sum(s3), slope_vec)
                qa_sc[sub, g, 0:HEAD_DIM, cols] = (
                    q_t[hh * HEAD_DIM:(hh + 1) * HEAD_DIM, :] * (scale * LOG2E)).astype(BF16)
                qa_sc[sub, g, HEAD_DIM:2 * HEAD_DIM, cols] = jnp.where(
                    frow == 0, s3[0], jnp.where(frow == 1, s3[1], jnp.where(frow == 2, s3[2], 0.0))).astype(BF16)
            slope_vecs.append(slope_vec)
            m_sc[sub, g] = jnp.full((1, RQ), M_INIT, F32)
            acc_sc[sub, g] = jnp.zeros((V_ROWS, RQ), F32)

        def scores(j, slot):
            ks = pl.multiple_of(j * TK, TK)
            for g in range(G):
                s_sc[sub, slot, g] = _dot(k_refs[g][pl.ds(ks, TK), :], qa_sc[sub, g])

        def tile(j, slot, mask, live):
            ks = pl.multiple_of(j * TK, TK)
            for g in range(G):
                st = s_sc[sub, slot, g]
                if mask is not None:
                    kpos = ks + lax.broadcasted_iota(jnp.int32, (TK, 1), 0)
                    ok = kpos <= qpos if mask == "causal" else kpos > qpos - W
                    st = jnp.where(ok, st, MASK_BIAS)
                if live is not None:
                    st = jnp.where(live, st, MASK_BIAS)
                m = m_sc[sub, g]
                c = slope_vecs[g] * ks.astype(F32)
                mn = jnp.maximum(m, jnp.max(st, axis=0, keepdims=True) + c)
                p = jnp.exp2(st - (mn - c))
                acc_sc[sub, g] = (jnp.exp2(m - mn) * acc_sc[sub, g]
                                  + _dot(v_refs[g][:, pl.ds(ks, TK)], p.astype(BF16)))
                m_sc[sub, g] = mn

        js = [qt - n_st + 1 + t for t in range(n_st)]
        jc = [jnp.maximum(j, 0) for j in js] if check else js
        scores(jc[0], 0)
        for t in range(n_st):
            if t + 1 < n_st:
                scores(jc[t + 1], (t + 1) & 1)
            mask = "causal" if t == n_st - 1 else ("window" if t == 0 else None)
            tile(jc[t], t & 1, mask, (js[t] >= 0) if check and t < n_st - 1 else None)

        pieces = []
        for g in range(G):
            acc = acc_sc[sub, g]
            num = acc[0:HEAD_DIM, :]
            den = acc[HEAD_DIM:HEAD_DIM + 1, :]
            if has_sink:
                sink_vec = jnp.zeros((1, RQ), F32)
                for r in range(R):
                    sink_vec = jnp.where(lane // TQ == r, sink_ref[g * R + r] * LOG2E, sink_vec)
                m_true = m_sc[sub, g] - slope_vecs[g] * qpos.astype(F32)
                mf = jnp.maximum(m_true, sink_vec)
                a = jnp.exp2(m_true - mf)
                num = num * a
                den = den * a + jnp.exp2(sink_vec - mf)
            o_t = num * (1.0 / den)
            for r in range(R):
                piece = o_t[:, r * TQ:(r + 1) * TQ]
                if gate_ref is not None:
                    c = 3 * (g * R + r) + gate_branch
                    piece = piece * gate_t[c:c + 1, :]
                pieces.append(piece)
        o_ref[rows, :] = jnp.concatenate(pieces, axis=0).T.astype(o_ref.dtype)

    all_tiles_exist = i * NSUB >= n_st - 1

    @pl.when(all_tiles_exist)
    def _():
        for sub in range(NSUB):
            process(sub, False)

    @pl.when(jnp.logical_not(all_tiles_exist))
    def _():
        for sub in range(NSUB):
            process(sub, True)


def _banded_attn(P, VT, B, S, *, name, NSUB, G, R, KD, q_col, k_col, vt_slab, W, slopes, scale, sinks=None,
                 gate_branch=None):
    T = B * S
    TQ = TQ_ATT
    TB = NSUB * TQ
    nq = S // TB
    RQ = R * TQ
    assert TK_BAND == TQ and W % TQ == 0 and S % TB == 0 and KD == 2 * HEAD_DIM
    QW = G * R * HEAD_DIM
    in_specs = [pl.BlockSpec((TB, QW), lambda b, i: (b * nq + i, q_col // QW))]
    args = [P]
    for g in range(G):
        in_specs.append(pl.BlockSpec((S, KD), lambda b, i, g=g: (b, k_col // KD + g)))
        args.append(P)
    for g in range(G):
        in_specs.append(pl.BlockSpec((V_ROWS, S), lambda b, i, g=g: (vt_slab + g, b)))
        args.append(VT)
    if sinks is not None:
        in_specs.append(pl.BlockSpec(memory_space=pltpu.SMEM))
        args.append(sinks.astype(F32))
    if gate_branch is not None:
        in_specs.append(pl.BlockSpec((TB, LANES), lambda b, i: (b * nq + i, P_GATE // LANES)))
        args.append(P)
    kern = functools.partial(_banded_kernel, TQ=TQ, NSUB=NSUB, G=G, R=R, W=W, slopes=slopes, scale=scale,
                             has_sink=sinks is not None, gate_branch=gate_branch)
    return pl.pallas_call(
        kern,
        grid=(B, nq),
        in_specs=in_specs,
        out_specs=pl.BlockSpec((TB, QW), lambda b, i: (b * nq + i, 0)),
        out_shape=jax.ShapeDtypeStruct((T, QW), BF16),
        scratch_shapes=[pltpu.VMEM((NSUB, G, KD, RQ), BF16), pltpu.VMEM((NSUB, G, 1, RQ), F32),
                        pltpu.VMEM((NSUB, G, V_ROWS, RQ), F32), pltpu.VMEM((NSUB, 2, G, TQ, RQ), F32)],
        compiler_params=_params("parallel", "parallel"),
        name=name,
    )(*args)


def _flash_attn(q, k, vt, B, S, *, name, TQ, TK, G, R, KD, q_col, q_width, k_col, vt_slab, slopes=None, scale=1.0,
                selb=None, tiles=None, gate=None, gate_branch=None, side=()):
    T = B * S
    nq = S // TQ
    RQ = R * TQ
    assert TK % TQ == 0 and S % TK == 0
    OW = G * R * HEAD_DIM
    if slopes is not None:
        in_specs = [pl.BlockSpec((TQ, q_width), lambda b, i, *_: (b * nq + i, q_col // q_width))]
    else:
        in_specs = [pl.BlockSpec((G * KD, TQ), lambda b, i, *_: (0, b * nq + i))]
    args = [q]
    if selb is not None:
        in_specs.append(pl.BlockSpec((LANES, TQ), lambda b, i, *_: (0, b * nq + i)))
        args.append(selb)
    for g in range(G):
        in_specs.append(pl.BlockSpec((S, KD), lambda b, i, *_, g=g: (b, k_col // KD + g)))
        args.append(k)
    for g in range(G):
        in_specs.append(pl.BlockSpec((V_ROWS, S), lambda b, i, *_, g=g: (vt_slab + g, b)))
        args.append(vt)
    if gate_branch is not None:
        in_specs.append(pl.BlockSpec((TQ, LANES), lambda b, i, *_: (b * nq + i, P_GATE // LANES)))
        args.append(gate)
    out_specs = [pl.BlockSpec((TQ, OW), lambda b, i, *_: (b * nq + i, 0))]
    out_shape = [jax.ShapeDtypeStruct((T, OW), BF16)]
    for w in side:
        rb = w.shape[0] // (B * nq)
        assert w.shape[0] % (B * nq) == 0 and rb % 16 == 0 and w.shape[1] % LANES == 0
        spec = pl.BlockSpec((rb, w.shape[1]), lambda b, i, *_: (b * nq + i, 0))
        in_specs.append(spec)
        args.append(w)
        out_specs.append(spec)
        out_shape.append(jax.ShapeDtypeStruct(w.shape, BF16))
    scratch = []
    if slopes is not None:
        scratch.append(pltpu.VMEM((G, KD, RQ), BF16))
    scratch += [pltpu.VMEM((G, 1, RQ), F32), pltpu.VMEM((G, V_ROWS, RQ), F32), pltpu.VMEM((2, G, TK, RQ), F32),
                pltpu.VMEM((2, G, 1, RQ), F32)]
    kern = functools.partial(_flash_kernel, TQ=TQ, TK=TK, G=G, R=R, NKT=S // TK, slopes=slopes, scale=scale,
                             sel=selb is not None, gate_branch=gate_branch, n_side=len(side))
    prefetch = list(tiles) if selb is not None else []
    grid_spec = pltpu.PrefetchScalarGridSpec(
        num_scalar_prefetch=len(prefetch),
        grid=(B, nq),
        in_specs=in_specs,
        out_specs=out_specs,
        scratch_shapes=scratch)
    outs = pl.pallas_call(
        kern,
        grid_spec=grid_spec,
        out_shape=out_shape,
        compiler_params=_params("parallel", "parallel"),
        name=name,
    )(*prefetch, *args)
    return list(outs)


def _mla_prep(cq, ckv, kr, krr, cos, sin, gq_ref, gkv_ref, wq_ref, wqr_ref, wk_ref, wvt_ref, vone_ref,
              q_out, k_out, vt_out):
    scale = (B_NOPE + B_ROPE) ** -0.5 * LOG2E
    nq = _rms(cq.astype(F32), gq_ref[...]).astype(BF16)
    nkv = _rms(ckv.astype(F32), gkv_ref[...]).astype(BF16)
    q1 = _dot_nt(wq_ref[...], nq)
    q2 = _dot_nt(wqr_ref[...], nq)
    kk = _dot(nkv, wk_ref[...])
    krope = kr.astype(F32) * cos + krr.astype(F32) * sin
    cos_t = cos.T
    sin_t = sin.T
    for h in range(B_HEADS):
        sl = slice(h * LANES, (h + 1) * LANES)
        q_out[sl, :] = ((q1[sl, :] * cos_t + q2[sl, :] * sin_t) * scale).astype(BF16)
        k_out[:, sl] = (kk[:, sl] + krope).astype(BF16)
    vt_out[...] = (_dot_nt(wvt_ref[...], nkv) + vone_ref[...]).astype(BF16)


def _build_mla_weights(w_uq, w_ukv):
    R1, R2 = w_uq.shape[0], w_ukv.shape[0]
    half = B_ROPE // 2
    hq = B_NOPE + B_ROPE
    wq, wqr, wk, wvt = [], [], [], []
    for h in range(B_HEADS):
        nope = w_uq[:, h * hq:h * hq + B_NOPE]
        rope = w_uq[:, h * hq + B_NOPE:(h + 1) * hq]
        rot = jnp.concatenate([-rope[:, half:], rope[:, :half]], axis=1)
        pad = jnp.zeros((R1, LANES - hq), w_uq.dtype)
        wq += [nope, rope, pad]
        wqr += [jnp.zeros((R1, B_NOPE), w_uq.dtype), rot, pad]
        wk += [w_ukv[:, h * 128:h * 128 + B_NOPE], jnp.zeros((R2, LANES - B_NOPE), w_ukv.dtype)]
        wvt += [w_ukv[:, h * 128 + B_NOPE:(h + 1) * 128].T, jnp.zeros((V_ROWS - HEAD_DIM, R2), w_ukv.dtype)]
    cat = lambda xs: jnp.concatenate(xs, axis=1).astype(BF16)
    return cat(wq).T, cat(wqr).T, cat(wk), jnp.concatenate(wvt, axis=0).astype(BF16)


def _gelu_tanh(x):
    return 0.5 * x * (1.0 + jnp.tanh(np.float32(np.sqrt(2.0 / np.pi)) * (x + 0.044715 * (x * x * x))))


def _compress_kernel(xk_ref, xv_ref, pek_ref, pev_ref, w1k_ref, w1v_ref, w2k_ref, w2v_ref, kc_ref, vct_ref, *, NC):
    for x_ref, pe_ref, w1_ref, w2_ref, is_v in ((xk_ref, pek_ref, w1k_ref, w2k_ref, False),
                                                 (xv_ref, pev_ref, w1v_ref, w2v_ref, True)):
        x = x_ref[...].astype(F32)
        a = _dot((x + pe_ref[0:1, :]).astype(BF16), w1_ref[0])
        b = _dot((x + pe_ref[1:2, :]).astype(BF16), w1_ref[1])
        hid = a + pltpu.roll(b, NC - 1, 0)
        o = _dot(_gelu_tanh(hid).astype(BF16), w2_ref[...])
        n = lax.broadcasted_iota(jnp.int32, o.shape, 0)
        o = jnp.where(n < NC - 1, o, 0.0)
        if is_v:
            vct_ref[...] = o.T.astype(vct_ref.dtype)
        else:
            kc_ref[...] = o.astype(kc_ref.dtype)


def _build_compress_weights(w1, w2, pe):
    half = C_CMP_BLOCK // 2
    G = C_KV_HEADS
    eye = jnp.eye(G, dtype=w1.dtype)
    w1s = [jnp.einsum('lde,hg->lhdge', w1[c * half:(c + 1) * half], eye).reshape(half * G * HEAD_DIM, G * C_CMP_HIDDEN)
           for c in range(2)]
    w1f = jnp.stack(w1s).astype(BF16)
    w2f = jnp.einsum('ed,hg->hegd', w2, eye).reshape(G * C_CMP_HIDDEN, G * HEAD_DIM).astype(BF16)
    pes = [jnp.broadcast_to(pe[c * half:(c + 1) * half, None, :], (half, G, HEAD_DIM)).reshape(1, -1) for c in range(2)]
    pef = jnp.concatenate(pes, axis=0).astype(F32)
    return w1f, w2f, pef


def _compress(P, B, S, k_weights, v_weights):
    NC = S // C_CMP_STRIDE
    CW = C_CMP_STRIDE * C_KV_HEADS * HEAD_DIM
    xk = P[:, P_CKC:P_CKC + LANES].reshape(B * NC, CW)
    xv = P[:, P_CVC:P_CVC + LANES].reshape(B * NC, CW)
    w1kf, w2kf, pekf = k_weights
    w1vf, w2vf, pevf = v_weights
    xspec = pl.BlockSpec((NC, CW), lambda b: (b, 0))
    full2 = lambda a: pl.BlockSpec(a.shape, lambda b: (0, 0))
    full3 = lambda a: pl.BlockSpec(a.shape, lambda b: (0, 0, 0))
    return pl.pallas_call(
        functools.partial(_compress_kernel, NC=NC),
        grid=(B,),
        in_specs=[xspec, xspec, full2(pekf), full2(pevf), full3(w1kf), full3(w1vf), full2(w2kf), full2(w2vf)],
        out_specs=[pl.BlockSpec((NC, LANES), lambda b: (b, 0)), pl.BlockSpec((LANES, NC), lambda b: (b, 0))],
        out_shape=[jax.ShapeDtypeStruct((B * NC, LANES), BF16), jax.ShapeDtypeStruct((B * LANES, NC), BF16)],
        compiler_params=_params("parallel"),
        name="nsa_compress",
    )(xk, xv, pekf, pevf, w1kf, w1vf, w2kf, w2vf)


def _cmp_sel_kernel(q_ref, kc_ref, vct_ref, cfeat_ref, ovt_ref, gate_ref, ocmp_ref, sel_ref, cnt_ref,
                    *, TQ, NC, NSEL, NTOP, slopes):
    i = pl.program_id(1)
    t0 = i * TQ
    R = C_HEADS // C_KV_HEADS
    scale = HEAD_DIM ** -0.5
    gate_t = _sigmoid(gate_ref[...].astype(F32)).T
    last = C_CMP_BLOCK - 1
    tlane = t0 + lax.broadcasted_iota(jnp.int32, (NC, TQ), 1)
    nrow = lax.broadcasted_iota(jnp.int32, (NC, TQ), 0)
    cvalid = (nrow * C_CMP_STRIDE + last <= tlane) & (nrow < NC - 1)
    hasc = ((t0 + lax.broadcasted_iota(jnp.int32, (1, TQ), 1)) >= last).astype(F32)
    jrow = lax.broadcasted_iota(jnp.int32, (NSEL, TQ), 0)
    tl = t0 + lax.broadcasted_iota(jnp.int32, (NSEL, TQ), 1)
    avail = jrow * C_SEL_BLOCK <= tl
    cur = tl // C_SEL_BLOCK
    forced = (jrow == 0) | (jrow == cur) | (jrow == cur - 1)
    q_t = q_ref[...].astype(F32).T
    frow = lax.broadcasted_iota(jnp.int32, (HEAD_DIM, TQ), 0)
    sub8 = lax.broadcasted_iota(jnp.int32, (8, TQ), 0)
    pieces, sel_parts = [], []
    picked = jnp.zeros((NSEL, TQ), F32)
    for g in range(C_KV_HEADS):
        kca = jnp.concatenate([kc_ref[:, g * HEAD_DIM:(g + 1) * HEAD_DIM], cfeat_ref[...]], axis=1)
        vo = jnp.concatenate([vct_ref[g * HEAD_DIM:(g + 1) * HEAD_DIM, :], ovt_ref[...]], axis=0)
        imp = jnp.zeros((HEAD_DIM, TQ), F32)
        for r in range(R):
            hh = g * R + r
            s3 = _split_bf16(slopes[hh] * LOG2E)
            feat = jnp.zeros((HEAD_DIM, TQ), F32)
            for k in range(3):
                feat = jnp.where(frow == k, s3[k] * 256.0, jnp.where(frow == 3 + k, s3[k], feat))
            qa = jnp.concatenate([q_t[hh * HEAD_DIM:(hh + 1) * HEAD_DIM, :] * (scale * LOG2E), feat],
                                 axis=0).astype(BF16)
            st = jnp.where(cvalid, _dot(kca, qa), NEG_INF)
            et = jnp.exp2(st - jnp.max(st, axis=0, keepdims=True))
            inv = hasc / jnp.sum(et, axis=0, keepdims=True)
            res = _dot(vo, et.astype(BF16))
            pieces.append(res[0:HEAD_DIM, :] * (inv * gate_t[3 * hh:3 * hh + 1, :]))
            imp = imp + res[HEAD_DIM:2 * HEAD_DIM, :] * inv
        v = jnp.where(avail, imp[0:NSEL, :], NEG_INF)
        v = jnp.where(forced, -NEG_INF, v)
        ranks = [jnp.zeros((8, TQ), F32) for _ in range(NSEL // 8)]
        for ii in range(NSEL):
            ri = v[ii:ii + 1, :]
            for k in range(NSEL // 8):
                vk = v[8 * k:8 * k + 8, :]
                if 8 * k > ii:
                    inc = jnp.where(ri >= vk, 1.0, 0.0)
                elif 8 * k + 7 < ii:
                    inc = jnp.where(ri > vk, 1.0, 0.0)
                else:
                    inc = jnp.where(sub8 + 8 * k > ii, jnp.where(ri >= vk, 1.0, 0.0), jnp.where(ri > vk, 1.0, 0.0))
                ranks[k] = ranks[k] + inc
        rank = jnp.concatenate(ranks, axis=0)
        sel_parts.append(jnp.where(rank < NTOP, 0.0, MASK_BIAS))
        if NSEL < HEAD_DIM:
            sel_parts.append(jnp.zeros((HEAD_DIM - NSEL, TQ), F32))
        picked = picked + jnp.where(rank < NTOP, 1.0, 0.0)
    sel_ref[...] = jnp.concatenate(sel_parts, axis=0).astype(sel_ref.dtype)
    cnt_ref[...] = _dot_nt(jnp.ones((8, TQ), BF16), picked.astype(BF16))
    ocmp_ref[...] = jnp.concatenate(pieces, axis=0).T.astype(ocmp_ref.dtype)


def _overlap_t(S):
    n_cmp = (S - C_CMP_BLOCK) // C_CMP_STRIDE + 1
    n_sel = S // C_SEL_BLOCK
    NC = S // C_CMP_STRIDE
    cstart = np.arange(n_cmp) * C_CMP_STRIDE
    cend = cstart + C_CMP_BLOCK - 1
    sstart = np.arange(n_sel) * C_SEL_BLOCK
    send = sstart + C_SEL_BLOCK - 1
    ov = np.clip(np.minimum(cend[:, None], send[None, :]) - np.maximum(cstart[:, None], sstart[None, :]) + 1,
                 0, None).astype(np.float32) / C_CMP_STRIDE
    ovt = np.zeros((HEAD_DIM, NC), np.float32)
    ovt[:n_sel, :n_cmp] = ov.T
    cend_all = np.arange(NC) * C_CMP_STRIDE + C_CMP_BLOCK - 1
    cfeat = np.zeros((NC, HEAD_DIM), np.float32)
    cfeat[:, 0:3] = (cend_all // 256)[:, None]
    cfeat[:, 3:6] = (cend_all % 256)[:, None]
    return jnp.asarray(ovt, dtype=BF16), jnp.asarray(cfeat, dtype=BF16)


def _cmp_sel(P, kc, vct, B, S, slopes):
    T = B * S
    TQ = TQ_ATT
    nq = S // TQ
    NC = S // C_CMP_STRIDE
    NSEL = S // C_SEL_BLOCK
    assert NSEL <= HEAD_DIM and NSEL % 8 == 0 and S <= 256 * 256
    QW = C_HEADS * HEAD_DIM
    ovt, cfeat = _overlap_t(S)
    kern = functools.partial(_cmp_sel_kernel, TQ=TQ, NC=NC, NSEL=NSEL, NTOP=min(C_SEL_TOP, NSEL), slopes=slopes)
    return pl.pallas_call(
        kern,
        grid=(B, nq),
        in_specs=[pl.BlockSpec((TQ, QW), lambda b, i: (b * nq + i, P_CQ // QW)),
                  pl.BlockSpec((NC, LANES), lambda b, i: (b, 0)),
                  pl.BlockSpec((LANES, NC), lambda b, i: (b, 0)),
                  pl.BlockSpec((NC, HEAD_DIM), lambda b, i: (0, 0)),
                  pl.BlockSpec((HEAD_DIM, NC), lambda b, i: (0, 0)),
                  pl.BlockSpec((TQ, LANES), lambda b, i: (b * nq + i, P_GATE // LANES))],
        out_specs=[pl.BlockSpec((TQ, QW), lambda b, i: (b * nq + i, 0)),
                   pl.BlockSpec((LANES, TQ), lambda b, i: (0, b * nq + i)),
                   pl.BlockSpec((8, NSEL), lambda b, i: (b * nq + i, 0))],
        out_shape=[jax.ShapeDtypeStruct((T, QW), BF16), jax.ShapeDtypeStruct((LANES, T), BF16),
                   jax.ShapeDtypeStruct((B * nq * 8, NSEL), F32)],
        compiler_params=_params("parallel", "parallel"),
        name="nsa_cmp_select",
    )(P, kc, vct, cfeat, ovt, P)


def _selected_tile_lists(cnt, B, S):
    nq = S // TQ_ATT
    nkt = S // TK_SEL
    per = TK_SEL // C_SEL_BLOCK
    blk = cnt.reshape(B * nq, 8, -1)[:, 0, :] > 0.5
    need = jnp.any(blk.reshape(B * nq, nkt, per), axis=-1)
    own = ((jnp.arange(B * nq, dtype=jnp.int32) % nq) * TQ_ATT + TQ_ATT - 1) // TK_SEL
    jj = jnp.arange(nkt, dtype=jnp.int32)[None, :]
    need = (need | (jj == 0) | (jj == own[:, None])) & (jj <= own[:, None])
    order = jnp.argsort(jnp.where(need, jj, nkt + jj), axis=-1).astype(jnp.int32)
    return order.reshape(-1), jnp.sum(need, axis=-1).astype(jnp.int32)


def _outproj_kernel(x_ref, oa_ref, ob_ref, oc1_ref, oc2_ref, oc3_ref, ga_ref, gb_ref, gc_ref, wo_ref, gp_ref, o_ref):
    oa = _rms(oa_ref[...].astype(F32), ga_ref[...]).astype(BF16)
    ob = _rms(ob_ref[...].astype(F32), gb_ref[...]).astype(BF16)
    oc = oc1_ref[...].astype(F32) + oc2_ref[...].astype(F32) + oc3_ref[...].astype(F32)
    oc = _rms(oc, gc_ref[...]).astype(BF16)
    m = _dot(oa, wo_ref[0:256, :]) + _dot(ob, wo_ref[256:512, :]) + _dot(oc, wo_ref[512:1024, :])
    o_ref[...] = x_ref[...] + _rms(m, gp_ref[...])


def _outproj(x, o_a, o_b, o_cmp, o_slc, o_win, g_oa, g_ob, g_oc, w_o, g_post):
    T = x.shape[0]
    TM = TM_OUT
    row = lambda w: pl.BlockSpec((TM, w), lambda i: (i, 0))
    full = lambda r, c: pl.BlockSpec((r, c), lambda i: (0, 0))
    return pl.pallas_call(
        _outproj_kernel,
        grid=(T // TM,),
        in_specs=[row(D_MODEL), row(256), row(256), row(512), row(512), row(512),
                  full(1, 256), full(1, 256), full(1, 512), full(D_MODEL, D_MODEL), full(1, D_MODEL)],
        out_specs=row(D_MODEL),
        out_shape=jax.ShapeDtypeStruct((T, D_MODEL), F32),
        compiler_params=_params("parallel"),
        name="outproj",
    )(x, o_a, o_b, o_cmp, o_slc, o_win, g_oa.reshape(1, -1), g_ob.reshape(1, -1), g_oc.reshape(1, -1),
      w_o.astype(BF16), g_post.reshape(1, -1))


def _swiglu_step(h, wg_ref, wu_ref, wd_ref, acc_ref):
    a = _dot(h, wg_ref[...])
    b = _dot(h, wu_ref[...])
    z = (a * _sigmoid(a) * b).astype(BF16)
    acc_ref[...] += _dot(z, wd_ref[...])


def _ffn_kernel(x_ref, gpre_ref, wg_ref, wu_ref, wd_ref, gpost_ref, o_ref, h_sc, acc_sc, *, nf):
    f = pl.program_id(1)

    @pl.when(f == 0)
    def _():
        h_sc[...] = _rms(x_ref[...], gpre_ref[...]).astype(BF16)
        acc_sc[...] = jnp.zeros_like(acc_sc)

    _swiglu_step(h_sc[...], wg_ref, wu_ref, wd_ref, acc_sc)

    @pl.when(f == nf - 1)
    def _():
        o_ref[...] = x_ref[...] + _rms(acc_sc[...], gpost_ref[...])


def _ffn(x, g_pre, wg, wu, wd, g_post):
    T = x.shape[0]
    TM, TF = TM_FFN, TF_FFN
    FF = wg.shape[1]
    nf = FF // TF
    return pl.pallas_call(
        functools.partial(_ffn_kernel, nf=nf),
        grid=(T // TM, nf),
        in_specs=[pl.BlockSpec((TM, D_MODEL), lambda i, f: (i, 0)),
                  pl.BlockSpec((1, D_MODEL), lambda i, f: (0, 0)),
                  pl.BlockSpec((D_MODEL, TF), lambda i, f: (0, f)),
                  pl.BlockSpec((D_MODEL, TF), lambda i, f: (0, f)),
                  pl.BlockSpec((TF, D_MODEL), lambda i, f: (f, 0)),
                  pl.BlockSpec((1, D_MODEL), lambda i, f: (0, 0))],
        out_specs=pl.BlockSpec((TM, D_MODEL), lambda i, f: (i, 0)),
        out_shape=jax.ShapeDtypeStruct((T, D_MODEL), F32),
        scratch_shapes=[pltpu.VMEM((TM, D_MODEL), BF16), pltpu.VMEM((TM, D_MODEL), F32)],
        compiler_params=_params("parallel", "arbitrary"),
        name="dense_ffn",
    )(x, g_pre.reshape(1, -1), wg.astype(BF16), wu.astype(BF16), wd.astype(BF16), g_post.reshape(1, -1))


def _expert_kernel(te_ref, nt_ref, x_ref, wg_ref, wu_ref, wd_ref, o_ref, acc_sc, *, nf):
    j = pl.program_id(0)
    f = pl.program_id(1)
    live = j < nt_ref[0]

    @pl.when(f == 0)
    def _():
        acc_sc[...] = jnp.zeros_like(acc_sc)

    @pl.when(live)
    def _():
        _swiglu_step(x_ref[...], wg_ref.at[0], wu_ref.at[0], wd_ref.at[0], acc_sc)

    @pl.when(f == nf - 1)
    def _():
        o_ref[...] = acc_sc[...].astype(o_ref.dtype)


def _experts(xs, tile_expert, n_tiles, wg, wu, wd):
    NT = xs.shape[0] // TM_MOE
    TM, TF = TM_MOE, TF_MOE
    assert wg.shape[2] % TF == 0
    FF = wg.shape[2]
    nf = FF // TF
    fidx = lambda j, f, nt: jnp.where(j < nt[0], f, nf - 1)
    grid_spec = pltpu.PrefetchScalarGridSpec(
        num_scalar_prefetch=2,
        grid=(NT, nf),
        in_specs=[pl.BlockSpec((TM, D_MODEL), lambda j, f, te, nt: (j, 0)),
                  pl.BlockSpec((1, D_MODEL, TF), lambda j, f, te, nt: (te[j], 0, fidx(j, f, nt))),
                  pl.BlockSpec((1, D_MODEL, TF), lambda j, f, te, nt: (te[j], 0, fidx(j, f, nt))),
                  pl.BlockSpec((1, TF, D_MODEL), lambda j, f, te, nt: (te[j], fidx(j, f, nt), 0))],
        out_specs=pl.BlockSpec((TM, D_MODEL), lambda j, f, te, nt: (j, 0)),
        scratch_shapes=[pltpu.VMEM((TM, D_MODEL), F32)])
    return pl.pallas_call(
        functools.partial(_expert_kernel, nf=nf),
        grid_spec=grid_spec,
        out_shape=jax.ShapeDtypeStruct(xs.shape, BF16),
        compiler_params=_params("parallel", "arbitrary"),
        name="moe_experts",
    )(tile_expert, n_tiles, xs, wg, wu, wd)


def _router_kernel(x_ref, g_ref, wr_ref, h_out, meta_out, metat_out, start_out, cnt_out, run_sc, *, TS):
    s = pl.program_id(0)

    @pl.when(s == 0)
    def _():
        run_sc[...] = jnp.zeros_like(run_sc)

    h = _rms(x_ref[...], g_ref[...])
    hb = h.astype(BF16)
    h_out[...] = hb
    hl = (h - hb.astype(F32)).astype(BF16)
    w = wr_ref[...]
    wh = w.astype(BF16)
    wl = (w - wh.astype(F32)).astype(BF16)
    logits = _dot(hb, wh) + _dot(hl, wh) + _dot(hb, wl)
    lane = lax.broadcasted_iota(jnp.int32, (TS, LANES), 1).astype(F32)
    logits = jnp.where(lane < N_EXPERTS, logits, NEG_INF)
    m1 = jnp.max(logits, axis=-1, keepdims=True)
    i1 = jnp.min(jnp.where(logits == m1, lane, float(LANES)), axis=-1, keepdims=True)
    rest = jnp.where(lane == i1, NEG_INF, logits)
    m2 = jnp.max(rest, axis=-1, keepdims=True)
    i2 = jnp.min(jnp.where(rest == m2, lane, float(LANES)), axis=-1, keepdims=True)
    ex = jnp.exp(m2 - m1)
    w1 = 1.0 / (1.0 + ex)
    w2 = ex / (1.0 + ex)
    oh1 = lane == i1
    oh2 = lane == i2
    oh = jnp.where(oh1, 1.0, 0.0) + jnp.where(oh2, 1.0, 0.0)
    tri = jnp.where(lax.broadcasted_iota(jnp.int32, (TS, TS), 0) > lax.broadcasted_iota(jnp.int32, (TS, TS), 1),
                    1.0, 0.0).astype(BF16)
    run = run_sc[...]
    before = _dot(tri, oh.astype(BF16)) + run
    r1 = jnp.sum(jnp.where(oh1, before, 0.0), axis=-1, keepdims=True)
    r2 = jnp.sum(jnp.where(oh2, before, 0.0), axis=-1, keepdims=True)
    start_out[0] = run
    run = run + jnp.sum(oh, axis=0, keepdims=True)
    run_sc[...] = run
    cnt_out[...] = run
    meta = jnp.where(lane == 0, i1, jnp.where(lane == 1, i2, jnp.where(lane == 2, r1, jnp.where(
        lane == 3, r2, jnp.where(lane == 4, w1, jnp.where(lane == 5, w2, 0.0))))))
    meta_out[...] = meta
    metat_out[...] = meta.T[0:8, :]


def _router(x, g, w_router):
    T = x.shape[0]
    TS = TS_MOE
    nt = T // TS
    wr = jnp.zeros((D_MODEL, LANES), F32).at[:, :N_EXPERTS].set(w_router.astype(F32))
    return pl.pallas_call(
        functools.partial(_router_kernel, TS=TS),
        grid=(nt,),
        in_specs=[pl.BlockSpec((TS, D_MODEL), lambda s: (s, 0)),
                  pl.BlockSpec((1, D_MODEL), lambda s: (0, 0)),
                  pl.BlockSpec((D_MODEL, LANES), lambda s: (0, 0))],
        out_specs=[pl.BlockSpec((TS, D_MODEL), lambda s: (s, 0)),
                   pl.BlockSpec((TS, LANES), lambda s: (s, 0)),
                   pl.BlockSpec((8, TS), lambda s: (0, s)),
                   pl.BlockSpec((1, 1, LANES), lambda s: (s, 0, 0)),
                   pl.BlockSpec((1, LANES), lambda s: (0, 0))],
        out_shape=[jax.ShapeDtypeStruct((T, D_MODEL), BF16),
                   jax.ShapeDtypeStruct((T, LANES), F32),
                   jax.ShapeDtypeStruct((8, T), F32),
                   jax.ShapeDtypeStruct((nt, 1, LANES), F32),
                   jax.ShapeDtypeStruct((1, LANES), F32)],
        scratch_shapes=[pltpu.VMEM((1, LANES), F32)],
        compiler_params=_params("arbitrary"),
        name="moe_router",
    )(x, g.reshape(1, -1), wr)


def _gather_kernel(te_ref, slo_ref, shi_ref, base_ref, metat_ref, h_ref, xs_out, acc_sc, *, TM, TS, CH):
    j = pl.program_id(0)
    e = te_ref[j]
    ef = e.astype(F32)
    nsub = TM // CH
    nt = h_ref.shape[0] // TS

    def picked_rows(s, want):
        off = pl.multiple_of(s * TS, TS)
        e1 = metat_ref[0:1, pl.ds(off, TS)]
        e2 = metat_ref[1:2, pl.ds(off, TS)]
        r1 = metat_ref[2:3, pl.ds(off, TS)]
        r2 = metat_ref[3:4, pl.ds(off, TS)]
        key = jnp.where(e1 == ef, r1, jnp.where(e2 == ef, r2, -1.0))
        sel = jnp.where(key == want, 1.0, 0.0).astype(BF16)
        return _dot(sel, h_ref[pl.ds(off, TS), :])

    wants = []
    for sub in range(nsub):
        q = j * nsub + sub
        first = j * TM + sub * CH - base_ref[e]
        want = (first + lax.broadcasted_iota(jnp.int32, (CH, 1), 0)).astype(F32)
        lo = jnp.minimum(slo_ref[q], nt - 1)
        second = jnp.where(lo + 1 < nt, want, -2.0)
        acc_sc[sub] = picked_rows(lo, want) + picked_rows(jnp.minimum(lo + 1, nt - 1), second)
        wants.append(want)
    for sub in range(nsub):
        q = j * nsub + sub

        def body(s, _, sub=sub):
            acc_sc[sub] += picked_rows(s, wants[sub])
            return 0

        lax.fori_loop(slo_ref[q] + 2, shi_ref[q] + 1, body, 0)
    for sub in range(nsub):
        xs_out[sub * CH:(sub + 1) * CH, :] = acc_sc[sub].astype(xs_out.dtype)


def _gather(hb, metat, tile_expert, s_lo, s_hi, base, NT):
    TM, TS = TM_MOE, TS_MOE
    assert hb.shape[0] % TS == 0
    grid_spec = pltpu.PrefetchScalarGridSpec(
        num_scalar_prefetch=4,
        grid=(NT,),
        in_specs=[pl.BlockSpec(memory_space=pltpu.VMEM), pl.BlockSpec(memory_space=pltpu.VMEM)],
        out_specs=pl.BlockSpec((TM, D_MODEL), lambda j, *_: (j, 0)),
        scratch_shapes=[pltpu.VMEM((TM // CH_MOE, CH_MOE, D_MODEL), F32)])
    return pl.pallas_call(
        functools.partial(_gather_kernel, TM=TM, TS=TS, CH=CH_MOE),
        grid_spec=grid_spec,
        out_shape=jax.ShapeDtypeStruct((NT * TM, D_MODEL), BF16),
        compiler_params=_params("arbitrary"),
        name="moe_gather",
    )(tile_expert, s_lo, s_hi, base, metat, hb)


def _combine_kernel(blk_ref, row0_ref, base_ref, meta_ref, x_ref, gpost_ref, ys_hbm, o_ref, buf, sem, selw_sc,
                    *, TS, CH, U, nt):
    s = pl.program_id(0)
    slot = s % 2

    def chunk_copies(step, slot_):
        out = []
        for u in range(U):
            r0 = pl.multiple_of(blk_ref[step * U + u] * CH, CH)
            out.append(pltpu.make_async_copy(ys_hbm.at[pl.ds(r0, CH), :], buf.at[slot_, pl.ds(u * CH, CH), :],
                                             sem.at[slot_]))
        return out

    @pl.when(s == 0)
    def _():
        for c in chunk_copies(0, 0):
            c.start()

    @pl.when(s + 1 < nt)
    def _():
        for c in chunk_copies(s + 1, 1 - slot):
            c.start()

    meta = meta_ref[...]
    e1, e2 = meta[:, 0:1], meta[:, 1:2]
    w1, w2 = meta[:, 4:5], meta[:, 5:6]
    b1 = jnp.zeros((TS, 1), F32)
    b2 = jnp.zeros((TS, 1), F32)
    for e in range(N_EXPERTS):
        be = base_ref[e].astype(F32)
        b1 = jnp.where(e1 == float(e), be, b1)
        b2 = jnp.where(e2 == float(e), be, b2)
    pos1 = b1 + meta[:, 2:3]
    pos2 = b2 + meta[:, 3:4]

    lane = lax.broadcasted_iota(jnp.int32, (1, CH), 1)
    for u in range(U):
        rows = (row0_ref[s * U + u] + lane).astype(F32)
        selw = jnp.where(pos1 == rows, w1, 0.0) + jnp.where(pos2 == rows, w2, 0.0)
        selw_sc[:, u * CH:(u + 1) * CH] = selw.astype(BF16)
    for c in chunk_copies(s, slot):
        c.wait()
    y = _dot(selw_sc[...], buf[slot])
    o_ref[...] = x_ref[...] + _rms(y, gpost_ref[...])


def _combine(x, meta, ys, chunk_blk, chunk_row0, base, g_post):
    T = x.shape[0]
    TS, CH, U = TS_MOE, CH_MOE, U_MOE
    nt = T // TS
    grid_spec = pltpu.PrefetchScalarGridSpec(
        num_scalar_prefetch=3,
        grid=(nt,),
        in_specs=[pl.BlockSpec((TS, LANES), lambda s, *_: (s, 0)),
                  pl.BlockSpec((TS, D_MODEL), lambda s, *_: (s, 0)),
                  pl.BlockSpec((1, D_MODEL), lambda s, *_: (0, 0)),
                  pl.BlockSpec(memory_space=pl.ANY)],
        out_specs=pl.BlockSpec((TS, D_MODEL), lambda s, *_: (s, 0)),
        scratch_shapes=[pltpu.VMEM((2, U * CH, D_MODEL), BF16), pltpu.SemaphoreType.DMA((2,)),
                        pltpu.VMEM((TS, U * CH), BF16)])
    return pl.pallas_call(
        functools.partial(_combine_kernel, TS=TS, CH=CH, U=U, nt=nt),
        grid_spec=grid_spec,
        out_shape=jax.ShapeDtypeStruct((T, D_MODEL), F32),
        compiler_params=_params("arbitrary"),
        name="moe_combine",
    )(chunk_blk, chunk_row0, base, meta, x, g_post.reshape(1, -1), ys)


def _moe(x, g_pre, w_router, w_gate, w_up, w_down, g_post):
    T = x.shape[0]
    TS, TM, CH, U = TS_MOE, TM_MOE, CH_MOE, U_MOE
    nt = T // TS
    NT = (2 * T) // TM + N_EXPERTS
    hb, meta, metat, start, cnt = _router(x, g_pre, w_router)

    i32 = jnp.int32
    counts = cnt[0, :N_EXPERTS].astype(i32)
    start = start[:, 0, :N_EXPERTS].astype(i32)
    start_ext = jnp.concatenate([start, counts[None, :]], axis=0)
    tiles_e = (counts + TM - 1) // TM
    tiles_cum = jnp.cumsum(tiles_e)
    base = (tiles_cum - tiles_e) * TM
    n_tiles = tiles_cum[-1:]
    jj = jnp.arange(NT, dtype=i32)
    tile_expert = jnp.minimum(jnp.sum(jj[:, None] >= tiles_cum[None, :], axis=1), N_EXPERTS - 1).astype(i32)
    live = jj < n_tiles[0]
    qq = jnp.arange(NT * (TM // CH), dtype=i32)
    q_exp = tile_expert[qq // (TM // CH)]
    a = qq * CH - base[q_exp]
    st_e = start_ext[:, q_exp]
    s_lo = jnp.sum(st_e[1:] <= a[None, :], axis=0).astype(i32)
    s_hi = (jnp.sum(st_e[:-1] < (a + CH)[None, :], axis=0) - 1).astype(i32)
    q_live = live[qq // (TM // CH)]
    s_lo = jnp.where(q_live, s_lo, 1)
    s_hi = jnp.where(q_live, s_hi, 0)

    xs = _gather(hb, metat, tile_expert, s_lo, s_hi, base.astype(i32), NT)
    ys = _experts(xs, tile_expert, n_tiles.astype(i32), w_gate.astype(BF16), w_up.astype(BF16), w_down.astype(BF16))

    lo = base[None, :] + start_ext[:-1]
    hi = base[None, :] + start_ext[1:]
    n_e = jnp.where(hi > lo, (hi - 1) // CH - lo // CH + 1, 0)
    cum = jnp.cumsum(n_e, axis=1)
    uu = jnp.arange(U, dtype=i32)
    e_u = jnp.minimum(jnp.sum(uu[None, :, None] >= cum[:, None, :], axis=2), N_EXPERTS - 1)
    first = jnp.take_along_axis(lo // CH, e_u, axis=1)
    skipped = jnp.take_along_axis(cum - n_e, e_u, axis=1)
    chunk_blk = jnp.clip(first + (uu[None, :] - skipped), 0, (NT * TM) // CH - 1).astype(i32)
    used = uu[None, :] < cum[:, -1:]
    chunk_blk = jnp.where(used, chunk_blk, chunk_blk[:, :1])
    chunk_row0 = jnp.where(used, chunk_blk * CH, -(NT * TM)).astype(i32)
    return _combine(x, meta, ys, chunk_blk.reshape(-1), chunk_row0.reshape(-1), base.astype(i32), g_post)


def kernel(x, positions, w_in, a_sinks, g_cq, w_uq, g_ckv, w_ukv, c_w1_k, c_w2_k, c_pe_k, c_w1_v, c_w2_v, c_pe_v,
           g_oa, g_ob, g_oc, w_o, g_pre_mix, g_post_mix, g_pre_ffn, g_post_ffn, ffn_w_gate, ffn_w_up, ffn_w_down,
           moe_router, moe_w_gate, moe_w_up, moe_w_down):
    B, S, _ = x.shape
    T = B * S
    depth = w_in.shape[0]
    a_slopes = _alibi_slopes(A_HEADS)
    c_slopes = _alibi_slopes(C_HEADS)
    xf = x.reshape(T, D_MODEL).astype(F32)
    cos, sin = _rope_tables(positions)
    att_scale = HEAD_DIM ** -0.5
    flat = lambda w: w.reshape(-1, w.shape[-1])
    bf16_w = {}
    w_p_all, w_vt_all = jax.vmap(_build_w_in)(w_in)
    mla_w_all = jax.vmap(_build_mla_weights)(w_uq, w_ukv)
    cmp_k_all = jax.vmap(_build_compress_weights)(c_w1_k, c_w2_k, c_pe_k)
    cmp_v_all = jax.vmap(_build_compress_weights)(c_w1_v, c_w2_v, c_pe_v)
    for l in range(depth):
        P, VT, qB, kB, vtB = _inproj(xf, g_pre_mix[l].reshape(1, -1), w_p_all[l], w_vt_all[l], S, cos, sin,
                                     g_cq[l], g_ckv[l], tuple(w[l] for w in mla_w_all))
        o_a = _banded_attn(P, VT, B, S, name="swa_attn", NSUB=NSUB_A, G=A_KV_HEADS, R=A_HEADS // A_KV_HEADS,
                           KD=KD_A, q_col=P_AQ, k_col=P_KA, vt_slab=VT_A, W=A_WINDOW, slopes=a_slopes,
                           scale=att_scale, sinks=a_sinks[l])
        if l % 2 == 0:
            side = (ffn_w_gate[l // 2], ffn_w_up[l // 2], ffn_w_down[l // 2])
        else:
            side = (flat(moe_w_down[l // 2]),)
        o_b, *cast = _flash_attn(qB, kB, vtB, B, S, name="mla_attn", TQ=TQ_MLA, TK=TQ_MLA, G=B_HEADS, R=1, KD=LANES,
                                 q_col=0, q_width=B_HEADS * LANES, k_col=0, vt_slab=0, side=side)
        if l % 2 == 0:
            bf16_w["ffn", l] = cast
        else:
            bf16_w["down", l] = cast[0]
        kc, vct = _compress(P, B, S, tuple(w[l] for w in cmp_k_all), tuple(w[l] for w in cmp_v_all))
        o_cmp, selb, pick_cnt = _cmp_sel(P, kc, vct, B, S, c_slopes)
        c_args = dict(TQ=TQ_ATT, G=C_KV_HEADS, R=C_HEADS // C_KV_HEADS, q_col=P_CQ, q_width=C_HEADS * HEAD_DIM,
                      slopes=c_slopes, scale=att_scale, gate=P)
        side = ()
        if l % 2 == 0 and l + 1 < depth:
            side = (flat(moe_w_gate[(l + 1) // 2]),)
        elif l % 2 == 1:
            side = (flat(moe_w_up[l // 2]),)
        o_slc, *cast = _flash_attn(P, P, VT, B, S, name="nsa_selected", TK=TK_SEL, KD=KD_SEL, k_col=P_KSEL,
                                   vt_slab=VT_SEL, selb=selb, tiles=_selected_tile_lists(pick_cnt, B, S),
                                   gate_branch=1, side=side, **c_args)
        if l % 2 == 0 and l + 1 < depth:
            bf16_w["gate", l + 1] = cast[0]
        elif l % 2 == 1:
            bf16_w["up", l] = cast[0]
        o_win = _banded_attn(P, VT, B, S, name="nsa_window", NSUB=NSUB_WIN, G=C_KV_HEADS, R=C_HEADS // C_KV_HEADS,
                             KD=KD_WIN, q_col=P_CQ, k_col=P_KWIN, vt_slab=VT_WIN, W=C_WINDOW, slopes=c_slopes,
                             scale=att_scale, gate_branch=2)
        xf = _outproj(xf, o_a, o_b, o_cmp, o_slc, o_win, g_oa[l], g_ob[l], g_oc[l], w_o[l], g_post_mix[l])
        if l % 2 == 0:
            wg, wu, wd = bf16_w["ffn", l]
            xf = _ffn(xf, g_pre_ffn[l], wg, wu, wd, g_post_ffn[l])
        else:
            e = l // 2
            wg = bf16_w["gate", l].reshape(moe_w_gate[e].shape)
            wu = bf16_w["up", l].reshape(moe_w_up[e].shape)
            wd = bf16_w["down", l].reshape(moe_w_down[e].shape)
            xf = _moe(xf, g_pre_ffn[l], moe_router[e], wg, wu, wd, g_post_ffn[l])
    return xf.reshape(B, S, D_MODEL)
```

```python
import functools

import numpy as np
import jax
import jax.numpy as jnp
from jax import lax
from jax.experimental import pallas as pl
from jax.experimental.pallas import tpu as pltpu

F32 = jnp.float32
BF16 = jnp.bfloat16

D_MODEL = 1024
HEAD_DIM = 64
NORM_EPS = 1e-6
NEG_INF = -1e30
ROPE_THETA = 10000.0
LOG2E = 1.4426950408889634

A_HEADS, A_KV_HEADS, A_WINDOW = 4, 2, 128
B_HEADS, B_Q_RANK, B_KV_RANK, B_NOPE, B_ROPE = 4, 256, 128, 64, 32
C_HEADS, C_KV_HEADS, C_WINDOW = 8, 2, 512
C_CMP_BLOCK, C_CMP_STRIDE, C_CMP_HIDDEN = 32, 16, 128
C_SEL_BLOCK, C_SEL_TOP = 64, 16
N_EXPERTS = 8

LANES = 128
VMEM_LIMIT = 48 * 1024 * 1024

P_CQ, P_AQ, P_BCQ = 0, 512, 768
P_BCKV, P_CKC, P_CVC, P_KR, P_KRROT, P_GATE = 1024, 1152, 1280, 1408, 1536, 1664
P_KSEL, KD_SEL = 1792, 256
P_KWIN, KD_WIN = 2304, 128
P_KA, KD_A = 2560, 128
P_WIDTH = 2816
V_ROWS = 80
VT_SEL, VT_WIN, VT_A = 0, 2, 4
VT_SLABS = 6
MASK_BIAS = -1e30
M_INIT = -1e29

TM_PROJ = 512
TM_OUT = 1024
TQ_ATT = 128
TK_BAND = 128
NSUB_A = 4
NSUB_WIN = 4
TQ_MLA = 256
TK_SEL = 256
TM_FFN = 512
TF_FFN = 2048
TF_MOE = 1792
TS_MOE = 512
TM_MOE = 512
CH_MOE = 128
U_MOE = 24


def _alibi_slopes(n):
    return [float(np.float32(2.0 ** (-8.0 * (i + 1) / n))) for i in range(n)]


def _split_bf16(x):
    parts = []
    for _ in range(3):
        p = float(np.asarray(x, np.float32).astype(jnp.bfloat16))
        parts.append(p)
        x = x - p
    return parts


def _dot(a, b):
    return jnp.dot(a, b, preferred_element_type=F32)


def _dot_nt(a, b):
    return lax.dot_general(a, b, (((1,), (1,)), ((), ())), preferred_element_type=F32)


def _rms(x, g):
    return x * lax.rsqrt(jnp.mean(x * x, axis=-1, keepdims=True) + NORM_EPS) * g


def _sigmoid(x):
    return 1.0 / (1.0 + jnp.exp(-x))


def _params(*sem):
    return pltpu.CompilerParams(dimension_semantics=sem, vmem_limit_bytes=VMEM_LIMIT)


def _inproj_kernel(x_ref, g_ref, w_ref, wvt_ref, vone_ref, kc_ref, cos_ref, sin_ref, gq_ref, gkv_ref,
                   wq_ref, wqr_ref, wk_ref, wvtb_ref, voneb_ref, o_ref, vt_ref, qb_out, kb_out, vtb_out,
                   kcmp_out, vcmp_out):
    h = _rms(x_ref[...], g_ref[...]).astype(BF16)
    for n in range(P_WIDTH // 256):
        o_ref[:, n * 256:(n + 1) * 256] = _dot(h, w_ref[:, n * 256:(n + 1) * 256]).astype(BF16)
    kcmp_out[...] = o_ref[:, P_CKC:P_CKC + LANES]
    vcmp_out[...] = o_ref[:, P_CVC:P_CVC + LANES]
    _mla_prep(o_ref[:, P_BCQ:P_BCQ + B_Q_RANK], o_ref[:, P_BCKV:P_BCKV + B_KV_RANK], o_ref[:, P_KR:P_KR + LANES],
              o_ref[:, P_KRROT:P_KRROT + LANES], cos_ref[...], sin_ref[...], gq_ref, gkv_ref,
              wq_ref, wqr_ref, wk_ref, wvtb_ref, voneb_ref, qb_out, kb_out, vtb_out)
    for g in range(C_KV_HEADS):
        o_ref[:, P_KSEL + g * KD_SEL + 64:P_KSEL + g * KD_SEL + 192] = kc_ref[:, 0:128]
        o_ref[:, P_KWIN + g * KD_WIN + 64:P_KWIN + (g + 1) * KD_WIN] = kc_ref[:, 128:192]
    for g in range(A_KV_HEADS):
        o_ref[:, P_KA + g * KD_A + 64:P_KA + (g + 1) * KD_A] = kc_ref[:, 128:192]
    vt_ref[...] = (_dot_nt(wvt_ref[...], h) + vone_ref[...]).astype(BF16)


def _key_constants(S):
    s = np.arange(S)
    kc = np.zeros((S, 256), np.float32)
    kc[:, 0:3] = (s % TK_SEL)[:, None]
    kc[s, 64 + s // C_SEL_BLOCK] = 1.0
    kc[:, 128:131] = (s % TK_BAND)[:, None]
    return jnp.asarray(kc, dtype=BF16)


def _inproj(x, g, w, wvt, S, cos, sin, g_cq, g_ckv, mla_w):
    T = x.shape[0]
    TM = TM_PROJ
    assert S // C_SEL_BLOCK <= 64 and TK_SEL <= 256 and TK_BAND <= 256
    vone = np.zeros((VT_SLABS * V_ROWS, 1), np.float32)
    vone[HEAD_DIM::V_ROWS] = 1.0
    QW = B_HEADS * LANES
    VR = B_HEADS * V_ROWS
    voneb = np.zeros((VR, 1), np.float32)
    voneb[HEAD_DIM::V_ROWS] = 1.0
    wq, wqr, wk, wvtb = mla_w
    nblk = S // TM
    full = lambda a: pl.BlockSpec(a.shape, lambda i: (0, 0))
    rows = lambda w_: pl.BlockSpec((TM, w_), lambda i: (i, 0))
    cols = lambda r: pl.BlockSpec((r, TM), lambda i: (0, i))
    consts = [g, w, wvt, jnp.asarray(vone)]
    mla_consts = [g_cq.reshape(1, -1), g_ckv.reshape(1, -1), wq, wqr, wk, wvtb, jnp.asarray(voneb)]
    return pl.pallas_call(
        _inproj_kernel,
        grid=(T // TM,),
        in_specs=[rows(D_MODEL)] + [full(a) for a in consts]
                 + [pl.BlockSpec((TM, 256), lambda i: (i % nblk, 0)), rows(LANES), rows(LANES)]
                 + [full(a) for a in mla_consts],
        out_specs=[rows(P_WIDTH), cols(VT_SLABS * V_ROWS), cols(QW), rows(QW), cols(VR), rows(LANES), rows(LANES)],
        out_shape=[jax.ShapeDtypeStruct((T, P_WIDTH), BF16),
                   jax.ShapeDtypeStruct((VT_SLABS * V_ROWS, T), BF16),
                   jax.ShapeDtypeStruct((QW, T), BF16), jax.ShapeDtypeStruct((T, QW), BF16),
                   jax.ShapeDtypeStruct((VR, T), BF16),
                   jax.ShapeDtypeStruct((T, LANES), BF16), jax.ShapeDtypeStruct((T, LANES), BF16)],
        compiler_params=_params("parallel"),
        name="inproj",
    )(x, *consts, _key_constants(S), cos, sin, *mla_consts)


def _build_w_in(w_in):
    cuts = np.cumsum([0, 256, 128, 128, 256, 128, 32, 512, 128, 128, 128, 128, 128, 128, 24])
    seg = [w_in[:, cuts[i]:cuts[i + 1]] for i in range(14)]
    a_q, a_k, a_v, b_cq, b_ckv, b_kr, c_q, c_kc, c_vc, c_ks, c_vs, c_kw, c_vw, c_g = seg
    z = lambda n: jnp.zeros((w_in.shape[0], n), w_in.dtype)
    half = B_ROPE // 2
    kr_rot = jnp.concatenate([-b_kr[:, half:], b_kr[:, :half]], axis=1)
    cols = [c_q, a_q, b_cq, b_ckv, c_kc, c_vc,
            z(B_NOPE), b_kr, z(LANES - B_NOPE - B_ROPE),
            z(B_NOPE), kr_rot, z(LANES - B_NOPE - B_ROPE),
            c_g, z(LANES - 24)]
    for k, kd in ((c_ks, KD_SEL), (c_kw, KD_WIN), (a_k, KD_A)):
        for g in range(2):
            cols += [k[:, g * HEAD_DIM:(g + 1) * HEAD_DIM], z(kd - HEAD_DIM)]
    w = jnp.concatenate(cols, axis=1).astype(BF16)
    rows = []
    for v in (c_vs, c_vw, a_v):
        for g in range(2):
            rows += [v[:, g * HEAD_DIM:(g + 1) * HEAD_DIM].T, jnp.zeros((V_ROWS - HEAD_DIM, w_in.shape[0]), w_in.dtype)]
    return w, jnp.concatenate(rows, axis=0).astype(BF16)


def _rope_tab_kernel(pos_ref, inv_ref, cos_ref, sin_ref):
    ang = pos_ref[...] * inv_ref[...]
    cos_ref[...] = jnp.cos(ang)
    sin_ref[...] = jnp.sin(ang)


def _rope_tables(positions):
    T = positions.size
    per = LANES // B_ROPE
    inv = (ROPE_THETA ** (-np.arange(0, B_ROPE, 2, dtype=np.float32) / B_ROPE)).astype(np.float32)
    inv128 = np.tile(np.concatenate([inv, inv]), per).reshape(1, LANES)
    pos = jnp.repeat(positions.reshape(T // per, per).astype(F32), B_ROPE, axis=1)
    rows = min(TM_PROJ, T // per)
    cos_c, sin_c = pl.pallas_call(
        _rope_tab_kernel,
        grid=(T // per // rows,),
        in_specs=[pl.BlockSpec((rows, LANES), lambda i: (i, 0)),
                  pl.BlockSpec((1, LANES), lambda i: (0, 0))],
        out_specs=[pl.BlockSpec((rows, LANES), lambda i: (i, 0))] * 2,
        out_shape=[jax.ShapeDtypeStruct((T // per, LANES), F32)] * 2,
        compiler_params=_params("parallel"),
        name="rope_tables",
    )(pos, jnp.asarray(inv128))
    widen = lambda t, fill: jnp.pad(t.reshape(T, B_ROPE), ((0, 0), (B_NOPE, LANES - B_NOPE - B_ROPE)),
                                    constant_values=fill)
    return widen(cos_c, 1.0), widen(sin_c, 0.0)


def _flash_kernel(*refs, TQ, TK, G, R, NKT, slopes, scale, sel, gate_branch, n_side):
    refs = list(refs)
    lst_ref, cnt_ref = (refs.pop(0), refs.pop(0)) if sel else (None, None)
    q_ref = refs.pop(0)
    selb_ref = refs.pop(0) if sel else None
    k_refs = [refs.pop(0) for _ in range(G)]
    v_refs = [refs.pop(0) for _ in range(G)]
    gate_ref = refs.pop(0) if gate_branch is not None else None
    side_in = [refs.pop(0) for _ in range(n_side)]
    o_ref = refs.pop(0)
    side_out = [refs.pop(0) for _ in range(n_side)]
    qa_sc = refs.pop(0) if slopes is not None else None
    m_sc, acc_sc, s_sc, mx_sc = refs
    for src, dst in zip(side_in, side_out):
        dst[...] = src[...].astype(BF16)
    i = pl.program_id(1)
    step = pl.program_id(0) * pl.num_programs(1) + i
    t0 = i * TQ
    RQ = R * TQ
    lane = lax.broadcasted_iota(jnp.int32, (1, RQ), 1)
    qpos = t0 + (lane & (TQ - 1))
    hi = (t0 + TQ + TK - 1) // TK
    if gate_ref is not None:
        gate_t = _sigmoid(gate_ref[...].astype(F32)).T
    slope_vecs = []
    if slopes is not None:
        q_t = q_ref[...].astype(F32).T
        frow = lax.broadcasted_iota(jnp.int32, (HEAD_DIM, TQ), 0)
    for g in range(G):
        if slopes is not None:
            slope_vec = jnp.zeros((1, RQ), F32)
            for r in range(R):
                hh = g * R + r
                cols = slice(r * TQ, (r + 1) * TQ)
                s3 = _split_bf16(slopes[hh] * LOG2E)
                slope_vec = jnp.where(lane // TQ == r, sum(s3), slope_vec)
                qa_sc[g, 0:HEAD_DIM, cols] = (q_t[hh * HEAD_DIM:(hh + 1) * HEAD_DIM, :] * (scale * LOG2E)).astype(BF16)
                qa_sc[g, HEAD_DIM:2 * HEAD_DIM, cols] = jnp.where(
                    frow == 0, s3[0], jnp.where(frow == 1, s3[1], jnp.where(frow == 2, s3[2], 0.0))).astype(BF16)
                if sel:
                    qa_sc[g, 2 * HEAD_DIM:3 * HEAD_DIM, cols] = selb_ref[g * HEAD_DIM:(g + 1) * HEAD_DIM, :]
                    qa_sc[g, 3 * HEAD_DIM:4 * HEAD_DIM, cols] = jnp.zeros((HEAD_DIM, TQ), BF16)
            slope_vecs.append(slope_vec)
        else:
            slope_vecs.append(None)
        m_sc[g] = jnp.full((1, RQ), M_INIT, F32)
        acc_sc[g] = jnp.zeros((V_ROWS, RQ), F32)

    def scores(j, slot):
        ks = pl.multiple_of(jnp.asarray(j, jnp.int32) * TK, TK)
        for g in range(G):
            qa_t = qa_sc[g] if slopes is not None else q_ref[g * LANES:(g + 1) * LANES, :]
            st = _dot(k_refs[g][pl.ds(ks, TK), :], qa_t)
            s_sc[slot, g] = st
            mx_sc[slot, g] = jnp.max(st, axis=0, keepdims=True)

    def tile(j, slot, causal):
        ks = pl.multiple_of(jnp.asarray(j, jnp.int32) * TK, TK)
        for g in range(G):
            st = s_sc[slot, g]
            if causal:
                kpos = ks + lax.broadcasted_iota(jnp.int32, (TK, 1), 0)
                st = jnp.where(kpos <= qpos, st, MASK_BIAS)
            m = m_sc[g]
            mx = jnp.max(st, axis=0, keepdims=True) if causal else mx_sc[slot, g]
            if slopes is not None:
                c = slope_vecs[g] * ks.astype(F32)
                mn = jnp.maximum(m, mx + c)
                p = jnp.exp2(st - (mn - c))
            else:
                mn = jnp.maximum(m, mx)
                p = jnp.exp2(st - mn)
            acc_sc[g] = jnp.exp2(m - mn) * acc_sc[g] + _dot(v_refs[g][:, pl.ds(ks, TK)], p.astype(BF16))
            m_sc[g] = mn

    if sel:
        n = cnt_ref[step]
        jt = lambda t: lst_ref[step * NKT + t]
    else:
        n = hi
        jt = lambda t: t
    n_pair = (n - 1) // 2

    def pair(p, carry):
        t = 2 * p
        scores(jt(t + 1), 1)
        tile(jt(t), 0, False)
        scores(jt(t + 2), 0)
        tile(jt(t + 1), 1, False)
        return carry

    scores(jt(0), 0)
    lax.fori_loop(0, n_pair, pair, 0)

    @pl.when(2 * n_pair == n - 1)
    def _():
        tile(jt(n - 1), 0, True)

    @pl.when(2 * n_pair == n - 2)
    def _():
        scores(jt(n - 1), 1)
        tile(jt(n - 2), 0, False)
        tile(jt(n - 1), 1, True)

    pieces = []
    for g in range(G):
        acc = acc_sc[g]
        num = acc[0:HEAD_DIM, :]
        den = acc[HEAD_DIM:HEAD_DIM + 1, :]
        o_t = num * (1.0 / den)
        for r in range(R):
            piece = o_t[:, r * TQ:(r + 1) * TQ]
            if gate_ref is not None:
                c = 3 * (g * R + r) + gate_branch
                piece = piece * gate_t[c:c + 1, :]
            pieces.append(piece)
    o_ref[...] = jnp.concatenate(pieces, axis=0).T.astype(o_ref.dtype)


def _banded_kernel(*refs, TQ, NSUB, G, R, W, slopes, scale, has_sink, gate_branch):
    refs = list(refs)
    q_ref = refs.pop(0)
    k_refs = [refs.pop(0) for _ in range(G)]
    v_refs = [refs.pop(0) for _ in range(G)]
    sink_ref = refs.pop(0) if has_sink else None
    gate_ref = refs.pop(0) if gate_branch is not None else None
    o_ref = refs.pop(0)
    qa_sc, m_sc, acc_sc, s_sc = refs
    TK = TQ
    RQ = R * TQ
    n_st = W // TK + 1
    i = pl.program_id(1)
    lane = lax.broadcasted_iota(jnp.int32, (1, RQ), 1)
    frow = lax.broadcasted_iota(jnp.int32, (HEAD_DIM, TQ), 0)

    def process(sub, check):
        qt = i * NSUB + sub
        t0 = qt * TQ
        rows = slice(sub * TQ, (sub + 1) * TQ)
        qpos = t0 + (lane & (TQ - 1))
        if gate_ref is not None:
            gate_t = _sigmoid(gate_ref[rows, :].astype(F32)).T
        q_t = q_ref[rows, :].astype(F32).T
        slope_vecs = []
        for g in range(G):
            slope_vec = jnp.zeros((1, RQ), F32)
            for r in range(R):
                hh = g * R + r
                cols = slice(r * TQ, (r + 1) * TQ)
                s3 = _split_bf16(slopes[hh] * LOG2E)
                slope_vec = jnp.where(lane // TQ == r, sum(s3), slope_vec)
                qa_sc[sub, g, 0:HEAD_DIM, cols] = (
                    q_t[hh * HEAD_DIM:(hh + 1) * HEAD_DIM, :] * (scale * LOG2E)).astype(BF16)
                qa_sc[sub, g, HEAD_DIM:2 * HEAD_DIM, cols] = jnp.where(
                    frow == 0, s3[0], jnp.where(frow == 1, s3[1], jnp.where(frow == 2, s3[2], 0.0))).astype(BF16)
            slope_vecs.append(slope_vec)
            m_sc[sub, g] = jnp.full((1, RQ), M_INIT, F32)
            acc_sc[sub, g] = jnp.zeros((V_ROWS, RQ), F32)

        def scores(j, slot):
            ks = pl.multiple_of(j * TK, TK)
            for g in range(G):
                s_sc[sub, slot, g] = _dot(k_refs[g][pl.ds(ks, TK), :], qa_sc[sub, g])

        def tile(j, slot, mask, live):
            ks = pl.multiple_of(j * TK, TK)
            for g in range(G):
                st = s_sc[sub, slot, g]
                if mask is not None:
                    kpos = ks + lax.broadcasted_iota(jnp.int32, (TK, 1), 0)
                    ok = kpos <= qpos if mask == "causal" else kpos > qpos - W
                    st = jnp.where(ok, st, MASK_BIAS)
                if live is not None:
                    st = jnp.where(live, st, MASK_BIAS)
                m = m_sc[sub, g]
                c = slope_vecs[g] * ks.astype(F32)
                mn = jnp.maximum(m, jnp.max(st, axis=0, keepdims=True) + c)
                p = jnp.exp2(st - (mn - c))
                acc_sc[sub, g] = (jnp.exp2(m - mn) * acc_sc[sub, g]
                                  + _dot(v_refs[g][:, pl.ds(ks, TK)], p.astype(BF16)))
                m_sc[sub, g] = mn

        js = [qt - n_st + 1 + t for t in range(n_st)]
        jc = [jnp.maximum(j, 0) for j in js] if check else js
        scores(jc[0], 0)
        for t in range(n_st):
            if t + 1 < n_st:
                scores(jc[t + 1], (t + 1) & 1)
            mask = "causal" if t == n_st - 1 else ("window" if t == 0 else None)
            tile(jc[t], t & 1, mask, (js[t] >= 0) if check and t < n_st - 1 else None)

        pieces = []
        for g in range(G):
            acc = acc_sc[sub, g]
            num = acc[0:HEAD_DIM, :]
            den = acc[HEAD_DIM:HEAD_DIM + 1, :]
            if has_sink:
                sink_vec = jnp.zeros((1, RQ), F32)
                for r in range(R):
                    sink_vec = jnp.where(lane // TQ == r, sink_ref[g * R + r] * LOG2E, sink_vec)
                m_true = m_sc[sub, g] - slope_vecs[g] * qpos.astype(F32)
                mf = jnp.maximum(m_true, sink_vec)
                a = jnp.exp2(m_true - mf)
                num = num * a
                den = den * a + jnp.exp2(sink_vec - mf)
            o_t = num * (1.0 / den)
            for r in range(R):
                piece = o_t[:, r * TQ:(r + 1) * TQ]
                if gate_ref is not None:
                    c = 3 * (g * R + r) + gate_branch
                    piece = piece * gate_t[c:c + 1, :]
                pieces.append(piece)
        o_ref[rows, :] = jnp.concatenate(pieces, axis=0).T.astype(o_ref.dtype)

    all_tiles_exist = i * NSUB >= n_st - 1

    @pl.when(all_tiles_exist)
    def _():
        for sub in range(NSUB):
            process(sub, False)

    @pl.when(jnp.logical_not(all_tiles_exist))
    def _():
        for sub in range(NSUB):
            process(sub, True)


def _banded_attn(P, VT, B, S, *, name, NSUB, G, R, KD, q_col, k_col, vt_slab, W, slopes, scale, sinks=None,
                 gate_branch=None):
    T = B * S
    TQ = TQ_ATT
    TB = NSUB * TQ
    nq = S // TB
    RQ = R * TQ
    assert TK_BAND == TQ and W % TQ == 0 and S % TB == 0 and KD == 2 * HEAD_DIM
    QW = G * R * HEAD_DIM
    in_specs = [pl.BlockSpec((TB, QW), lambda b, i: (b * nq + i, q_col // QW))]
    args = [P]
    for g in range(G):
        in_specs.append(pl.BlockSpec((S, KD), lambda b, i, g=g: (b, k_col // KD + g)))
        args.append(P)
    for g in range(G):
        in_specs.append(pl.BlockSpec((V_ROWS, S), lambda b, i, g=g: (vt_slab + g, b)))
        args.append(VT)
    if sinks is not None:
        in_specs.append(pl.BlockSpec(memory_space=pltpu.SMEM))
        args.append(sinks.astype(F32))
    if gate_branch is not None:
        in_specs.append(pl.BlockSpec((TB, LANES), lambda b, i: (b * nq + i, P_GATE // LANES)))
        args.append(P)
    kern = functools.partial(_banded_kernel, TQ=TQ, NSUB=NSUB, G=G, R=R, W=W, slopes=slopes, scale=scale,
                             has_sink=sinks is not None, gate_branch=gate_branch)
    return pl.pallas_call(
        kern,
        grid=(B, nq),
        in_specs=in_specs,
        out_specs=pl.BlockSpec((TB, QW), lambda b, i: (b * nq + i, 0)),
        out_shape=jax.ShapeDtypeStruct((T, QW), BF16),
        scratch_shapes=[pltpu.VMEM((NSUB, G, KD, RQ), BF16), pltpu.VMEM((NSUB, G, 1, RQ), F32),
                        pltpu.VMEM((NSUB, G, V_ROWS, RQ), F32), pltpu.VMEM((NSUB, 2, G, TQ, RQ), F32)],
        compiler_params=_params("parallel", "parallel"),
        name=name,
    )(*args)


def _flash_attn(q, k, vt, B, S, *, name, TQ, TK, G, R, KD, q_col, q_width, k_col, vt_slab, slopes=None, scale=1.0,
                selb=None, tiles=None, gate=None, gate_branch=None, side=()):
    T = B * S
    nq = S // TQ
    RQ = R * TQ
    assert TK % TQ == 0 and S % TK == 0
    OW = G * R * HEAD_DIM
    if slopes is not None:
        in_specs = [pl.BlockSpec((TQ, q_width), lambda b, i, *_: (b * nq + i, q_col // q_width))]
    else:
        in_specs = [pl.BlockSpec((G * KD, TQ), lambda b, i, *_: (0, b * nq + i))]
    args = [q]
    if selb is not None:
        in_specs.append(pl.BlockSpec((LANES, TQ), lambda b, i, *_: (0, b * nq + i)))
        args.append(selb)
    for g in range(G):
        in_specs.append(pl.BlockSpec((S, KD), lambda b, i, *_, g=g: (b, k_col // KD + g)))
        args.append(k)
    for g in range(G):
        in_specs.append(pl.BlockSpec((V_ROWS, S), lambda b, i, *_, g=g: (vt_slab + g, b)))
        args.append(vt)
    if gate_branch is not None:
        in_specs.append(pl.BlockSpec((TQ, LANES), lambda b, i, *_: (b * nq + i, P_GATE // LANES)))
        args.append(gate)
    out_specs = [pl.BlockSpec((TQ, OW), lambda b, i, *_: (b * nq + i, 0))]
    out_shape = [jax.ShapeDtypeStruct((T, OW), BF16)]
    for w in side:
        rb = w.shape[0] // (B * nq)
        assert w.shape[0] % (B * nq) == 0 and rb % 16 == 0 and w.shape[1] % LANES == 0
        spec = pl.BlockSpec((rb, w.shape[1]), lambda b, i, *_: (b * nq + i, 0))
        in_specs.append(spec)
        args.append(w)
        out_specs.append(spec)
        out_shape.append(jax.ShapeDtypeStruct(w.shape, BF16))
    scratch = []
    if slopes is not None:
        scratch.append(pltpu.VMEM((G, KD, RQ), BF16))
    scratch += [pltpu.VMEM((G, 1, RQ), F32), pltpu.VMEM((G, V_ROWS, RQ), F32), pltpu.VMEM((2, G, TK, RQ), F32),
                pltpu.VMEM((2, G, 1, RQ), F32)]
    kern = functools.partial(_flash_kernel, TQ=TQ, TK=TK, G=G, R=R, NKT=S // TK, slopes=slopes, scale=scale,
                             sel=selb is not None, gate_branch=gate_branch, n_side=len(side))
    prefetch = list(tiles) if selb is not None else []
    grid_spec = pltpu.PrefetchScalarGridSpec(
        num_scalar_prefetch=len(prefetch),
        grid=(B, nq),
        in_specs=in_specs,
        out_specs=out_specs,
        scratch_shapes=scratch)
    outs = pl.pallas_call(
        kern,
        grid_spec=grid_spec,
        out_shape=out_shape,
        compiler_params=_params("parallel", "parallel"),
        name=name,
    )(*prefetch, *args)
    return list(outs)


def _mla_prep(cq, ckv, kr, krr, cos, sin, gq_ref, gkv_ref, wq_ref, wqr_ref, wk_ref, wvt_ref, vone_ref,
              q_out, k_out, vt_out):
    scale = (B_NOPE + B_ROPE) ** -0.5 * LOG2E
    nq = _rms(cq.astype(F32), gq_ref[...]).astype(BF16)
    nkv = _rms(ckv.astype(F32), gkv_ref[...]).astype(BF16)
    q1 = _dot_nt(wq_ref[...], nq)
    q2 = _dot_nt(wqr_ref[...], nq)
    kk = _dot(nkv, wk_ref[...])
    krope = kr.astype(F32) * cos + krr.astype(F32) * sin
    cos_t = cos.T
    sin_t = sin.T
    for h in range(B_HEADS):
        sl = slice(h * LANES, (h + 1) * LANES)
        q_out[sl, :] = ((q1[sl, :] * cos_t + q2[sl, :] * sin_t) * scale).astype(BF16)
        k_out[:, sl] = (kk[:, sl] + krope).astype(BF16)
    vt_out[...] = (_dot_nt(wvt_ref[...], nkv) + vone_ref[...]).astype(BF16)


def _build_mla_weights(w_uq, w_ukv):
    R1, R2 = w_uq.shape[0], w_ukv.shape[0]
    half = B_ROPE // 2
    hq = B_NOPE + B_ROPE
    wq, wqr, wk, wvt = [], [], [], []
    for h in range(B_HEADS):
        nope = w_uq[:, h * hq:h * hq + B_NOPE]
        rope = w_uq[:, h * hq + B_NOPE:(h + 1) * hq]
        rot = jnp.concatenate([-rope[:, half:], rope[:, :half]], axis=1)
        pad = jnp.zeros((R1, LANES - hq), w_uq.dtype)
        wq += [nope, rope, pad]
        wqr += [jnp.zeros((R1, B_NOPE), w_uq.dtype), rot, pad]
        wk += [w_ukv[:, h * 128:h * 128 + B_NOPE], jnp.zeros((R2, LANES - B_NOPE), w_ukv.dtype)]
        wvt += [w_ukv[:, h * 128 + B_NOPE:(h + 1) * 128].T, jnp.zeros((V_ROWS - HEAD_DIM, R2), w_ukv.dtype)]
    cat = lambda xs: jnp.concatenate(xs, axis=1).astype(BF16)
    return cat(wq).T, cat(wqr).T, cat(wk), jnp.concatenate(wvt, axis=0).astype(BF16)


def _gelu_tanh(x):
    return 0.5 * x * (1.0 + jnp.tanh(np.float32(np.sqrt(2.0 / np.pi)) * (x + 0.044715 * (x * x * x))))


def _compress_kernel(xk_ref, xv_ref, pek_ref, pev_ref, w1k_ref, w1v_ref, w2k_ref, w2v_ref, kc_ref, vct_ref, *, NC):
    for x_ref, pe_ref, w1_ref, w2_ref, is_v in ((xk_ref, pek_ref, w1k_ref, w2k_ref, False),
                                                 (xv_ref, pev_ref, w1v_ref, w2v_ref, True)):
        x = x_ref[...].astype(F32)
        a = _dot((x + pe_ref[0:1, :]).astype(BF16), w1_ref[0])
        b = _dot((x + pe_ref[1:2, :]).astype(BF16), w1_ref[1])
        hid = a + pltpu.roll(b, NC - 1, 0)
        o = _dot(_gelu_tanh(hid).astype(BF16), w2_ref[...])
        n = lax.broadcasted_iota(jnp.int32, o.shape, 0)
        o = jnp.where(n < NC - 1, o, 0.0)
        if is_v:
            vct_ref[...] = o.T.astype(vct_ref.dtype)
        else:
            kc_ref[...] = o.astype(kc_ref.dtype)


def _build_compress_weights(w1, w2, pe):
    half = C_CMP_BLOCK // 2
    G = C_KV_HEADS
    eye = jnp.eye(G, dtype=w1.dtype)
    w1s = [jnp.einsum('lde,hg->lhdge', w1[c * half:(c + 1) * half], eye).reshape(half * G * HEAD_DIM, G * C_CMP_HIDDEN)
           for c in range(2)]
    w1f = jnp.stack(w1s).astype(BF16)
    w2f = jnp.einsum('ed,hg->hegd', w2, eye).reshape(G * C_CMP_HIDDEN, G * HEAD_DIM).astype(BF16)
    pes = [jnp.broadcast_to(pe[c * half:(c + 1) * half, None, :], (half, G, HEAD_DIM)).reshape(1, -1) for c in range(2)]
    pef = jnp.concatenate(pes, axis=0).astype(F32)
    return w1f, w2f, pef


def _compress(k_cmp, v_cmp, B, S, k_weights, v_weights):
    NC = S // C_CMP_STRIDE
    CW = C_CMP_STRIDE * C_KV_HEADS * HEAD_DIM
    xk = k_cmp.reshape(B * NC, CW)
    xv = v_cmp.reshape(B * NC, CW)
    w1kf, w2kf, pekf = k_weights
    w1vf, w2vf, pevf = v_weights
    xspec = pl.BlockSpec((NC, CW), lambda b: (b, 0))
    full2 = lambda a: pl.BlockSpec(a.shape, lambda b: (0, 0))
    full3 = lambda a: pl.BlockSpec(a.shape, lambda b: (0, 0, 0))
    return pl.pallas_call(
        functools.partial(_compress_kernel, NC=NC),
        grid=(B,),
        in_specs=[xspec, xspec, full2(pekf), full2(pevf), full3(w1kf), full3(w1vf), full2(w2kf), full2(w2vf)],
        out_specs=[pl.BlockSpec((NC, LANES), lambda b: (b, 0)), pl.BlockSpec((LANES, NC), lambda b: (b, 0))],
        out_shape=[jax.ShapeDtypeStruct((B * NC, LANES), BF16), jax.ShapeDtypeStruct((B * LANES, NC), BF16)],
        compiler_params=_params("parallel"),
        name="nsa_compress",
    )(xk, xv, pekf, pevf, w1kf, w1vf, w2kf, w2vf)


def _cmp_sel_kernel(q_ref, kc_ref, vct_ref, cfeat_ref, ovt_ref, gate_ref, ocmp_ref, sel_ref, cnt_ref, rank_sc,
                    *, TQ, NC, NSEL, NTOP, slopes):
    i = pl.program_id(1)
    t0 = i * TQ
    R = C_HEADS // C_KV_HEADS
    scale = HEAD_DIM ** -0.5
    gate_t = _sigmoid(gate_ref[...].astype(F32)).T
    last = C_CMP_BLOCK - 1
    tlane = t0 + lax.broadcasted_iota(jnp.int32, (NC, TQ), 1)
    nrow = lax.broadcasted_iota(jnp.int32, (NC, TQ), 0)
    cvalid = (nrow * C_CMP_STRIDE + last <= tlane) & (nrow < NC - 1)
    hasc = ((t0 + lax.broadcasted_iota(jnp.int32, (1, TQ), 1)) >= last).astype(F32)
    jrow = lax.broadcasted_iota(jnp.int32, (NSEL, TQ), 0)
    tl = t0 + lax.broadcasted_iota(jnp.int32, (NSEL, TQ), 1)
    avail = jrow * C_SEL_BLOCK <= tl
    cur = tl // C_SEL_BLOCK
    forced = (jrow == 0) | (jrow == cur) | (jrow == cur - 1)
    q_t = q_ref[...].astype(F32).T
    frow = lax.broadcasted_iota(jnp.int32, (HEAD_DIM, TQ), 0)
    sub8 = lax.broadcasted_iota(jnp.int32, (8, TQ), 0)
    pieces, sel_parts = [], []
    picked = jnp.zeros((NSEL, TQ), F32)
    for g in range(C_KV_HEADS):
        kca = jnp.concatenate([kc_ref[:, g * HEAD_DIM:(g + 1) * HEAD_DIM], cfeat_ref[...]], axis=1)
        vo = jnp.concatenate([vct_ref[g * HEAD_DIM:(g + 1) * HEAD_DIM, :], ovt_ref[...]], axis=0)
        imp = jnp.zeros((HEAD_DIM, TQ), F32)
        for r in range(R):
            hh = g * R + r
            s3 = _split_bf16(slopes[hh] * LOG2E)
            feat = jnp.zeros((HEAD_DIM, TQ), F32)
            for k in range(3):
                feat = jnp.where(frow == k, s3[k] * 256.0, jnp.where(frow == 3 + k, s3[k], feat))
            qa = jnp.concatenate([q_t[hh * HEAD_DIM:(hh + 1) * HEAD_DIM, :] * (scale * LOG2E), feat],
                                 axis=0).astype(BF16)
            st = jnp.where(cvalid, _dot(kca, qa), NEG_INF)
            et = jnp.exp2(st - jnp.max(st, axis=0, keepdims=True))
            inv = hasc / jnp.sum(et, axis=0, keepdims=True)
            res = _dot(vo, et.astype(BF16))
            pieces.append(res[0:HEAD_DIM, :] * (inv * gate_t[3 * hh:3 * hh + 1, :]))
            imp = imp + res[HEAD_DIM:2 * HEAD_DIM, :] * inv
        v = jnp.where(avail, imp[0:NSEL, :], NEG_INF)
        v = jnp.where(forced, -NEG_INF, v)
        def rank_first(nb, v=v, g=g):
            ranks = [jnp.zeros((8, TQ), F32) for _ in range(nb // 8)]
            for ii in range(nb):
                ri = v[ii:ii + 1, :]
                for k in range(nb // 8):
                    vk = v[8 * k:8 * k + 8, :]
                    if 8 * k > ii:
                        inc = jnp.where(ri >= vk, 1.0, 0.0)
                    elif 8 * k + 7 < ii:
                        inc = jnp.where(ri > vk, 1.0, 0.0)
                    else:
                        inc = jnp.where(sub8 + 8 * k > ii, jnp.where(ri >= vk, 1.0, 0.0),
                                        jnp.where(ri > vk, 1.0, 0.0))
                    ranks[k] = ranks[k] + inc
            rank_sc[g, 0:nb, :] = jnp.concatenate(ranks, axis=0)
            if nb < NSEL:
                rank_sc[g, nb:NSEL, :] = jnp.full((NSEL - nb, TQ), float(NSEL), F32)

        n_started = (t0 + TQ - 1) // C_SEL_BLOCK + 1
        ladder = sorted({min(NSEL, b) for b in (NSEL // 4, NSEL // 2, 3 * NSEL // 4, NSEL) if b % 8 == 0} | {NSEL})
        for lo_b, nb in zip([0] + ladder[:-1], ladder):
            pl.when((n_started > lo_b) & (n_started <= nb))(functools.partial(rank_first, nb))
        rank = rank_sc[g]
        sel_parts.append(jnp.where(rank < NTOP, 0.0, MASK_BIAS))
        if NSEL < HEAD_DIM:
            sel_parts.append(jnp.zeros((HEAD_DIM - NSEL, TQ), F32))
        picked = picked + jnp.where(rank < NTOP, 1.0, 0.0)
    sel_ref[...] = jnp.concatenate(sel_parts, axis=0).astype(sel_ref.dtype)
    cnt_ref[...] = _dot_nt(jnp.ones((8, TQ), BF16), picked.astype(BF16))
    ocmp_ref[...] = jnp.concatenate(pieces, axis=0).T.astype(ocmp_ref.dtype)


def _overlap_t(S):
    n_cmp = (S - C_CMP_BLOCK) // C_CMP_STRIDE + 1
    n_sel = S // C_SEL_BLOCK
    NC = S // C_CMP_STRIDE
    cstart = np.arange(n_cmp) * C_CMP_STRIDE
    cend = cstart + C_CMP_BLOCK - 1
    sstart = np.arange(n_sel) * C_SEL_BLOCK
    send = sstart + C_SEL_BLOCK - 1
    ov = np.clip(np.minimum(cend[:, None], send[None, :]) - np.maximum(cstart[:, None], sstart[None, :]) + 1,
                 0, None).astype(np.float32) / C_CMP_STRIDE
    ovt = np.zeros((HEAD_DIM, NC), np.float32)
    ovt[:n_sel, :n_cmp] = ov.T
    cend_all = np.arange(NC) * C_CMP_STRIDE + C_CMP_BLOCK - 1
    cfeat = np.zeros((NC, HEAD_DIM), np.float32)
    cfeat[:, 0:3] = (cend_all // 256)[:, None]
    cfeat[:, 3:6] = (cend_all % 256)[:, None]
    return jnp.asarray(ovt, dtype=BF16), jnp.asarray(cfeat, dtype=BF16)


def _cmp_sel(P, kc, vct, B, S, slopes):
    T = B * S
    TQ = TQ_ATT
    nq = S // TQ
    NC = S // C_CMP_STRIDE
    NSEL = S // C_SEL_BLOCK
    assert NSEL <= HEAD_DIM and NSEL % 8 == 0 and S <= 256 * 256
    QW = C_HEADS * HEAD_DIM
    ovt, cfeat = _overlap_t(S)
    kern = functools.partial(_cmp_sel_kernel, TQ=TQ, NC=NC, NSEL=NSEL, NTOP=min(C_SEL_TOP, NSEL), slopes=slopes)
    return pl.pallas_call(
        kern,
        grid=(B, nq),
        in_specs=[pl.BlockSpec((TQ, QW), lambda b, i: (b * nq + i, P_CQ // QW)),
                  pl.BlockSpec((NC, LANES), lambda b, i: (b, 0)),
                  pl.BlockSpec((LANES, NC), lambda b, i: (b, 0)),
                  pl.BlockSpec((NC, HEAD_DIM), lambda b, i: (0, 0)),
                  pl.BlockSpec((HEAD_DIM, NC), lambda b, i: (0, 0)),
                  pl.BlockSpec((TQ, LANES), lambda b, i: (b * nq + i, P_GATE // LANES))],
        out_specs=[pl.BlockSpec((TQ, QW), lambda b, i: (b * nq + i, 0)),
                   pl.BlockSpec((LANES, TQ), lambda b, i: (0, b * nq + i)),
                   pl.BlockSpec((8, NSEL), lambda b, i: (b * nq + i, 0))],
        out_shape=[jax.ShapeDtypeStruct((T, QW), BF16), jax.ShapeDtypeStruct((LANES, T), BF16),
                   jax.ShapeDtypeStruct((B * nq * 8, NSEL), F32)],
        scratch_shapes=[pltpu.VMEM((C_KV_HEADS, NSEL, TQ), F32)],
        compiler_params=_params("parallel", "parallel"),
        name="nsa_cmp_select",
    )(P, kc, vct, cfeat, ovt, P)


def _selected_tile_lists(cnt, B, S):
    nq = S // TQ_ATT
    nkt = S // TK_SEL
    per = TK_SEL // C_SEL_BLOCK
    blk = cnt.reshape(B * nq, 8, -1)[:, 0, :] > 0.5
    need = jnp.any(blk.reshape(B * nq, nkt, per), axis=-1)
    own = ((jnp.arange(B * nq, dtype=jnp.int32) % nq) * TQ_ATT + TQ_ATT - 1) // TK_SEL
    jj = jnp.arange(nkt, dtype=jnp.int32)[None, :]
    need = (need | (jj == 0) | (jj == own[:, None])) & (jj <= own[:, None])
    order = jnp.argsort(jnp.where(need, jj, nkt + jj), axis=-1).astype(jnp.int32)
    return order.reshape(-1), jnp.sum(need, axis=-1).astype(jnp.int32)


def _outproj_kernel(x_ref, oa_ref, ob_ref, oc1_ref, oc2_ref, oc3_ref, ga_ref, gb_ref, gc_ref, wo_ref, gp_ref, o_ref):
    oa = _rms(oa_ref[...].astype(F32), ga_ref[...]).astype(BF16)
    ob = _rms(ob_ref[...].astype(F32), gb_ref[...]).astype(BF16)
    oc = oc1_ref[...].astype(F32) + oc2_ref[...].astype(F32) + oc3_ref[...].astype(F32)
    oc = _rms(oc, gc_ref[...]).astype(BF16)
    m = _dot(oa, wo_ref[0:256, :]) + _dot(ob, wo_ref[256:512, :]) + _dot(oc, wo_ref[512:1024, :])
    o_ref[...] = x_ref[...] + _rms(m, gp_ref[...])


def _outproj(x, o_a, o_b, o_cmp, o_slc, o_win, g_oa, g_ob, g_oc, w_o, g_post):
    T = x.shape[0]
    TM = TM_OUT
    row = lambda w: pl.BlockSpec((TM, w), lambda i: (i, 0))
    full = lambda r, c: pl.BlockSpec((r, c), lambda i: (0, 0))
    return pl.pallas_call(
        _outproj_kernel,
        grid=(T // TM,),
        in_specs=[row(D_MODEL), row(256), row(256), row(512), row(512), row(512),
                  full(1, 256), full(1, 256), full(1, 512), full(D_MODEL, D_MODEL), full(1, D_MODEL)],
        out_specs=row(D_MODEL),
        out_shape=jax.ShapeDtypeStruct((T, D_MODEL), F32),
        compiler_params=_params("parallel"),
        name="outproj",
    )(x, o_a, o_b, o_cmp, o_slc, o_win, g_oa.reshape(1, -1), g_ob.reshape(1, -1), g_oc.reshape(1, -1),
      w_o.astype(BF16), g_post.reshape(1, -1))


def _swiglu_step(h, wg_ref, wu_ref, wd_ref, acc_ref):
    a = _dot(h, wg_ref[...])
    b = _dot(h, wu_ref[...])
    z = (a * _sigmoid(a) * b).astype(BF16)
    acc_ref[...] += _dot(z, wd_ref[...])


def _ffn_kernel(x_ref, gpre_ref, wg_ref, wu_ref, wd_ref, gpost_ref, o_ref, h_sc, acc_sc, *, nf):
    f = pl.program_id(1)

    @pl.when(f == 0)
    def _():
        h_sc[...] = _rms(x_ref[...], gpre_ref[...]).astype(BF16)
        acc_sc[...] = jnp.zeros_like(acc_sc)

    _swiglu_step(h_sc[...], wg_ref, wu_ref, wd_ref, acc_sc)

    @pl.when(f == nf - 1)
    def _():
        o_ref[...] = x_ref[...] + _rms(acc_sc[...], gpost_ref[...])


def _ffn(x, g_pre, wg, wu, wd, g_post):
    T = x.shape[0]
    TM, TF = TM_FFN, TF_FFN
    FF = wg.shape[1]
    nf = FF // TF
    return pl.pallas_call(
        functools.partial(_ffn_kernel, nf=nf),
        grid=(T // TM, nf),
        in_specs=[pl.BlockSpec((TM, D_MODEL), lambda i, f: (i, 0)),
                  pl.BlockSpec((1, D_MODEL), lambda i, f: (0, 0)),
                  pl.BlockSpec((D_MODEL, TF), lambda i, f: (0, f)),
                  pl.BlockSpec((D_MODEL, TF), lambda i, f: (0, f)),
                  pl.BlockSpec((TF, D_MODEL), lambda i, f: (f, 0)),
                  pl.BlockSpec((1, D_MODEL), lambda i, f: (0, 0))],
        out_specs=pl.BlockSpec((TM, D_MODEL), lambda i, f: (i, 0)),
        out_shape=jax.ShapeDtypeStruct((T, D_MODEL), F32),
        scratch_shapes=[pltpu.VMEM((TM, D_MODEL), BF16), pltpu.VMEM((TM, D_MODEL), F32)],
        compiler_params=_params("parallel", "arbitrary"),
        name="dense_ffn",
    )(x, g_pre.reshape(1, -1), wg.astype(BF16), wu.astype(BF16), wd.astype(BF16), g_post.reshape(1, -1))


def _expert_kernel(te_ref, nt_ref, x_ref, wg_ref, wu_ref, wd_ref, o_ref, acc_sc, *, nf):
    j = pl.program_id(0)
    f = pl.program_id(1)
    live = j < nt_ref[0]

    @pl.when(f == 0)
    def _():
        acc_sc[...] = jnp.zeros_like(acc_sc)

    @pl.when(live)
    def _():
        _swiglu_step(x_ref[...], wg_ref.at[0], wu_ref.at[0], wd_ref.at[0], acc_sc)

    @pl.when(f == nf - 1)
    def _():
        o_ref[...] = acc_sc[...].astype(o_ref.dtype)


def _experts(xs, tile_expert, n_tiles, wg, wu, wd):
    NT = xs.shape[0] // TM_MOE
    TM, TF = TM_MOE, TF_MOE
    assert wg.shape[2] % TF == 0
    FF = wg.shape[2]
    nf = FF // TF
    fidx = lambda j, f, nt: jnp.where(j < nt[0], f, nf - 1)
    grid_spec = pltpu.PrefetchScalarGridSpec(
        num_scalar_prefetch=2,
        grid=(NT, nf),
        in_specs=[pl.BlockSpec((TM, D_MODEL), lambda j, f, te, nt: (j, 0)),
                  pl.BlockSpec((1, D_MODEL, TF), lambda j, f, te, nt: (te[j], 0, fidx(j, f, nt))),
                  pl.BlockSpec((1, D_MODEL, TF), lambda j, f, te, nt: (te[j], 0, fidx(j, f, nt))),
                  pl.BlockSpec((1, TF, D_MODEL), lambda j, f, te, nt: (te[j], fidx(j, f, nt), 0))],
        out_specs=pl.BlockSpec((TM, D_MODEL), lambda j, f, te, nt: (j, 0)),
        scratch_shapes=[pltpu.VMEM((TM, D_MODEL), F32)])
    return pl.pallas_call(
        functools.partial(_expert_kernel, nf=nf),
        grid_spec=grid_spec,
        out_shape=jax.ShapeDtypeStruct(xs.shape, BF16),
        compiler_params=_params("parallel", "arbitrary"),
        name="moe_experts",
    )(tile_expert, n_tiles, xs, wg, wu, wd)


def _router_kernel(x_ref, g_ref, wr_ref, h_out, meta_out, metat_out, start_out, cnt_out, run_sc, *, TS):
    s = pl.program_id(0)

    @pl.when(s == 0)
    def _():
        run_sc[...] = jnp.zeros_like(run_sc)

    h = _rms(x_ref[...], g_ref[...])
    hb = h.astype(BF16)
    h_out[...] = hb
    hl = (h - hb.astype(F32)).astype(BF16)
    w = wr_ref[...]
    wh = w.astype(BF16)
    wl = (w - wh.astype(F32)).astype(BF16)
    logits = _dot(hb, wh) + _dot(hl, wh) + _dot(hb, wl)
    lane = lax.broadcasted_iota(jnp.int32, (TS, LANES), 1).astype(F32)
    logits = jnp.where(lane < N_EXPERTS, logits, NEG_INF)
    m1 = jnp.max(logits, axis=-1, keepdims=True)
    i1 = jnp.min(jnp.where(logits == m1, lane, float(LANES)), axis=-1, keepdims=True)
    rest = jnp.where(lane == i1, NEG_INF, logits)
    m2 = jnp.max(rest, axis=-1, keepdims=True)
    i2 = jnp.min(jnp.where(rest == m2, lane, float(LANES)), axis=-1, keepdims=True)
    ex = jnp.exp(m2 - m1)
    w1 = 1.0 / (1.0 + ex)
    w2 = ex / (1.0 + ex)
    oh1 = lane == i1
    oh2 = lane == i2
    oh = jnp.where(oh1, 1.0, 0.0) + jnp.where(oh2, 1.0, 0.0)
    tri = jnp.where(lax.broadcasted_iota(jnp.int32, (TS, TS), 0) > lax.broadcasted_iota(jnp.int32, (TS, TS), 1),
                    1.0, 0.0).astype(BF16)
    run = run_sc[...]
    before = _dot(tri, oh.astype(BF16)) + run
    r1 = jnp.sum(jnp.where(oh1, before, 0.0), axis=-1, keepdims=True)
    r2 = jnp.sum(jnp.where(oh2, before, 0.0), axis=-1, keepdims=True)
    start_out[0] = run
    run = run + jnp.sum(oh, axis=0, keepdims=True)
    run_sc[...] = run
    cnt_out[...] = run
    meta = jnp.where(lane == 0, i1, jnp.where(lane == 1, i2, jnp.where(lane == 2, r1, jnp.where(
        lane == 3, r2, jnp.where(lane == 4, w1, jnp.where(lane == 5, w2, 0.0))))))
    meta_out[...] = meta
    metat_out[...] = meta.T[0:8, :]


def _router(x, g, w_router):
    T = x.shape[0]
    TS = TS_MOE
    nt = T // TS
    wr = jnp.zeros((D_MODEL, LANES), F32).at[:, :N_EXPERTS].set(w_router.astype(F32))
    return pl.pallas_call(
        functools.partial(_router_kernel, TS=TS),
        grid=(nt,),
        in_specs=[pl.BlockSpec((TS, D_MODEL), lambda s: (s, 0)),
                  pl.BlockSpec((1, D_MODEL), lambda s: (0, 0)),
                  pl.BlockSpec((D_MODEL, LANES), lambda s: (0, 0))],
        out_specs=[pl.BlockSpec((TS, D_MODEL), lambda s: (s, 0)),
                   pl.BlockSpec((TS, LANES), lambda s: (s, 0)),
                   pl.BlockSpec((8, TS), lambda s: (0, s)),
                   pl.BlockSpec((1, 1, LANES), lambda s: (s, 0, 0)),
                   pl.BlockSpec((1, LANES), lambda s: (0, 0))],
        out_shape=[jax.ShapeDtypeStruct((T, D_MODEL), BF16),
                   jax.ShapeDtypeStruct((T, LANES), F32),
                   jax.ShapeDtypeStruct((8, T), F32),
                   jax.ShapeDtypeStruct((nt, 1, LANES), F32),
                   jax.ShapeDtypeStruct((1, LANES), F32)],
        scratch_shapes=[pltpu.VMEM((1, LANES), F32)],
        compiler_params=_params("arbitrary"),
        name="moe_router",
    )(x, g.reshape(1, -1), wr)


def _gather_kernel(te_ref, slo_ref, shi_ref, base_ref, metat_ref, h_ref, xs_out, acc_sc, *, TM, TS, CH):
    j = pl.program_id(0)
    e = te_ref[j]
    ef = e.astype(F32)
    nsub = TM // CH
    nt = h_ref.shape[0] // TS

    def picked_rows(s, want):
        off = pl.multiple_of(s * TS, TS)
        e1 = metat_ref[0:1, pl.ds(off, TS)]
        e2 = metat_ref[1:2, pl.ds(off, TS)]
        r1 = metat_ref[2:3, pl.ds(off, TS)]
        r2 = metat_ref[3:4, pl.ds(off, TS)]
        key = jnp.where(e1 == ef, r1, jnp.where(e2 == ef, r2, -1.0))
        sel = jnp.where(key == want, 1.0, 0.0).astype(BF16)
        return _dot(sel, h_ref[pl.ds(off, TS), :])

    wants = []
    for sub in range(nsub):
        q = j * nsub + sub
        first = j * TM + sub * CH - base_ref[e]
        want = (first + lax.broadcasted_iota(jnp.int32, (CH, 1), 0)).astype(F32)
        lo = jnp.minimum(slo_ref[q], nt - 1)
        second = jnp.where(lo + 1 < nt, want, -2.0)
        acc_sc[sub] = picked_rows(lo, want) + picked_rows(jnp.minimum(lo + 1, nt - 1), second)
        wants.append(want)
    for sub in range(nsub):
        q = j * nsub + sub

        def body(s, _, sub=sub):
            acc_sc[sub] += picked_rows(s, wants[sub])
            return 0

        lax.fori_loop(slo_ref[q] + 2, shi_ref[q] + 1, body, 0)
    for sub in range(nsub):
        xs_out[sub * CH:(sub + 1) * CH, :] = acc_sc[sub].astype(xs_out.dtype)


def _gather(hb, metat, tile_expert, s_lo, s_hi, base, NT):
    TM, TS = TM_MOE, TS_MOE
    assert hb.shape[0] % TS == 0
    grid_spec = pltpu.PrefetchScalarGridSpec(
        num_scalar_prefetch=4,
        grid=(NT,),
        in_specs=[pl.BlockSpec(memory_space=pltpu.VMEM), pl.BlockSpec(memory_space=pltpu.VMEM)],
        out_specs=pl.BlockSpec((TM, D_MODEL), lambda j, *_: (j, 0)),
        scratch_shapes=[pltpu.VMEM((TM // CH_MOE, CH_MOE, D_MODEL), F32)])
    return pl.pallas_call(
        functools.partial(_gather_kernel, TM=TM, TS=TS, CH=CH_MOE),
        grid_spec=grid_spec,
        out_shape=jax.ShapeDtypeStruct((NT * TM, D_MODEL), BF16),
        compiler_params=_params("arbitrary"),
        name="moe_gather",
    )(tile_expert, s_lo, s_hi, base, metat, hb)


def _combine_kernel(blk_ref, row0_ref, base_ref, meta_ref, x_ref, gpost_ref, ys_hbm, o_ref, buf, sem, selw_sc,
                    *, TS, CH, U, nt):
    s = pl.program_id(0)
    slot = s % 2

    def chunk_copies(step, slot_):
        out = []
        for u in range(U):
            r0 = pl.multiple_of(blk_ref[step * U + u] * CH, CH)
            out.append(pltpu.make_async_copy(ys_hbm.at[pl.ds(r0, CH), :], buf.at[slot_, pl.ds(u * CH, CH), :],
                                             sem.at[slot_]))
        return out

    @pl.when(s == 0)
    def _():
        for c in chunk_copies(0, 0):
            c.start()

    @pl.when(s + 1 < nt)
    def _():
        for c in chunk_copies(s + 1, 1 - slot):
            c.start()

    meta = meta_ref[...]
    e1, e2 = meta[:, 0:1], meta[:, 1:2]
    w1, w2 = meta[:, 4:5], meta[:, 5:6]
    b1 = jnp.zeros((TS, 1), F32)
    b2 = jnp.zeros((TS, 1), F32)
    for e in range(N_EXPERTS):
        be = base_ref[e].astype(F32)
        b1 = jnp.where(e1 == float(e), be, b1)
        b2 = jnp.where(e2 == float(e), be, b2)
    pos1 = b1 + meta[:, 2:3]
    pos2 = b2 + meta[:, 3:4]

    lane = lax.broadcasted_iota(jnp.int32, (1, CH), 1)
    for u in range(U):
        rows = (row0_ref[s * U + u] + lane).astype(F32)
        selw = jnp.where(pos1 == rows, w1, 0.0) + jnp.where(pos2 == rows, w2, 0.0)
        selw_sc[:, u * CH:(u + 1) * CH] = selw.astype(BF16)
    for c in chunk_copies(s, slot):
        c.wait()
    y = _dot(selw_sc[...], buf[slot])
    o_ref[...] = x_ref[...] + _rms(y, gpost_ref[...])


def _combine(x, meta, ys, chunk_blk, chunk_row0, base, g_post):
    T = x.shape[0]
    TS, CH, U = TS_MOE, CH_MOE, U_MOE
    nt = T // TS
    grid_spec = pltpu.PrefetchScalarGridSpec(
        num_scalar_prefetch=3,
        grid=(nt,),
        in_specs=[pl.BlockSpec((TS, LANES), lambda s, *_: (s, 0)),
                  pl.BlockSpec((TS, D_MODEL), lambda s, *_: (s, 0)),
                  pl.BlockSpec((1, D_MODEL), lambda s, *_: (0, 0)),
                  pl.BlockSpec(memory_space=pl.ANY)],
        out_specs=pl.BlockSpec((TS, D_MODEL), lambda s, *_: (s, 0)),
        scratch_shapes=[pltpu.VMEM((2, U * CH, D_MODEL), BF16), pltpu.SemaphoreType.DMA((2,)),
                        pltpu.VMEM((TS, U * CH), BF16)])
    return pl.pallas_call(
        functools.partial(_combine_kernel, TS=TS, CH=CH, U=U, nt=nt),
        grid_spec=grid_spec,
        out_shape=jax.ShapeDtypeStruct((T, D_MODEL), F32),
        compiler_params=_params("arbitrary"),
        name="moe_combine",
    )(chunk_blk, chunk_row0, base, meta, x, g_post.reshape(1, -1), ys)


def _moe(x, g_pre, w_router, w_gate, w_up, w_down, g_post):
    T = x.shape[0]
    TS, TM, CH, U = TS_MOE, TM_MOE, CH_MOE, U_MOE
    nt = T // TS
    NT = (2 * T) // TM + N_EXPERTS
    hb, meta, metat, start, cnt = _router(x, g_pre, w_router)

    i32 = jnp.int32
    counts = cnt[0, :N_EXPERTS].astype(i32)
    start = start[:, 0, :N_EXPERTS].astype(i32)
    start_ext = jnp.concatenate([start, counts[None, :]], axis=0)
    tiles_e = (counts + TM - 1) // TM
    tiles_cum = jnp.cumsum(tiles_e)
    base = (tiles_cum - tiles_e) * TM
    n_tiles = tiles_cum[-1:]
    jj = jnp.arange(NT, dtype=i32)
    tile_expert = jnp.minimum(jnp.sum(jj[:, None] >= tiles_cum[None, :], axis=1), N_EXPERTS - 1).astype(i32)
    live = jj < n_tiles[0]
    qq = jnp.arange(NT * (TM // CH), dtype=i32)
    q_exp = tile_expert[qq // (TM // CH)]
    a = qq * CH - base[q_exp]
    st_e = start_ext[:, q_exp]
    s_lo = jnp.sum(st_e[1:] <= a[None, :], axis=0).astype(i32)
    s_hi = (jnp.sum(st_e[:-1] < (a + CH)[None, :], axis=0) - 1).astype(i32)
    q_live = live[qq // (TM // CH)]
    s_lo = jnp.where(q_live, s_lo, 1)
    s_hi = jnp.where(q_live, s_hi, 0)

    xs = _gather(hb, metat, tile_expert, s_lo, s_hi, base.astype(i32), NT)
    ys = _experts(xs, tile_expert, n_tiles.astype(i32), w_gate.astype(BF16), w_up.astype(BF16), w_down.astype(BF16))

    lo = base[None, :] + start_ext[:-1]
    hi = base[None, :] + start_ext[1:]
    n_e = jnp.where(hi > lo, (hi - 1) // CH - lo // CH + 1, 0)
    cum = jnp.cumsum(n_e, axis=1)
    uu = jnp.arange(U, dtype=i32)
    e_u = jnp.minimum(jnp.sum(uu[None, :, None] >= cum[:, None, :], axis=2), N_EXPERTS - 1)
    first = jnp.take_along_axis(lo // CH, e_u, axis=1)
    skipped = jnp.take_along_axis(cum - n_e, e_u, axis=1)
    chunk_blk = jnp.clip(first + (uu[None, :] - skipped), 0, (NT * TM) // CH - 1).astype(i32)
    used = uu[None, :] < cum[:, -1:]
    chunk_blk = jnp.where(used, chunk_blk, chunk_blk[:, :1])
    chunk_row0 = jnp.where(used, chunk_blk * CH, -(NT * TM)).astype(i32)
    return _combine(x, meta, ys, chunk_blk.reshape(-1), chunk_row0.reshape(-1), base.astype(i32), g_post)


def kernel(x, positions, w_in, a_sinks, g_cq, w_uq, g_ckv, w_ukv, c_w1_k, c_w2_k, c_pe_k, c_w1_v, c_w2_v, c_pe_v,
           g_oa, g_ob, g_oc, w_o, g_pre_mix, g_post_mix, g_pre_ffn, g_post_ffn, ffn_w_gate, ffn_w_up, ffn_w_down,
           moe_router, moe_w_gate, moe_w_up, moe_w_down):
    B, S, _ = x.shape
    T = B * S
    depth = w_in.shape[0]
    a_slopes = _alibi_slopes(A_HEADS)
    c_slopes = _alibi_slopes(C_HEADS)
    xf = x.reshape(T, D_MODEL).astype(F32)
    cos, sin = _rope_tables(positions)
    att_scale = HEAD_DIM ** -0.5
    flat = lambda w: w.reshape(-1, w.shape[-1])
    bf16_w = {}
    w_p_all, w_vt_all = jax.vmap(_build_w_in)(w_in)
    mla_w_all = jax.vmap(_build_mla_weights)(w_uq, w_ukv)
    cmp_k_all = jax.vmap(_build_compress_weights)(c_w1_k, c_w2_k, c_pe_k)
    cmp_v_all = jax.vmap(_build_compress_weights)(c_w1_v, c_w2_v, c_pe_v)
    for l in range(depth):
        P, VT, qB, kB, vtB, k_cmp, v_cmp = _inproj(xf, g_pre_mix[l].reshape(1, -1), w_p_all[l], w_vt_all[l], S, cos, sin,
                                     g_cq[l], g_ckv[l], tuple(w[l] for w in mla_w_all))
        o_a = _banded_attn(P, VT, B, S, name="swa_attn", NSUB=NSUB_A, G=A_KV_HEADS, R=A_HEADS // A_KV_HEADS,
                           KD=KD_A, q_col=P_AQ, k_col=P_KA, vt_slab=VT_A, W=A_WINDOW, slopes=a_slopes,
                           scale=att_scale, sinks=a_sinks[l])
        if l % 2 == 0:
            side = (ffn_w_gate[l // 2], ffn_w_up[l // 2], ffn_w_down[l // 2])
        else:
            side = (flat(moe_w_down[l // 2]),)
        o_b, *cast = _flash_attn(qB, kB, vtB, B, S, name="mla_attn", TQ=TQ_MLA, TK=TQ_MLA, G=B_HEADS, R=1, KD=LANES,
                                 q_col=0, q_width=B_HEADS * LANES, k_col=0, vt_slab=0, side=side)
        if l % 2 == 0:
            bf16_w["ffn", l] = cast
        else:
            bf16_w["down", l] = cast[0]
        kc, vct = _compress(k_cmp, v_cmp, B, S, tuple(w[l] for w in cmp_k_all), tuple(w[l] for w in cmp_v_all))
        o_cmp, selb, pick_cnt = _cmp_sel(P, kc, vct, B, S, c_slopes)
        c_args = dict(TQ=TQ_ATT, G=C_KV_HEADS, R=C_HEADS // C_KV_HEADS, q_col=P_CQ, q_width=C_HEADS * HEAD_DIM,
                      slopes=c_slopes, scale=att_scale, gate=P)
        side = ()
        if l % 2 == 0 and l + 1 < depth:
            side = (flat(moe_w_gate[(l + 1) // 2]),)
        elif l % 2 == 1:
            side = (flat(moe_w_up[l // 2]),)
        o_slc, *cast = _flash_attn(P, P, VT, B, S, name="nsa_selected", TK=TK_SEL, KD=KD_SEL, k_col=P_KSEL,
                                   vt_slab=VT_SEL, selb=selb, tiles=_selected_tile_lists(pick_cnt, B, S),
                                   gate_branch=1, side=side, **c_args)
        if l % 2 == 0 and l + 1 < depth:
            bf16_w["gate", l + 1] = cast[0]
        elif l % 2 == 1:
            bf16_w["up", l] = cast[0]
        o_win = _banded_attn(P, VT, B, S, name="nsa_window", NSUB=NSUB_WIN, G=C_KV_HEADS, R=C_HEADS // C_KV_HEADS,
                             KD=KD_WIN, q_col=P_CQ, k_col=P_KWIN, vt_slab=VT_WIN, W=C_WINDOW, slopes=c_slopes,
                             scale=att_scale, gate_branch=2)
        xf = _outproj(xf, o_a, o_b, o_cmp, o_slc, o_win, g_oa[l], g_ob[l], g_oc[l], w_o[l], g_post_mix[l])
        if l % 2 == 0:
            wg, wu, wd = bf16_w["ffn", l]
            xf = _ffn(xf, g_pre_ffn[l], wg, wu, wd, g_post_ffn[l])
        else:
            e = l // 2
            wg = bf16_w["gate", l].reshape(moe_w_gate[e].shape)
            wu = bf16_w["up", l].reshape(moe_w_up[e].shape)
            wd = bf16_w["down", l].reshape(moe_w_down[e].shape)
            xf = _moe(xf, g_pre_ffn[l], moe_router[e], wg, wu, wd, g_post_ffn[l])
    return xf.reshape(B, S, D_MODEL)
```

```python
import functools

import numpy as np
import jax
import jax.numpy as jnp
from jax import lax
from jax.experimental import pallas as pl
from jax.experimental.pallas import tpu as pltpu

F32 = jnp.float32
BF16 = jnp.bfloat16

D_MODEL = 1024
HEAD_DIM = 64
NORM_EPS = 1e-6
NEG_INF = -1e30
ROPE_THETA = 10000.0
LOG2E = 1.4426950408889634

A_HEADS, A_KV_HEADS, A_WINDOW = 4, 2, 128
B_HEADS, B_Q_RANK, B_KV_RANK, B_NOPE, B_ROPE = 4, 256, 128, 64, 32
C_HEADS, C_KV_HEADS, C_WINDOW = 8, 2, 512
C_CMP_BLOCK, C_CMP_STRIDE, C_CMP_HIDDEN = 32, 16, 128
C_SEL_BLOCK, C_SEL_TOP = 64, 16
N_EXPERTS = 8

LANES = 128
VMEM_LIMIT = 48 * 1024 * 1024

P_CQ, P_AQ, P_BCQ = 0, 512, 768
P_BCKV, P_CKC, P_CVC, P_KR, P_KRROT, P_GATE = 1024, 1152, 1280, 1408, 1536, 1664
P_KSEL, KD_SEL = 1792, 256
P_KWIN, KD_WIN = 2304, 128
P_KA, KD_A = 2560, 128
P_WIDTH = 2816
V_ROWS = 80
VT_SEL, VT_WIN, VT_A = 0, 2, 4
VT_SLABS = 6
MASK_BIAS = -1e30
M_INIT = -1e29

TM_PROJ = 512
TM_OUT = 1024
TQ_ATT = 128
TK_BAND = 128
NSUB_A = 4
NSUB_WIN = 4
TQ_MLA = 256
TK_SEL = 256
TM_FFN = 512
TF_FFN = 2048
TF_MOE = 1792
TS_MOE = 512
TM_MOE = 512
CH_MOE = 128
U_MOE = 24


def _alibi_slopes(n):
    return [float(np.float32(2.0 ** (-8.0 * (i + 1) / n))) for i in range(n)]


def _split_bf16(x):
    parts = []
    for _ in range(3):
        p = float(np.asarray(x, np.float32).astype(jnp.bfloat16))
        parts.append(p)
        x = x - p
    return parts


def _dot(a, b):
    return jnp.dot(a, b, preferred_element_type=F32)


def _dot_nt(a, b):
    return lax.dot_general(a, b, (((1,), (1,)), ((), ())), preferred_element_type=F32)


def _rms(x, g):
    return x * lax.rsqrt(jnp.mean(x * x, axis=-1, keepdims=True) + NORM_EPS) * g


def _sigmoid(x):
    return 1.0 / (1.0 + jnp.exp(-x))


def _params(*sem):
    return pltpu.CompilerParams(dimension_semantics=sem, vmem_limit_bytes=VMEM_LIMIT)


def _inproj_kernel(x_ref, g_ref, w_ref, wvt_ref, vone_ref, kc_ref, cos_ref, sin_ref, gq_ref, gkv_ref,
                   wq_ref, wqr_ref, wk_ref, wvtb_ref, voneb_ref, o_ref, vt_ref, qb_out, kb_out, vtb_out,
                   kcmp_out, vcmp_out):
    h = _rms(x_ref[...], g_ref[...]).astype(BF16)
    for n in range(P_WIDTH // 256):
        o_ref[:, n * 256:(n + 1) * 256] = _dot(h, w_ref[:, n * 256:(n + 1) * 256]).astype(BF16)
    kcmp_out[...] = o_ref[:, P_CKC:P_CKC + LANES]
    vcmp_out[...] = o_ref[:, P_CVC:P_CVC + LANES]
    _mla_prep(o_ref[:, P_BCQ:P_BCQ + B_Q_RANK], o_ref[:, P_BCKV:P_BCKV + B_KV_RANK], o_ref[:, P_KR:P_KR + LANES],
              o_ref[:, P_KRROT:P_KRROT + LANES], cos_ref[...], sin_ref[...], gq_ref, gkv_ref,
              wq_ref, wqr_ref, wk_ref, wvtb_ref, voneb_ref, qb_out, kb_out, vtb_out)
    for g in range(C_KV_HEADS):
        o_ref[:, P_KSEL + g * KD_SEL + 64:P_KSEL + g * KD_SEL + 192] = kc_ref[:, 0:128]
        o_ref[:, P_KWIN + g * KD_WIN + 64:P_KWIN + (g + 1) * KD_WIN] = kc_ref[:, 128:192]
    for g in range(A_KV_HEADS):
        o_ref[:, P_KA + g * KD_A + 64:P_KA + (g + 1) * KD_A] = kc_ref[:, 128:192]
    vt_ref[...] = (_dot_nt(wvt_ref[...], h) + vone_ref[...]).astype(BF16)


def _key_constants(S):
    s = np.arange(S)
    kc = np.zeros((S, 256), np.float32)
    kc[:, 0:3] = (s % TK_SEL)[:, None]
    kc[s, 64 + s // C_SEL_BLOCK] = 1.0
    kc[:, 128:131] = (s % TK_BAND)[:, None]
    return jnp.asarray(kc, dtype=BF16)


def _inproj(x, g, w, wvt, S, cos, sin, g_cq, g_ckv, mla_w):
    T = x.shape[0]
    TM = TM_PROJ
    assert S // C_SEL_BLOCK <= 64 and TK_SEL <= 256 and TK_BAND <= 256
    vone = np.zeros((VT_SLABS * V_ROWS, 1), np.float32)
    vone[HEAD_DIM::V_ROWS] = 1.0
    QW = B_HEADS * LANES
    VR = B_HEADS * V_ROWS
    voneb = np.zeros((VR, 1), np.float32)
    voneb[HEAD_DIM::V_ROWS] = 1.0
    wq, wqr, wk, wvtb = mla_w
    nblk = S // TM
    full = lambda a: pl.BlockSpec(a.shape, lambda i: (0, 0))
    rows = lambda w_: pl.BlockSpec((TM, w_), lambda i: (i, 0))
    cols = lambda r: pl.BlockSpec((r, TM), lambda i: (0, i))
    consts = [g, w, wvt, jnp.asarray(vone)]
    mla_consts = [g_cq.reshape(1, -1), g_ckv.reshape(1, -1), wq, wqr, wk, wvtb, jnp.asarray(voneb)]
    return pl.pallas_call(
        _inproj_kernel,
        grid=(T // TM,),
        in_specs=[rows(D_MODEL)] + [full(a) for a in consts]
                 + [pl.BlockSpec((TM, 256), lambda i: (i % nblk, 0)), rows(LANES), rows(LANES)]
                 + [full(a) for a in mla_consts],
        out_specs=[rows(P_WIDTH), cols(VT_SLABS * V_ROWS), cols(QW), rows(QW), cols(VR), rows(LANES), rows(LANES)],
        out_shape=[jax.ShapeDtypeStruct((T, P_WIDTH), BF16),
                   jax.ShapeDtypeStruct((VT_SLABS * V_ROWS, T), BF16),
                   jax.ShapeDtypeStruct((QW, T), BF16), jax.ShapeDtypeStruct((T, QW), BF16),
                   jax.ShapeDtypeStruct((VR, T), BF16),
                   jax.ShapeDtypeStruct((T, LANES), BF16), jax.ShapeDtypeStruct((T, LANES), BF16)],
        compiler_params=_params("parallel"),
        name="inproj",
    )(x, *consts, _key_constants(S), cos, sin, *mla_consts)


def _build_w_in(w_in):
    cuts = np.cumsum([0, 256, 128, 128, 256, 128, 32, 512, 128, 128, 128, 128, 128, 128, 24])
    seg = [w_in[:, cuts[i]:cuts[i + 1]] for i in range(14)]
    a_q, a_k, a_v, b_cq, b_ckv, b_kr, c_q, c_kc, c_vc, c_ks, c_vs, c_kw, c_vw, c_g = seg
    z = lambda n: jnp.zeros((w_in.shape[0], n), w_in.dtype)
    half = B_ROPE // 2
    kr_rot = jnp.concatenate([-b_kr[:, half:], b_kr[:, :half]], axis=1)
    cols = [c_q, a_q, b_cq, b_ckv, c_kc, c_vc,
            z(B_NOPE), b_kr, z(LANES - B_NOPE - B_ROPE),
            z(B_NOPE), kr_rot, z(LANES - B_NOPE - B_ROPE),
            c_g, z(LANES - 24)]
    for k, kd in ((c_ks, KD_SEL), (c_kw, KD_WIN), (a_k, KD_A)):
        for g in range(2):
            cols += [k[:, g * HEAD_DIM:(g + 1) * HEAD_DIM], z(kd - HEAD_DIM)]
    w = jnp.concatenate(cols, axis=1).astype(BF16)
    rows = []
    for v in (c_vs, c_vw, a_v):
        for g in range(2):
            rows += [v[:, g * HEAD_DIM:(g + 1) * HEAD_DIM].T, jnp.zeros((V_ROWS - HEAD_DIM, w_in.shape[0]), w_in.dtype)]
    return w, jnp.concatenate(rows, axis=0).astype(BF16)


def _rope_tab_kernel(pos_ref, inv_ref, cos_ref, sin_ref):
    ang = pos_ref[...] * inv_ref[...]
    cos_ref[...] = jnp.cos(ang)
    sin_ref[...] = jnp.sin(ang)


def _rope_tables(positions):
    T = positions.size
    per = LANES // B_ROPE
    inv = (ROPE_THETA ** (-np.arange(0, B_ROPE, 2, dtype=np.float32) / B_ROPE)).astype(np.float32)
    inv128 = np.tile(np.concatenate([inv, inv]), per).reshape(1, LANES)
    pos = jnp.repeat(positions.reshape(T // per, per).astype(F32), B_ROPE, axis=1)
    rows = min(TM_PROJ, T // per)
    cos_c, sin_c = pl.pallas_call(
        _rope_tab_kernel,
        grid=(T // per // rows,),
        in_specs=[pl.BlockSpec((rows, LANES), lambda i: (i, 0)),
                  pl.BlockSpec((1, LANES), lambda i: (0, 0))],
        out_specs=[pl.BlockSpec((rows, LANES), lambda i: (i, 0))] * 2,
        out_shape=[jax.ShapeDtypeStruct((T // per, LANES), F32)] * 2,
        compiler_params=_params("parallel"),
        name="rope_tables",
    )(pos, jnp.asarray(inv128))
    widen = lambda t, fill: jnp.pad(t.reshape(T, B_ROPE), ((0, 0), (B_NOPE, LANES - B_NOPE - B_ROPE)),
                                    constant_values=fill)
    return widen(cos_c, 1.0), widen(sin_c, 0.0)


def _flash_kernel(*refs, TQ, TK, G, R, NKT, slopes, scale, sel, gate_branch, n_side):
    refs = list(refs)
    lst_ref, cnt_ref = (refs.pop(0), refs.pop(0)) if sel else (None, None)
    q_ref = refs.pop(0)
    selb_ref = refs.pop(0) if sel else None
    k_refs = [refs.pop(0) for _ in range(G)]
    v_refs = [refs.pop(0) for _ in range(G)]
    gate_ref = refs.pop(0) if gate_branch is not None else None
    side_in = [refs.pop(0) for _ in range(n_side)]
    o_ref = refs.pop(0)
    side_out = [refs.pop(0) for _ in range(n_side)]
    qa_sc = refs.pop(0) if slopes is not None else None
    m_sc, acc_sc, s_sc, mx_sc = refs
    for src, dst in zip(side_in, side_out):
        dst[...] = src[...].astype(BF16)
    i = pl.program_id(1)
    step = pl.program_id(0) * pl.num_programs(1) + i
    t0 = i * TQ
    RQ = R * TQ
    lane = lax.broadcasted_iota(jnp.int32, (1, RQ), 1)
    qpos = t0 + (lane & (TQ - 1))
    hi = (t0 + TQ + TK - 1) // TK
    if gate_ref is not None:
        gate_t = _sigmoid(gate_ref[...].astype(F32)).T
    slope_vecs = []
    if slopes is not None:
        q_t = q_ref[...].astype(F32).T
        frow = lax.broadcasted_iota(jnp.int32, (HEAD_DIM, TQ), 0)
    for g in range(G):
        if slopes is not None:
            slope_vec = jnp.zeros((1, RQ), F32)
            for r in range(R):
                hh = g * R + r
                cols = slice(r * TQ, (r + 1) * TQ)
                s3 = _split_bf16(slopes[hh] * LOG2E)
                slope_vec = jnp.where(lane // TQ == r, sum(s3), slope_vec)
                qa_sc[g, 0:HEAD_DIM, cols] = (q_t[hh * HEAD_DIM:(hh + 1) * HEAD_DIM, :] * (scale * LOG2E)).astype(BF16)
                qa_sc[g, HEAD_DIM:2 * HEAD_DIM, cols] = jnp.where(
                    frow == 0, s3[0], jnp.where(frow == 1, s3[1], jnp.where(frow == 2, s3[2], 0.0))).astype(BF16)
                if sel:
                    qa_sc[g, 2 * HEAD_DIM:3 * HEAD_DIM, cols] = selb_ref[g * HEAD_DIM:(g + 1) * HEAD_DIM, :]
                    qa_sc[g, 3 * HEAD_DIM:4 * HEAD_DIM, cols] = jnp.zeros((HEAD_DIM, TQ), BF16)
            slope_vecs.append(slope_vec)
        else:
            slope_vecs.append(None)
        m_sc[g] = jnp.full((1, RQ), M_INIT, F32)
        acc_sc[g] = jnp.zeros((V_ROWS, RQ), F32)

    def scores(j, slot):
        ks = pl.multiple_of(jnp.asarray(j, jnp.int32) * TK, TK)
        for g in range(G):
            qa_t = qa_sc[g] if slopes is not None else q_ref[g * LANES:(g + 1) * LANES, :]
            st = _dot(k_refs[g][pl.ds(ks, TK), :], qa_t)
            s_sc[slot, g] = st
            mx_sc[slot, g] = jnp.max(st, axis=0, keepdims=True)

    def tile(j, slot, causal):
        ks = pl.multiple_of(jnp.asarray(j, jnp.int32) * TK, TK)
        for g in range(G):
            st = s_sc[slot, g]
            if causal:
                kpos = ks + lax.broadcasted_iota(jnp.int32, (TK, 1), 0)
                st = jnp.where(kpos <= qpos, st, MASK_BIAS)
            m = m_sc[g]
            mx = jnp.max(st, axis=0, keepdims=True) if causal else mx_sc[slot, g]
            if slopes is not None:
                c = slope_vecs[g] * ks.astype(F32)
                mn = jnp.maximum(m, mx + c)
                p = jnp.exp2(st - (mn - c))
            else:
                mn = jnp.maximum(m, mx)
                p = jnp.exp2(st - mn)
            acc_sc[g] = jnp.exp2(m - mn) * acc_sc[g] + _dot(v_refs[g][:, pl.ds(ks, TK)], p.astype(BF16))
            m_sc[g] = mn

    if sel:
        n = cnt_ref[step]
        jt = lambda t: lst_ref[step * NKT + t]
    else:
        n = hi
        jt = lambda t: t
    n_pair = (n - 1) // 2

    def pair(p, carry):
        t = 2 * p
        scores(jt(t + 1), 1)
        tile(jt(t), 0, False)
        scores(jt(t + 2), 0)
        tile(jt(t + 1), 1, False)
        return carry

    scores(jt(0), 0)
    lax.fori_loop(0, n_pair, pair, 0)

    @pl.when(2 * n_pair == n - 1)
    def _():
        tile(jt(n - 1), 0, True)

    @pl.when(2 * n_pair == n - 2)
    def _():
        scores(jt(n - 1), 1)
        tile(jt(n - 2), 0, False)
        tile(jt(n - 1), 1, True)

    pieces = []
    for g in range(G):
        acc = acc_sc[g]
        num = acc[0:HEAD_DIM, :]
        den = acc[HEAD_DIM:HEAD_DIM + 1, :]
        o_t = num * (1.0 / den)
        for r in range(R):
            piece = o_t[:, r * TQ:(r + 1) * TQ]
            if gate_ref is not None:
                c = 3 * (g * R + r) + gate_branch
                piece = piece * gate_t[c:c + 1, :]
            pieces.append(piece)
    o_ref[...] = jnp.concatenate(pieces, axis=0).T.astype(o_ref.dtype)


def _banded_kernel(*refs, TQ, NSUB, G, R, W, slopes, scale, has_sink, gate_branch):
    refs = list(refs)
    q_ref = refs.pop(0)
    k_refs = [refs.pop(0) for _ in range(G)]
    v_refs = [refs.pop(0) for _ in range(G)]
    sink_ref = refs.pop(0) if has_sink else None
    gate_ref = refs.pop(0) if gate_branch is not None else None
    o_ref = refs.pop(0)
    qa_sc, m_sc, acc_sc, s_sc = refs
    TK = TQ
    RQ = R * TQ
    n_st = W // TK + 1
    i = pl.program_id(1)
    lane = lax.broadcasted_iota(jnp.int32, (1, RQ), 1)
    frow = lax.broadcasted_iota(jnp.int32, (HEAD_DIM, TQ), 0)

    def process(sub, check):
        qt = i * NSUB + sub
        t0 = qt * TQ
        rows = slice(sub * TQ, (sub + 1) * TQ)
        qpos = t0 + (lane & (TQ - 1))
        if gate_ref is not None:
            gate_t = _sigmoid(gate_ref[rows, :].astype(F32)).T
        q_t = q_ref[rows, :].astype(F32).T
        slope_vecs = []
        for g in range(G):
            slope_vec = jnp.zeros((1, RQ), F32)
            for r in range(R):
                hh = g * R + r
                cols = slice(r * TQ, (r + 1) * TQ)
                s3 = _split_bf16(slopes[hh] * LOG2E)
                slope_vec = jnp.where(lane // TQ == r, sum(s3), slope_vec)
                qa_sc[sub, g, 0:HEAD_DIM, cols] = (
                    q_t[hh * HEAD_DIM:(hh + 1) * HEAD_DIM, :] * (scale * LOG2E)).astype(BF16)
                qa_sc[sub, g, HEAD_DIM:2 * HEAD_DIM, cols] = jnp.where(
                    frow == 0, s3[0], jnp.where(frow == 1, s3[1], jnp.where(frow == 2, s3[2], 0.0))).astype(BF16)
            slope_vecs.append(slope_vec)
            m_sc[sub, g] = jnp.full((1, RQ), M_INIT, F32)
            acc_sc[sub, g] = jnp.zeros((V_ROWS, RQ), F32)

        def scores(j, slot):
            ks = pl.multiple_of(j * TK, TK)
            for g in range(G):
                s_sc[sub, slot, g] = _dot(k_refs[g][pl.ds(ks, TK), :], qa_sc[sub, g])

        def tile(j, slot, mask, live):
            ks = pl.multiple_of(j * TK, TK)
            for g in range(G):
                st = s_sc[sub, slot, g]
                if mask is not None:
                    kpos = ks + lax.broadcasted_iota(jnp.int32, (TK, 1), 0)
                    ok = kpos <= qpos if mask == "causal" else kpos > qpos - W
                    st = jnp.where(ok, st, MASK_BIAS)
                if live is not None:
                    st = jnp.where(live, st, MASK_BIAS)
                m = m_sc[sub, g]
                c = slope_vecs[g] * ks.astype(F32)
                mn = jnp.maximum(m, jnp.max(st, axis=0, keepdims=True) + c)
                p = jnp.exp2(st - (mn - c))
                acc_sc[sub, g] = (jnp.exp2(m - mn) * acc_sc[sub, g]
                                  + _dot(v_refs[g][:, pl.ds(ks, TK)], p.astype(BF16)))
                m_sc[sub, g] = mn

        js = [qt - n_st + 1 + t for t in range(n_st)]
        jc = [jnp.maximum(j, 0) for j in js] if check else js
        scores(jc[0], 0)
        for t in range(n_st):
            if t + 1 < n_st:
                scores(jc[t + 1], (t + 1) & 1)
            mask = "causal" if t == n_st - 1 else ("window" if t == 0 else None)
            tile(jc[t], t & 1, mask, (js[t] >= 0) if check and t < n_st - 1 else None)

        pieces = []
        for g in range(G):
            acc = acc_sc[sub, g]
            num = acc[0:HEAD_DIM, :]
            den = acc[HEAD_DIM:HEAD_DIM + 1, :]
            if has_sink:
                sink_vec = jnp.zeros((1, RQ), F32)
                for r in range(R):
                    sink_vec = jnp.where(lane // TQ == r, sink_ref[g * R + r] * LOG2E, sink_vec)
                m_true = m_sc[sub, g] - slope_vecs[g] * qpos.astype(F32)
                mf = jnp.maximum(m_true, sink_vec)
                a = jnp.exp2(m_true - mf)
                num = num * a
                den = den * a + jnp.exp2(sink_vec - mf)
            o_t = num * (1.0 / den)
            for r in range(R):
                piece = o_t[:, r * TQ:(r + 1) * TQ]
                if gate_ref is not None:
                    c = 3 * (g * R + r) + gate_branch
                    piece = piece * gate_t[c:c + 1, :]
                pieces.append(piece)
        o_ref[rows, :] = jnp.concatenate(pieces, axis=0).T.astype(o_ref.dtype)

    all_tiles_exist = i * NSUB >= n_st - 1

    @pl.when(all_tiles_exist)
    def _():
        for sub in range(NSUB):
            process(sub, False)

    @pl.when(jnp.logical_not(all_tiles_exist))
    def _():
        for sub in range(NSUB):
            process(sub, True)


def _banded_attn(P, VT, B, S, *, name, NSUB, G, R, KD, q_col, k_col, vt_slab, W, slopes, scale, sinks=None,
                 gate_branch=None):
    T = B * S
    TQ = TQ_ATT
    TB = NSUB * TQ
    nq = S // TB
    RQ = R * TQ
    assert TK_BAND == TQ and W % TQ == 0 and S % TB == 0 and KD == 2 * HEAD_DIM
    QW = G * R * HEAD_DIM
    in_specs = [pl.BlockSpec((TB, QW), lambda b, i: (b * nq + i, q_col // QW))]
    args = [P]
    for g in range(G):
        in_specs.append(pl.BlockSpec((S, KD), lambda b, i, g=g: (b, k_col // KD + g)))
        args.append(P)
    for g in range(G):
        in_specs.append(pl.BlockSpec((V_ROWS, S), lambda b, i, g=g: (vt_slab + g, b)))
        args.append(VT)
    if sinks is not None:
        in_specs.append(pl.BlockSpec(memory_space=pltpu.SMEM))
        args.append(sinks.astype(F32))
    if gate_branch is not None:
        in_specs.append(pl.BlockSpec((TB, LANES), lambda b, i: (b * nq + i, P_GATE // LANES)))
        args.append(P)
    kern = functools.partial(_banded_kernel, TQ=TQ, NSUB=NSUB, G=G, R=R, W=W, slopes=slopes, scale=scale,
                             has_sink=sinks is not None, gate_branch=gate_branch)
    return pl.pallas_call(
        kern,
        grid=(B, nq),
        in_specs=in_specs,
        out_specs=pl.BlockSpec((TB, QW), lambda b, i: (b * nq + i, 0)),
        out_shape=jax.ShapeDtypeStruct((T, QW), BF16),
        scratch_shapes=[pltpu.VMEM((NSUB, G, KD, RQ), BF16), pltpu.VMEM((NSUB, G, 1, RQ), F32),
                        pltpu.VMEM((NSUB, G, V_ROWS, RQ), F32), pltpu.VMEM((NSUB, 2, G, TQ, RQ), F32)],
        compiler_params=_params("parallel", "parallel"),
        name=name,
    )(*args)


def _flash_attn(q, k, vt, B, S, *, name, TQ, TK, G, R, KD, q_col, q_width, k_col, vt_slab, slopes=None, scale=1.0,
                selb=None, tiles=None, gate=None, gate_branch=None, side=()):
    T = B * S
    nq = S // TQ
    RQ = R * TQ
    assert TK % TQ == 0 and S % TK == 0
    OW = G * R * HEAD_DIM
    if slopes is not None:
        in_specs = [pl.BlockSpec((TQ, q_width), lambda b, i, *_: (b * nq + i, q_col // q_width))]
    else:
        in_specs = [pl.BlockSpec((G * KD, TQ), lambda b, i, *_: (0, b * nq + i))]
    args = [q]
    if selb is not None:
        in_specs.append(pl.BlockSpec((LANES, TQ), lambda b, i, *_: (0, b * nq + i)))
        args.append(selb)
    for g in range(G):
        in_specs.append(pl.BlockSpec((S, KD), lambda b, i, *_, g=g: (b, k_col // KD + g)))
        args.append(k)
    for g in range(G):
        in_specs.append(pl.BlockSpec((V_ROWS, S), lambda b, i, *_, g=g: (vt_slab + g, b)))
        args.append(vt)
    if gate_branch is not None:
        in_specs.append(pl.BlockSpec((TQ, LANES), lambda b, i, *_: (b * nq + i, P_GATE // LANES)))
        args.append(gate)
    out_specs = [pl.BlockSpec((TQ, OW), lambda b, i, *_: (b * nq + i, 0))]
    out_shape = [jax.ShapeDtypeStruct((T, OW), BF16)]
    for w in side:
        rb = w.shape[0] // (B * nq)
        assert w.shape[0] % (B * nq) == 0 and rb % 16 == 0 and w.shape[1] % LANES == 0
        spec = pl.BlockSpec((rb, w.shape[1]), lambda b, i, *_: (b * nq + i, 0))
        in_specs.append(spec)
        args.append(w)
        out_specs.append(spec)
        out_shape.append(jax.ShapeDtypeStruct(w.shape, BF16))
    scratch = []
    if slopes is not None:
        scratch.append(pltpu.VMEM((G, KD, RQ), BF16))
    scratch += [pltpu.VMEM((G, 1, RQ), F32), pltpu.VMEM((G, V_ROWS, RQ), F32), pltpu.VMEM((2, G, TK, RQ), F32),
                pltpu.VMEM((2, G, 1, RQ), F32)]
    kern = functools.partial(_flash_kernel, TQ=TQ, TK=TK, G=G, R=R, NKT=S // TK, slopes=slopes, scale=scale,
                             sel=selb is not None, gate_branch=gate_branch, n_side=len(side))
    prefetch = list(tiles) if selb is not None else []
    grid_spec = pltpu.PrefetchScalarGridSpec(
        num_scalar_prefetch=len(prefetch),
        grid=(B, nq),
        in_specs=in_specs,
        out_specs=out_specs,
        scratch_shapes=scratch)
    outs = pl.pallas_call(
        kern,
        grid_spec=grid_spec,
        out_shape=out_shape,
        compiler_params=_params("parallel", "parallel"),
        name=name,
    )(*prefetch, *args)
    return list(outs)


def _mla_prep(cq, ckv, kr, krr, cos, sin, gq_ref, gkv_ref, wq_ref, wqr_ref, wk_ref, wvt_ref, vone_ref,
              q_out, k_out, vt_out):
    scale = (B_NOPE + B_ROPE) ** -0.5 * LOG2E
    nq = _rms(cq.astype(F32), gq_ref[...]).astype(BF16)
    nkv = _rms(ckv.astype(F32), gkv_ref[...]).astype(BF16)
    q1 = _dot_nt(wq_ref[...], nq)
    q2 = _dot_nt(wqr_ref[...], nq)
    kk = _dot(nkv, wk_ref[...])
    krope = kr.astype(F32) * cos + krr.astype(F32) * sin
    cos_t = cos.T
    sin_t = sin.T
    for h in range(B_HEADS):
        sl = slice(h * LANES, (h + 1) * LANES)
        q_out[sl, :] = ((q1[sl, :] * cos_t + q2[sl, :] * sin_t) * scale).astype(BF16)
        k_out[:, sl] = (kk[:, sl] + krope).astype(BF16)
    vt_out[...] = (_dot_nt(wvt_ref[...], nkv) + vone_ref[...]).astype(BF16)


def _build_mla_weights(w_uq, w_ukv):
    R1, R2 = w_uq.shape[0], w_ukv.shape[0]
    half = B_ROPE // 2
    hq = B_NOPE + B_ROPE
    wq, wqr, wk, wvt = [], [], [], []
    for h in range(B_HEADS):
        nope = w_uq[:, h * hq:h * hq + B_NOPE]
        rope = w_uq[:, h * hq + B_NOPE:(h + 1) * hq]
        rot = jnp.concatenate([-rope[:, half:], rope[:, :half]], axis=1)
        pad = jnp.zeros((R1, LANES - hq), w_uq.dtype)
        wq += [nope, rope, pad]
        wqr += [jnp.zeros((R1, B_NOPE), w_uq.dtype), rot, pad]
        wk += [w_ukv[:, h * 128:h * 128 + B_NOPE], jnp.zeros((R2, LANES - B_NOPE), w_ukv.dtype)]
        wvt += [w_ukv[:, h * 128 + B_NOPE:(h + 1) * 128].T, jnp.zeros((V_ROWS - HEAD_DIM, R2), w_ukv.dtype)]
    cat = lambda xs: jnp.concatenate(xs, axis=1).astype(BF16)
    return cat(wq).T, cat(wqr).T, cat(wk), jnp.concatenate(wvt, axis=0).astype(BF16)


def _gelu_tanh(x):
    return 0.5 * x * (1.0 + jnp.tanh(np.float32(np.sqrt(2.0 / np.pi)) * (x + 0.044715 * (x * x * x))))


def _compress_kernel(xk_ref, xv_ref, pek_ref, pev_ref, w1k_ref, w1v_ref, w2k_ref, w2v_ref, kc_ref, vct_ref, *, NC):
    for x_ref, pe_ref, w1_ref, w2_ref, is_v in ((xk_ref, pek_ref, w1k_ref, w2k_ref, False),
                                                 (xv_ref, pev_ref, w1v_ref, w2v_ref, True)):
        x = x_ref[...].astype(F32)
        a = _dot((x + pe_ref[0:1, :]).astype(BF16), w1_ref[0])
        b = _dot((x + pe_ref[1:2, :]).astype(BF16), w1_ref[1])
        hid = a + pltpu.roll(b, NC - 1, 0)
        o = _dot(_gelu_tanh(hid).astype(BF16), w2_ref[...])
        n = lax.broadcasted_iota(jnp.int32, o.shape, 0)
        o = jnp.where(n < NC - 1, o, 0.0)
        if is_v:
            vct_ref[...] = o.T.astype(vct_ref.dtype)
        else:
            kc_ref[...] = o.astype(kc_ref.dtype)


def _build_compress_weights(w1, w2, pe):
    half = C_CMP_BLOCK // 2
    G = C_KV_HEADS
    eye = jnp.eye(G, dtype=w1.dtype)
    w1s = [jnp.einsum('lde,hg->lhdge', w1[c * half:(c + 1) * half], eye).reshape(half * G * HEAD_DIM, G * C_CMP_HIDDEN)
           for c in range(2)]
    w1f = jnp.stack(w1s).astype(BF16)
    w2f = jnp.einsum('ed,hg->hegd', w2, eye).reshape(G * C_CMP_HIDDEN, G * HEAD_DIM).astype(BF16)
    pes = [jnp.broadcast_to(pe[c * half:(c + 1) * half, None, :], (half, G, HEAD_DIM)).reshape(1, -1) for c in range(2)]
    pef = jnp.concatenate(pes, axis=0).astype(F32)
    return w1f, w2f, pef


def _compress(k_cmp, v_cmp, B, S, k_weights, v_weights):
    NC = S // C_CMP_STRIDE
    CW = C_CMP_STRIDE * C_KV_HEADS * HEAD_DIM
    xk = k_cmp.reshape(B * NC, CW)
    xv = v_cmp.reshape(B * NC, CW)
    w1kf, w2kf, pekf = k_weights
    w1vf, w2vf, pevf = v_weights
    xspec = pl.BlockSpec((NC, CW), lambda b: (b, 0))
    full2 = lambda a: pl.BlockSpec(a.shape, lambda b: (0, 0))
    full3 = lambda a: pl.BlockSpec(a.shape, lambda b: (0, 0, 0))
    return pl.pallas_call(
        functools.partial(_compress_kernel, NC=NC),
        grid=(B,),
        in_specs=[xspec, xspec, full2(pekf), full2(pevf), full3(w1kf), full3(w1vf), full2(w2kf), full2(w2vf)],
        out_specs=[pl.BlockSpec((NC, LANES), lambda b: (b, 0)), pl.BlockSpec((LANES, NC), lambda b: (b, 0))],
        out_shape=[jax.ShapeDtypeStruct((B * NC, LANES), BF16), jax.ShapeDtypeStruct((B * LANES, NC), BF16)],
        compiler_params=_params("parallel"),
        name="nsa_compress",
    )(xk, xv, pekf, pevf, w1kf, w1vf, w2kf, w2vf)


def _cmp_sel_kernel(q_ref, kc_ref, vct_ref, cfeat_ref, ovt_ref, gate_ref, ocmp_ref, sel_ref, cnt_ref,
                    *, TQ, NC, NSEL, NTOP, slopes):
    i = pl.program_id(1)
    t0 = i * TQ
    R = C_HEADS // C_KV_HEADS
    scale = HEAD_DIM ** -0.5
    gate_t = _sigmoid(gate_ref[...].astype(F32)).T
    last = C_CMP_BLOCK - 1
    tlane = t0 + lax.broadcasted_iota(jnp.int32, (NC, TQ), 1)
    nrow = lax.broadcasted_iota(jnp.int32, (NC, TQ), 0)
    cvalid = (nrow * C_CMP_STRIDE + last <= tlane) & (nrow < NC - 1)
    hasc = ((t0 + lax.broadcasted_iota(jnp.int32, (1, TQ), 1)) >= last).astype(F32)
    jrow = lax.broadcasted_iota(jnp.int32, (NSEL, TQ), 0)
    tl = t0 + lax.broadcasted_iota(jnp.int32, (NSEL, TQ), 1)
    avail = jrow * C_SEL_BLOCK <= tl
    cur = tl // C_SEL_BLOCK
    forced = (jrow == 0) | (jrow == cur) | (jrow == cur - 1)
    q_t = q_ref[...].astype(F32).T
    frow = lax.broadcasted_iota(jnp.int32, (HEAD_DIM, TQ), 0)
    sub8 = lax.broadcasted_iota(jnp.int32, (8, TQ), 0)
    pieces, sel_parts = [], []
    picked = jnp.zeros((NSEL, TQ), F32)
    for g in range(C_KV_HEADS):
        kca = jnp.concatenate([kc_ref[:, g * HEAD_DIM:(g + 1) * HEAD_DIM], cfeat_ref[...]], axis=1)
        vo = jnp.concatenate([vct_ref[g * HEAD_DIM:(g + 1) * HEAD_DIM, :], ovt_ref[...]], axis=0)
        imp = jnp.zeros((HEAD_DIM, TQ), F32)
        for r in range(R):
            hh = g * R + r
            s3 = _split_bf16(slopes[hh] * LOG2E)
            feat = jnp.zeros((HEAD_DIM, TQ), F32)
            for k in range(3):
                feat = jnp.where(frow == k, s3[k] * 256.0, jnp.where(frow == 3 + k, s3[k], feat))
            qa = jnp.concatenate([q_t[hh * HEAD_DIM:(hh + 1) * HEAD_DIM, :] * (scale * LOG2E), feat],
                                 axis=0).astype(BF16)
            st = jnp.where(cvalid, _dot(kca, qa), NEG_INF)
            et = jnp.exp2(st - jnp.max(st, axis=0, keepdims=True))
            inv = hasc / jnp.sum(et, axis=0, keepdims=True)
            res = _dot(vo, et.astype(BF16))
            pieces.append(res[0:HEAD_DIM, :] * (inv * gate_t[3 * hh:3 * hh + 1, :]))
            imp = imp + res[HEAD_DIM:2 * HEAD_DIM, :] * inv
        v = jnp.where(avail, imp[0:NSEL, :], NEG_INF)
        v = jnp.where(forced, -NEG_INF, v)
        ranks = [jnp.zeros((8, TQ), F32) for _ in range(NSEL // 8)]
        for ii in range(NSEL):
            ri = v[ii:ii + 1, :]
            for k in range(NSEL // 8):
                vk = v[8 * k:8 * k + 8, :]
                if 8 * k > ii:
                    inc = jnp.where(ri >= vk, 1.0, 0.0)
                elif 8 * k + 7 < ii:
                    inc = jnp.where(ri > vk, 1.0, 0.0)
                else:
                    inc = jnp.where(sub8 + 8 * k > ii, jnp.where(ri >= vk, 1.0, 0.0), jnp.where(ri > vk, 1.0, 0.0))
                ranks[k] = ranks[k] + inc
        rank = jnp.concatenate(ranks, axis=0)
        sel_parts.append(jnp.where(rank < NTOP, 0.0, MASK_BIAS))
        if NSEL < HEAD_DIM:
            sel_parts.append(jnp.zeros((HEAD_DIM - NSEL, TQ), F32))
        picked = picked + jnp.where(rank < NTOP, 1.0, 0.0)
    sel_ref[...] = jnp.concatenate(sel_parts, axis=0).astype(sel_ref.dtype)
    cnt_ref[...] = _dot_nt(jnp.ones((8, TQ), BF16), picked.astype(BF16))
    ocmp_ref[...] = jnp.concatenate(pieces, axis=0).T.astype(ocmp_ref.dtype)


def _overlap_t(S):
    n_cmp = (S - C_CMP_BLOCK) // C_CMP_STRIDE + 1
    n_sel = S // C_SEL_BLOCK
    NC = S // C_CMP_STRIDE
    cstart = np.arange(n_cmp) * C_CMP_STRIDE
    cend = cstart + C_CMP_BLOCK - 1
    sstart = np.arange(n_sel) * C_SEL_BLOCK
    send = sstart + C_SEL_BLOCK - 1
    ov = np.clip(np.minimum(cend[:, None], send[None, :]) - np.maximum(cstart[:, None], sstart[None, :]) + 1,
                 0, None).astype(np.float32) / C_CMP_STRIDE
    ovt = np.zeros((HEAD_DIM, NC), np.float32)
    ovt[:n_sel, :n_cmp] = ov.T
    cend_all = np.arange(NC) * C_CMP_STRIDE + C_CMP_BLOCK - 1
    cfeat = np.zeros((NC, HEAD_DIM), np.float32)
    cfeat[:, 0:3] = (cend_all // 256)[:, None]
    cfeat[:, 3:6] = (cend_all % 256)[:, None]
    return jnp.asarray(ovt, dtype=BF16), jnp.asarray(cfeat, dtype=BF16)


def _cmp_sel(P, kc, vct, B, S, slopes):
    T = B * S
    TQ = TQ_ATT
    nq = S // TQ
    NC = S // C_CMP_STRIDE
    NSEL = S // C_SEL_BLOCK
    assert NSEL <= HEAD_DIM and NSEL % 8 == 0 and S <= 256 * 256
    QW = C_HEADS * HEAD_DIM
    ovt, cfeat = _overlap_t(S)
    kern = functools.partial(_cmp_sel_kernel, TQ=TQ, NC=NC, NSEL=NSEL, NTOP=min(C_SEL_TOP, NSEL), slopes=slopes)
    return pl.pallas_call(
        kern,
        grid=(B, nq),
        in_specs=[pl.BlockSpec((TQ, QW), lambda b, i: (b * nq + i, P_CQ // QW)),
                  pl.BlockSpec((NC, LANES), lambda b, i: (b, 0)),
                  pl.BlockSpec((LANES, NC), lambda b, i: (b, 0)),
                  pl.BlockSpec((NC, HEAD_DIM), lambda b, i: (0, 0)),
                  pl.BlockSpec((HEAD_DIM, NC), lambda b, i: (0, 0)),
                  pl.BlockSpec((TQ, LANES), lambda b, i: (b * nq + i, P_GATE // LANES))],
        out_specs=[pl.BlockSpec((TQ, QW), lambda b, i: (b * nq + i, 0)),
                   pl.BlockSpec((LANES, TQ), lambda b, i: (0, b * nq + i)),
                   pl.BlockSpec((8, NSEL), lambda b, i: (b * nq + i, 0))],
        out_shape=[jax.ShapeDtypeStruct((T, QW), BF16), jax.ShapeDtypeStruct((LANES, T), BF16),
                   jax.ShapeDtypeStruct((B * nq * 8, NSEL), F32)],
        compiler_params=_params("parallel", "parallel"),
        name="nsa_cmp_select",
    )(P, kc, vct, cfeat, ovt, P)


def _selected_tile_lists(cnt, B, S):
    nq = S // TQ_ATT
    nkt = S // TK_SEL
    per = TK_SEL // C_SEL_BLOCK
    blk = cnt.reshape(B * nq, 8, -1)[:, 0, :] > 0.5
    need = jnp.any(blk.reshape(B * nq, nkt, per), axis=-1)
    own = ((jnp.arange(B * nq, dtype=jnp.int32) % nq) * TQ_ATT + TQ_ATT - 1) // TK_SEL
    jj = jnp.arange(nkt, dtype=jnp.int32)[None, :]
    need = (need | (jj == 0) | (jj == own[:, None])) & (jj <= own[:, None])
    order = jnp.argsort(jnp.where(need, jj, nkt + jj), axis=-1).astype(jnp.int32)
    return order.reshape(-1), jnp.sum(need, axis=-1).astype(jnp.int32)


def _outproj_kernel(x_ref, oa_ref, ob_ref, oc1_ref, oc2_ref, oc3_ref, ga_ref, gb_ref, gc_ref, wo_ref, gp_ref, o_ref):
    oa = _rms(oa_ref[...].astype(F32), ga_ref[...]).astype(BF16)
    ob = _rms(ob_ref[...].astype(F32), gb_ref[...]).astype(BF16)
    oc = oc1_ref[...].astype(F32) + oc2_ref[...].astype(F32) + oc3_ref[...].astype(F32)
    oc = _rms(oc, gc_ref[...]).astype(BF16)
    m = _dot(oa, wo_ref[0:256, :]) + _dot(ob, wo_ref[256:512, :]) + _dot(oc, wo_ref[512:1024, :])
    o_ref[...] = x_ref[...] + _rms(m, gp_ref[...])


def _outproj(x, o_a, o_b, o_cmp, o_slc, o_win, g_oa, g_ob, g_oc, w_o, g_post):
    T = x.shape[0]
    TM = TM_OUT
    row = lambda w: pl.BlockSpec((TM, w), lambda i: (i, 0))
    full = lambda r, c: pl.BlockSpec((r, c), lambda i: (0, 0))
    return pl.pallas_call(
        _outproj_kernel,
        grid=(T // TM,),
        in_specs=[row(D_MODEL), row(256), row(256), row(512), row(512), row(512),
                  full(1, 256), full(1, 256), full(1, 512), full(D_MODEL, D_MODEL), full(1, D_MODEL)],
        out_specs=row(D_MODEL),
        out_shape=jax.ShapeDtypeStruct((T, D_MODEL), F32),
        compiler_params=_params("parallel"),
        name="outproj",
    )(x, o_a, o_b, o_cmp, o_slc, o_win, g_oa.reshape(1, -1), g_ob.reshape(1, -1), g_oc.reshape(1, -1),
      w_o.astype(BF16), g_post.reshape(1, -1))


def _swiglu_step(h, wg_ref, wu_ref, wd_ref, acc_ref):
    a = _dot(h, wg_ref[...])
    b = _dot(h, wu_ref[...])
    z = (a * _sigmoid(a) * b).astype(BF16)
    acc_ref[...] += _dot(z, wd_ref[...])


def _ffn_kernel(x_ref, gpre_ref, wg_ref, wu_ref, wd_ref, gpost_ref, o_ref, h_sc, acc_sc, *, nf):
    f = pl.program_id(1)

    @pl.when(f == 0)
    def _():
        h_sc[...] = _rms(x_ref[...], gpre_ref[...]).astype(BF16)
        acc_sc[...] = jnp.zeros_like(acc_sc)

    _swiglu_step(h_sc[...], wg_ref, wu_ref, wd_ref, acc_sc)

    @pl.when(f == nf - 1)
    def _():
        o_ref[...] = x_ref[...] + _rms(acc_sc[...], gpost_ref[...])


def _ffn(x, g_pre, wg, wu, wd, g_post):
    T = x.shape[0]
    TM, TF = TM_FFN, TF_FFN
    FF = wg.shape[1]
    nf = FF // TF
    return pl.pallas_call(
        functools.partial(_ffn_kernel, nf=nf),
        grid=(T // TM, nf),
        in_specs=[pl.BlockSpec((TM, D_MODEL), lambda i, f: (i, 0)),
                  pl.BlockSpec((1, D_MODEL), lambda i, f: (0, 0)),
                  pl.BlockSpec((D_MODEL, TF), lambda i, f: (0, f)),
                  pl.BlockSpec((D_MODEL, TF), lambda i, f: (0, f)),
                  pl.BlockSpec((TF, D_MODEL), lambda i, f: (f, 0)),
                  pl.BlockSpec((1, D_MODEL), lambda i, f: (0, 0))],
        out_specs=pl.BlockSpec((TM, D_MODEL), lambda i, f: (i, 0)),
        out_shape=jax.ShapeDtypeStruct((T, D_MODEL), F32),
        scratch_shapes=[pltpu.VMEM((TM, D_MODEL), BF16), pltpu.VMEM((TM, D_MODEL), F32)],
        compiler_params=_params("parallel", "arbitrary"),
        name="dense_ffn",
    )(x, g_pre.reshape(1, -1), wg.astype(BF16), wu.astype(BF16), wd.astype(BF16), g_post.reshape(1, -1))


def _expert_kernel(te_ref, nt_ref, x_ref, wg_ref, wu_ref, wd_ref, o_ref, acc_sc, *, nf):
    j = pl.program_id(0)
    f = pl.program_id(1)
    live = j < nt_ref[0]

    @pl.when(f == 0)
    def _():
        acc_sc[...] = jnp.zeros_like(acc_sc)

    @pl.when(live)
    def _():
        _swiglu_step(x_ref[...], wg_ref.at[0], wu_ref.at[0], wd_ref.at[0], acc_sc)

    @pl.when(f == nf - 1)
    def _():
        o_ref[...] = acc_sc[...].astype(o_ref.dtype)


def _experts(xs, tile_expert, n_tiles, wg, wu, wd):
    NT = xs.shape[0] // TM_MOE
    TM, TF = TM_MOE, TF_MOE
    assert wg.shape[2] % TF == 0
    FF = wg.shape[2]
    nf = FF // TF
    fidx = lambda j, f, nt: jnp.where(j < nt[0], f, nf - 1)
    grid_spec = pltpu.PrefetchScalarGridSpec(
        num_scalar_prefetch=2,
        grid=(NT, nf),
        in_specs=[pl.BlockSpec((TM, D_MODEL), lambda j, f, te, nt: (j, 0)),
                  pl.BlockSpec((1, D_MODEL, TF), lambda j, f, te, nt: (te[j], 0, fidx(j, f, nt))),
                  pl.BlockSpec((1, D_MODEL, TF), lambda j, f, te, nt: (te[j], 0, fidx(j, f, nt))),
                  pl.BlockSpec((1, TF, D_MODEL), lambda j, f, te, nt: (te[j], fidx(j, f, nt), 0))],
        out_specs=pl.BlockSpec((TM, D_MODEL), lambda j, f, te, nt: (j, 0)),
        scratch_shapes=[pltpu.VMEM((TM, D_MODEL), F32)])
    return pl.pallas_call(
        functools.partial(_expert_kernel, nf=nf),
        grid_spec=grid_spec,
        out_shape=jax.ShapeDtypeStruct(xs.shape, BF16),
        compiler_params=_params("parallel", "arbitrary"),
        name="moe_experts",
    )(tile_expert, n_tiles, xs, wg, wu, wd)


def _router_kernel(x_ref, g_ref, wr_ref, h_out, meta_out, metat_out, start_out, cnt_out, run_sc, *, TS):
    s = pl.program_id(0)

    @pl.when(s == 0)
    def _():
        run_sc[...] = jnp.zeros_like(run_sc)

    h = _rms(x_ref[...], g_ref[...])
    hb = h.astype(BF16)
    h_out[...] = hb
    hl = (h - hb.astype(F32)).astype(BF16)
    w = wr_ref[...]
    wh = w.astype(BF16)
    wl = (w - wh.astype(F32)).astype(BF16)
    logits = _dot(hb, wh) + _dot(hl, wh) + _dot(hb, wl)
    lane = lax.broadcasted_iota(jnp.int32, (TS, LANES), 1).astype(F32)
    logits = jnp.where(lane < N_EXPERTS, logits, NEG_INF)
    m1 = jnp.max(logits, axis=-1, keepdims=True)
    i1 = jnp.min(jnp.where(logits == m1, lane, float(LANES)), axis=-1, keepdims=True)
    rest = jnp.where(lane == i1, NEG_INF, logits)
    m2 = jnp.max(rest, axis=-1, keepdims=True)
    i2 = jnp.min(jnp.where(rest == m2, lane, float(LANES)), axis=-1, keepdims=True)
    ex = jnp.exp(m2 - m1)
    w1 = 1.0 / (1.0 + ex)
    w2 = ex / (1.0 + ex)
    oh1 = lane == i1
    oh2 = lane == i2
    oh = jnp.where(oh1, 1.0, 0.0) + jnp.where(oh2, 1.0, 0.0)
    tri = jnp.where(lax.broadcasted_iota(jnp.int32, (TS, TS), 0) > lax.broadcasted_iota(jnp.int32, (TS, TS), 1),
                    1.0, 0.0).astype(BF16)
    run = run_sc[...]
    before = _dot(tri, oh.astype(BF16)) + run
    r1 = jnp.sum(jnp.where(oh1, before, 0.0), axis=-1, keepdims=True)
    r2 = jnp.sum(jnp.where(oh2, before, 0.0), axis=-1, keepdims=True)
    start_out[0] = run
    run = run + jnp.sum(oh, axis=0, keepdims=True)
    run_sc[...] = run
    cnt_out[...] = run
    meta = jnp.where(lane == 0, i1, jnp.where(lane == 1, i2, jnp.where(lane == 2, r1, jnp.where(
        lane == 3, r2, jnp.where(lane == 4, w1, jnp.where(lane == 5, w2, 0.0))))))
    meta_out[...] = meta
    metat_out[...] = meta.T[0:8, :]


def _router(x, g, w_router):
    T = x.shape[0]
    TS = TS_MOE
    nt = T // TS
    wr = jnp.zeros((D_MODEL, LANES), F32).at[:, :N_EXPERTS].set(w_router.astype(F32))
    return pl.pallas_call(
        functools.partial(_router_kernel, TS=TS),
        grid=(nt,),
        in_specs=[pl.BlockSpec((TS, D_MODEL), lambda s: (s, 0)),
                  pl.BlockSpec((1, D_MODEL), lambda s: (0, 0)),
                  pl.BlockSpec((D_MODEL, LANES), lambda s: (0, 0))],
        out_specs=[pl.BlockSpec((TS, D_MODEL), lambda s: (s, 0)),
                   pl.BlockSpec((TS, LANES), lambda s: (s, 0)),
                   pl.BlockSpec((8, TS), lambda s: (0, s)),
                   pl.BlockSpec((1, 1, LANES), lambda s: (s, 0, 0)),
                   pl.BlockSpec((1, LANES), lambda s: (0, 0))],
        out_shape=[jax.ShapeDtypeStruct((T, D_MODEL), BF16),
                   jax.ShapeDtypeStruct((T, LANES), F32),
                   jax.ShapeDtypeStruct((8, T), F32),
                   jax.ShapeDtypeStruct((nt, 1, LANES), F32),
                   jax.ShapeDtypeStruct((1, LANES), F32)],
        scratch_shapes=[pltpu.VMEM((1, LANES), F32)],
        compiler_params=_params("arbitrary"),
        name="moe_router",
    )(x, g.reshape(1, -1), wr)


def _gather_kernel(te_ref, slo_ref, shi_ref, base_ref, metat_ref, h_ref, xs_out, acc_sc, *, TM, TS, CH):
    j = pl.program_id(0)
    e = te_ref[j]
    ef = e.astype(F32)
    nsub = TM // CH
    nt = h_ref.shape[0] // TS

    def picked_rows(s, want):
        off = pl.multiple_of(s * TS, TS)
        e1 = metat_ref[0:1, pl.ds(off, TS)]
        e2 = metat_ref[1:2, pl.ds(off, TS)]
        r1 = metat_ref[2:3, pl.ds(off, TS)]
        r2 = metat_ref[3:4, pl.ds(off, TS)]
        key = jnp.where(e1 == ef, r1, jnp.where(e2 == ef, r2, -1.0))
        sel = jnp.where(key == want, 1.0, 0.0).astype(BF16)
        return _dot(sel, h_ref[pl.ds(off, TS), :])

    wants = []
    for sub in range(nsub):
        q = j * nsub + sub
        first = j * TM + sub * CH - base_ref[e]
        want = (first + lax.broadcasted_iota(jnp.int32, (CH, 1), 0)).astype(F32)
        lo = jnp.minimum(slo_ref[q], nt - 1)
        second = jnp.where(lo + 1 < nt, want, -2.0)
        acc_sc[sub] = picked_rows(lo, want) + picked_rows(jnp.minimum(lo + 1, nt - 1), second)
        wants.append(want)
    for sub in range(nsub):
        q = j * nsub + sub

        def body(s, _, sub=sub):
            acc_sc[sub] += picked_rows(s, wants[sub])
            return 0

        lax.fori_loop(slo_ref[q] + 2, shi_ref[q] + 1, body, 0)
    for sub in range(nsub):
        xs_out[sub * CH:(sub + 1) * CH, :] = acc_sc[sub].astype(xs_out.dtype)


def _gather(hb, metat, tile_expert, s_lo, s_hi, base, NT):
    TM, TS = TM_MOE, TS_MOE
    assert hb.shape[0] % TS == 0
    grid_spec = pltpu.PrefetchScalarGridSpec(
        num_scalar_prefetch=4,
        grid=(NT,),
        in_specs=[pl.BlockSpec(memory_space=pltpu.VMEM), pl.BlockSpec(memory_space=pltpu.VMEM)],
        out_specs=pl.BlockSpec((TM, D_MODEL), lambda j, *_: (j, 0)),
        scratch_shapes=[pltpu.VMEM((TM // CH_MOE, CH_MOE, D_MODEL), F32)])
    return pl.pallas_call(
        functools.partial(_gather_kernel, TM=TM, TS=TS, CH=CH_MOE),
        grid_spec=grid_spec,
        out_shape=jax.ShapeDtypeStruct((NT * TM, D_MODEL), BF16),
        compiler_params=_params("arbitrary"),
        name="moe_gather",
    )(tile_expert, s_lo, s_hi, base, metat, hb)


def _combine_kernel(blk_ref, row0_ref, base_ref, meta_ref, x_ref, gpost_ref, ys_hbm, o_ref, buf, sem, selw_sc,
                    *, TS, CH, U, nt):
    s = pl.program_id(0)
    slot = s % 2

    def chunk_copies(step, slot_):
        out = []
        for u in range(U):
            r0 = pl.multiple_of(blk_ref[step * U + u] * CH, CH)
            out.append(pltpu.make_async_copy(ys_hbm.at[pl.ds(r0, CH), :], buf.at[slot_, pl.ds(u * CH, CH), :],
                                             sem.at[slot_]))
        return out

    @pl.when(s == 0)
    def _():
        for c in chunk_copies(0, 0):
            c.start()

    @pl.when(s + 1 < nt)
    def _():
        for c in chunk_copies(s + 1, 1 - slot):
            c.start()

    meta = meta_ref[...]
    e1, e2 = meta[:, 0:1], meta[:, 1:2]
    w1, w2 = meta[:, 4:5], meta[:, 5:6]
    b1 = jnp.zeros((TS, 1), F32)
    b2 = jnp.zeros((TS, 1), F32)
    for e in range(N_EXPERTS):
        be = base_ref[e].astype(F32)
        b1 = jnp.where(e1 == float(e), be, b1)
        b2 = jnp.where(e2 == float(e), be, b2)
    pos1 = b1 + meta[:, 2:3]
    pos2 = b2 + meta[:, 3:4]

    lane = lax.broadcasted_iota(jnp.int32, (1, CH), 1)
    for u in range(U):
        rows = (row0_ref[s * U + u] + lane).astype(F32)
        selw = jnp.where(pos1 == rows, w1, 0.0) + jnp.where(pos2 == rows, w2, 0.0)
        selw_sc[:, u * CH:(u + 1) * CH] = selw.astype(BF16)
    for c in chunk_copies(s, slot):
        c.wait()
    y = _dot(selw_sc[...], buf[slot])
    o_ref[...] = x_ref[...] + _rms(y, gpost_ref[...])


def _combine(x, meta, ys, chunk_blk, chunk_row0, base, g_post):
    T = x.shape[0]
    TS, CH, U = TS_MOE, CH_MOE, U_MOE
    nt = T // TS
    grid_spec = pltpu.PrefetchScalarGridSpec(
        num_scalar_prefetch=3,
        grid=(nt,),
        in_specs=[pl.BlockSpec((TS, LANES), lambda s, *_: (s, 0)),
                  pl.BlockSpec((TS, D_MODEL), lambda s, *_: (s, 0)),
                  pl.BlockSpec((1, D_MODEL), lambda s, *_: (0, 0)),
                  pl.BlockSpec(memory_space=pl.ANY)],
        out_specs=pl.BlockSpec((TS, D_MODEL), lambda s, *_: (s, 0)),
        scratch_shapes=[pltpu.VMEM((2, U * CH, D_MODEL), BF16), pltpu.SemaphoreType.DMA((2,)),
                        pltpu.VMEM((TS, U * CH), BF16)])
    return pl.pallas_call(
        functools.partial(_combine_kernel, TS=TS, CH=CH, U=U, nt=nt),
        grid_spec=grid_spec,
        out_shape=jax.ShapeDtypeStruct((T, D_MODEL), F32),
        compiler_params=_params("arbitrary"),
        name="moe_combine",
    )(chunk_blk, chunk_row0, base, meta, x, g_post.reshape(1, -1), ys)


def _moe(x, g_pre, w_router, w_gate, w_up, w_down, g_post):
    T = x.shape[0]
    TS, TM, CH, U = TS_MOE, TM_MOE, CH_MOE, U_MOE
    nt = T // TS
    NT = (2 * T) // TM + N_EXPERTS
    hb, meta, metat, start, cnt = _router(x, g_pre, w_router)

    i32 = jnp.int32
    counts = cnt[0, :N_EXPERTS].astype(i32)
    start = start[:, 0, :N_EXPERTS].astype(i32)
    start_ext = jnp.concatenate([start, counts[None, :]], axis=0)
    tiles_e = (counts + TM - 1) // TM
    tiles_cum = jnp.cumsum(tiles_e)
    base = (tiles_cum - tiles_e) * TM
    n_tiles = tiles_cum[-1:]
    jj = jnp.arange(NT, dtype=i32)
    tile_expert = jnp.minimum(jnp.sum(jj[:, None] >= tiles_cum[None, :], axis=1), N_EXPERTS - 1).astype(i32)
    live = jj < n_tiles[0]
    qq = jnp.arange(NT * (TM // CH), dtype=i32)
    q_exp = tile_expert[qq // (TM // CH)]
    a = qq * CH - base[q_exp]
    st_e = start_ext[:, q_exp]
    s_lo = jnp.sum(st_e[1:] <= a[None, :], axis=0).astype(i32)
    s_hi = (jnp.sum(st_e[:-1] < (a + CH)[None, :], axis=0) - 1).astype(i32)
    q_live = live[qq // (TM // CH)]
    s_lo = jnp.where(q_live, s_lo, 1)
    s_hi = jnp.where(q_live, s_hi, 0)

    xs = _gather(hb, metat, tile_expert, s_lo, s_hi, base.astype(i32), NT)
    ys = _experts(xs, tile_expert, n_tiles.astype(i32), w_gate.astype(BF16), w_up.astype(BF16), w_down.astype(BF16))

    lo = base[None, :] + start_ext[:-1]
    hi = base[None, :] + start_ext[1:]
    n_e = jnp.where(hi > lo, (hi - 1) // CH - lo // CH + 1, 0)
    cum = jnp.cumsum(n_e, axis=1)
    uu = jnp.arange(U, dtype=i32)
    e_u = jnp.minimum(jnp.sum(uu[None, :, None] >= cum[:, None, :], axis=2), N_EXPERTS - 1)
    first = jnp.take_along_axis(lo // CH, e_u, axis=1)
    skipped = jnp.take_along_axis(cum - n_e, e_u, axis=1)
    chunk_blk = jnp.clip(first + (uu[None, :] - skipped), 0, (NT * TM) // CH - 1).astype(i32)
    used = uu[None, :] < cum[:, -1:]
    chunk_blk = jnp.where(used, chunk_blk, chunk_blk[:, :1])
    chunk_row0 = jnp.where(used, chunk_blk * CH, -(NT * TM)).astype(i32)
    return _combine(x, meta, ys, chunk_blk.reshape(-1), chunk_row0.reshape(-1), base.astype(i32), g_post)


def kernel(x, positions, w_in, a_sinks, g_cq, w_uq, g_ckv, w_ukv, c_w1_k, c_w2_k, c_pe_k, c_w1_v, c_w2_v, c_pe_v,
           g_oa, g_ob, g_oc, w_o, g_pre_mix, g_post_mix, g_pre_ffn, g_post_ffn, ffn_w_gate, ffn_w_up, ffn_w_down,
           moe_router, moe_w_gate, moe_w_up, moe_w_down):
    B, S, _ = x.shape
    T = B * S
    depth = w_in.shape[0]
    a_slopes = _alibi_slopes(A_HEADS)
    c_slopes = _alibi_slopes(C_HEADS)
    xf = x.reshape(T, D_MODEL).astype(F32)
    cos, sin = _rope_tables(positions)
    att_scale = HEAD_DIM ** -0.5
    flat = lambda w: w.reshape(-1, w.shape[-1])
    bf16_w = {}
    w_p_all, w_vt_all = jax.vmap(_build_w_in)(w_in)
    mla_w_all = jax.vmap(_build_mla_weights)(w_uq, w_ukv)
    cmp_k_all = jax.vmap(_build_compress_weights)(c_w1_k, c_w2_k, c_pe_k)
    cmp_v_all = jax.vmap(_build_compress_weights)(c_w1_v, c_w2_v, c_pe_v)
    for l in range(depth):
        P, VT, qB, kB, vtB, k_cmp, v_cmp = _inproj(xf, g_pre_mix[l].reshape(1, -1), w_p_all[l], w_vt_all[l], S, cos, sin,
                                     g_cq[l], g_ckv[l], tuple(w[l] for w in mla_w_all))
        o_a = _banded_attn(P, VT, B, S, name="swa_attn", NSUB=NSUB_A, G=A_KV_HEADS, R=A_HEADS // A_KV_HEADS,
                           KD=KD_A, q_col=P_AQ, k_col=P_KA, vt_slab=VT_A, W=A_WINDOW, slopes=a_slopes,
                           scale=att_scale, sinks=a_sinks[l])
        if l % 2 == 0:
            side = (ffn_w_gate[l // 2], ffn_w_up[l // 2], ffn_w_down[l // 2])
        else:
            side = (flat(moe_w_down[l // 2]),)
        o_b, *cast = _flash_attn(qB, kB, vtB, B, S, name="mla_attn", TQ=TQ_MLA, TK=TQ_MLA, G=B_HEADS, R=1, KD=LANES,
                                 q_col=0, q_width=B_HEADS * LANES, k_col=0, vt_slab=0, side=side)
        if l % 2 == 0:
            bf16_w["ffn", l] = cast
        else:
            bf16_w["down", l] = cast[0]
        kc, vct = _compress(k_cmp, v_cmp, B, S, tuple(w[l] for w in cmp_k_all), tuple(w[l] for w in cmp_v_all))
        o_cmp, selb, pick_cnt = _cmp_sel(P, kc, vct, B, S, c_slopes)
        c_args = dict(TQ=TQ_ATT, G=C_KV_HEADS, R=C_HEADS // C_KV_HEADS, q_col=P_CQ, q_width=C_HEADS * HEAD_DIM,
                      slopes=c_slopes, scale=att_scale, gate=P)
        side = ()
        if l % 2 == 0 and l + 1 < depth:
            side = (flat(moe_w_gate[(l + 1) // 2]),)
        elif l % 2 == 1:
            side = (flat(moe_w_up[l // 2]),)
        o_slc, *cast = _flash_attn(P, P, VT, B, S, name="nsa_selected", TK=TK_SEL, KD=KD_SEL, k_col=P_KSEL,
                                   vt_slab=VT_SEL, selb=selb, tiles=_selected_tile_lists(pick_cnt, B, S),
                                   gate_branch=1, side=side, **c_args)
        if l % 2 == 0 and l + 1 < depth:
            bf16_w["gate", l + 1] = cast[0]
        elif l % 2 == 1:
            bf16_w["up", l] = cast[0]
        o_win = _banded_attn(P, VT, B, S, name="nsa_window", NSUB=NSUB_WIN, G=C_KV_HEADS, R=C_HEADS // C_KV_HEADS,
                             KD=KD_WIN, q_col=P_CQ, k_col=P_KWIN, vt_slab=VT_WIN, W=C_WINDOW, slopes=c_slopes,
                             scale=att_scale, gate_branch=2)
        xf = _outproj(xf, o_a, o_b, o_cmp, o_slc, o_win, g_oa[l], g_ob[l], g_oc[l], w_o[l], g_post_mix[l])
        if l % 2 == 0:
            wg, wu, wd = bf16_w["ffn", l]
            xf = _ffn(xf, g_pre_ffn[l], wg, wu, wd, g_post_ffn[l])
        else:
            e = l // 2
            wg = bf16_w["gate", l].reshape(moe_w_gate[e].shape)
            wu = bf16_w["up", l].reshape(moe_w_up[e].shape)
            wd = bf16_w["down", l].reshape(moe_w_down[e].shape)
            xf = _moe(xf, g_pre_ffn[l], moe_router[e], wg, wu, wd, g_post_ffn[l])
    return xf.reshape(B, S, D_MODEL)
```

```python
import functools

import numpy as np
import jax
import jax.numpy as jnp
from jax import lax
from jax.experimental import pallas as pl
from jax.experimental.pallas import tpu as pltpu

F32 = jnp.float32
BF16 = jnp.bfloat16

D_MODEL = 1024
HEAD_DIM = 64
NORM_EPS = 1e-6
NEG_INF = -1e30
ROPE_THETA = 10000.0
LOG2E = 1.4426950408889634

A_HEADS, A_KV_HEADS, A_WINDOW = 4, 2, 128
B_HEADS, B_Q_RANK, B_KV_RANK, B_NOPE, B_ROPE = 4, 256, 128, 64, 32
C_HEADS, C_KV_HEADS, C_WINDOW = 8, 2, 512
C_CMP_BLOCK, C_CMP_STRIDE, C_CMP_HIDDEN = 32, 16, 128
C_SEL_BLOCK, C_SEL_TOP = 64, 16
N_EXPERTS = 8

LANES = 128
VMEM_LIMIT = 48 * 1024 * 1024

P_CQ, P_AQ, P_BCQ = 0, 512, 768
P_BCKV, P_CKC, P_CVC, P_KR, P_KRROT, P_GATE = 1024, 1152, 1280, 1408, 1536, 1664
P_KSEL, KD_SEL = 1792, 256
P_KWIN, KD_WIN = 2304, 128
P_KA, KD_A = 2560, 128
P_WIDTH = 2816
V_ROWS = 80
VT_SEL, VT_WIN, VT_A = 0, 2, 4
VT_SLABS = 6
MASK_BIAS = -1e30
M_INIT = -1e29

TM_PROJ = 512
TM_OUT = 1024
TQ_ATT = 128
TK_BAND = 128
NSUB_A = 4
NSUB_WIN = 4
TQ_MLA = 512
TK_MLA = 256
TK_SEL = 256
TM_FFN = 512
TF_FFN = 2048
TF_MOE = 1792
TS_MOE = 512
TM_MOE = 512
CH_MOE = 128
U_MOE = 24


def _alibi_slopes(n):
    return [float(np.float32(2.0 ** (-8.0 * (i + 1) / n))) for i in range(n)]


def _split_bf16(x):
    parts = []
    for _ in range(3):
        p = float(np.asarray(x, np.float32).astype(jnp.bfloat16))
        parts.append(p)
        x = x - p
    return parts


def _dot(a, b):
    return jnp.dot(a, b, preferred_element_type=F32)


def _dot_nt(a, b):
    return lax.dot_general(a, b, (((1,), (1,)), ((), ())), preferred_element_type=F32)


def _rms(x, g):
    return x * lax.rsqrt(jnp.mean(x * x, axis=-1, keepdims=True) + NORM_EPS) * g


def _sigmoid(x):
    return 1.0 / (1.0 + jnp.exp(-x))


def _params(*sem):
    return pltpu.CompilerParams(dimension_semantics=sem, vmem_limit_bytes=VMEM_LIMIT)


def _inproj_kernel(x_ref, g_ref, w_ref, wvt_ref, vone_ref, kc_ref, cos_ref, sin_ref, gq_ref, gkv_ref,
                   wq_ref, wqr_ref, wk_ref, wvtb_ref, voneb_ref, o_ref, vt_ref, qb_out, kb_out, vtb_out,
                   kcmp_out, vcmp_out):
    h = _rms(x_ref[...], g_ref[...]).astype(BF16)
    for n in range(P_WIDTH // 256):
        o_ref[:, n * 256:(n + 1) * 256] = _dot(h, w_ref[:, n * 256:(n + 1) * 256]).astype(BF16)
    kcmp_out[...] = o_ref[:, P_CKC:P_CKC + LANES]
    vcmp_out[...] = o_ref[:, P_CVC:P_CVC + LANES]
    _mla_prep(o_ref[:, P_BCQ:P_BCQ + B_Q_RANK], o_ref[:, P_BCKV:P_BCKV + B_KV_RANK], o_ref[:, P_KR:P_KR + LANES],
              o_ref[:, P_KRROT:P_KRROT + LANES], cos_ref[...], sin_ref[...], gq_ref, gkv_ref,
              wq_ref, wqr_ref, wk_ref, wvtb_ref, voneb_ref, qb_out, kb_out, vtb_out)
    for g in range(C_KV_HEADS):
        o_ref[:, P_KSEL + g * KD_SEL + 64:P_KSEL + g * KD_SEL + 192] = kc_ref[:, 0:128]
        o_ref[:, P_KWIN + g * KD_WIN + 64:P_KWIN + (g + 1) * KD_WIN] = kc_ref[:, 128:192]
    for g in range(A_KV_HEADS):
        o_ref[:, P_KA + g * KD_A + 64:P_KA + (g + 1) * KD_A] = kc_ref[:, 128:192]
    vt_ref[...] = (_dot_nt(wvt_ref[...], h) + vone_ref[...]).astype(BF16)


def _key_constants(S):
    s = np.arange(S)
    kc = np.zeros((S, 256), np.float32)
    kc[:, 0:3] = (s % TK_SEL)[:, None]
    kc[s, 64 + s // C_SEL_BLOCK] = 1.0
    kc[:, 128:131] = (s % TK_BAND)[:, None]
    return jnp.asarray(kc, dtype=BF16)


def _inproj(x, g, w, wvt, S, cos, sin, g_cq, g_ckv, mla_w):
    T = x.shape[0]
    TM = TM_PROJ
    assert S // C_SEL_BLOCK <= 64 and TK_SEL <= 256 and TK_BAND <= 256
    vone = np.zeros((VT_SLABS * V_ROWS, 1), np.float32)
    vone[HEAD_DIM::V_ROWS] = 1.0
    QW = B_HEADS * LANES
    VR = B_HEADS * V_ROWS
    voneb = np.zeros((VR, 1), np.float32)
    voneb[HEAD_DIM::V_ROWS] = 1.0
    wq, wqr, wk, wvtb = mla_w
    nblk = S // TM
    full = lambda a: pl.BlockSpec(a.shape, lambda i: (0, 0))
    rows = lambda w_: pl.BlockSpec((TM, w_), lambda i: (i, 0))
    cols = lambda r: pl.BlockSpec((r, TM), lambda i: (0, i))
    consts = [g, w, wvt, jnp.asarray(vone)]
    mla_consts = [g_cq.reshape(1, -1), g_ckv.reshape(1, -1), wq, wqr, wk, wvtb, jnp.asarray(voneb)]
    return pl.pallas_call(
        _inproj_kernel,
        grid=(T // TM,),
        in_specs=[rows(D_MODEL)] + [full(a) for a in consts]
                 + [pl.BlockSpec((TM, 256), lambda i: (i % nblk, 0)), rows(LANES), rows(LANES)]
                 + [full(a) for a in mla_consts],
        out_specs=[rows(P_WIDTH), cols(VT_SLABS * V_ROWS), cols(QW), rows(QW), cols(VR), rows(LANES), rows(LANES)],
        out_shape=[jax.ShapeDtypeStruct((T, P_WIDTH), BF16),
                   jax.ShapeDtypeStruct((VT_SLABS * V_ROWS, T), BF16),
                   jax.ShapeDtypeStruct((QW, T), BF16), jax.ShapeDtypeStruct((T, QW), BF16),
                   jax.ShapeDtypeStruct((VR, T), BF16),
                   jax.ShapeDtypeStruct((T, LANES), BF16), jax.ShapeDtypeStruct((T, LANES), BF16)],
        compiler_params=_params("parallel"),
        name="inproj",
    )(x, *consts, _key_constants(S), cos, sin, *mla_consts)


def _build_w_in(w_in):
    cuts = np.cumsum([0, 256, 128, 128, 256, 128, 32, 512, 128, 128, 128, 128, 128, 128, 24])
    seg = [w_in[:, cuts[i]:cuts[i + 1]] for i in range(14)]
    a_q, a_k, a_v, b_cq, b_ckv, b_kr, c_q, c_kc, c_vc, c_ks, c_vs, c_kw, c_vw, c_g = seg
    z = lambda n: jnp.zeros((w_in.shape[0], n), w_in.dtype)
    half = B_ROPE // 2
    kr_rot = jnp.concatenate([-b_kr[:, half:], b_kr[:, :half]], axis=1)
    cols = [c_q, a_q, b_cq, b_ckv, c_kc, c_vc,
            z(B_NOPE), b_kr, z(LANES - B_NOPE - B_ROPE),
            z(B_NOPE), kr_rot, z(LANES - B_NOPE - B_ROPE),
            c_g, z(LANES - 24)]
    for k, kd in ((c_ks, KD_SEL), (c_kw, KD_WIN), (a_k, KD_A)):
        for g in range(2):
            cols += [k[:, g * HEAD_DIM:(g + 1) * HEAD_DIM], z(kd - HEAD_DIM)]
    w = jnp.concatenate(cols, axis=1).astype(BF16)
    rows = []
    for v in (c_vs, c_vw, a_v):
        for g in range(2):
            rows += [v[:, g * HEAD_DIM:(g + 1) * HEAD_DIM].T, jnp.zeros((V_ROWS - HEAD_DIM, w_in.shape[0]), w_in.dtype)]
    return w, jnp.concatenate(rows, axis=0).astype(BF16)


def _rope_tab_kernel(pos_ref, inv_ref, cos_ref, sin_ref):
    ang = pos_ref[...] * inv_ref[...]
    cos_ref[...] = jnp.cos(ang)
    sin_ref[...] = jnp.sin(ang)


def _rope_tables(positions):
    T = positions.size
    per = LANES // B_ROPE
    inv = (ROPE_THETA ** (-np.arange(0, B_ROPE, 2, dtype=np.float32) / B_ROPE)).astype(np.float32)
    inv128 = np.tile(np.concatenate([inv, inv]), per).reshape(1, LANES)
    pos = jnp.repeat(positions.reshape(T // per, per).astype(F32), B_ROPE, axis=1)
    rows = min(TM_PROJ, T // per)
    cos_c, sin_c = pl.pallas_call(
        _rope_tab_kernel,
        grid=(T // per // rows,),
        in_specs=[pl.BlockSpec((rows, LANES), lambda i: (i, 0)),
                  pl.BlockSpec((1, LANES), lambda i: (0, 0))],
        out_specs=[pl.BlockSpec((rows, LANES), lambda i: (i, 0))] * 2,
        out_shape=[jax.ShapeDtypeStruct((T // per, LANES), F32)] * 2,
        compiler_params=_params("parallel"),
        name="rope_tables",
    )(pos, jnp.asarray(inv128))
    widen = lambda t, fill: jnp.pad(t.reshape(T, B_ROPE), ((0, 0), (B_NOPE, LANES - B_NOPE - B_ROPE)),
                                    constant_values=fill)
    return widen(cos_c, 1.0), widen(sin_c, 0.0)


def _flash_kernel(*refs, TQ, TK, G, R, NKT, slopes, scale, sel, gate_branch, n_side):
    refs = list(refs)
    lst_ref, cnt_ref = (refs.pop(0), refs.pop(0)) if sel else (None, None)
    q_ref = refs.pop(0)
    selb_ref = refs.pop(0) if sel else None
    k_refs = [refs.pop(0) for _ in range(G)]
    v_refs = [refs.pop(0) for _ in range(G)]
    gate_ref = refs.pop(0) if gate_branch is not None else None
    side_in = [refs.pop(0) for _ in range(n_side)]
    o_ref = refs.pop(0)
    side_out = [refs.pop(0) for _ in range(n_side)]
    qa_sc = refs.pop(0) if slopes is not None else None
    m_sc, acc_sc, s_sc, mx_sc = refs
    for src, dst in zip(side_in, side_out):
        dst[...] = src[...].astype(BF16)
    i = pl.program_id(1)
    step = pl.program_id(0) * pl.num_programs(1) + i
    t0 = i * TQ
    RQ = R * TQ
    lane = lax.broadcasted_iota(jnp.int32, (1, RQ), 1)
    qpos = t0 + (lane & (TQ - 1))
    hi = (t0 + TQ + TK - 1) // TK
    if gate_ref is not None:
        gate_t = _sigmoid(gate_ref[...].astype(F32)).T
    slope_vecs = []
    if slopes is not None:
        q_t = q_ref[...].astype(F32).T
        frow = lax.broadcasted_iota(jnp.int32, (HEAD_DIM, TQ), 0)
    for g in range(G):
        if slopes is not None:
            slope_vec = jnp.zeros((1, RQ), F32)
            for r in range(R):
                hh = g * R + r
                cols = slice(r * TQ, (r + 1) * TQ)
                s3 = _split_bf16(slopes[hh] * LOG2E)
                slope_vec = jnp.where(lane // TQ == r, sum(s3), slope_vec)
                qa_sc[g, 0:HEAD_DIM, cols] = (q_t[hh * HEAD_DIM:(hh + 1) * HEAD_DIM, :] * (scale * LOG2E)).astype(BF16)
                qa_sc[g, HEAD_DIM:2 * HEAD_DIM, cols] = jnp.where(
                    frow == 0, s3[0], jnp.where(frow == 1, s3[1], jnp.where(frow == 2, s3[2], 0.0))).astype(BF16)
                if sel:
                    qa_sc[g, 2 * HEAD_DIM:3 * HEAD_DIM, cols] = selb_ref[g * HEAD_DIM:(g + 1) * HEAD_DIM, :]
                    qa_sc[g, 3 * HEAD_DIM:4 * HEAD_DIM, cols] = jnp.zeros((HEAD_DIM, TQ), BF16)
            slope_vecs.append(slope_vec)
        else:
            slope_vecs.append(None)
        m_sc[g] = jnp.full((1, RQ), M_INIT, F32)
        acc_sc[g] = jnp.zeros((V_ROWS, RQ), F32)

    def scores(j, slot):
        ks = pl.multiple_of(jnp.asarray(j, jnp.int32) * TK, TK)
        for g in range(G):
            qa_t = qa_sc[g] if slopes is not None else q_ref[g * LANES:(g + 1) * LANES, :]
            st = _dot(k_refs[g][pl.ds(ks, TK), :], qa_t)
            s_sc[slot, g] = st
            mx_sc[slot, g] = jnp.max(st, axis=0, keepdims=True)

    def tile(j, slot, causal):
        ks = pl.multiple_of(jnp.asarray(j, jnp.int32) * TK, TK)
        for g in range(G):
            st = s_sc[slot, g]
            if causal:
                kpos = ks + lax.broadcasted_iota(jnp.int32, (TK, 1), 0)
                st = jnp.where(kpos <= qpos, st, MASK_BIAS)
            m = m_sc[g]
            mx = jnp.max(st, axis=0, keepdims=True) if causal else mx_sc[slot, g]
            if slopes is not None:
                c = slope_vecs[g] * ks.astype(F32)
                mn = jnp.maximum(m, mx + c)
                p = jnp.exp2(st - (mn - c))
            else:
                mn = jnp.maximum(m, mx)
                p = jnp.exp2(st - mn)
            acc_sc[g] = jnp.exp2(m - mn) * acc_sc[g] + _dot(v_refs[g][:, pl.ds(ks, TK)], p.astype(BF16))
            m_sc[g] = mn

    if sel:
        n = cnt_ref[step]
        jt = lambda t: lst_ref[step * NKT + t]
    else:
        n = hi
        jt = lambda t: t
    n_pair = (n - 1) // 2

    def pair(p, carry):
        t = 2 * p
        scores(jt(t + 1), 1)
        tile(jt(t), 0, False)
        scores(jt(t + 2), 0)
        tile(jt(t + 1), 1, False)
        return carry

    scores(jt(0), 0)
    lax.fori_loop(0, n_pair, pair, 0)

    @pl.when(2 * n_pair == n - 1)
    def _():
        tile(jt(n - 1), 0, True)

    @pl.when(2 * n_pair == n - 2)
    def _():
        scores(jt(n - 1), 1)
        tile(jt(n - 2), 0, TQ > TK)
        tile(jt(n - 1), 1, True)

    pieces = []
    for g in range(G):
        acc = acc_sc[g]
        num = acc[0:HEAD_DIM, :]
        den = acc[HEAD_DIM:HEAD_DIM + 1, :]
        o_t = num * (1.0 / den)
        for r in range(R):
            piece = o_t[:, r * TQ:(r + 1) * TQ]
            if gate_ref is not None:
                c = 3 * (g * R + r) + gate_branch
                piece = piece * gate_t[c:c + 1, :]
            pieces.append(piece)
    o_ref[...] = jnp.concatenate(pieces, axis=0).T.astype(o_ref.dtype)


def _banded_kernel(*refs, TQ, NSUB, G, R, W, slopes, scale, has_sink, gate_branch):
    refs = list(refs)
    q_ref = refs.pop(0)
    k_refs = [refs.pop(0) for _ in range(G)]
    v_refs = [refs.pop(0) for _ in range(G)]
    sink_ref = refs.pop(0) if has_sink else None
    gate_ref = refs.pop(0) if gate_branch is not None else None
    o_ref = refs.pop(0)
    qa_sc, m_sc, acc_sc, s_sc = refs
    TK = TQ
    RQ = R * TQ
    n_st = W // TK + 1
    i = pl.program_id(1)
    lane = lax.broadcasted_iota(jnp.int32, (1, RQ), 1)
    frow = lax.broadcasted_iota(jnp.int32, (HEAD_DIM, TQ), 0)

    def process(sub, check):
        qt = i * NSUB + sub
        t0 = qt * TQ
        rows = slice(sub * TQ, (sub + 1) * TQ)
        qpos = t0 + (lane & (TQ - 1))
        if gate_ref is not None:
            gate_t = _sigmoid(gate_ref[rows, :].astype(F32)).T
        q_t = q_ref[rows, :].astype(F32).T
        slope_vecs = []
        for g in range(G):
            slope_vec = jnp.zeros((1, RQ), F32)
            for r in range(R):
                hh = g * R + r
                cols = slice(r * TQ, (r + 1) * TQ)
                s3 = _split_bf16(slopes[hh] * LOG2E)
                slope_vec = jnp.where(lane // TQ == r, sum(s3), slope_vec)
                qa_sc[sub, g, 0:HEAD_DIM, cols] = (
                    q_t[hh * HEAD_DIM:(hh + 1) * HEAD_DIM, :] * (scale * LOG2E)).astype(BF16)
                qa_sc[sub, g, HEAD_DIM:2 * HEAD_DIM, cols] = jnp.where(
                    frow == 0, s3[0], jnp.where(frow == 1, s3[1], jnp.where(frow == 2, s3[2], 0.0))).astype(BF16)
            slope_vecs.append(slope_vec)
            m_sc[sub, g] = jnp.full((1, RQ), M_INIT, F32)
            acc_sc[sub, g] = jnp.zeros((V_ROWS, RQ), F32)

        def scores(j, slot):
            ks = pl.multiple_of(j * TK, TK)
            for g in range(G):
                s_sc[sub, slot, g] = _dot(k_refs[g][pl.ds(ks, TK), :], qa_sc[sub, g])

        def tile(j, slot, mask, live):
            ks = pl.multiple_of(j * TK, TK)
            for g in range(G):
                st = s_sc[sub, slot, g]
                if mask is not None:
                    kpos = ks + lax.broadcasted_iota(jnp.int32, (TK, 1), 0)
                    ok = kpos <= qpos if mask == "causal" else kpos > qpos - W
                    st = jnp.where(ok, st, MASK_BIAS)
                if live is not None:
                    st = jnp.where(live, st, MASK_BIAS)
                m = m_sc[sub, g]
                c = slope_vecs[g] * ks.astype(F32)
                mn = jnp.maximum(m, jnp.max(st, axis=0, keepdims=True) + c)
                p = jnp.exp2(st - (mn - c))
                acc_sc[sub, g] = (jnp.exp2(m - mn) * acc_sc[sub, g]
                                  + _dot(v_refs[g][:, pl.ds(ks, TK)], p.astype(BF16)))
                m_sc[sub, g] = mn

        js = [qt - n_st + 1 + t for t in range(n_st)]
        jc = [jnp.maximum(j, 0) for j in js] if check else js
        scores(jc[0], 0)
        for t in range(n_st):
            if t + 1 < n_st:
                scores(jc[t + 1], (t + 1) & 1)
            mask = "causal" if t == n_st - 1 else ("window" if t == 0 else None)
            tile(jc[t], t & 1, mask, (js[t] >= 0) if check and t < n_st - 1 else None)

        pieces = []
        for g in range(G):
            acc = acc_sc[sub, g]
            num = acc[0:HEAD_DIM, :]
            den = acc[HEAD_DIM:HEAD_DIM + 1, :]
            if has_sink:
                sink_vec = jnp.zeros((1, RQ), F32)
                for r in range(R):
                    sink_vec = jnp.where(lane // TQ == r, sink_ref[g * R + r] * LOG2E, sink_vec)
                m_true = m_sc[sub, g] - slope_vecs[g] * qpos.astype(F32)
                mf = jnp.maximum(m_true, sink_vec)
                a = jnp.exp2(m_true - mf)
                num = num * a
                den = den * a + jnp.exp2(sink_vec - mf)
            o_t = num * (1.0 / den)
            for r in range(R):
                piece = o_t[:, r * TQ:(r + 1) * TQ]
                if gate_ref is not None:
                    c = 3 * (g * R + r) + gate_branch
                    piece = piece * gate_t[c:c + 1, :]
                pieces.append(piece)
        o_ref[rows, :] = jnp.concatenate(pieces, axis=0).T.astype(o_ref.dtype)

    all_tiles_exist = i * NSUB >= n_st - 1

    @pl.when(all_tiles_exist)
    def _():
        for sub in range(NSUB):
            process(sub, False)

    @pl.when(jnp.logical_not(all_tiles_exist))
    def _():
        for sub in range(NSUB):
            process(sub, True)


def _banded_attn(P, VT, B, S, *, name, NSUB, G, R, KD, q_col, k_col, vt_slab, W, slopes, scale, sinks=None,
                 gate_branch=None):
    T = B * S
    TQ = TQ_ATT
    TB = NSUB * TQ
    nq = S // TB
    RQ = R * TQ
    assert TK_BAND == TQ and W % TQ == 0 and S % TB == 0 and KD == 2 * HEAD_DIM
    QW = G * R * HEAD_DIM
    in_specs = [pl.BlockSpec((TB, QW), lambda b, i: (b * nq + i, q_col // QW))]
    args = [P]
    for g in range(G):
        in_specs.append(pl.BlockSpec((S, KD), lambda b, i, g=g: (b, k_col // KD + g)))
        args.append(P)
    for g in range(G):
        in_specs.append(pl.BlockSpec((V_ROWS, S), lambda b, i, g=g: (vt_slab + g, b)))
        args.append(VT)
    if sinks is not None:
        in_specs.append(pl.BlockSpec(memory_space=pltpu.SMEM))
        args.append(sinks.astype(F32))
    if gate_branch is not None:
        in_specs.append(pl.BlockSpec((TB, LANES), lambda b, i: (b * nq + i, P_GATE // LANES)))
        args.append(P)
    kern = functools.partial(_banded_kernel, TQ=TQ, NSUB=NSUB, G=G, R=R, W=W, slopes=slopes, scale=scale,
                             has_sink=sinks is not None, gate_branch=gate_branch)
    return pl.pallas_call(
        kern,
        grid=(B, nq),
        in_specs=in_specs,
        out_specs=pl.BlockSpec((TB, QW), lambda b, i: (b * nq + i, 0)),
        out_shape=jax.ShapeDtypeStruct((T, QW), BF16),
        scratch_shapes=[pltpu.VMEM((NSUB, G, KD, RQ), BF16), pltpu.VMEM((NSUB, G, 1, RQ), F32),
                        pltpu.VMEM((NSUB, G, V_ROWS, RQ), F32), pltpu.VMEM((NSUB, 2, G, TQ, RQ), F32)],
        compiler_params=_params("parallel", "parallel"),
        name=name,
    )(*args)


def _flash_attn(q, k, vt, B, S, *, name, TQ, TK, G, R, KD, q_col, q_width, k_col, vt_slab, slopes=None, scale=1.0,
                selb=None, tiles=None, gate=None, gate_branch=None, side=()):
    T = B * S
    nq = S // TQ
    RQ = R * TQ
    assert (TK % TQ == 0 or TQ == 2 * TK) and S % TK == 0 and S % TQ == 0
    OW = G * R * HEAD_DIM
    if slopes is not None:
        in_specs = [pl.BlockSpec((TQ, q_width), lambda b, i, *_: (b * nq + i, q_col // q_width))]
    else:
        in_specs = [pl.BlockSpec((G * KD, TQ), lambda b, i, *_: (0, b * nq + i))]
    args = [q]
    if selb is not None:
        in_specs.append(pl.BlockSpec((LANES, TQ), lambda b, i, *_: (0, b * nq + i)))
        args.append(selb)
    for g in range(G):
        in_specs.append(pl.BlockSpec((S, KD), lambda b, i, *_, g=g: (b, k_col // KD + g)))
        args.append(k)
    for g in range(G):
        in_specs.append(pl.BlockSpec((V_ROWS, S), lambda b, i, *_, g=g: (vt_slab + g, b)))
        args.append(vt)
    if gate_branch is not None:
        in_specs.append(pl.BlockSpec((TQ, LANES), lambda b, i, *_: (b * nq + i, P_GATE // LANES)))
        args.append(gate)
    out_specs = [pl.BlockSpec((TQ, OW), lambda b, i, *_: (b * nq + i, 0))]
    out_shape = [jax.ShapeDtypeStruct((T, OW), BF16)]
    for w in side:
        rb = w.shape[0] // (B * nq)
        assert w.shape[0] % (B * nq) == 0 and rb % 16 == 0 and w.shape[1] % LANES == 0
        spec = pl.BlockSpec((rb, w.shape[1]), lambda b, i, *_: (b * nq + i, 0))
        in_specs.append(spec)
        args.append(w)
        out_specs.append(spec)
        out_shape.append(jax.ShapeDtypeStruct(w.shape, BF16))
    scratch = []
    if slopes is not None:
        scratch.append(pltpu.VMEM((G, KD, RQ), BF16))
    scratch += [pltpu.VMEM((G, 1, RQ), F32), pltpu.VMEM((G, V_ROWS, RQ), F32), pltpu.VMEM((2, G, TK, RQ), F32),
                pltpu.VMEM((2, G, 1, RQ), F32)]
    kern = functools.partial(_flash_kernel, TQ=TQ, TK=TK, G=G, R=R, NKT=S // TK, slopes=slopes, scale=scale,
                             sel=selb is not None, gate_branch=gate_branch, n_side=len(side))
    prefetch = list(tiles) if selb is not None else []
    grid_spec = pltpu.PrefetchScalarGridSpec(
        num_scalar_prefetch=len(prefetch),
        grid=(B, nq),
        in_specs=in_specs,
        out_specs=out_specs,
        scratch_shapes=scratch)
    outs = pl.pallas_call(
        kern,
        grid_spec=grid_spec,
        out_shape=out_shape,
        compiler_params=_params("parallel", "parallel"),
        name=name,
    )(*prefetch, *args)
    return list(outs)


def _mla_prep(cq, ckv, kr, krr, cos, sin, gq_ref, gkv_ref, wq_ref, wqr_ref, wk_ref, wvt_ref, vone_ref,
              q_out, k_out, vt_out):
    scale = (B_NOPE + B_ROPE) ** -0.5 * LOG2E
    nq = _rms(cq.astype(F32), gq_ref[...]).astype(BF16)
    nkv = _rms(ckv.astype(F32), gkv_ref[...]).astype(BF16)
    q1 = _dot_nt(wq_ref[...], nq)
    q2 = _dot_nt(wqr_ref[...], nq)
    kk = _dot(nkv, wk_ref[...])
    krope = kr.astype(F32) * cos + krr.astype(F32) * sin
    cos_t = cos.T
    sin_t = sin.T
    for h in range(B_HEADS):
        sl = slice(h * LANES, (h + 1) * LANES)
        q_out[sl, :] = ((q1[sl, :] * cos_t + q2[sl, :] * sin_t) * scale).astype(BF16)
        k_out[:, sl] = (kk[:, sl] + krope).astype(BF16)
    vt_out[...] = (_dot_nt(wvt_ref[...], nkv) + vone_ref[...]).astype(BF16)


def _build_mla_weights(w_uq, w_ukv):
    R1, R2 = w_uq.shape[0], w_ukv.shape[0]
    half = B_ROPE // 2
    hq = B_NOPE + B_ROPE
    wq, wqr, wk, wvt = [], [], [], []
    for h in range(B_HEADS):
        nope = w_uq[:, h * hq:h * hq + B_NOPE]
        rope = w_uq[:, h * hq + B_NOPE:(h + 1) * hq]
        rot = jnp.concatenate([-rope[:, half:], rope[:, :half]], axis=1)
        pad = jnp.zeros((R1, LANES - hq), w_uq.dtype)
        wq += [nope, rope, pad]
        wqr += [jnp.zeros((R1, B_NOPE), w_uq.dtype), rot, pad]
        wk += [w_ukv[:, h * 128:h * 128 + B_NOPE], jnp.zeros((R2, LANES - B_NOPE), w_ukv.dtype)]
        wvt += [w_ukv[:, h * 128 + B_NOPE:(h + 1) * 128].T, jnp.zeros((V_ROWS - HEAD_DIM, R2), w_ukv.dtype)]
    cat = lambda xs: jnp.concatenate(xs, axis=1).astype(BF16)
    return cat(wq).T, cat(wqr).T, cat(wk), jnp.concatenate(wvt, axis=0).astype(BF16)


def _gelu_tanh(x):
    return 0.5 * x * (1.0 + jnp.tanh(np.float32(np.sqrt(2.0 / np.pi)) * (x + 0.044715 * (x * x * x))))


def _compress_kernel(xk_ref, xv_ref, pek_ref, pev_ref, w1k_ref, w1v_ref, w2k_ref, w2v_ref, kc_ref, vct_ref, *, NC):
    for x_ref, pe_ref, w1_ref, w2_ref, is_v in ((xk_ref, pek_ref, w1k_ref, w2k_ref, False),
                                                 (xv_ref, pev_ref, w1v_ref, w2v_ref, True)):
        x = x_ref[...].astype(F32)
        a = _dot((x + pe_ref[0:1, :]).astype(BF16), w1_ref[0])
        b = _dot((x + pe_ref[1:2, :]).astype(BF16), w1_ref[1])
        hid = a + pltpu.roll(b, NC - 1, 0)
        o = _dot(_gelu_tanh(hid).astype(BF16), w2_ref[...])
        n = lax.broadcasted_iota(jnp.int32, o.shape, 0)
        o = jnp.where(n < NC - 1, o, 0.0)
        if is_v:
            vct_ref[...] = o.T.astype(vct_ref.dtype)
        else:
            kc_ref[...] = o.astype(kc_ref.dtype)


def _build_compress_weights(w1, w2, pe):
    half = C_CMP_BLOCK // 2
    G = C_KV_HEADS
    eye = jnp.eye(G, dtype=w1.dtype)
    w1s = [jnp.einsum('lde,hg->lhdge', w1[c * half:(c + 1) * half], eye).reshape(half * G * HEAD_DIM, G * C_CMP_HIDDEN)
           for c in range(2)]
    w1f = jnp.stack(w1s).astype(BF16)
    w2f = jnp.einsum('ed,hg->hegd', w2, eye).reshape(G * C_CMP_HIDDEN, G * HEAD_DIM).astype(BF16)
    pes = [jnp.broadcast_to(pe[c * half:(c + 1) * half, None, :], (half, G, HEAD_DIM)).reshape(1, -1) for c in range(2)]
    pef = jnp.concatenate(pes, axis=0).astype(F32)
    return w1f, w2f, pef


def _compress(k_cmp, v_cmp, B, S, k_weights, v_weights):
    NC = S // C_CMP_STRIDE
    CW = C_CMP_STRIDE * C_KV_HEADS * HEAD_DIM
    xk = k_cmp.reshape(B * NC, CW)
    xv = v_cmp.reshape(B * NC, CW)
    w1kf, w2kf, pekf = k_weights
    w1vf, w2vf, pevf = v_weights
    xspec = pl.BlockSpec((NC, CW), lambda b: (b, 0))
    full2 = lambda a: pl.BlockSpec(a.shape, lambda b: (0, 0))
    full3 = lambda a: pl.BlockSpec(a.shape, lambda b: (0, 0, 0))
    return pl.pallas_call(
        functools.partial(_compress_kernel, NC=NC),
        grid=(B,),
        in_specs=[xspec, xspec, full2(pekf), full2(pevf), full3(w1kf), full3(w1vf), full2(w2kf), full2(w2vf)],
        out_specs=[pl.BlockSpec((NC, LANES), lambda b: (b, 0)), pl.BlockSpec((LANES, NC), lambda b: (b, 0))],
        out_shape=[jax.ShapeDtypeStruct((B * NC, LANES), BF16), jax.ShapeDtypeStruct((B * LANES, NC), BF16)],
        compiler_params=_params("parallel"),
        name="nsa_compress",
    )(xk, xv, pekf, pevf, w1kf, w1vf, w2kf, w2vf)


def _cmp_sel_kernel(q_ref, kc_ref, vct_ref, cfeat_ref, ovt_ref, gate_ref, ocmp_ref, sel_ref, cnt_ref,
                    *, TQ, NC, NSEL, NTOP, slopes):
    i = pl.program_id(1)
    t0 = i * TQ
    R = C_HEADS // C_KV_HEADS
    scale = HEAD_DIM ** -0.5
    gate_t = _sigmoid(gate_ref[...].astype(F32)).T
    last = C_CMP_BLOCK - 1
    tlane = t0 + lax.broadcasted_iota(jnp.int32, (NC, TQ), 1)
    nrow = lax.broadcasted_iota(jnp.int32, (NC, TQ), 0)
    cvalid = (nrow * C_CMP_STRIDE + last <= tlane) & (nrow < NC - 1)
    hasc = ((t0 + lax.broadcasted_iota(jnp.int32, (1, TQ), 1)) >= last).astype(F32)
    jrow = lax.broadcasted_iota(jnp.int32, (NSEL, TQ), 0)
    tl = t0 + lax.broadcasted_iota(jnp.int32, (NSEL, TQ), 1)
    avail = jrow * C_SEL_BLOCK <= tl
    cur = tl // C_SEL_BLOCK
    forced = (jrow == 0) | (jrow == cur) | (jrow == cur - 1)
    q_t = q_ref[...].astype(F32).T
    frow = lax.broadcasted_iota(jnp.int32, (HEAD_DIM, TQ), 0)
    sub8 = lax.broadcasted_iota(jnp.int32, (8, TQ), 0)
    pieces, sel_parts = [], []
    picked = jnp.zeros((NSEL, TQ), F32)
    for g in range(C_KV_HEADS):
        kca = jnp.concatenate([kc_ref[:, g * HEAD_DIM:(g + 1) * HEAD_DIM], cfeat_ref[...]], axis=1)
        vo = jnp.concatenate([vct_ref[g * HEAD_DIM:(g + 1) * HEAD_DIM, :], ovt_ref[...]], axis=0)
        imp = jnp.zeros((HEAD_DIM, TQ), F32)
        for r in range(R):
            hh = g * R + r
            s3 = _split_bf16(slopes[hh] * LOG2E)
            feat = jnp.zeros((HEAD_DIM, TQ), F32)
            for k in range(3):
                feat = jnp.where(frow == k, s3[k] * 256.0, jnp.where(frow == 3 + k, s3[k], feat))
            qa = jnp.concatenate([q_t[hh * HEAD_DIM:(hh + 1) * HEAD_DIM, :] * (scale * LOG2E), feat],
                                 axis=0).astype(BF16)
            st = jnp.where(cvalid, _dot(kca, qa), NEG_INF)
            et = jnp.exp2(st - jnp.max(st, axis=0, keepdims=True))
            inv = hasc / jnp.sum(et, axis=0, keepdims=True)
            res = _dot(vo, et.astype(BF16))
            pieces.append(res[0:HEAD_DIM, :] * (inv * gate_t[3 * hh:3 * hh + 1, :]))
            imp = imp + res[HEAD_DIM:2 * HEAD_DIM, :] * inv
        v = jnp.where(avail, imp[0:NSEL, :], NEG_INF)
        v = jnp.where(forced, -NEG_INF, v)
        ranks = [jnp.zeros((8, TQ), F32) for _ in range(NSEL // 8)]
        for ii in range(NSEL):
            ri = v[ii:ii + 1, :]
            for k in range(NSEL // 8):
                vk = v[8 * k:8 * k + 8, :]
                if 8 * k > ii:
                    inc = jnp.where(ri >= vk, 1.0, 0.0)
                elif 8 * k + 7 < ii:
                    inc = jnp.where(ri > vk, 1.0, 0.0)
                else:
                    inc = jnp.where(sub8 + 8 * k > ii, jnp.where(ri >= vk, 1.0, 0.0), jnp.where(ri > vk, 1.0, 0.0))
                ranks[k] = ranks[k] + inc
        rank = jnp.concatenate(ranks, axis=0)
        sel_parts.append(jnp.where(rank < NTOP, 0.0, MASK_BIAS))
        if NSEL < HEAD_DIM:
            sel_parts.append(jnp.zeros((HEAD_DIM - NSEL, TQ), F32))
        picked = picked + jnp.where(rank < NTOP, 1.0, 0.0)
    sel_ref[...] = jnp.concatenate(sel_parts, axis=0).astype(sel_ref.dtype)
    cnt_ref[...] = _dot_nt(jnp.ones((8, TQ), BF16), picked.astype(BF16))
    ocmp_ref[...] = jnp.concatenate(pieces, axis=0).T.astype(ocmp_ref.dtype)


def _overlap_t(S):
    n_cmp = (S - C_CMP_BLOCK) // C_CMP_STRIDE + 1
    n_sel = S // C_SEL_BLOCK
    NC = S // C_CMP_STRIDE
    cstart = np.arange(n_cmp) * C_CMP_STRIDE
    cend = cstart + C_CMP_BLOCK - 1
    sstart = np.arange(n_sel) * C_SEL_BLOCK
    send = sstart + C_SEL_BLOCK - 1
    ov = np.clip(np.minimum(cend[:, None], send[None, :]) - np.maximum(cstart[:, None], sstart[None, :]) + 1,
                 0, None).astype(np.float32) / C_CMP_STRIDE
    ovt = np.zeros((HEAD_DIM, NC), np.float32)
    ovt[:n_sel, :n_cmp] = ov.T
    cend_all = np.arange(NC) * C_CMP_STRIDE + C_CMP_BLOCK - 1
    cfeat = np.zeros((NC, HEAD_DIM), np.float32)
    cfeat[:, 0:3] = (cend_all // 256)[:, None]
    cfeat[:, 3:6] = (cend_all % 256)[:, None]
    return jnp.asarray(ovt, dtype=BF16), jnp.asarray(cfeat, dtype=BF16)


def _cmp_sel(P, kc, vct, B, S, slopes):
    T = B * S
    TQ = TQ_ATT
    nq = S // TQ
    NC = S // C_CMP_STRIDE
    NSEL = S // C_SEL_BLOCK
    assert NSEL <= HEAD_DIM and NSEL % 8 == 0 and S <= 256 * 256
    QW = C_HEADS * HEAD_DIM
    ovt, cfeat = _overlap_t(S)
    kern = functools.partial(_cmp_sel_kernel, TQ=TQ, NC=NC, NSEL=NSEL, NTOP=min(C_SEL_TOP, NSEL), slopes=slopes)
    return pl.pallas_call(
        kern,
        grid=(B, nq),
        in_specs=[pl.BlockSpec((TQ, QW), lambda b, i: (b * nq + i, P_CQ // QW)),
                  pl.BlockSpec((NC, LANES), lambda b, i: (b, 0)),
                  pl.BlockSpec((LANES, NC), lambda b, i: (b, 0)),
                  pl.BlockSpec((NC, HEAD_DIM), lambda b, i: (0, 0)),
                  pl.BlockSpec((HEAD_DIM, NC), lambda b, i: (0, 0)),
                  pl.BlockSpec((TQ, LANES), lambda b, i: (b * nq + i, P_GATE // LANES))],
        out_specs=[pl.BlockSpec((TQ, QW), lambda b, i: (b * nq + i, 0)),
                   pl.BlockSpec((LANES, TQ), lambda b, i: (0, b * nq + i)),
                   pl.BlockSpec((8, NSEL), lambda b, i: (b * nq + i, 0))],
        out_shape=[jax.ShapeDtypeStruct((T, QW), BF16), jax.ShapeDtypeStruct((LANES, T), BF16),
                   jax.ShapeDtypeStruct((B * nq * 8, NSEL), F32)],
        compiler_params=_params("parallel", "parallel"),
        name="nsa_cmp_select",
    )(P, kc, vct, cfeat, ovt, P)


def _selected_tile_lists(cnt, B, S):
    nq = S // TQ_ATT
    nkt = S // TK_SEL
    per = TK_SEL // C_SEL_BLOCK
    blk = cnt.reshape(B * nq, 8, -1)[:, 0, :] > 0.5
    need = jnp.any(blk.reshape(B * nq, nkt, per), axis=-1)
    own = ((jnp.arange(B * nq, dtype=jnp.int32) % nq) * TQ_ATT + TQ_ATT - 1) // TK_SEL
    jj = jnp.arange(nkt, dtype=jnp.int32)[None, :]
    need = (need | (jj == 0) | (jj == own[:, None])) & (jj <= own[:, None])
    order = jnp.argsort(jnp.where(need, jj, nkt + jj), axis=-1).astype(jnp.int32)
    return order.reshape(-1), jnp.sum(need, axis=-1).astype(jnp.int32)


def _outproj_kernel(x_ref, oa_ref, ob_ref, oc1_ref, oc2_ref, oc3_ref, ga_ref, gb_ref, gc_ref, wo_ref, gp_ref, o_ref):
    oa = _rms(oa_ref[...].astype(F32), ga_ref[...]).astype(BF16)
    ob = _rms(ob_ref[...].astype(F32), gb_ref[...]).astype(BF16)
    oc = oc1_ref[...].astype(F32) + oc2_ref[...].astype(F32) + oc3_ref[...].astype(F32)
    oc = _rms(oc, gc_ref[...]).astype(BF16)
    m = _dot(oa, wo_ref[0:256, :]) + _dot(ob, wo_ref[256:512, :]) + _dot(oc, wo_ref[512:1024, :])
    o_ref[...] = x_ref[...] + _rms(m, gp_ref[...])


def _outproj(x, o_a, o_b, o_cmp, o_slc, o_win, g_oa, g_ob, g_oc, w_o, g_post):
    T = x.shape[0]
    TM = TM_OUT
    row = lambda w: pl.BlockSpec((TM, w), lambda i: (i, 0))
    full = lambda r, c: pl.BlockSpec((r, c), lambda i: (0, 0))
    return pl.pallas_call(
        _outproj_kernel,
        grid=(T // TM,),
        in_specs=[row(D_MODEL), row(256), row(256), row(512), row(512), row(512),
                  full(1, 256), full(1, 256), full(1, 512), full(D_MODEL, D_MODEL), full(1, D_MODEL)],
        out_specs=row(D_MODEL),
        out_shape=jax.ShapeDtypeStruct((T, D_MODEL), F32),
        compiler_params=_params("parallel"),
        name="outproj",
    )(x, o_a, o_b, o_cmp, o_slc, o_win, g_oa.reshape(1, -1), g_ob.reshape(1, -1), g_oc.reshape(1, -1),
      w_o.astype(BF16), g_post.reshape(1, -1))


def _swiglu_step(h, wg_ref, wu_ref, wd_ref, acc_ref):
    a = _dot(h, wg_ref[...])
    b = _dot(h, wu_ref[...])
    z = (a * _sigmoid(a) * b).astype(BF16)
    acc_ref[...] += _dot(z, wd_ref[...])


def _ffn_kernel(x_ref, gpre_ref, wg_ref, wu_ref, wd_ref, gpost_ref, o_ref, h_sc, acc_sc, *, nf):
    f = pl.program_id(1)

    @pl.when(f == 0)
    def _():
        h_sc[...] = _rms(x_ref[...], gpre_ref[...]).astype(BF16)
        acc_sc[...] = jnp.zeros_like(acc_sc)

    _swiglu_step(h_sc[...], wg_ref, wu_ref, wd_ref, acc_sc)

    @pl.when(f == nf - 1)
    def _():
        o_ref[...] = x_ref[...] + _rms(acc_sc[...], gpost_ref[...])


def _ffn(x, g_pre, wg, wu, wd, g_post):
    T = x.shape[0]
    TM, TF = TM_FFN, TF_FFN
    FF = wg.shape[1]
    nf = FF // TF
    return pl.pallas_call(
        functools.partial(_ffn_kernel, nf=nf),
        grid=(T // TM, nf),
        in_specs=[pl.BlockSpec((TM, D_MODEL), lambda i, f: (i, 0)),
                  pl.BlockSpec((1, D_MODEL), lambda i, f: (0, 0)),
                  pl.BlockSpec((D_MODEL, TF), lambda i, f: (0, f)),
                  pl.BlockSpec((D_MODEL, TF), lambda i, f: (0, f)),
                  pl.BlockSpec((TF, D_MODEL), lambda i, f: (f, 0)),
                  pl.BlockSpec((1, D_MODEL), lambda i, f: (0, 0))],
        out_specs=pl.BlockSpec((TM, D_MODEL), lambda i, f: (i, 0)),
        out_shape=jax.ShapeDtypeStruct((T, D_MODEL), F32),
        scratch_shapes=[pltpu.VMEM((TM, D_MODEL), BF16), pltpu.VMEM((TM, D_MODEL), F32)],
        compiler_params=_params("parallel", "arbitrary"),
        name="dense_ffn",
    )(x, g_pre.reshape(1, -1), wg.astype(BF16), wu.astype(BF16), wd.astype(BF16), g_post.reshape(1, -1))


def _expert_kernel(te_ref, nt_ref, x_ref, wg_ref, wu_ref, wd_ref, o_ref, acc_sc, *, nf):
    j = pl.program_id(0)
    f = pl.program_id(1)
    live = j < nt_ref[0]

    @pl.when(f == 0)
    def _():
        acc_sc[...] = jnp.zeros_like(acc_sc)

    @pl.when(live)
    def _():
        _swiglu_step(x_ref[...], wg_ref.at[0], wu_ref.at[0], wd_ref.at[0], acc_sc)

    @pl.when(f == nf - 1)
    def _():
        o_ref[...] = acc_sc[...].astype(o_ref.dtype)


def _experts(xs, tile_expert, n_tiles, wg, wu, wd):
    NT = xs.shape[0] // TM_MOE
    TM, TF = TM_MOE, TF_MOE
    assert wg.shape[2] % TF == 0
    FF = wg.shape[2]
    nf = FF // TF
    fidx = lambda j, f, nt: jnp.where(j < nt[0], f, nf - 1)
    grid_spec = pltpu.PrefetchScalarGridSpec(
        num_scalar_prefetch=2,
        grid=(NT, nf),
        in_specs=[pl.BlockSpec((TM, D_MODEL), lambda j, f, te, nt: (j, 0)),
                  pl.BlockSpec((1, D_MODEL, TF), lambda j, f, te, nt: (te[j], 0, fidx(j, f, nt))),
                  pl.BlockSpec((1, D_MODEL, TF), lambda j, f, te, nt: (te[j], 0, fidx(j, f, nt))),
                  pl.BlockSpec((1, TF, D_MODEL), lambda j, f, te, nt: (te[j], fidx(j, f, nt), 0))],
        out_specs=pl.BlockSpec((TM, D_MODEL), lambda j, f, te, nt: (j, 0)),
        scratch_shapes=[pltpu.VMEM((TM, D_MODEL), F32)])
    return pl.pallas_call(
        functools.partial(_expert_kernel, nf=nf),
        grid_spec=grid_spec,
        out_shape=jax.ShapeDtypeStruct(xs.shape, BF16),
        compiler_params=_params("parallel", "arbitrary"),
        name="moe_experts",
    )(tile_expert, n_tiles, xs, wg, wu, wd)


def _router_kernel(x_ref, g_ref, wr_ref, h_out, meta_out, metat_out, start_out, cnt_out, run_sc, *, TS):
    s = pl.program_id(0)

    @pl.when(s == 0)
    def _():
        run_sc[...] = jnp.zeros_like(run_sc)

    h = _rms(x_ref[...], g_ref[...])
    hb = h.astype(BF16)
    h_out[...] = hb
    hl = (h - hb.astype(F32)).astype(BF16)
    w = wr_ref[...]
    wh = w.astype(BF16)
    wl = (w - wh.astype(F32)).astype(BF16)
    logits = _dot(hb, wh) + _dot(hl, wh) + _dot(hb, wl)
    lane = lax.broadcasted_iota(jnp.int32, (TS, LANES), 1).astype(F32)
    logits = jnp.where(lane < N_EXPERTS, logits, NEG_INF)
    m1 = jnp.max(logits, axis=-1, keepdims=True)
    i1 = jnp.min(jnp.where(logits == m1, lane, float(LANES)), axis=-1, keepdims=True)
    rest = jnp.where(lane == i1, NEG_INF, logits)
    m2 = jnp.max(rest, axis=-1, keepdims=True)
    i2 = jnp.min(jnp.where(rest == m2, lane, float(LANES)), axis=-1, keepdims=True)
    ex = jnp.exp(m2 - m1)
    w1 = 1.0 / (1.0 + ex)
    w2 = ex / (1.0 + ex)
    oh1 = lane == i1
    oh2 = lane == i2
    oh = jnp.where(oh1, 1.0, 0.0) + jnp.where(oh2, 1.0, 0.0)
    tri = jnp.where(lax.broadcasted_iota(jnp.int32, (TS, TS), 0) > lax.broadcasted_iota(jnp.int32, (TS, TS), 1),
                    1.0, 0.0).astype(BF16)
    run = run_sc[...]
    before = _dot(tri, oh.astype(BF16)) + run
    r1 = jnp.sum(jnp.where(oh1, before, 0.0), axis=-1, keepdims=True)
    r2 = jnp.sum(jnp.where(oh2, before, 0.0), axis=-1, keepdims=True)
    start_out[0] = run
    run = run + jnp.sum(oh, axis=0, keepdims=True)
    run_sc[...] = run
    cnt_out[...] = run
    meta = jnp.where(lane == 0, i1, jnp.where(lane == 1, i2, jnp.where(lane == 2, r1, jnp.where(
        lane == 3, r2, jnp.where(lane == 4, w1, jnp.where(lane == 5, w2, 0.0))))))
    meta_out[...] = meta
    metat_out[...] = meta.T[0:8, :]


def _router(x, g, w_router):
    T = x.shape[0]
    TS = TS_MOE
    nt = T // TS
    wr = jnp.zeros((D_MODEL, LANES), F32).at[:, :N_EXPERTS].set(w_router.astype(F32))
    return pl.pallas_call(
        functools.partial(_router_kernel, TS=TS),
        grid=(nt,),
        in_specs=[pl.BlockSpec((TS, D_MODEL), lambda s: (s, 0)),
                  pl.BlockSpec((1, D_MODEL), lambda s: (0, 0)),
                  pl.BlockSpec((D_MODEL, LANES), lambda s: (0, 0))],
        out_specs=[pl.BlockSpec((TS, D_MODEL), lambda s: (s, 0)),
                   pl.BlockSpec((TS, LANES), lambda s: (s, 0)),
                   pl.BlockSpec((8, TS), lambda s: (0, s)),
                   pl.BlockSpec((1, 1, LANES), lambda s: (s, 0, 0)),
                   pl.BlockSpec((1, LANES), lambda s: (0, 0))],
        out_shape=[jax.ShapeDtypeStruct((T, D_MODEL), BF16),
                   jax.ShapeDtypeStruct((T, LANES), F32),
                   jax.ShapeDtypeStruct((8, T), F32),
                   jax.ShapeDtypeStruct((nt, 1, LANES), F32),
                   jax.ShapeDtypeStruct((1, LANES), F32)],
        scratch_shapes=[pltpu.VMEM((1, LANES), F32)],
        compiler_params=_params("arbitrary"),
        name="moe_router",
    )(x, g.reshape(1, -1), wr)


def _gather_kernel(te_ref, slo_ref, shi_ref, base_ref, metat_ref, h_ref, xs_out, acc_sc, *, TM, TS, CH):
    j = pl.program_id(0)
    e = te_ref[j]
    ef = e.astype(F32)
    nsub = TM // CH
    nt = h_ref.shape[0] // TS

    def picked_rows(s, want):
        off = pl.multiple_of(s * TS, TS)
        e1 = metat_ref[0:1, pl.ds(off, TS)]
        e2 = metat_ref[1:2, pl.ds(off, TS)]
        r1 = metat_ref[2:3, pl.ds(off, TS)]
        r2 = metat_ref[3:4, pl.ds(off, TS)]
        key = jnp.where(e1 == ef, r1, jnp.where(e2 == ef, r2, -1.0))
        sel = jnp.where(key == want, 1.0, 0.0).astype(BF16)
        return _dot(sel, h_ref[pl.ds(off, TS), :])

    wants = []
    for sub in range(nsub):
        q = j * nsub + sub
        first = j * TM + sub * CH - base_ref[e]
        want = (first + lax.broadcasted_iota(jnp.int32, (CH, 1), 0)).astype(F32)
        lo = jnp.minimum(slo_ref[q], nt - 1)
        second = jnp.where(lo + 1 < nt, want, -2.0)
        acc_sc[sub] = picked_rows(lo, want) + picked_rows(jnp.minimum(lo + 1, nt - 1), second)
        wants.append(want)
    for sub in range(nsub):
        q = j * nsub + sub

        def body(s, _, sub=sub):
            acc_sc[sub] += picked_rows(s, wants[sub])
            return 0

        lax.fori_loop(slo_ref[q] + 2, shi_ref[q] + 1, body, 0)
    for sub in range(nsub):
        xs_out[sub * CH:(sub + 1) * CH, :] = acc_sc[sub].astype(xs_out.dtype)


def _gather(hb, metat, tile_expert, s_lo, s_hi, base, NT):
    TM, TS = TM_MOE, TS_MOE
    assert hb.shape[0] % TS == 0
    grid_spec = pltpu.PrefetchScalarGridSpec(
        num_scalar_prefetch=4,
        grid=(NT,),
        in_specs=[pl.BlockSpec(memory_space=pltpu.VMEM), pl.BlockSpec(memory_space=pltpu.VMEM)],
        out_specs=pl.BlockSpec((TM, D_MODEL), lambda j, *_: (j, 0)),
        scratch_shapes=[pltpu.VMEM((TM // CH_MOE, CH_MOE, D_MODEL), F32)])
    return pl.pallas_call(
        functools.partial(_gather_kernel, TM=TM, TS=TS, CH=CH_MOE),
        grid_spec=grid_spec,
        out_shape=jax.ShapeDtypeStruct((NT * TM, D_MODEL), BF16),
        compiler_params=_params("arbitrary"),
        name="moe_gather",
    )(tile_expert, s_lo, s_hi, base, metat, hb)


def _combine_kernel(blk_ref, row0_ref, base_ref, meta_ref, x_ref, gpost_ref, ys_hbm, o_ref, buf, sem, selw_sc,
                    *, TS, CH, U, nt):
    s = pl.program_id(0)
    slot = s % 2

    def chunk_copies(step, slot_):
        out = []
        for u in range(U):
            r0 = pl.multiple_of(blk_ref[step * U + u] * CH, CH)
            out.append(pltpu.make_async_copy(ys_hbm.at[pl.ds(r0, CH), :], buf.at[slot_, pl.ds(u * CH, CH), :],
                                             sem.at[slot_]))
        return out

    @pl.when(s == 0)
    def _():
        for c in chunk_copies(0, 0):
            c.start()

    @pl.when(s + 1 < nt)
    def _():
        for c in chunk_copies(s + 1, 1 - slot):
            c.start()

    meta = meta_ref[...]
    e1, e2 = meta[:, 0:1], meta[:, 1:2]
    w1, w2 = meta[:, 4:5], meta[:, 5:6]
    b1 = jnp.zeros((TS, 1), F32)
    b2 = jnp.zeros((TS, 1), F32)
    for e in range(N_EXPERTS):
        be = base_ref[e].astype(F32)
        b1 = jnp.where(e1 == float(e), be, b1)
        b2 = jnp.where(e2 == float(e), be, b2)
    pos1 = b1 + meta[:, 2:3]
    pos2 = b2 + meta[:, 3:4]

    lane = lax.broadcasted_iota(jnp.int32, (1, CH), 1)
    for u in range(U):
        rows = (row0_ref[s * U + u] + lane).astype(F32)
        selw = jnp.where(pos1 == rows, w1, 0.0) + jnp.where(pos2 == rows, w2, 0.0)
        selw_sc[:, u * CH:(u + 1) * CH] = selw.astype(BF16)
    for c in chunk_copies(s, slot):
        c.wait()
    y = _dot(selw_sc[...], buf[slot])
    o_ref[...] = x_ref[...] + _rms(y, gpost_ref[...])


def _combine(x, meta, ys, chunk_blk, chunk_row0, base, g_post):
    T = x.shape[0]
    TS, CH, U = TS_MOE, CH_MOE, U_MOE
    nt = T // TS
    grid_spec = pltpu.PrefetchScalarGridSpec(
        num_scalar_prefetch=3,
        grid=(nt,),
        in_specs=[pl.BlockSpec((TS, LANES), lambda s, *_: (s, 0)),
                  pl.BlockSpec((TS, D_MODEL), lambda s, *_: (s, 0)),
                  pl.BlockSpec((1, D_MODEL), lambda s, *_: (0, 0)),
                  pl.BlockSpec(memory_space=pl.ANY)],
        out_specs=pl.BlockSpec((TS, D_MODEL), lambda s, *_: (s, 0)),
        scratch_shapes=[pltpu.VMEM((2, U * CH, D_MODEL), BF16), pltpu.SemaphoreType.DMA((2,)),
                        pltpu.VMEM((TS, U * CH), BF16)])
    return pl.pallas_call(
        functools.partial(_combine_kernel, TS=TS, CH=CH, U=U, nt=nt),
        grid_spec=grid_spec,
        out_shape=jax.ShapeDtypeStruct((T, D_MODEL), F32),
        compiler_params=_params("arbitrary"),
        name="moe_combine",
    )(chunk_blk, chunk_row0, base, meta, x, g_post.reshape(1, -1), ys)


def _moe(x, g_pre, w_router, w_gate, w_up, w_down, g_post):
    T = x.shape[0]
    TS, TM, CH, U = TS_MOE, TM_MOE, CH_MOE, U_MOE
    nt = T // TS
    NT = (2 * T) // TM + N_EXPERTS
    hb, meta, metat, start, cnt = _router(x, g_pre, w_router)

    i32 = jnp.int32
    counts = cnt[0, :N_EXPERTS].astype(i32)
    start = start[:, 0, :N_EXPERTS].astype(i32)
    start_ext = jnp.concatenate([start, counts[None, :]], axis=0)
    tiles_e = (counts + TM - 1) // TM
    tiles_cum = jnp.cumsum(tiles_e)
    base = (tiles_cum - tiles_e) * TM
    n_tiles = tiles_cum[-1:]
    jj = jnp.arange(NT, dtype=i32)
    tile_expert = jnp.minimum(jnp.sum(jj[:, None] >= tiles_cum[None, :], axis=1), N_EXPERTS - 1).astype(i32)
    live = jj < n_tiles[0]
    qq = jnp.arange(NT * (TM // CH), dtype=i32)
    q_exp = tile_expert[qq // (TM // CH)]
    a = qq * CH - base[q_exp]
    st_e = start_ext[:, q_exp]
    s_lo = jnp.sum(st_e[1:] <= a[None, :], axis=0).astype(i32)
    s_hi = (jnp.sum(st_e[:-1] < (a + CH)[None, :], axis=0) - 1).astype(i32)
    q_live = live[qq // (TM // CH)]
    s_lo = jnp.where(q_live, s_lo, 1)
    s_hi = jnp.where(q_live, s_hi, 0)

    xs = _gather(hb, metat, tile_expert, s_lo, s_hi, base.astype(i32), NT)
    ys = _experts(xs, tile_expert, n_tiles.astype(i32), w_gate.astype(BF16), w_up.astype(BF16), w_down.astype(BF16))

    lo = base[None, :] + start_ext[:-1]
    hi = base[None, :] + start_ext[1:]
    n_e = jnp.where(hi > lo, (hi - 1) // CH - lo // CH + 1, 0)
    cum = jnp.cumsum(n_e, axis=1)
    uu = jnp.arange(U, dtype=i32)
    e_u = jnp.minimum(jnp.sum(uu[None, :, None] >= cum[:, None, :], axis=2), N_EXPERTS - 1)
    first = jnp.take_along_axis(lo // CH, e_u, axis=1)
    skipped = jnp.take_along_axis(cum - n_e, e_u, axis=1)
    chunk_blk = jnp.clip(first + (uu[None, :] - skipped), 0, (NT * TM) // CH - 1).astype(i32)
    used = uu[None, :] < cum[:, -1:]
    chunk_blk = jnp.where(used, chunk_blk, chunk_blk[:, :1])
    chunk_row0 = jnp.where(used, chunk_blk * CH, -(NT * TM)).astype(i32)
    return _combine(x, meta, ys, chunk_blk.reshape(-1), chunk_row0.reshape(-1), base.astype(i32), g_post)


def kernel(x, positions, w_in, a_sinks, g_cq, w_uq, g_ckv, w_ukv, c_w1_k, c_w2_k, c_pe_k, c_w1_v, c_w2_v, c_pe_v,
           g_oa, g_ob, g_oc, w_o, g_pre_mix, g_post_mix, g_pre_ffn, g_post_ffn, ffn_w_gate, ffn_w_up, ffn_w_down,
           moe_router, moe_w_gate, moe_w_up, moe_w_down):
    B, S, _ = x.shape
    T = B * S
    depth = w_in.shape[0]
    a_slopes = _alibi_slopes(A_HEADS)
    c_slopes = _alibi_slopes(C_HEADS)
    xf = x.reshape(T, D_MODEL).astype(F32)
    cos, sin = _rope_tables(positions)
    att_scale = HEAD_DIM ** -0.5
    flat = lambda w: w.reshape(-1, w.shape[-1])
    bf16_w = {}
    w_p_all, w_vt_all = jax.vmap(_build_w_in)(w_in)
    mla_w_all = jax.vmap(_build_mla_weights)(w_uq, w_ukv)
    cmp_k_all = jax.vmap(_build_compress_weights)(c_w1_k, c_w2_k, c_pe_k)
    cmp_v_all = jax.vmap(_build_compress_weights)(c_w1_v, c_w2_v, c_pe_v)
    for l in range(depth):
        P, VT, qB, kB, vtB, k_cmp, v_cmp = _inproj(xf, g_pre_mix[l].reshape(1, -1), w_p_all[l], w_vt_all[l], S, cos, sin,
                                     g_cq[l], g_ckv[l], tuple(w[l] for w in mla_w_all))
        o_a = _banded_attn(P, VT, B, S, name="swa_attn", NSUB=NSUB_A, G=A_KV_HEADS, R=A_HEADS // A_KV_HEADS,
                           KD=KD_A, q_col=P_AQ, k_col=P_KA, vt_slab=VT_A, W=A_WINDOW, slopes=a_slopes,
                           scale=att_scale, sinks=a_sinks[l])
        if l % 2 == 0:
            side = (ffn_w_gate[l // 2], ffn_w_up[l // 2], ffn_w_down[l // 2])
        else:
            side = (flat(moe_w_down[l // 2]),)
        o_b, *cast = _flash_attn(qB, kB, vtB, B, S, name="mla_attn", TQ=TQ_MLA, TK=TK_MLA, G=B_HEADS, R=1, KD=LANES,
                                 q_col=0, q_width=B_HEADS * LANES, k_col=0, vt_slab=0, side=side)
        if l % 2 == 0:
            bf16_w["ffn", l] = cast
        else:
            bf16_w["down", l] = cast[0]
        kc, vct = _compress(k_cmp, v_cmp, B, S, tuple(w[l] for w in cmp_k_all), tuple(w[l] for w in cmp_v_all))
        o_cmp, selb, pick_cnt = _cmp_sel(P, kc, vct, B, S, c_slopes)
        c_args = dict(TQ=TQ_ATT, G=C_KV_HEADS, R=C_HEADS // C_KV_HEADS, q_col=P_CQ, q_width=C_HEADS * HEAD_DIM,
                      slopes=c_slopes, scale=att_scale, gate=P)
        side = ()
        if l % 2 == 0 and l + 1 < depth:
            side = (flat(moe_w_gate[(l + 1) // 2]),)
        elif l % 2 == 1:
            side = (flat(moe_w_up[l // 2]),)
        o_slc, *cast = _flash_attn(P, P, VT, B, S, name="nsa_selected", TK=TK_SEL, KD=KD_SEL, k_col=P_KSEL,
                                   vt_slab=VT_SEL, selb=selb, tiles=_selected_tile_lists(pick_cnt, B, S),
                                   gate_branch=1, side=side, **c_args)
        if l % 2 == 0 and l + 1 < depth:
            bf16_w["gate", l + 1] = cast[0]
        elif l % 2 == 1:
            bf16_w["up", l] = cast[0]
        o_win = _banded_attn(P, VT, B, S, name="nsa_window", NSUB=NSUB_WIN, G=C_KV_HEADS, R=C_HEADS // C_KV_HEADS,
                             KD=KD_WIN, q_col=P_CQ, k_col=P_KWIN, vt_slab=VT_WIN, W=C_WINDOW, slopes=c_slopes,
                             scale=att_scale, gate_branch=2)
        xf = _outproj(xf, o_a, o_b, o_cmp, o_slc, o_win, g_oa[l], g_ob[l], g_oc[l], w_o[l], g_post_mix[l])
        if l % 2 == 0:
            wg, wu, wd = bf16_w["ffn", l]
            xf = _ffn(xf, g_pre_ffn[l], wg, wu, wd, g_post_ffn[l])
        else:
            e = l // 2
            wg = bf16_w["gate", l].reshape(moe_w_gate[e].shape)
            wu = bf16_w["up", l].reshape(moe_w_up[e].shape)
            wd = bf16_w["down", l].reshape(moe_w_down[e].shape)
            xf = _moe(xf, g_pre_ffn[l], moe_router[e], wg, wu, wd, g_post_ffn[l])
    return xf.reshape(B, S, D_MODEL)
```

```python
import functools

import numpy as np
import jax
import jax.numpy as jnp
from jax import lax
from jax.experimental import pallas as pl
from jax.experimental.pallas import tpu as pltpu

F32 = jnp.float32
BF16 = jnp.bfloat16

D_MODEL = 1024
HEAD_DIM = 64
NORM_EPS = 1e-6
NEG_INF = -1e30
ROPE_THETA = 10000.0
LOG2E = 1.4426950408889634

A_HEADS, A_KV_HEADS, A_WINDOW = 4, 2, 128
B_HEADS, B_Q_RANK, B_KV_RANK, B_NOPE, B_ROPE = 4, 256, 128, 64, 32
C_HEADS, C_KV_HEADS, C_WINDOW = 8, 2, 512
C_CMP_BLOCK, C_CMP_STRIDE, C_CMP_HIDDEN = 32, 16, 128
C_SEL_BLOCK, C_SEL_TOP = 64, 16
N_EXPERTS = 8

LANES = 128
VMEM_LIMIT = 48 * 1024 * 1024

P_CQ, P_AQ, P_BCQ = 0, 512, 768
P_BCKV, P_CKC, P_CVC, P_KR, P_KRROT, P_GATE = 1024, 1152, 1280, 1408, 1536, 1664
P_KSEL, KD_SEL = 1792, 256
P_KWIN, KD_WIN = 2304, 128
P_KA, KD_A = 2560, 128
P_WIDTH = 2816
V_ROWS = 80
VT_SEL, VT_WIN, VT_A = 0, 2, 4
VT_SLABS = 6
MASK_BIAS = -1e30
M_INIT = -1e29

TM_PROJ = 512
TM_OUT = 1024
TQ_ATT = 128
TK_BAND = 128
NSUB_A = 4
NSUB_WIN = 4
TQ_MLA = 512
TK_MLA = 256
TK_SEL = 256
TQ_SEL = 256
TM_FFN = 512
TF_FFN = 2048
TF_MOE = 1792
TS_MOE = 512
TM_MOE = 512
CH_MOE = 128
U_MOE = 24


def _alibi_slopes(n):
    return [float(np.float32(2.0 ** (-8.0 * (i + 1) / n))) for i in range(n)]


def _split_bf16(x):
    parts = []
    for _ in range(3):
        p = float(np.asarray(x, np.float32).astype(jnp.bfloat16))
        parts.append(p)
        x = x - p
    return parts


def _dot(a, b):
    return jnp.dot(a, b, preferred_element_type=F32)


def _dot_nt(a, b):
    return lax.dot_general(a, b, (((1,), (1,)), ((), ())), preferred_element_type=F32)


def _rms(x, g):
    return x * lax.rsqrt(jnp.mean(x * x, axis=-1, keepdims=True) + NORM_EPS) * g


def _sigmoid(x):
    return 1.0 / (1.0 + jnp.exp(-x))


def _params(*sem):
    return pltpu.CompilerParams(dimension_semantics=sem, vmem_limit_bytes=VMEM_LIMIT)


def _inproj_kernel(x_ref, g_ref, w_ref, wvt_ref, vone_ref, kc_ref, cos_ref, sin_ref, gq_ref, gkv_ref,
                   wq_ref, wqr_ref, wk_ref, wvtb_ref, voneb_ref, o_ref, vt_ref, qb_out, kb_out, vtb_out,
                   kcmp_out, vcmp_out):
    h = _rms(x_ref[...], g_ref[...]).astype(BF16)
    for n in range(P_WIDTH // 256):
        o_ref[:, n * 256:(n + 1) * 256] = _dot(h, w_ref[:, n * 256:(n + 1) * 256]).astype(BF16)
    kcmp_out[...] = o_ref[:, P_CKC:P_CKC + LANES]
    vcmp_out[...] = o_ref[:, P_CVC:P_CVC + LANES]
    _mla_prep(o_ref[:, P_BCQ:P_BCQ + B_Q_RANK], o_ref[:, P_BCKV:P_BCKV + B_KV_RANK], o_ref[:, P_KR:P_KR + LANES],
              o_ref[:, P_KRROT:P_KRROT + LANES], cos_ref[...], sin_ref[...], gq_ref, gkv_ref,
              wq_ref, wqr_ref, wk_ref, wvtb_ref, voneb_ref, qb_out, kb_out, vtb_out)
    for g in range(C_KV_HEADS):
        o_ref[:, P_KSEL + g * KD_SEL + 64:P_KSEL + g * KD_SEL + 192] = kc_ref[:, 0:128]
        o_ref[:, P_KWIN + g * KD_WIN + 64:P_KWIN + (g + 1) * KD_WIN] = kc_ref[:, 128:192]
    for g in range(A_KV_HEADS):
        o_ref[:, P_KA + g * KD_A + 64:P_KA + (g + 1) * KD_A] = kc_ref[:, 128:192]
    vt_ref[...] = (_dot_nt(wvt_ref[...], h) + vone_ref[...]).astype(BF16)


def _key_constants(S):
    s = np.arange(S)
    kc = np.zeros((S, 256), np.float32)
    kc[:, 0:3] = (s % TK_SEL)[:, None]
    kc[s, 64 + s // C_SEL_BLOCK] = 1.0
    kc[:, 128:131] = (s % TK_BAND)[:, None]
    return jnp.asarray(kc, dtype=BF16)


def _inproj(x, g, w, wvt, S, cos, sin, g_cq, g_ckv, mla_w):
    T = x.shape[0]
    TM = TM_PROJ
    assert S // C_SEL_BLOCK <= 64 and TK_SEL <= 256 and TK_BAND <= 256
    vone = np.zeros((VT_SLABS * V_ROWS, 1), np.float32)
    vone[HEAD_DIM::V_ROWS] = 1.0
    QW = B_HEADS * LANES
    VR = B_HEADS * V_ROWS
    voneb = np.zeros((VR, 1), np.float32)
    voneb[HEAD_DIM::V_ROWS] = 1.0
    wq, wqr, wk, wvtb = mla_w
    nblk = S // TM
    full = lambda a: pl.BlockSpec(a.shape, lambda i: (0, 0))
    rows = lambda w_: pl.BlockSpec((TM, w_), lambda i: (i, 0))
    cols = lambda r: pl.BlockSpec((r, TM), lambda i: (0, i))
    consts = [g, w, wvt, jnp.asarray(vone)]
    mla_consts = [g_cq.reshape(1, -1), g_ckv.reshape(1, -1), wq, wqr, wk, wvtb, jnp.asarray(voneb)]
    return pl.pallas_call(
        _inproj_kernel,
        grid=(T // TM,),
        in_specs=[rows(D_MODEL)] + [full(a) for a in consts]
                 + [pl.BlockSpec((TM, 256), lambda i: (i % nblk, 0)), rows(LANES), rows(LANES)]
                 + [full(a) for a in mla_consts],
        out_specs=[rows(P_WIDTH), cols(VT_SLABS * V_ROWS), cols(QW), rows(QW), cols(VR), rows(LANES), rows(LANES)],
        out_shape=[jax.ShapeDtypeStruct((T, P_WIDTH), BF16),
                   jax.ShapeDtypeStruct((VT_SLABS * V_ROWS, T), BF16),
                   jax.ShapeDtypeStruct((QW, T), BF16), jax.ShapeDtypeStruct((T, QW), BF16),
                   jax.ShapeDtypeStruct((VR, T), BF16),
                   jax.ShapeDtypeStruct((T, LANES), BF16), jax.ShapeDtypeStruct((T, LANES), BF16)],
        compiler_params=_params("parallel"),
        name="inproj",
    )(x, *consts, _key_constants(S), cos, sin, *mla_consts)


def _build_w_in(w_in):
    cuts = np.cumsum([0, 256, 128, 128, 256, 128, 32, 512, 128, 128, 128, 128, 128, 128, 24])
    seg = [w_in[:, cuts[i]:cuts[i + 1]] for i in range(14)]
    a_q, a_k, a_v, b_cq, b_ckv, b_kr, c_q, c_kc, c_vc, c_ks, c_vs, c_kw, c_vw, c_g = seg
    z = lambda n: jnp.zeros((w_in.shape[0], n), w_in.dtype)
    half = B_ROPE // 2
    kr_rot = jnp.concatenate([-b_kr[:, half:], b_kr[:, :half]], axis=1)
    cols = [c_q, a_q, b_cq, b_ckv, c_kc, c_vc,
            z(B_NOPE), b_kr, z(LANES - B_NOPE - B_ROPE),
            z(B_NOPE), kr_rot, z(LANES - B_NOPE - B_ROPE),
            c_g, z(LANES - 24)]
    for k, kd in ((c_ks, KD_SEL), (c_kw, KD_WIN), (a_k, KD_A)):
        for g in range(2):
            cols += [k[:, g * HEAD_DIM:(g + 1) * HEAD_DIM], z(kd - HEAD_DIM)]
    w = jnp.concatenate(cols, axis=1).astype(BF16)
    rows = []
    for v in (c_vs, c_vw, a_v):
        for g in range(2):
            rows += [v[:, g * HEAD_DIM:(g + 1) * HEAD_DIM].T, jnp.zeros((V_ROWS - HEAD_DIM, w_in.shape[0]), w_in.dtype)]
    return w, jnp.concatenate(rows, axis=0).astype(BF16)


def _rope_tab_kernel(pos_ref, inv_ref, cos_ref, sin_ref):
    ang = pos_ref[...] * inv_ref[...]
    cos_ref[...] = jnp.cos(ang)
    sin_ref[...] = jnp.sin(ang)


def _rope_tables(positions):
    T = positions.size
    per = LANES // B_ROPE
    inv = (ROPE_THETA ** (-np.arange(0, B_ROPE, 2, dtype=np.float32) / B_ROPE)).astype(np.float32)
    inv128 = np.tile(np.concatenate([inv, inv]), per).reshape(1, LANES)
    pos = jnp.repeat(positions.reshape(T // per, per).astype(F32), B_ROPE, axis=1)
    rows = min(TM_PROJ, T // per)
    cos_c, sin_c = pl.pallas_call(
        _rope_tab_kernel,
        grid=(T // per // rows,),
        in_specs=[pl.BlockSpec((rows, LANES), lambda i: (i, 0)),
                  pl.BlockSpec((1, LANES), lambda i: (0, 0))],
        out_specs=[pl.BlockSpec((rows, LANES), lambda i: (i, 0))] * 2,
        out_shape=[jax.ShapeDtypeStruct((T // per, LANES), F32)] * 2,
        compiler_params=_params("parallel"),
        name="rope_tables",
    )(pos, jnp.asarray(inv128))
    widen = lambda t, fill: jnp.pad(t.reshape(T, B_ROPE), ((0, 0), (B_NOPE, LANES - B_NOPE - B_ROPE)),
                                    constant_values=fill)
    return widen(cos_c, 1.0), widen(sin_c, 0.0)


def _flash_kernel(*refs, TQ, TK, G, R, NKT, slopes, scale, sel, gate_branch, n_side):
    refs = list(refs)
    lst_ref, cnt_ref = (refs.pop(0), refs.pop(0)) if sel else (None, None)
    q_ref = refs.pop(0)
    selb_ref = refs.pop(0) if sel else None
    k_refs = [refs.pop(0) for _ in range(G)]
    v_refs = [refs.pop(0) for _ in range(G)]
    gate_ref = refs.pop(0) if gate_branch is not None else None
    side_in = [refs.pop(0) for _ in range(n_side)]
    o_ref = refs.pop(0)
    side_out = [refs.pop(0) for _ in range(n_side)]
    qa_sc = refs.pop(0) if slopes is not None else None
    m_sc, acc_sc, s_sc, mx_sc = refs
    for src, dst in zip(side_in, side_out):
        dst[...] = src[...].astype(BF16)
    i = pl.program_id(1)
    step = pl.program_id(0) * pl.num_programs(1) + i
    t0 = i * TQ
    RQ = R * TQ
    lane = lax.broadcasted_iota(jnp.int32, (1, RQ), 1)
    qpos = t0 + (lane & (TQ - 1))
    hi = (t0 + TQ + TK - 1) // TK
    if gate_ref is not None:
        gate_t = _sigmoid(gate_ref[...].astype(F32)).T
    slope_vecs = []
    if slopes is not None:
        q_t = q_ref[...].astype(F32).T
        frow = lax.broadcasted_iota(jnp.int32, (HEAD_DIM, TQ), 0)
    for g in range(G):
        if slopes is not None:
            slope_vec = jnp.zeros((1, RQ), F32)
            for r in range(R):
                hh = g * R + r
                cols = slice(r * TQ, (r + 1) * TQ)
                s3 = _split_bf16(slopes[hh] * LOG2E)
                slope_vec = jnp.where(lane // TQ == r, sum(s3), slope_vec)
                qa_sc[g, 0:HEAD_DIM, cols] = (q_t[hh * HEAD_DIM:(hh + 1) * HEAD_DIM, :] * (scale * LOG2E)).astype(BF16)
                qa_sc[g, HEAD_DIM:2 * HEAD_DIM, cols] = jnp.where(
                    frow == 0, s3[0], jnp.where(frow == 1, s3[1], jnp.where(frow == 2, s3[2], 0.0))).astype(BF16)
                if sel:
                    qa_sc[g, 2 * HEAD_DIM:3 * HEAD_DIM, cols] = selb_ref[g * HEAD_DIM:(g + 1) * HEAD_DIM, :]
                    qa_sc[g, 3 * HEAD_DIM:4 * HEAD_DIM, cols] = jnp.zeros((HEAD_DIM, TQ), BF16)
            slope_vecs.append(slope_vec)
        else:
            slope_vecs.append(None)
        m_sc[g] = jnp.full((1, RQ), M_INIT, F32)
        acc_sc[g] = jnp.zeros((V_ROWS, RQ), F32)

    def scores(j, slot):
        ks = pl.multiple_of(jnp.asarray(j, jnp.int32) * TK, TK)
        for g in range(G):
            qa_t = qa_sc[g] if slopes is not None else q_ref[g * LANES:(g + 1) * LANES, :]
            st = _dot(k_refs[g][pl.ds(ks, TK), :], qa_t)
            s_sc[slot, g] = st
            mx_sc[slot, g] = jnp.max(st, axis=0, keepdims=True)

    def tile(j, slot, causal):
        ks = pl.multiple_of(jnp.asarray(j, jnp.int32) * TK, TK)
        for g in range(G):
            st = s_sc[slot, g]
            if causal:
                kpos = ks + lax.broadcasted_iota(jnp.int32, (TK, 1), 0)
                st = jnp.where(kpos <= qpos, st, MASK_BIAS)
            m = m_sc[g]
            mx = jnp.max(st, axis=0, keepdims=True) if causal else mx_sc[slot, g]
            if slopes is not None:
                c = slope_vecs[g] * ks.astype(F32)
                mn = jnp.maximum(m, mx + c)
                p = jnp.exp2(st - (mn - c))
            else:
                mn = jnp.maximum(m, mx)
                p = jnp.exp2(st - mn)
            acc_sc[g] = jnp.exp2(m - mn) * acc_sc[g] + _dot(v_refs[g][:, pl.ds(ks, TK)], p.astype(BF16))
            m_sc[g] = mn

    if sel:
        n = cnt_ref[step]
        jt = lambda t: lst_ref[step * NKT + t]
    else:
        n = hi
        jt = lambda t: t
    n_pair = (n - 1) // 2

    def pair(p, carry):
        t = 2 * p
        scores(jt(t + 1), 1)
        tile(jt(t), 0, False)
        scores(jt(t + 2), 0)
        tile(jt(t + 1), 1, False)
        return carry

    scores(jt(0), 0)
    lax.fori_loop(0, n_pair, pair, 0)

    @pl.when(2 * n_pair == n - 1)
    def _():
        tile(jt(n - 1), 0, True)

    @pl.when(2 * n_pair == n - 2)
    def _():
        scores(jt(n - 1), 1)
        tile(jt(n - 2), 0, TQ > TK)
        tile(jt(n - 1), 1, True)

    pieces = []
    for g in range(G):
        acc = acc_sc[g]
        num = acc[0:HEAD_DIM, :]
        den = acc[HEAD_DIM:HEAD_DIM + 1, :]
        o_t = num * (1.0 / den)
        for r in range(R):
            piece = o_t[:, r * TQ:(r + 1) * TQ]
            if gate_ref is not None:
                c = 3 * (g * R + r) + gate_branch
                piece = piece * gate_t[c:c + 1, :]
            pieces.append(piece)
    o_ref[...] = jnp.concatenate(pieces, axis=0).T.astype(o_ref.dtype)


def _banded_kernel(*refs, TQ, NSUB, G, R, W, slopes, scale, has_sink, gate_branch):
    refs = list(refs)
    q_ref = refs.pop(0)
    k_refs = [refs.pop(0) for _ in range(G)]
    v_refs = [refs.pop(0) for _ in range(G)]
    sink_ref = refs.pop(0) if has_sink else None
    gate_ref = refs.pop(0) if gate_branch is not None else None
    o_ref = refs.pop(0)
    qa_sc, m_sc, acc_sc, s_sc = refs
    TK = TQ
    RQ = R * TQ
    n_st = W // TK + 1
    i = pl.program_id(1)
    lane = lax.broadcasted_iota(jnp.int32, (1, RQ), 1)
    frow = lax.broadcasted_iota(jnp.int32, (HEAD_DIM, TQ), 0)

    def process(sub, check):
        qt = i * NSUB + sub
        t0 = qt * TQ
        rows = slice(sub * TQ, (sub + 1) * TQ)
        qpos = t0 + (lane & (TQ - 1))
        if gate_ref is not None:
            gate_t = _sigmoid(gate_ref[rows, :].astype(F32)).T
        q_t = q_ref[rows, :].astype(F32).T
        slope_vecs = []
        for g in range(G):
            slope_vec = jnp.zeros((1, RQ), F32)
            for r in range(R):
                hh = g * R + r
                cols = slice(r * TQ, (r + 1) * TQ)
                s3 = _split_bf16(slopes[hh] * LOG2E)
                slope_vec = jnp.where(lane // TQ == r, sum(s3), slope_vec)
                qa_sc[sub, g, 0:HEAD_DIM, cols] = (
                    q_t[hh * HEAD_DIM:(hh + 1) * HEAD_DIM, :] * (scale * LOG2E)).astype(BF16)
                qa_sc[sub, g, HEAD_DIM:2 * HEAD_DIM, cols] = jnp.where(
                    frow == 0, s3[0], jnp.where(frow == 1, s3[1], jnp.where(frow == 2, s3[2], 0.0))).astype(BF16)
            slope_vecs.append(slope_vec)
            m_sc[sub, g] = jnp.full((1, RQ), M_INIT, F32)
            acc_sc[sub, g] = jnp.zeros((V_ROWS, RQ), F32)

        def scores(j, slot):
            ks = pl.multiple_of(j * TK, TK)
            for g in range(G):
                s_sc[sub, slot, g] = _dot(k_refs[g][pl.ds(ks, TK), :], qa_sc[sub, g])

        def tile(j, slot, mask, live):
            ks = pl.multiple_of(j * TK, TK)
            for g in range(G):
                st = s_sc[sub, slot, g]
                if mask is not None:
                    kpos = ks + lax.broadcasted_iota(jnp.int32, (TK, 1), 0)
                    ok = kpos <= qpos if mask == "causal" else kpos > qpos - W
                    st = jnp.where(ok, st, MASK_BIAS)
                if live is not None:
                    st = jnp.where(live, st, MASK_BIAS)
                m = m_sc[sub, g]
                c = slope_vecs[g] * ks.astype(F32)
                mn = jnp.maximum(m, jnp.max(st, axis=0, keepdims=True) + c)
                p = jnp.exp2(st - (mn - c))
                acc_sc[sub, g] = (jnp.exp2(m - mn) * acc_sc[sub, g]
                                  + _dot(v_refs[g][:, pl.ds(ks, TK)], p.astype(BF16)))
                m_sc[sub, g] = mn

        js = [qt - n_st + 1 + t for t in range(n_st)]
        jc = [jnp.maximum(j, 0) for j in js] if check else js
        scores(jc[0], 0)
        for t in range(n_st):
            if t + 1 < n_st:
                scores(jc[t + 1], (t + 1) & 1)
            mask = "causal" if t == n_st - 1 else ("window" if t == 0 else None)
            tile(jc[t], t & 1, mask, (js[t] >= 0) if check and t < n_st - 1 else None)

        pieces = []
        for g in range(G):
            acc = acc_sc[sub, g]
            num = acc[0:HEAD_DIM, :]
            den = acc[HEAD_DIM:HEAD_DIM + 1, :]
            if has_sink:
                sink_vec = jnp.zeros((1, RQ), F32)
                for r in range(R):
                    sink_vec = jnp.where(lane // TQ == r, sink_ref[g * R + r] * LOG2E, sink_vec)
                m_true = m_sc[sub, g] - slope_vecs[g] * qpos.astype(F32)
                mf = jnp.maximum(m_true, sink_vec)
                a = jnp.exp2(m_true - mf)
                num = num * a
                den = den * a + jnp.exp2(sink_vec - mf)
            o_t = num * (1.0 / den)
            for r in range(R):
                piece = o_t[:, r * TQ:(r + 1) * TQ]
                if gate_ref is not None:
                    c = 3 * (g * R + r) + gate_branch
                    piece = piece * gate_t[c:c + 1, :]
                pieces.append(piece)
        o_ref[rows, :] = jnp.concatenate(pieces, axis=0).T.astype(o_ref.dtype)

    all_tiles_exist = i * NSUB >= n_st - 1

    @pl.when(all_tiles_exist)
    def _():
        for sub in range(NSUB):
            process(sub, False)

    @pl.when(jnp.logical_not(all_tiles_exist))
    def _():
        for sub in range(NSUB):
            process(sub, True)


def _banded_attn(P, VT, B, S, *, name, NSUB, G, R, KD, q_col, k_col, vt_slab, W, slopes, scale, sinks=None,
                 gate_branch=None):
    T = B * S
    TQ = TQ_ATT
    TB = NSUB * TQ
    nq = S // TB
    RQ = R * TQ
    assert TK_BAND == TQ and W % TQ == 0 and S % TB == 0 and KD == 2 * HEAD_DIM
    QW = G * R * HEAD_DIM
    in_specs = [pl.BlockSpec((TB, QW), lambda b, i: (b * nq + i, q_col // QW))]
    args = [P]
    for g in range(G):
        in_specs.append(pl.BlockSpec((S, KD), lambda b, i, g=g: (b, k_col // KD + g)))
        args.append(P)
    for g in range(G):
        in_specs.append(pl.BlockSpec((V_ROWS, S), lambda b, i, g=g: (vt_slab + g, b)))
        args.append(VT)
    if sinks is not None:
        in_specs.append(pl.BlockSpec(memory_space=pltpu.SMEM))
        args.append(sinks.astype(F32))
    if gate_branch is not None:
        in_specs.append(pl.BlockSpec((TB, LANES), lambda b, i: (b * nq + i, P_GATE // LANES)))
        args.append(P)
    kern = functools.partial(_banded_kernel, TQ=TQ, NSUB=NSUB, G=G, R=R, W=W, slopes=slopes, scale=scale,
                             has_sink=sinks is not None, gate_branch=gate_branch)
    return pl.pallas_call(
        kern,
        grid=(B, nq),
        in_specs=in_specs,
        out_specs=pl.BlockSpec((TB, QW), lambda b, i: (b * nq + i, 0)),
        out_shape=jax.ShapeDtypeStruct((T, QW), BF16),
        scratch_shapes=[pltpu.VMEM((NSUB, G, KD, RQ), BF16), pltpu.VMEM((NSUB, G, 1, RQ), F32),
                        pltpu.VMEM((NSUB, G, V_ROWS, RQ), F32), pltpu.VMEM((NSUB, 2, G, TQ, RQ), F32)],
        compiler_params=_params("parallel", "parallel"),
        name=name,
    )(*args)


def _flash_attn(q, k, vt, B, S, *, name, TQ, TK, G, R, KD, q_col, q_width, k_col, vt_slab, slopes=None, scale=1.0,
                selb=None, tiles=None, gate=None, gate_branch=None, side=()):
    T = B * S
    nq = S // TQ
    RQ = R * TQ
    assert (TK % TQ == 0 or TQ == 2 * TK) and S % TK == 0 and S % TQ == 0
    OW = G * R * HEAD_DIM
    if slopes is not None:
        in_specs = [pl.BlockSpec((TQ, q_width), lambda b, i, *_: (b * nq + i, q_col // q_width))]
    else:
        in_specs = [pl.BlockSpec((G * KD, TQ), lambda b, i, *_: (0, b * nq + i))]
    args = [q]
    if selb is not None:
        in_specs.append(pl.BlockSpec((LANES, TQ), lambda b, i, *_: (0, b * nq + i)))
        args.append(selb)
    for g in range(G):
        in_specs.append(pl.BlockSpec((S, KD), lambda b, i, *_, g=g: (b, k_col // KD + g)))
        args.append(k)
    for g in range(G):
        in_specs.append(pl.BlockSpec((V_ROWS, S), lambda b, i, *_, g=g: (vt_slab + g, b)))
        args.append(vt)
    if gate_branch is not None:
        in_specs.append(pl.BlockSpec((TQ, LANES), lambda b, i, *_: (b * nq + i, P_GATE // LANES)))
        args.append(gate)
    out_specs = [pl.BlockSpec((TQ, OW), lambda b, i, *_: (b * nq + i, 0))]
    out_shape = [jax.ShapeDtypeStruct((T, OW), BF16)]
    for w in side:
        rb = w.shape[0] // (B * nq)
        assert w.shape[0] % (B * nq) == 0 and rb % 16 == 0 and w.shape[1] % LANES == 0
        spec = pl.BlockSpec((rb, w.shape[1]), lambda b, i, *_: (b * nq + i, 0))
        in_specs.append(spec)
        args.append(w)
        out_specs.append(spec)
        out_shape.append(jax.ShapeDtypeStruct(w.shape, BF16))
    scratch = []
    if slopes is not None:
        scratch.append(pltpu.VMEM((G, KD, RQ), BF16))
    scratch += [pltpu.VMEM((G, 1, RQ), F32), pltpu.VMEM((G, V_ROWS, RQ), F32), pltpu.VMEM((2, G, TK, RQ), F32),
                pltpu.VMEM((2, G, 1, RQ), F32)]
    kern = functools.partial(_flash_kernel, TQ=TQ, TK=TK, G=G, R=R, NKT=S // TK, slopes=slopes, scale=scale,
                             sel=selb is not None, gate_branch=gate_branch, n_side=len(side))
    prefetch = list(tiles) if selb is not None else []
    grid_spec = pltpu.PrefetchScalarGridSpec(
        num_scalar_prefetch=len(prefetch),
        grid=(B, nq),
        in_specs=in_specs,
        out_specs=out_specs,
        scratch_shapes=scratch)
    outs = pl.pallas_call(
        kern,
        grid_spec=grid_spec,
        out_shape=out_shape,
        compiler_params=_params("parallel", "parallel"),
        name=name,
    )(*prefetch, *args)
    return list(outs)


def _mla_prep(cq, ckv, kr, krr, cos, sin, gq_ref, gkv_ref, wq_ref, wqr_ref, wk_ref, wvt_ref, vone_ref,
              q_out, k_out, vt_out):
    scale = (B_NOPE + B_ROPE) ** -0.5 * LOG2E
    nq = _rms(cq.astype(F32), gq_ref[...]).astype(BF16)
    nkv = _rms(ckv.astype(F32), gkv_ref[...]).astype(BF16)
    q1 = _dot_nt(wq_ref[...], nq)
    q2 = _dot_nt(wqr_ref[...], nq)
    kk = _dot(nkv, wk_ref[...])
    krope = kr.astype(F32) * cos + krr.astype(F32) * sin
    cos_t = cos.T
    sin_t = sin.T
    for h in range(B_HEADS):
        sl = slice(h * LANES, (h + 1) * LANES)
        q_out[sl, :] = ((q1[sl, :] * cos_t + q2[sl, :] * sin_t) * scale).astype(BF16)
        k_out[:, sl] = (kk[:, sl] + krope).astype(BF16)
    vt_out[...] = (_dot_nt(wvt_ref[...], nkv) + vone_ref[...]).astype(BF16)


def _build_mla_weights(w_uq, w_ukv):
    R1, R2 = w_uq.shape[0], w_ukv.shape[0]
    half = B_ROPE // 2
    hq = B_NOPE + B_ROPE
    wq, wqr, wk, wvt = [], [], [], []
    for h in range(B_HEADS):
        nope = w_uq[:, h * hq:h * hq + B_NOPE]
        rope = w_uq[:, h * hq + B_NOPE:(h + 1) * hq]
        rot = jnp.concatenate([-rope[:, half:], rope[:, :half]], axis=1)
        pad = jnp.zeros((R1, LANES - hq), w_uq.dtype)
        wq += [nope, rope, pad]
        wqr += [jnp.zeros((R1, B_NOPE), w_uq.dtype), rot, pad]
        wk += [w_ukv[:, h * 128:h * 128 + B_NOPE], jnp.zeros((R2, LANES - B_NOPE), w_ukv.dtype)]
        wvt += [w_ukv[:, h * 128 + B_NOPE:(h + 1) * 128].T, jnp.zeros((V_ROWS - HEAD_DIM, R2), w_ukv.dtype)]
    cat = lambda xs: jnp.concatenate(xs, axis=1).astype(BF16)
    return cat(wq).T, cat(wqr).T, cat(wk), jnp.concatenate(wvt, axis=0).astype(BF16)


def _gelu_tanh(x):
    return 0.5 * x * (1.0 + jnp.tanh(np.float32(np.sqrt(2.0 / np.pi)) * (x + 0.044715 * (x * x * x))))


def _compress_kernel(xk_ref, xv_ref, pek_ref, pev_ref, w1k_ref, w1v_ref, w2k_ref, w2v_ref, kc_ref, vct_ref, *, NC):
    for x_ref, pe_ref, w1_ref, w2_ref, is_v in ((xk_ref, pek_ref, w1k_ref, w2k_ref, False),
                                                 (xv_ref, pev_ref, w1v_ref, w2v_ref, True)):
        x = x_ref[...].astype(F32)
        a = _dot((x + pe_ref[0:1, :]).astype(BF16), w1_ref[0])
        b = _dot((x + pe_ref[1:2, :]).astype(BF16), w1_ref[1])
        hid = a + pltpu.roll(b, NC - 1, 0)
        o = _dot(_gelu_tanh(hid).astype(BF16), w2_ref[...])
        n = lax.broadcasted_iota(jnp.int32, o.shape, 0)
        o = jnp.where(n < NC - 1, o, 0.0)
        if is_v:
            vct_ref[...] = o.T.astype(vct_ref.dtype)
        else:
            kc_ref[...] = o.astype(kc_ref.dtype)


def _build_compress_weights(w1, w2, pe):
    half = C_CMP_BLOCK // 2
    G = C_KV_HEADS
    eye = jnp.eye(G, dtype=w1.dtype)
    w1s = [jnp.einsum('lde,hg->lhdge', w1[c * half:(c + 1) * half], eye).reshape(half * G * HEAD_DIM, G * C_CMP_HIDDEN)
           for c in range(2)]
    w1f = jnp.stack(w1s).astype(BF16)
    w2f = jnp.einsum('ed,hg->hegd', w2, eye).reshape(G * C_CMP_HIDDEN, G * HEAD_DIM).astype(BF16)
    pes = [jnp.broadcast_to(pe[c * half:(c + 1) * half, None, :], (half, G, HEAD_DIM)).reshape(1, -1) for c in range(2)]
    pef = jnp.concatenate(pes, axis=0).astype(F32)
    return w1f, w2f, pef


def _compress(k_cmp, v_cmp, B, S, k_weights, v_weights):
    NC = S // C_CMP_STRIDE
    CW = C_CMP_STRIDE * C_KV_HEADS * HEAD_DIM
    xk = k_cmp.reshape(B * NC, CW)
    xv = v_cmp.reshape(B * NC, CW)
    w1kf, w2kf, pekf = k_weights
    w1vf, w2vf, pevf = v_weights
    xspec = pl.BlockSpec((NC, CW), lambda b: (b, 0))
    full2 = lambda a: pl.BlockSpec(a.shape, lambda b: (0, 0))
    full3 = lambda a: pl.BlockSpec(a.shape, lambda b: (0, 0, 0))
    return pl.pallas_call(
        functools.partial(_compress_kernel, NC=NC),
        grid=(B,),
        in_specs=[xspec, xspec, full2(pekf), full2(pevf), full3(w1kf), full3(w1vf), full2(w2kf), full2(w2vf)],
        out_specs=[pl.BlockSpec((NC, LANES), lambda b: (b, 0)), pl.BlockSpec((LANES, NC), lambda b: (b, 0))],
        out_shape=[jax.ShapeDtypeStruct((B * NC, LANES), BF16), jax.ShapeDtypeStruct((B * LANES, NC), BF16)],
        compiler_params=_params("parallel"),
        name="nsa_compress",
    )(xk, xv, pekf, pevf, w1kf, w1vf, w2kf, w2vf)


def _cmp_sel_kernel(q_ref, kc_ref, vct_ref, cfeat_ref, ovt_ref, gate_ref, ocmp_ref, sel_ref, cnt_ref,
                    *, TQ, NC, NSEL, NTOP, slopes):
    i = pl.program_id(1)
    t0 = i * TQ
    R = C_HEADS // C_KV_HEADS
    scale = HEAD_DIM ** -0.5
    gate_t = _sigmoid(gate_ref[...].astype(F32)).T
    last = C_CMP_BLOCK - 1
    tlane = t0 + lax.broadcasted_iota(jnp.int32, (NC, TQ), 1)
    nrow = lax.broadcasted_iota(jnp.int32, (NC, TQ), 0)
    cvalid = (nrow * C_CMP_STRIDE + last <= tlane) & (nrow < NC - 1)
    hasc = ((t0 + lax.broadcasted_iota(jnp.int32, (1, TQ), 1)) >= last).astype(F32)
    jrow = lax.broadcasted_iota(jnp.int32, (NSEL, TQ), 0)
    tl = t0 + lax.broadcasted_iota(jnp.int32, (NSEL, TQ), 1)
    avail = jrow * C_SEL_BLOCK <= tl
    cur = tl // C_SEL_BLOCK
    forced = (jrow == 0) | (jrow == cur) | (jrow == cur - 1)
    q_t = q_ref[...].astype(F32).T
    frow = lax.broadcasted_iota(jnp.int32, (HEAD_DIM, TQ), 0)
    sub8 = lax.broadcasted_iota(jnp.int32, (8, TQ), 0)
    pieces, sel_parts = [], []
    picked = jnp.zeros((NSEL, TQ), F32)
    for g in range(C_KV_HEADS):
        kca = jnp.concatenate([kc_ref[:, g * HEAD_DIM:(g + 1) * HEAD_DIM], cfeat_ref[...]], axis=1)
        vo = jnp.concatenate([vct_ref[g * HEAD_DIM:(g + 1) * HEAD_DIM, :], ovt_ref[...]], axis=0)
        imp = jnp.zeros((HEAD_DIM, TQ), F32)
        for r in range(R):
            hh = g * R + r
            s3 = _split_bf16(slopes[hh] * LOG2E)
            feat = jnp.zeros((HEAD_DIM, TQ), F32)
            for k in range(3):
                feat = jnp.where(frow == k, s3[k] * 256.0, jnp.where(frow == 3 + k, s3[k], feat))
            qa = jnp.concatenate([q_t[hh * HEAD_DIM:(hh + 1) * HEAD_DIM, :] * (scale * LOG2E), feat],
                                 axis=0).astype(BF16)
            st = jnp.where(cvalid, _dot(kca, qa), NEG_INF)
            et = jnp.exp2(st - jnp.max(st, axis=0, keepdims=True))
            inv = hasc / jnp.sum(et, axis=0, keepdims=True)
            res = _dot(vo, et.astype(BF16))
            pieces.append(res[0:HEAD_DIM, :] * (inv * gate_t[3 * hh:3 * hh + 1, :]))
            imp = imp + res[HEAD_DIM:2 * HEAD_DIM, :] * inv
        v = jnp.where(avail, imp[0:NSEL, :], NEG_INF)
        v = jnp.where(forced, -NEG_INF, v)
        ranks = [jnp.zeros((8, TQ), F32) for _ in range(NSEL // 8)]
        for ii in range(NSEL):
            ri = v[ii:ii + 1, :]
            for k in range(NSEL // 8):
                vk = v[8 * k:8 * k + 8, :]
                if 8 * k > ii:
                    inc = jnp.where(ri >= vk, 1.0, 0.0)
                elif 8 * k + 7 < ii:
                    inc = jnp.where(ri > vk, 1.0, 0.0)
                else:
                    inc = jnp.where(sub8 + 8 * k > ii, jnp.where(ri >= vk, 1.0, 0.0), jnp.where(ri > vk, 1.0, 0.0))
                ranks[k] = ranks[k] + inc
        rank = jnp.concatenate(ranks, axis=0)
        sel_parts.append(jnp.where(rank < NTOP, 0.0, MASK_BIAS))
        if NSEL < HEAD_DIM:
            sel_parts.append(jnp.zeros((HEAD_DIM - NSEL, TQ), F32))
        picked = picked + jnp.where(rank < NTOP, 1.0, 0.0)
    sel_ref[...] = jnp.concatenate(sel_parts, axis=0).astype(sel_ref.dtype)
    cnt_ref[...] = _dot_nt(jnp.ones((8, TQ), BF16), picked.astype(BF16))
    ocmp_ref[...] = jnp.concatenate(pieces, axis=0).T.astype(ocmp_ref.dtype)


def _overlap_t(S):
    n_cmp = (S - C_CMP_BLOCK) // C_CMP_STRIDE + 1
    n_sel = S // C_SEL_BLOCK
    NC = S // C_CMP_STRIDE
    cstart = np.arange(n_cmp) * C_CMP_STRIDE
    cend = cstart + C_CMP_BLOCK - 1
    sstart = np.arange(n_sel) * C_SEL_BLOCK
    send = sstart + C_SEL_BLOCK - 1
    ov = np.clip(np.minimum(cend[:, None], send[None, :]) - np.maximum(cstart[:, None], sstart[None, :]) + 1,
                 0, None).astype(np.float32) / C_CMP_STRIDE
    ovt = np.zeros((HEAD_DIM, NC), np.float32)
    ovt[:n_sel, :n_cmp] = ov.T
    cend_all = np.arange(NC) * C_CMP_STRIDE + C_CMP_BLOCK - 1
    cfeat = np.zeros((NC, HEAD_DIM), np.float32)
    cfeat[:, 0:3] = (cend_all // 256)[:, None]
    cfeat[:, 3:6] = (cend_all % 256)[:, None]
    return jnp.asarray(ovt, dtype=BF16), jnp.asarray(cfeat, dtype=BF16)


def _cmp_sel(P, kc, vct, B, S, slopes):
    T = B * S
    TQ = TQ_ATT
    nq = S // TQ
    NC = S // C_CMP_STRIDE
    NSEL = S // C_SEL_BLOCK
    assert NSEL <= HEAD_DIM and NSEL % 8 == 0 and S <= 256 * 256
    QW = C_HEADS * HEAD_DIM
    ovt, cfeat = _overlap_t(S)
    kern = functools.partial(_cmp_sel_kernel, TQ=TQ, NC=NC, NSEL=NSEL, NTOP=min(C_SEL_TOP, NSEL), slopes=slopes)
    return pl.pallas_call(
        kern,
        grid=(B, nq),
        in_specs=[pl.BlockSpec((TQ, QW), lambda b, i: (b * nq + i, P_CQ // QW)),
                  pl.BlockSpec((NC, LANES), lambda b, i: (b, 0)),
                  pl.BlockSpec((LANES, NC), lambda b, i: (b, 0)),
                  pl.BlockSpec((NC, HEAD_DIM), lambda b, i: (0, 0)),
                  pl.BlockSpec((HEAD_DIM, NC), lambda b, i: (0, 0)),
                  pl.BlockSpec((TQ, LANES), lambda b, i: (b * nq + i, P_GATE // LANES))],
        out_specs=[pl.BlockSpec((TQ, QW), lambda b, i: (b * nq + i, 0)),
                   pl.BlockSpec((LANES, TQ), lambda b, i: (0, b * nq + i)),
                   pl.BlockSpec((8, NSEL), lambda b, i: (b * nq + i, 0))],
        out_shape=[jax.ShapeDtypeStruct((T, QW), BF16), jax.ShapeDtypeStruct((LANES, T), BF16),
                   jax.ShapeDtypeStruct((B * nq * 8, NSEL), F32)],
        compiler_params=_params("parallel", "parallel"),
        name="nsa_cmp_select",
    )(P, kc, vct, cfeat, ovt, P)


def _selected_tile_lists(cnt, B, S):
    nq = S // TQ_SEL
    nkt = S // TK_SEL
    per = TK_SEL // C_SEL_BLOCK
    blk = jnp.any(cnt.reshape(B * nq, TQ_SEL // TQ_ATT, 8, -1)[:, :, 0, :] > 0.5, axis=1)
    need = jnp.any(blk.reshape(B * nq, nkt, per), axis=-1)
    own = ((jnp.arange(B * nq, dtype=jnp.int32) % nq) * TQ_SEL + TQ_SEL - 1) // TK_SEL
    jj = jnp.arange(nkt, dtype=jnp.int32)[None, :]
    need = (need | (jj == 0) | (jj == own[:, None])) & (jj <= own[:, None])
    order = jnp.argsort(jnp.where(need, jj, nkt + jj), axis=-1).astype(jnp.int32)
    return order.reshape(-1), jnp.sum(need, axis=-1).astype(jnp.int32)


def _outproj_kernel(x_ref, oa_ref, ob_ref, oc1_ref, oc2_ref, oc3_ref, ga_ref, gb_ref, gc_ref, wo_ref, gp_ref, o_ref):
    oa = _rms(oa_ref[...].astype(F32), ga_ref[...]).astype(BF16)
    ob = _rms(ob_ref[...].astype(F32), gb_ref[...]).astype(BF16)
    oc = oc1_ref[...].astype(F32) + oc2_ref[...].astype(F32) + oc3_ref[...].astype(F32)
    oc = _rms(oc, gc_ref[...]).astype(BF16)
    m = _dot(oa, wo_ref[0:256, :]) + _dot(ob, wo_ref[256:512, :]) + _dot(oc, wo_ref[512:1024, :])
    o_ref[...] = x_ref[...] + _rms(m, gp_ref[...])


def _outproj(x, o_a, o_b, o_cmp, o_slc, o_win, g_oa, g_ob, g_oc, w_o, g_post):
    T = x.shape[0]
    TM = TM_OUT
    row = lambda w: pl.BlockSpec((TM, w), lambda i: (i, 0))
    full = lambda r, c: pl.BlockSpec((r, c), lambda i: (0, 0))
    return pl.pallas_call(
        _outproj_kernel,
        grid=(T // TM,),
        in_specs=[row(D_MODEL), row(256), row(256), row(512), row(512), row(512),
                  full(1, 256), full(1, 256), full(1, 512), full(D_MODEL, D_MODEL), full(1, D_MODEL)],
        out_specs=row(D_MODEL),
        out_shape=jax.ShapeDtypeStruct((T, D_MODEL), F32),
        compiler_params=_params("parallel"),
        name="outproj",
    )(x, o_a, o_b, o_cmp, o_slc, o_win, g_oa.reshape(1, -1), g_ob.reshape(1, -1), g_oc.reshape(1, -1),
      w_o.astype(BF16), g_post.reshape(1, -1))


def _swiglu_step(h, wg_ref, wu_ref, wd_ref, acc_ref):
    a = _dot(h, wg_ref[...])
    b = _dot(h, wu_ref[...])
    z = (a * _sigmoid(a) * b).astype(BF16)
    acc_ref[...] += _dot(z, wd_ref[...])


def _ffn_kernel(x_ref, gpre_ref, wg_ref, wu_ref, wd_ref, gpost_ref, o_ref, h_sc, acc_sc, *, nf):
    f = pl.program_id(1)

    @pl.when(f == 0)
    def _():
        h_sc[...] = _rms(x_ref[...], gpre_ref[...]).astype(BF16)
        acc_sc[...] = jnp.zeros_like(acc_sc)

    _swiglu_step(h_sc[...], wg_ref, wu_ref, wd_ref, acc_sc)

    @pl.when(f == nf - 1)
    def _():
        o_ref[...] = x_ref[...] + _rms(acc_sc[...], gpost_ref[...])


def _ffn(x, g_pre, wg, wu, wd, g_post):
    T = x.shape[0]
    TM, TF = TM_FFN, TF_FFN
    FF = wg.shape[1]
    nf = FF // TF
    return pl.pallas_call(
        functools.partial(_ffn_kernel, nf=nf),
        grid=(T // TM, nf),
        in_specs=[pl.BlockSpec((TM, D_MODEL), lambda i, f: (i, 0)),
                  pl.BlockSpec((1, D_MODEL), lambda i, f: (0, 0)),
                  pl.BlockSpec((D_MODEL, TF), lambda i, f: (0, f)),
                  pl.BlockSpec((D_MODEL, TF), lambda i, f: (0, f)),
                  pl.BlockSpec((TF, D_MODEL), lambda i, f: (f, 0)),
                  pl.BlockSpec((1, D_MODEL), lambda i, f: (0, 0))],
        out_specs=pl.BlockSpec((TM, D_MODEL), lambda i, f: (i, 0)),
        out_shape=jax.ShapeDtypeStruct((T, D_MODEL), F32),
        scratch_shapes=[pltpu.VMEM((TM, D_MODEL), BF16), pltpu.VMEM((TM, D_MODEL), F32)],
        compiler_params=_params("parallel", "arbitrary"),
        name="dense_ffn",
    )(x, g_pre.reshape(1, -1), wg.astype(BF16), wu.astype(BF16), wd.astype(BF16), g_post.reshape(1, -1))


def _expert_kernel(te_ref, nt_ref, x_ref, wg_ref, wu_ref, wd_ref, o_ref, acc_sc, *, nf):
    j = pl.program_id(0)
    f = pl.program_id(1)
    live = j < nt_ref[0]

    @pl.when(f == 0)
    def _():
        acc_sc[...] = jnp.zeros_like(acc_sc)

    @pl.when(live)
    def _():
        _swiglu_step(x_ref[...], wg_ref.at[0], wu_ref.at[0], wd_ref.at[0], acc_sc)

    @pl.when(f == nf - 1)
    def _():
        o_ref[...] = acc_sc[...].astype(o_ref.dtype)


def _experts(xs, tile_expert, n_tiles, wg, wu, wd):
    NT = xs.shape[0] // TM_MOE
    TM, TF = TM_MOE, TF_MOE
    assert wg.shape[2] % TF == 0
    FF = wg.shape[2]
    nf = FF // TF
    fidx = lambda j, f, nt: jnp.where(j < nt[0], f, nf - 1)
    grid_spec = pltpu.PrefetchScalarGridSpec(
        num_scalar_prefetch=2,
        grid=(NT, nf),
        in_specs=[pl.BlockSpec((TM, D_MODEL), lambda j, f, te, nt: (j, 0)),
                  pl.BlockSpec((1, D_MODEL, TF), lambda j, f, te, nt: (te[j], 0, fidx(j, f, nt))),
                  pl.BlockSpec((1, D_MODEL, TF), lambda j, f, te, nt: (te[j], 0, fidx(j, f, nt))),
                  pl.BlockSpec((1, TF, D_MODEL), lambda j, f, te, nt: (te[j], fidx(j, f, nt), 0))],
        out_specs=pl.BlockSpec((TM, D_MODEL), lambda j, f, te, nt: (j, 0)),
        scratch_shapes=[pltpu.VMEM((TM, D_MODEL), F32)])
    return pl.pallas_call(
        functools.partial(_expert_kernel, nf=nf),
        grid_spec=grid_spec,
        out_shape=jax.ShapeDtypeStruct(xs.shape, BF16),
        compiler_params=_params("parallel", "arbitrary"),
        name="moe_experts",
    )(tile_expert, n_tiles, xs, wg, wu, wd)


def _router_kernel(x_ref, g_ref, wr_ref, h_out, meta_out, metat_out, start_out, cnt_out, run_sc, *, TS):
    s = pl.program_id(0)

    @pl.when(s == 0)
    def _():
        run_sc[...] = jnp.zeros_like(run_sc)

    h = _rms(x_ref[...], g_ref[...])
    hb = h.astype(BF16)
    h_out[...] = hb
    hl = (h - hb.astype(F32)).astype(BF16)
    w = wr_ref[...]
    wh = w.astype(BF16)
    wl = (w - wh.astype(F32)).astype(BF16)
    logits = _dot(hb, wh) + _dot(hl, wh) + _dot(hb, wl)
    lane = lax.broadcasted_iota(jnp.int32, (TS, LANES), 1).astype(F32)
    logits = jnp.where(lane < N_EXPERTS, logits, NEG_INF)
    m1 = jnp.max(logits, axis=-1, keepdims=True)
    i1 = jnp.min(jnp.where(logits == m1, lane, float(LANES)), axis=-1, keepdims=True)
    rest = jnp.where(lane == i1, NEG_INF, logits)
    m2 = jnp.max(rest, axis=-1, keepdims=True)
    i2 = jnp.min(jnp.where(rest == m2, lane, float(LANES)), axis=-1, keepdims=True)
    ex = jnp.exp(m2 - m1)
    w1 = 1.0 / (1.0 + ex)
    w2 = ex / (1.0 + ex)
    oh1 = lane == i1
    oh2 = lane == i2
    oh = jnp.where(oh1, 1.0, 0.0) + jnp.where(oh2, 1.0, 0.0)
    tri = jnp.where(lax.broadcasted_iota(jnp.int32, (TS, TS), 0) > lax.broadcasted_iota(jnp.int32, (TS, TS), 1),
                    1.0, 0.0).astype(BF16)
    run = run_sc[...]
    before = _dot(tri, oh.astype(BF16)) + run
    r1 = jnp.sum(jnp.where(oh1, before, 0.0), axis=-1, keepdims=True)
    r2 = jnp.sum(jnp.where(oh2, before, 0.0), axis=-1, keepdims=True)
    start_out[0] = run
    run = run + jnp.sum(oh, axis=0, keepdims=True)
    run_sc[...] = run
    cnt_out[...] = run
    meta = jnp.where(lane == 0, i1, jnp.where(lane == 1, i2, jnp.where(lane == 2, r1, jnp.where(
        lane == 3, r2, jnp.where(lane == 4, w1, jnp.where(lane == 5, w2, 0.0))))))
    meta_out[...] = meta
    metat_out[...] = meta.T[0:8, :]


def _router(x, g, w_router):
    T = x.shape[0]
    TS = TS_MOE
    nt = T // TS
    wr = jnp.zeros((D_MODEL, LANES), F32).at[:, :N_EXPERTS].set(w_router.astype(F32))
    return pl.pallas_call(
        functools.partial(_router_kernel, TS=TS),
        grid=(nt,),
        in_specs=[pl.BlockSpec((TS, D_MODEL), lambda s: (s, 0)),
                  pl.BlockSpec((1, D_MODEL), lambda s: (0, 0)),
                  pl.BlockSpec((D_MODEL, LANES), lambda s: (0, 0))],
        out_specs=[pl.BlockSpec((TS, D_MODEL), lambda s: (s, 0)),
                   pl.BlockSpec((TS, LANES), lambda s: (s, 0)),
                   pl.BlockSpec((8, TS), lambda s: (0, s)),
                   pl.BlockSpec((1, 1, LANES), lambda s: (s, 0, 0)),
                   pl.BlockSpec((1, LANES), lambda s: (0, 0))],
        out_shape=[jax.ShapeDtypeStruct((T, D_MODEL), BF16),
                   jax.ShapeDtypeStruct((T, LANES), F32),
                   jax.ShapeDtypeStruct((8, T), F32),
                   jax.ShapeDtypeStruct((nt, 1, LANES), F32),
                   jax.ShapeDtypeStruct((1, LANES), F32)],
        scratch_shapes=[pltpu.VMEM((1, LANES), F32)],
        compiler_params=_params("arbitrary"),
        name="moe_router",
    )(x, g.reshape(1, -1), wr)


def _gather_kernel(te_ref, slo_ref, shi_ref, base_ref, metat_ref, h_ref, xs_out, acc_sc, *, TM, TS, CH):
    j = pl.program_id(0)
    e = te_ref[j]
    ef = e.astype(F32)
    nsub = TM // CH
    nt = h_ref.shape[0] // TS

    def picked_rows(s, want):
        off = pl.multiple_of(s * TS, TS)
        e1 = metat_ref[0:1, pl.ds(off, TS)]
        e2 = metat_ref[1:2, pl.ds(off, TS)]
        r1 = metat_ref[2:3, pl.ds(off, TS)]
        r2 = metat_ref[3:4, pl.ds(off, TS)]
        key = jnp.where(e1 == ef, r1, jnp.where(e2 == ef, r2, -1.0))
        sel = jnp.where(key == want, 1.0, 0.0).astype(BF16)
        return _dot(sel, h_ref[pl.ds(off, TS), :])

    wants = []
    for sub in range(nsub):
        q = j * nsub + sub
        first = j * TM + sub * CH - base_ref[e]
        want = (first + lax.broadcasted_iota(jnp.int32, (CH, 1), 0)).astype(F32)
        lo = jnp.minimum(slo_ref[q], nt - 1)
        second = jnp.where(lo + 1 < nt, want, -2.0)
        acc_sc[sub] = picked_rows(lo, want) + picked_rows(jnp.minimum(lo + 1, nt - 1), second)
        wants.append(want)
    for sub in range(nsub):
        q = j * nsub + sub

        def body(s, _, sub=sub):
            acc_sc[sub] += picked_rows(s, wants[sub])
            return 0

        lax.fori_loop(slo_ref[q] + 2, shi_ref[q] + 1, body, 0)
    for sub in range(nsub):
        xs_out[sub * CH:(sub + 1) * CH, :] = acc_sc[sub].astype(xs_out.dtype)


def _gather(hb, metat, tile_expert, s_lo, s_hi, base, NT):
    TM, TS = TM_MOE, TS_MOE
    assert hb.shape[0] % TS == 0
    grid_spec = pltpu.PrefetchScalarGridSpec(
        num_scalar_prefetch=4,
        grid=(NT,),
        in_specs=[pl.BlockSpec(memory_space=pltpu.VMEM), pl.BlockSpec(memory_space=pltpu.VMEM)],
        out_specs=pl.BlockSpec((TM, D_MODEL), lambda j, *_: (j, 0)),
        scratch_shapes=[pltpu.VMEM((TM // CH_MOE, CH_MOE, D_MODEL), F32)])
    return pl.pallas_call(
        functools.partial(_gather_kernel, TM=TM, TS=TS, CH=CH_MOE),
        grid_spec=grid_spec,
        out_shape=jax.ShapeDtypeStruct((NT * TM, D_MODEL), BF16),
        compiler_params=_params("arbitrary"),
        name="moe_gather",
    )(tile_expert, s_lo, s_hi, base, metat, hb)


def _combine_kernel(blk_ref, row0_ref, base_ref, meta_ref, x_ref, gpost_ref, ys_hbm, o_ref, buf, sem, selw_sc,
                    *, TS, CH, U, nt):
    s = pl.program_id(0)
    slot = s % 2

    def chunk_copies(step, slot_):
        out = []
        for u in range(U):
            r0 = pl.multiple_of(blk_ref[step * U + u] * CH, CH)
            out.append(pltpu.make_async_copy(ys_hbm.at[pl.ds(r0, CH), :], buf.at[slot_, pl.ds(u * CH, CH), :],
                                             sem.at[slot_]))
        return out

    @pl.when(s == 0)
    def _():
        for c in chunk_copies(0, 0):
            c.start()

    @pl.when(s + 1 < nt)
    def _():
        for c in chunk_copies(s + 1, 1 - slot):
            c.start()

    meta = meta_ref[...]
    e1, e2 = meta[:, 0:1], meta[:, 1:2]
    w1, w2 = meta[:, 4:5], meta[:, 5:6]
    b1 = jnp.zeros((TS, 1), F32)
    b2 = jnp.zeros((TS, 1), F32)
    for e in range(N_EXPERTS):
        be = base_ref[e].astype(F32)
        b1 = jnp.where(e1 == float(e), be, b1)
        b2 = jnp.where(e2 == float(e), be, b2)
    pos1 = b1 + meta[:, 2:3]
    pos2 = b2 + meta[:, 3:4]

    lane = lax.broadcasted_iota(jnp.int32, (1, CH), 1)
    for u in range(U):
        rows = (row0_ref[s * U + u] + lane).astype(F32)
        selw = jnp.where(pos1 == rows, w1, 0.0) + jnp.where(pos2 == rows, w2, 0.0)
        selw_sc[:, u * CH:(u + 1) * CH] = selw.astype(BF16)
    for c in chunk_copies(s, slot):
        c.wait()
    y = _dot(selw_sc[...], buf[slot])
    o_ref[...] = x_ref[...] + _rms(y, gpost_ref[...])


def _combine(x, meta, ys, chunk_blk, chunk_row0, base, g_post):
    T = x.shape[0]
    TS, CH, U = TS_MOE, CH_MOE, U_MOE
    nt = T // TS
    grid_spec = pltpu.PrefetchScalarGridSpec(
        num_scalar_prefetch=3,
        grid=(nt,),
        in_specs=[pl.BlockSpec((TS, LANES), lambda s, *_: (s, 0)),
                  pl.BlockSpec((TS, D_MODEL), lambda s, *_: (s, 0)),
                  pl.BlockSpec((1, D_MODEL), lambda s, *_: (0, 0)),
                  pl.BlockSpec(memory_space=pl.ANY)],
        out_specs=pl.BlockSpec((TS, D_MODEL), lambda s, *_: (s, 0)),
        scratch_shapes=[pltpu.VMEM((2, U * CH, D_MODEL), BF16), pltpu.SemaphoreType.DMA((2,)),
                        pltpu.VMEM((TS, U * CH), BF16)])
    return pl.pallas_call(
        functools.partial(_combine_kernel, TS=TS, CH=CH, U=U, nt=nt),
        grid_spec=grid_spec,
        out_shape=jax.ShapeDtypeStruct((T, D_MODEL), F32),
        compiler_params=_params("arbitrary"),
        name="moe_combine",
    )(chunk_blk, chunk_row0, base, meta, x, g_post.reshape(1, -1), ys)


def _moe(x, g_pre, w_router, w_gate, w_up, w_down, g_post):
    T = x.shape[0]
    TS, TM, CH, U = TS_MOE, TM_MOE, CH_MOE, U_MOE
    nt = T // TS
    NT = (2 * T) // TM + N_EXPERTS
    hb, meta, metat, start, cnt = _router(x, g_pre, w_router)

    i32 = jnp.int32
    counts = cnt[0, :N_EXPERTS].astype(i32)
    start = start[:, 0, :N_EXPERTS].astype(i32)
    start_ext = jnp.concatenate([start, counts[None, :]], axis=0)
    tiles_e = (counts + TM - 1) // TM
    tiles_cum = jnp.cumsum(tiles_e)
    base = (tiles_cum - tiles_e) * TM
    n_tiles = tiles_cum[-1:]
    jj = jnp.arange(NT, dtype=i32)
    tile_expert = jnp.minimum(jnp.sum(jj[:, None] >= tiles_cum[None, :], axis=1), N_EXPERTS - 1).astype(i32)
    live = jj < n_tiles[0]
    qq = jnp.arange(NT * (TM // CH), dtype=i32)
    q_exp = tile_expert[qq // (TM // CH)]
    a = qq * CH - base[q_exp]
    st_e = start_ext[:, q_exp]
    s_lo = jnp.sum(st_e[1:] <= a[None, :], axis=0).astype(i32)
    s_hi = (jnp.sum(st_e[:-1] < (a + CH)[None, :], axis=0) - 1).astype(i32)
    q_live = live[qq // (TM // CH)]
    s_lo = jnp.where(q_live, s_lo, 1)
    s_hi = jnp.where(q_live, s_hi, 0)

    xs = _gather(hb, metat, tile_expert, s_lo, s_hi, base.astype(i32), NT)
    ys = _experts(xs, tile_expert, n_tiles.astype(i32), w_gate.astype(BF16), w_up.astype(BF16), w_down.astype(BF16))

    lo = base[None, :] + start_ext[:-1]
    hi = base[None, :] + start_ext[1:]
    n_e = jnp.where(hi > lo, (hi - 1) // CH - lo // CH + 1, 0)
    cum = jnp.cumsum(n_e, axis=1)
    uu = jnp.arange(U, dtype=i32)
    e_u = jnp.minimum(jnp.sum(uu[None, :, None] >= cum[:, None, :], axis=2), N_EXPERTS - 1)
    first = jnp.take_along_axis(lo // CH, e_u, axis=1)
    skipped = jnp.take_along_axis(cum - n_e, e_u, axis=1)
    chunk_blk = jnp.clip(first + (uu[None, :] - skipped), 0, (NT * TM) // CH - 1).astype(i32)
    used = uu[None, :] < cum[:, -1:]
    chunk_blk = jnp.where(used, chunk_blk, chunk_blk[:, :1])
    chunk_row0 = jnp.where(used, chunk_blk * CH, -(NT * TM)).astype(i32)
    return _combine(x, meta, ys, chunk_blk.reshape(-1), chunk_row0.reshape(-1), base.astype(i32), g_post)


def kernel(x, positions, w_in, a_sinks, g_cq, w_uq, g_ckv, w_ukv, c_w1_k, c_w2_k, c_pe_k, c_w1_v, c_w2_v, c_pe_v,
           g_oa, g_ob, g_oc, w_o, g_pre_mix, g_post_mix, g_pre_ffn, g_post_ffn, ffn_w_gate, ffn_w_up, ffn_w_down,
           moe_router, moe_w_gate, moe_w_up, moe_w_down):
    B, S, _ = x.shape
    T = B * S
    depth = w_in.shape[0]
    a_slopes = _alibi_slopes(A_HEADS)
    c_slopes = _alibi_slopes(C_HEADS)
    xf = x.reshape(T, D_MODEL).astype(F32)
    cos, sin = _rope_tables(positions)
    att_scale = HEAD_DIM ** -0.5
    flat = lambda w: w.reshape(-1, w.shape[-1])
    bf16_w = {}
    w_p_all, w_vt_all = jax.vmap(_build_w_in)(w_in)
    mla_w_all = jax.vmap(_build_mla_weights)(w_uq, w_ukv)
    cmp_k_all = jax.vmap(_build_compress_weights)(c_w1_k, c_w2_k, c_pe_k)
    cmp_v_all = jax.vmap(_build_compress_weights)(c_w1_v, c_w2_v, c_pe_v)
    for l in range(depth):
        P, VT, qB, kB, vtB, k_cmp, v_cmp = _inproj(xf, g_pre_mix[l].reshape(1, -1), w_p_all[l], w_vt_all[l], S, cos, sin,
                                     g_cq[l], g_ckv[l], tuple(w[l] for w in mla_w_all))
        o_a = _banded_attn(P, VT, B, S, name="swa_attn", NSUB=NSUB_A, G=A_KV_HEADS, R=A_HEADS // A_KV_HEADS,
                           KD=KD_A, q_col=P_AQ, k_col=P_KA, vt_slab=VT_A, W=A_WINDOW, slopes=a_slopes,
                           scale=att_scale, sinks=a_sinks[l])
        if l % 2 == 0:
            side = (ffn_w_gate[l // 2], ffn_w_up[l // 2], ffn_w_down[l // 2])
        else:
            side = (flat(moe_w_down[l // 2]),)
        o_b, *cast = _flash_attn(qB, kB, vtB, B, S, name="mla_attn", TQ=TQ_MLA, TK=TK_MLA, G=B_HEADS, R=1, KD=LANES,
                                 q_col=0, q_width=B_HEADS * LANES, k_col=0, vt_slab=0, side=side)
        if l % 2 == 0:
            bf16_w["ffn", l] = cast
        else:
            bf16_w["down", l] = cast[0]
        kc, vct = _compress(k_cmp, v_cmp, B, S, tuple(w[l] for w in cmp_k_all), tuple(w[l] for w in cmp_v_all))
        o_cmp, selb, pick_cnt = _cmp_sel(P, kc, vct, B, S, c_slopes)
        c_args = dict(TQ=TQ_ATT, G=C_KV_HEADS, R=C_HEADS // C_KV_HEADS, q_col=P_CQ, q_width=C_HEADS * HEAD_DIM,
                      slopes=c_slopes, scale=att_scale, gate=P)
        side = ()
        if l % 2 == 0 and l + 1 < depth:
            side = (flat(moe_w_gate[(l + 1) // 2]),)
        elif l % 2 == 1:
            side = (flat(moe_w_up[l // 2]),)
        o_slc, *cast = _flash_attn(P, P, VT, B, S, name="nsa_selected", TK=TK_SEL, KD=KD_SEL, k_col=P_KSEL,
                                   vt_slab=VT_SEL, selb=selb, tiles=_selected_tile_lists(pick_cnt, B, S),
                                   gate_branch=1, side=side, **dict(c_args, TQ=TQ_SEL))
        if l % 2 == 0 and l + 1 < depth:
            bf16_w["gate", l + 1] = cast[0]
        elif l % 2 == 1:
            bf16_w["up", l] = cast[0]
        o_win = _banded_attn(P, VT, B, S, name="nsa_window", NSUB=NSUB_WIN, G=C_KV_HEADS, R=C_HEADS // C_KV_HEADS,
                             KD=KD_WIN, q_col=P_CQ, k_col=P_KWIN, vt_slab=VT_WIN, W=C_WINDOW, slopes=c_slopes,
                             scale=att_scale, gate_branch=2)
        xf = _outproj(xf, o_a, o_b, o_cmp, o_slc, o_win, g_oa[l], g_ob[l], g_oc[l], w_o[l], g_post_mix[l])
        if l % 2 == 0:
            wg, wu, wd = bf16_w["ffn", l]
            xf = _ffn(xf, g_pre_ffn[l], wg, wu, wd, g_post_ffn[l])
        else:
            e = l // 2
            wg = bf16_w["gate", l].reshape(moe_w_gate[e].shape)
            wu = bf16_w["up", l].reshape(moe_w_up[e].shape)
            wd = bf16_w["down", l].reshape(moe_w_down[e].shape)
            xf = _moe(xf, g_pre_ffn[l], moe_router[e], wg, wu, wd, g_post_ffn[l])
    return xf.reshape(B, S, D_MODEL)
```

```python
import functools

import numpy as np
import jax
import jax.numpy as jnp
from jax import lax
from jax.experimental import pallas as pl
from jax.experimental.pallas import tpu as pltpu

F32 = jnp.float32
BF16 = jnp.bfloat16

D_MODEL = 1024
HEAD_DIM = 64
NORM_EPS = 1e-6
NEG_INF = -1e30
ROPE_THETA = 10000.0
LOG2E = 1.4426950408889634

A_HEADS, A_KV_HEADS, A_WINDOW = 4, 2, 128
B_HEADS, B_Q_RANK, B_KV_RANK, B_NOPE, B_ROPE = 4, 256, 128, 64, 32
C_HEADS, C_KV_HEADS, C_WINDOW = 8, 2, 512
C_CMP_BLOCK, C_CMP_STRIDE, C_CMP_HIDDEN = 32, 16, 128
C_SEL_BLOCK, C_SEL_TOP = 64, 16
N_EXPERTS = 8

LANES = 128
VMEM_LIMIT = 48 * 1024 * 1024

P_CQ, P_AQ, P_BCQ = 0, 512, 768
P_BCKV, P_CKC, P_CVC, P_KR, P_KRROT, P_GATE = 1024, 1152, 1280, 1408, 1536, 1664
P_KSEL, KD_SEL = 1792, 256
P_KWIN, KD_WIN = 2304, 128
P_KA, KD_A = 2560, 128
P_WIDTH = 2816
V_ROWS = 80
VT_SEL, VT_WIN, VT_A = 0, 2, 4
VT_SLABS = 6
MASK_BIAS = -1e30
M_INIT = -1e29

TM_PROJ = 512
TM_OUT = 1024
TQ_ATT = 128
TK_BAND = 128
NSUB_A = 8
NSUB_WIN = 4
TQ_MLA = 512
TK_MLA = 256
TK_SEL = 256
TQ_SEL = 256
TM_FFN = 512
TF_FFN = 2048
TF_MOE = 1792
TS_MOE = 512
TM_MOE = 512
CH_MOE = 128
U_MOE = 24


def _alibi_slopes(n):
    return [float(np.float32(2.0 ** (-8.0 * (i + 1) / n))) for i in range(n)]


def _split_bf16(x):
    parts = []
    for _ in range(3):
        p = float(np.asarray(x, np.float32).astype(jnp.bfloat16))
        parts.append(p)
        x = x - p
    return parts


def _dot(a, b):
    return jnp.dot(a, b, preferred_element_type=F32)


def _dot_nt(a, b):
    return lax.dot_general(a, b, (((1,), (1,)), ((), ())), preferred_element_type=F32)


def _rms(x, g):
    return x * lax.rsqrt(jnp.mean(x * x, axis=-1, keepdims=True) + NORM_EPS) * g


def _sigmoid(x):
    return 1.0 / (1.0 + jnp.exp(-x))


def _params(*sem):
    return pltpu.CompilerParams(dimension_semantics=sem, vmem_limit_bytes=VMEM_LIMIT)


def _inproj_kernel(x_ref, g_ref, w_ref, wvt_ref, vone_ref, kc_ref, cos_ref, sin_ref, gq_ref, gkv_ref,
                   wq_ref, wqr_ref, wk_ref, wvtb_ref, voneb_ref, o_ref, vt_ref, qb_out, kb_out, vtb_out,
                   kcmp_out, vcmp_out):
    h = _rms(x_ref[...], g_ref[...]).astype(BF16)
    for n in range(P_WIDTH // 256):
        o_ref[:, n * 256:(n + 1) * 256] = _dot(h, w_ref[:, n * 256:(n + 1) * 256]).astype(BF16)
    kcmp_out[...] = o_ref[:, P_CKC:P_CKC + LANES]
    vcmp_out[...] = o_ref[:, P_CVC:P_CVC + LANES]
    _mla_prep(o_ref[:, P_BCQ:P_BCQ + B_Q_RANK], o_ref[:, P_BCKV:P_BCKV + B_KV_RANK], o_ref[:, P_KR:P_KR + LANES],
              o_ref[:, P_KRROT:P_KRROT + LANES], cos_ref[...], sin_ref[...], gq_ref, gkv_ref,
              wq_ref, wqr_ref, wk_ref, wvtb_ref, voneb_ref, qb_out, kb_out, vtb_out)
    for g in range(C_KV_HEADS):
        o_ref[:, P_KSEL + g * KD_SEL + 64:P_KSEL + g * KD_SEL + 192] = kc_ref[:, 0:128]
        o_ref[:, P_KWIN + g * KD_WIN + 64:P_KWIN + (g + 1) * KD_WIN] = kc_ref[:, 128:192]
    for g in range(A_KV_HEADS):
        o_ref[:, P_KA + g * KD_A + 64:P_KA + (g + 1) * KD_A] = kc_ref[:, 128:192]
    vt_ref[...] = (_dot_nt(wvt_ref[...], h) + vone_ref[...]).astype(BF16)


def _key_constants(S):
    s = np.arange(S)
    kc = np.zeros((S, 256), np.float32)
    kc[:, 0:3] = (s % TK_SEL)[:, None]
    kc[s, 64 + s // C_SEL_BLOCK] = 1.0
    kc[:, 128:131] = (s % TK_BAND)[:, None]
    return jnp.asarray(kc, dtype=BF16)


def _inproj(x, g, w, wvt, S, cos, sin, g_cq, g_ckv, mla_w):
    T = x.shape[0]
    TM = TM_PROJ
    assert S // C_SEL_BLOCK <= 64 and TK_SEL <= 256 and TK_BAND <= 256
    vone = np.zeros((VT_SLABS * V_ROWS, 1), np.float32)
    vone[HEAD_DIM::V_ROWS] = 1.0
    QW = B_HEADS * LANES
    VR = B_HEADS * V_ROWS
    voneb = np.zeros((VR, 1), np.float32)
    voneb[HEAD_DIM::V_ROWS] = 1.0
    wq, wqr, wk, wvtb = mla_w
    nblk = S // TM
    full = lambda a: pl.BlockSpec(a.shape, lambda i: (0, 0))
    rows = lambda w_: pl.BlockSpec((TM, w_), lambda i: (i, 0))
    cols = lambda r: pl.BlockSpec((r, TM), lambda i: (0, i))
    consts = [g, w, wvt, jnp.asarray(vone)]
    mla_consts = [g_cq.reshape(1, -1), g_ckv.reshape(1, -1), wq, wqr, wk, wvtb, jnp.asarray(voneb)]
    return pl.pallas_call(
        _inproj_kernel,
        grid=(T // TM,),
        in_specs=[rows(D_MODEL)] + [full(a) for a in consts]
                 + [pl.BlockSpec((TM, 256), lambda i: (i % nblk, 0)), rows(LANES), rows(LANES)]
                 + [full(a) for a in mla_consts],
        out_specs=[rows(P_WIDTH), cols(VT_SLABS * V_ROWS), cols(QW), rows(QW), cols(VR), rows(LANES), rows(LANES)],
        out_shape=[jax.ShapeDtypeStruct((T, P_WIDTH), BF16),
                   jax.ShapeDtypeStruct((VT_SLABS * V_ROWS, T), BF16),
                   jax.ShapeDtypeStruct((QW, T), BF16), jax.ShapeDtypeStruct((T, QW), BF16),
                   jax.ShapeDtypeStruct((VR, T), BF16),
                   jax.ShapeDtypeStruct((T, LANES), BF16), jax.ShapeDtypeStruct((T, LANES), BF16)],
        compiler_params=_params("parallel"),
        name="inproj",
    )(x, *consts, _key_constants(S), cos, sin, *mla_consts)


def _build_w_in(w_in):
    cuts = np.cumsum([0, 256, 128, 128, 256, 128, 32, 512, 128, 128, 128, 128, 128, 128, 24])
    seg = [w_in[:, cuts[i]:cuts[i + 1]] for i in range(14)]
    a_q, a_k, a_v, b_cq, b_ckv, b_kr, c_q, c_kc, c_vc, c_ks, c_vs, c_kw, c_vw, c_g = seg
    z = lambda n: jnp.zeros((w_in.shape[0], n), w_in.dtype)
    half = B_ROPE // 2
    kr_rot = jnp.concatenate([-b_kr[:, half:], b_kr[:, :half]], axis=1)
    cols = [c_q, a_q, b_cq, b_ckv, c_kc, c_vc,
            z(B_NOPE), b_kr, z(LANES - B_NOPE - B_ROPE),
            z(B_NOPE), kr_rot, z(LANES - B_NOPE - B_ROPE),
            c_g, z(LANES - 24)]
    for k, kd in ((c_ks, KD_SEL), (c_kw, KD_WIN), (a_k, KD_A)):
        for g in range(2):
            cols += [k[:, g * HEAD_DIM:(g + 1) * HEAD_DIM], z(kd - HEAD_DIM)]
    w = jnp.concatenate(cols, axis=1).astype(BF16)
    rows = []
    for v in (c_vs, c_vw, a_v):
        for g in range(2):
            rows += [v[:, g * HEAD_DIM:(g + 1) * HEAD_DIM].T, jnp.zeros((V_ROWS - HEAD_DIM, w_in.shape[0]), w_in.dtype)]
    return w, jnp.concatenate(rows, axis=0).astype(BF16)


def _rope_tab_kernel(pos_ref, inv_ref, cos_ref, sin_ref):
    ang = pos_ref[...] * inv_ref[...]
    cos_ref[...] = jnp.cos(ang)
    sin_ref[...] = jnp.sin(ang)


def _rope_tables(positions):
    T = positions.size
    per = LANES // B_ROPE
    inv = (ROPE_THETA ** (-np.arange(0, B_ROPE, 2, dtype=np.float32) / B_ROPE)).astype(np.float32)
    inv128 = np.tile(np.concatenate([inv, inv]), per).reshape(1, LANES)
    pos = jnp.repeat(positions.reshape(T // per, per).astype(F32), B_ROPE, axis=1)
    rows = min(TM_PROJ, T // per)
    cos_c, sin_c = pl.pallas_call(
        _rope_tab_kernel,
        grid=(T // per // rows,),
        in_specs=[pl.BlockSpec((rows, LANES), lambda i: (i, 0)),
                  pl.BlockSpec((1, LANES), lambda i: (0, 0))],
        out_specs=[pl.BlockSpec((rows, LANES), lambda i: (i, 0))] * 2,
        out_shape=[jax.ShapeDtypeStruct((T // per, LANES), F32)] * 2,
        compiler_params=_params("parallel"),
        name="rope_tables",
    )(pos, jnp.asarray(inv128))
    widen = lambda t, fill: jnp.pad(t.reshape(T, B_ROPE), ((0, 0), (B_NOPE, LANES - B_NOPE - B_ROPE)),
                                    constant_values=fill)
    return widen(cos_c, 1.0), widen(sin_c, 0.0)


def _flash_kernel(*refs, TQ, TK, G, R, NKT, slopes, scale, sel, gate_branch, n_side):
    refs = list(refs)
    lst_ref, cnt_ref = (refs.pop(0), refs.pop(0)) if sel else (None, None)
    q_ref = refs.pop(0)
    selb_ref = refs.pop(0) if sel else None
    k_refs = [refs.pop(0) for _ in range(G)]
    v_refs = [refs.pop(0) for _ in range(G)]
    gate_ref = refs.pop(0) if gate_branch is not None else None
    side_in = [refs.pop(0) for _ in range(n_side)]
    o_ref = refs.pop(0)
    side_out = [refs.pop(0) for _ in range(n_side)]
    qa_sc = refs.pop(0) if slopes is not None else None
    m_sc, acc_sc, s_sc, mx_sc = refs
    for src, dst in zip(side_in, side_out):
        dst[...] = src[...].astype(BF16)
    i = pl.program_id(1)
    step = pl.program_id(0) * pl.num_programs(1) + i
    t0 = i * TQ
    RQ = R * TQ
    lane = lax.broadcasted_iota(jnp.int32, (1, RQ), 1)
    qpos = t0 + (lane & (TQ - 1))
    hi = (t0 + TQ + TK - 1) // TK
    if gate_ref is not None:
        gate_t = _sigmoid(gate_ref[...].astype(F32)).T
    slope_vecs = []
    if slopes is not None:
        q_t = q_ref[...].astype(F32).T
        frow = lax.broadcasted_iota(jnp.int32, (HEAD_DIM, TQ), 0)
    for g in range(G):
        if slopes is not None:
            slope_vec = jnp.zeros((1, RQ), F32)
            for r in range(R):
                hh = g * R + r
                cols = slice(r * TQ, (r + 1) * TQ)
                s3 = _split_bf16(slopes[hh] * LOG2E)
                slope_vec = jnp.where(lane // TQ == r, sum(s3), slope_vec)
                qa_sc[g, 0:HEAD_DIM, cols] = (q_t[hh * HEAD_DIM:(hh + 1) * HEAD_DIM, :] * (scale * LOG2E)).astype(BF16)
                qa_sc[g, HEAD_DIM:2 * HEAD_DIM, cols] = jnp.where(
                    frow == 0, s3[0], jnp.where(frow == 1, s3[1], jnp.where(frow == 2, s3[2], 0.0))).astype(BF16)
                if sel:
                    qa_sc[g, 2 * HEAD_DIM:3 * HEAD_DIM, cols] = selb_ref[g * HEAD_DIM:(g + 1) * HEAD_DIM, :]
                    qa_sc[g, 3 * HEAD_DIM:4 * HEAD_DIM, cols] = jnp.zeros((HEAD_DIM, TQ), BF16)
            slope_vecs.append(slope_vec)
        else:
            slope_vecs.append(None)
        m_sc[g] = jnp.full((1, RQ), M_INIT, F32)
        acc_sc[g] = jnp.zeros((V_ROWS, RQ), F32)

    def scores(j, slot):
        ks = pl.multiple_of(jnp.asarray(j, jnp.int32) * TK, TK)
        for g in range(G):
            qa_t = qa_sc[g] if slopes is not None else q_ref[g * LANES:(g + 1) * LANES, :]
            st = _dot(k_refs[g][pl.ds(ks, TK), :], qa_t)
            s_sc[slot, g] = st
            mx_sc[slot, g] = jnp.max(st, axis=0, keepdims=True)

    def tile(j, slot, causal):
        ks = pl.multiple_of(jnp.asarray(j, jnp.int32) * TK, TK)
        for g in range(G):
            st = s_sc[slot, g]
            if causal:
                kpos = ks + lax.broadcasted_iota(jnp.int32, (TK, 1), 0)
                st = jnp.where(kpos <= qpos, st, MASK_BIAS)
            m = m_sc[g]
            mx = jnp.max(st, axis=0, keepdims=True) if causal else mx_sc[slot, g]
            if slopes is not None:
                c = slope_vecs[g] * ks.astype(F32)
                mn = jnp.maximum(m, mx + c)
                p = jnp.exp2(st - (mn - c))
            else:
                mn = jnp.maximum(m, mx)
                p = jnp.exp2(st - mn)
            acc_sc[g] = jnp.exp2(m - mn) * acc_sc[g] + _dot(v_refs[g][:, pl.ds(ks, TK)], p.astype(BF16))
            m_sc[g] = mn

    if sel:
        n = cnt_ref[step]
        jt = lambda t: lst_ref[step * NKT + t]
    else:
        n = hi
        jt = lambda t: t
    n_pair = (n - 1) // 2

    def pair(p, carry):
        t = 2 * p
        scores(jt(t + 1), 1)
        tile(jt(t), 0, False)
        scores(jt(t + 2), 0)
        tile(jt(t + 1), 1, False)
        return carry

    scores(jt(0), 0)
    lax.fori_loop(0, n_pair, pair, 0)

    @pl.when(2 * n_pair == n - 1)
    def _():
        tile(jt(n - 1), 0, True)

    @pl.when(2 * n_pair == n - 2)
    def _():
        scores(jt(n - 1), 1)
        tile(jt(n - 2), 0, TQ > TK)
        tile(jt(n - 1), 1, True)

    pieces = []
    for g in range(G):
        acc = acc_sc[g]
        num = acc[0:HEAD_DIM, :]
        den = acc[HEAD_DIM:HEAD_DIM + 1, :]
        o_t = num * (1.0 / den)
        for r in range(R):
            piece = o_t[:, r * TQ:(r + 1) * TQ]
            if gate_ref is not None:
                c = 3 * (g * R + r) + gate_branch
                piece = piece * gate_t[c:c + 1, :]
            pieces.append(piece)
    o_ref[...] = jnp.concatenate(pieces, axis=0).T.astype(o_ref.dtype)


def _banded_kernel(*refs, TQ, NSUB, G, R, W, slopes, scale, has_sink, gate_branch):
    refs = list(refs)
    q_ref = refs.pop(0)
    k_refs = [refs.pop(0) for _ in range(G)]
    v_refs = [refs.pop(0) for _ in range(G)]
    sink_ref = refs.pop(0) if has_sink else None
    gate_ref = refs.pop(0) if gate_branch is not None else None
    o_ref = refs.pop(0)
    qa_sc, m_sc, acc_sc, s_sc = refs
    TK = TQ
    RQ = R * TQ
    n_st = W // TK + 1
    i = pl.program_id(1)
    lane = lax.broadcasted_iota(jnp.int32, (1, RQ), 1)
    frow = lax.broadcasted_iota(jnp.int32, (HEAD_DIM, TQ), 0)

    def process(sub, check):
        qt = i * NSUB + sub
        t0 = qt * TQ
        rows = slice(sub * TQ, (sub + 1) * TQ)
        qpos = t0 + (lane & (TQ - 1))
        if gate_ref is not None:
            gate_t = _sigmoid(gate_ref[rows, :].astype(F32)).T
        q_t = q_ref[rows, :].astype(F32).T
        slope_vecs = []
        for g in range(G):
            slope_vec = jnp.zeros((1, RQ), F32)
            for r in range(R):
                hh = g * R + r
                cols = slice(r * TQ, (r + 1) * TQ)
                s3 = _split_bf16(slopes[hh] * LOG2E)
                slope_vec = jnp.where(lane // TQ == r, sum(s3), slope_vec)
                qa_sc[sub, g, 0:HEAD_DIM, cols] = (
                    q_t[hh * HEAD_DIM:(hh + 1) * HEAD_DIM, :] * (scale * LOG2E)).astype(BF16)
                qa_sc[sub, g, HEAD_DIM:2 * HEAD_DIM, cols] = jnp.where(
                    frow == 0, s3[0], jnp.where(frow == 1, s3[1], jnp.where(frow == 2, s3[2], 0.0))).astype(BF16)
            slope_vecs.append(slope_vec)
            m_sc[sub, g] = jnp.full((1, RQ), M_INIT, F32)
            acc_sc[sub, g] = jnp.zeros((V_ROWS, RQ), F32)

        def scores(j, slot):
            ks = pl.multiple_of(j * TK, TK)
            for g in range(G):
                s_sc[sub, slot, g] = _dot(k_refs[g][pl.ds(ks, TK), :], qa_sc[sub, g])

        def tile(j, slot, mask, live):
            ks = pl.multiple_of(j * TK, TK)
            for g in range(G):
                st = s_sc[sub, slot, g]
                if mask is not None:
                    kpos = ks + lax.broadcasted_iota(jnp.int32, (TK, 1), 0)
                    ok = kpos <= qpos if mask == "causal" else kpos > qpos - W
                    st = jnp.where(ok, st, MASK_BIAS)
                if live is not None:
                    st = jnp.where(live, st, MASK_BIAS)
                m = m_sc[sub, g]
                c = slope_vecs[g] * ks.astype(F32)
                mn = jnp.maximum(m, jnp.max(st, axis=0, keepdims=True) + c)
                p = jnp.exp2(st - (mn - c))
                acc_sc[sub, g] = (jnp.exp2(m - mn) * acc_sc[sub, g]
                                  + _dot(v_refs[g][:, pl.ds(ks, TK)], p.astype(BF16)))
                m_sc[sub, g] = mn

        js = [qt - n_st + 1 + t for t in range(n_st)]
        jc = [jnp.maximum(j, 0) for j in js] if check else js
        scores(jc[0], 0)
        for t in range(n_st):
            if t + 1 < n_st:
                scores(jc[t + 1], (t + 1) & 1)
            mask = "causal" if t == n_st - 1 else ("window" if t == 0 else None)
            tile(jc[t], t & 1, mask, (js[t] >= 0) if check and t < n_st - 1 else None)

        pieces = []
        for g in range(G):
            acc = acc_sc[sub, g]
            num = acc[0:HEAD_DIM, :]
            den = acc[HEAD_DIM:HEAD_DIM + 1, :]
            if has_sink:
                sink_vec = jnp.zeros((1, RQ), F32)
                for r in range(R):
                    sink_vec = jnp.where(lane // TQ == r, sink_ref[g * R + r] * LOG2E, sink_vec)
                m_true = m_sc[sub, g] - slope_vecs[g] * qpos.astype(F32)
                mf = jnp.maximum(m_true, sink_vec)
                a = jnp.exp2(m_true - mf)
                num = num * a
                den = den * a + jnp.exp2(sink_vec - mf)
            o_t = num * (1.0 / den)
            for r in range(R):
                piece = o_t[:, r * TQ:(r + 1) * TQ]
                if gate_ref is not None:
                    c = 3 * (g * R + r) + gate_branch
                    piece = piece * gate_t[c:c + 1, :]
                pieces.append(piece)
        o_ref[rows, :] = jnp.concatenate(pieces, axis=0).T.astype(o_ref.dtype)

    all_tiles_exist = i * NSUB >= n_st - 1

    @pl.when(all_tiles_exist)
    def _():
        for sub in range(NSUB):
            process(sub, False)

    @pl.when(jnp.logical_not(all_tiles_exist))
    def _():
        for sub in range(NSUB):
            process(sub, True)


def _banded_attn(P, VT, B, S, *, name, NSUB, G, R, KD, q_col, k_col, vt_slab, W, slopes, scale, sinks=None,
                 gate_branch=None):
    T = B * S
    TQ = TQ_ATT
    TB = NSUB * TQ
    nq = S // TB
    RQ = R * TQ
    assert TK_BAND == TQ and W % TQ == 0 and S % TB == 0 and KD == 2 * HEAD_DIM
    QW = G * R * HEAD_DIM
    in_specs = [pl.BlockSpec((TB, QW), lambda b, i: (b * nq + i, q_col // QW))]
    args = [P]
    for g in range(G):
        in_specs.append(pl.BlockSpec((S, KD), lambda b, i, g=g: (b, k_col // KD + g)))
        args.append(P)
    for g in range(G):
        in_specs.append(pl.BlockSpec((V_ROWS, S), lambda b, i, g=g: (vt_slab + g, b)))
        args.append(VT)
    if sinks is not None:
        in_specs.append(pl.BlockSpec(memory_space=pltpu.SMEM))
        args.append(sinks.astype(F32))
    if gate_branch is not None:
        in_specs.append(pl.BlockSpec((TB, LANES), lambda b, i: (b * nq + i, P_GATE // LANES)))
        args.append(P)
    kern = functools.partial(_banded_kernel, TQ=TQ, NSUB=NSUB, G=G, R=R, W=W, slopes=slopes, scale=scale,
                             has_sink=sinks is not None, gate_branch=gate_branch)
    return pl.pallas_call(
        kern,
        grid=(B, nq),
        in_specs=in_specs,
        out_specs=pl.BlockSpec((TB, QW), lambda b, i: (b * nq + i, 0)),
        out_shape=jax.ShapeDtypeStruct((T, QW), BF16),
        scratch_shapes=[pltpu.VMEM((NSUB, G, KD, RQ), BF16), pltpu.VMEM((NSUB, G, 1, RQ), F32),
                        pltpu.VMEM((NSUB, G, V_ROWS, RQ), F32), pltpu.VMEM((NSUB, 2, G, TQ, RQ), F32)],
        compiler_params=_params("parallel", "parallel"),
        name=name,
    )(*args)


def _flash_attn(q, k, vt, B, S, *, name, TQ, TK, G, R, KD, q_col, q_width, k_col, vt_slab, slopes=None, scale=1.0,
                selb=None, tiles=None, gate=None, gate_branch=None, side=()):
    T = B * S
    nq = S // TQ
    RQ = R * TQ
    assert (TK % TQ == 0 or TQ == 2 * TK) and S % TK == 0 and S % TQ == 0
    OW = G * R * HEAD_DIM
    if slopes is not None:
        in_specs = [pl.BlockSpec((TQ, q_width), lambda b, i, *_: (b * nq + i, q_col // q_width))]
    else:
        in_specs = [pl.BlockSpec((G * KD, TQ), lambda b, i, *_: (0, b * nq + i))]
    args = [q]
    if selb is not None:
        in_specs.append(pl.BlockSpec((LANES, TQ), lambda b, i, *_: (0, b * nq + i)))
        args.append(selb)
    for g in range(G):
        in_specs.append(pl.BlockSpec((S, KD), lambda b, i, *_, g=g: (b, k_col // KD + g)))
        args.append(k)
    for g in range(G):
        in_specs.append(pl.BlockSpec((V_ROWS, S), lambda b, i, *_, g=g: (vt_slab + g, b)))
        args.append(vt)
    if gate_branch is not None:
        in_specs.append(pl.BlockSpec((TQ, LANES), lambda b, i, *_: (b * nq + i, P_GATE // LANES)))
        args.append(gate)
    out_specs = [pl.BlockSpec((TQ, OW), lambda b, i, *_: (b * nq + i, 0))]
    out_shape = [jax.ShapeDtypeStruct((T, OW), BF16)]
    for w in side:
        rb = w.shape[0] // (B * nq)
        assert w.shape[0] % (B * nq) == 0 and rb % 16 == 0 and w.shape[1] % LANES == 0
        spec = pl.BlockSpec((rb, w.shape[1]), lambda b, i, *_: (b * nq + i, 0))
        in_specs.append(spec)
        args.append(w)
        out_specs.append(spec)
        out_shape.append(jax.ShapeDtypeStruct(w.shape, BF16))
    scratch = []
    if slopes is not None:
        scratch.append(pltpu.VMEM((G, KD, RQ), BF16))
    scratch += [pltpu.VMEM((G, 1, RQ), F32), pltpu.VMEM((G, V_ROWS, RQ), F32), pltpu.VMEM((2, G, TK, RQ), F32),
                pltpu.VMEM((2, G, 1, RQ), F32)]
    kern = functools.partial(_flash_kernel, TQ=TQ, TK=TK, G=G, R=R, NKT=S // TK, slopes=slopes, scale=scale,
                             sel=selb is not None, gate_branch=gate_branch, n_side=len(side))
    prefetch = list(tiles) if selb is not None else []
    grid_spec = pltpu.PrefetchScalarGridSpec(
        num_scalar_prefetch=len(prefetch),
        grid=(B, nq),
        in_specs=in_specs,
        out_specs=out_specs,
        scratch_shapes=scratch)
    outs = pl.pallas_call(
        kern,
        grid_spec=grid_spec,
        out_shape=out_shape,
        compiler_params=_params("parallel", "parallel"),
        name=name,
    )(*prefetch, *args)
    return list(outs)


def _mla_prep(cq, ckv, kr, krr, cos, sin, gq_ref, gkv_ref, wq_ref, wqr_ref, wk_ref, wvt_ref, vone_ref,
              q_out, k_out, vt_out):
    scale = (B_NOPE + B_ROPE) ** -0.5 * LOG2E
    nq = _rms(cq.astype(F32), gq_ref[...]).astype(BF16)
    nkv = _rms(ckv.astype(F32), gkv_ref[...]).astype(BF16)
    q1 = _dot_nt(wq_ref[...], nq)
    q2 = _dot_nt(wqr_ref[...], nq)
    kk = _dot(nkv, wk_ref[...])
    krope = kr.astype(F32) * cos + krr.astype(F32) * sin
    cos_t = cos.T
    sin_t = sin.T
    for h in range(B_HEADS):
        sl = slice(h * LANES, (h + 1) * LANES)
        q_out[sl, :] = ((q1[sl, :] * cos_t + q2[sl, :] * sin_t) * scale).astype(BF16)
        k_out[:, sl] = (kk[:, sl] + krope).astype(BF16)
    vt_out[...] = (_dot_nt(wvt_ref[...], nkv) + vone_ref[...]).astype(BF16)


def _build_mla_weights(w_uq, w_ukv):
    R1, R2 = w_uq.shape[0], w_ukv.shape[0]
    half = B_ROPE // 2
    hq = B_NOPE + B_ROPE
    wq, wqr, wk, wvt = [], [], [], []
    for h in range(B_HEADS):
        nope = w_uq[:, h * hq:h * hq + B_NOPE]
        rope = w_uq[:, h * hq + B_NOPE:(h + 1) * hq]
        rot = jnp.concatenate([-rope[:, half:], rope[:, :half]], axis=1)
        pad = jnp.zeros((R1, LANES - hq), w_uq.dtype)
        wq += [nope, rope, pad]
        wqr += [jnp.zeros((R1, B_NOPE), w_uq.dtype), rot, pad]
        wk += [w_ukv[:, h * 128:h * 128 + B_NOPE], jnp.zeros((R2, LANES - B_NOPE), w_ukv.dtype)]
        wvt += [w_ukv[:, h * 128 + B_NOPE:(h + 1) * 128].T, jnp.zeros((V_ROWS - HEAD_DIM, R2), w_ukv.dtype)]
    cat = lambda xs: jnp.concatenate(xs, axis=1).astype(BF16)
    return cat(wq).T, cat(wqr).T, cat(wk), jnp.concatenate(wvt, axis=0).astype(BF16)


def _gelu_tanh(x):
    return 0.5 * x * (1.0 + jnp.tanh(np.float32(np.sqrt(2.0 / np.pi)) * (x + 0.044715 * (x * x * x))))


def _compress_kernel(xk_ref, xv_ref, pek_ref, pev_ref, w1k_ref, w1v_ref, w2k_ref, w2v_ref, kc_ref, vct_ref, *, NC):
    for x_ref, pe_ref, w1_ref, w2_ref, is_v in ((xk_ref, pek_ref, w1k_ref, w2k_ref, False),
                                                 (xv_ref, pev_ref, w1v_ref, w2v_ref, True)):
        x = x_ref[...].astype(F32)
        a = _dot((x + pe_ref[0:1, :]).astype(BF16), w1_ref[0])
        b = _dot((x + pe_ref[1:2, :]).astype(BF16), w1_ref[1])
        hid = a + pltpu.roll(b, NC - 1, 0)
        o = _dot(_gelu_tanh(hid).astype(BF16), w2_ref[...])
        n = lax.broadcasted_iota(jnp.int32, o.shape, 0)
        o = jnp.where(n < NC - 1, o, 0.0)
        if is_v:
            vct_ref[...] = o.T.astype(vct_ref.dtype)
        else:
            kc_ref[...] = o.astype(kc_ref.dtype)


def _build_compress_weights(w1, w2, pe):
    half = C_CMP_BLOCK // 2
    G = C_KV_HEADS
    eye = jnp.eye(G, dtype=w1.dtype)
    w1s = [jnp.einsum('lde,hg->lhdge', w1[c * half:(c + 1) * half], eye).reshape(half * G * HEAD_DIM, G * C_CMP_HIDDEN)
           for c in range(2)]
    w1f = jnp.stack(w1s).astype(BF16)
    w2f = jnp.einsum('ed,hg->hegd', w2, eye).reshape(G * C_CMP_HIDDEN, G * HEAD_DIM).astype(BF16)
    pes = [jnp.broadcast_to(pe[c * half:(c + 1) * half, None, :], (half, G, HEAD_DIM)).reshape(1, -1) for c in range(2)]
    pef = jnp.concatenate(pes, axis=0).astype(F32)
    return w1f, w2f, pef


def _compress(k_cmp, v_cmp, B, S, k_weights, v_weights):
    NC = S // C_CMP_STRIDE
    CW = C_CMP_STRIDE * C_KV_HEADS * HEAD_DIM
    xk = k_cmp.reshape(B * NC, CW)
    xv = v_cmp.reshape(B * NC, CW)
    w1kf, w2kf, pekf = k_weights
    w1vf, w2vf, pevf = v_weights
    xspec = pl.BlockSpec((NC, CW), lambda b: (b, 0))
    full2 = lambda a: pl.BlockSpec(a.shape, lambda b: (0, 0))
    full3 = lambda a: pl.BlockSpec(a.shape, lambda b: (0, 0, 0))
    return pl.pallas_call(
        functools.partial(_compress_kernel, NC=NC),
        grid=(B,),
        in_specs=[xspec, xspec, full2(pekf), full2(pevf), full3(w1kf), full3(w1vf), full2(w2kf), full2(w2vf)],
        out_specs=[pl.BlockSpec((NC, LANES), lambda b: (b, 0)), pl.BlockSpec((LANES, NC), lambda b: (b, 0))],
        out_shape=[jax.ShapeDtypeStruct((B * NC, LANES), BF16), jax.ShapeDtypeStruct((B * LANES, NC), BF16)],
        compiler_params=_params("parallel"),
        name="nsa_compress",
    )(xk, xv, pekf, pevf, w1kf, w1vf, w2kf, w2vf)


def _cmp_sel_kernel(q_ref, kc_ref, vct_ref, cfeat_ref, ovt_ref, gate_ref, ocmp_ref, sel_ref, cnt_ref,
                    *, TQ, NC, NSEL, NTOP, slopes):
    i = pl.program_id(1)
    t0 = i * TQ
    R = C_HEADS // C_KV_HEADS
    scale = HEAD_DIM ** -0.5
    gate_t = _sigmoid(gate_ref[...].astype(F32)).T
    last = C_CMP_BLOCK - 1
    tlane = t0 + lax.broadcasted_iota(jnp.int32, (NC, TQ), 1)
    nrow = lax.broadcasted_iota(jnp.int32, (NC, TQ), 0)
    cvalid = (nrow * C_CMP_STRIDE + last <= tlane) & (nrow < NC - 1)
    hasc = ((t0 + lax.broadcasted_iota(jnp.int32, (1, TQ), 1)) >= last).astype(F32)
    jrow = lax.broadcasted_iota(jnp.int32, (NSEL, TQ), 0)
    tl = t0 + lax.broadcasted_iota(jnp.int32, (NSEL, TQ), 1)
    avail = jrow * C_SEL_BLOCK <= tl
    cur = tl // C_SEL_BLOCK
    forced = (jrow == 0) | (jrow == cur) | (jrow == cur - 1)
    q_t = q_ref[...].astype(F32).T
    frow = lax.broadcasted_iota(jnp.int32, (HEAD_DIM, TQ), 0)
    sub8 = lax.broadcasted_iota(jnp.int32, (8, TQ), 0)
    pieces, sel_parts = [], []
    picked = jnp.zeros((NSEL, TQ), F32)
    for g in range(C_KV_HEADS):
        kca = jnp.concatenate([kc_ref[:, g * HEAD_DIM:(g + 1) * HEAD_DIM], cfeat_ref[...]], axis=1)
        vo = jnp.concatenate([vct_ref[g * HEAD_DIM:(g + 1) * HEAD_DIM, :], ovt_ref[...]], axis=0)
        imp = jnp.zeros((HEAD_DIM, TQ), F32)
        for r in range(R):
            hh = g * R + r
            s3 = _split_bf16(slopes[hh] * LOG2E)
            feat = jnp.zeros((HEAD_DIM, TQ), F32)
            for k in range(3):
                feat = jnp.where(frow == k, s3[k] * 256.0, jnp.where(frow == 3 + k, s3[k], feat))
            qa = jnp.concatenate([q_t[hh * HEAD_DIM:(hh + 1) * HEAD_DIM, :] * (scale * LOG2E), feat],
                                 axis=0).astype(BF16)
            st = jnp.where(cvalid, _dot(kca, qa), NEG_INF)
            et = jnp.exp2(st - jnp.max(st, axis=0, keepdims=True))
            inv = hasc / jnp.sum(et, axis=0, keepdims=True)
            res = _dot(vo, et.astype(BF16))
            pieces.append(res[0:HEAD_DIM, :] * (inv * gate_t[3 * hh:3 * hh + 1, :]))
            imp = imp + res[HEAD_DIM:2 * HEAD_DIM, :] * inv
        v = jnp.where(avail, imp[0:NSEL, :], NEG_INF)
        v = jnp.where(forced, -NEG_INF, v)
        ranks = [jnp.zeros((8, TQ), F32) for _ in range(NSEL // 8)]
        for ii in range(NSEL):
            ri = v[ii:ii + 1, :]
            for k in range(NSEL // 8):
                vk = v[8 * k:8 * k + 8, :]
                if 8 * k > ii:
                    inc = jnp.where(ri >= vk, 1.0, 0.0)
                elif 8 * k + 7 < ii:
                    inc = jnp.where(ri > vk, 1.0, 0.0)
                else:
                    inc = jnp.where(sub8 + 8 * k > ii, jnp.where(ri >= vk, 1.0, 0.0), jnp.where(ri > vk, 1.0, 0.0))
                ranks[k] = ranks[k] + inc
        rank = jnp.concatenate(ranks, axis=0)
        sel_parts.append(jnp.where(rank < NTOP, 0.0, MASK_BIAS))
        if NSEL < HEAD_DIM:
            sel_parts.append(jnp.zeros((HEAD_DIM - NSEL, TQ), F32))
        picked = picked + jnp.where(rank < NTOP, 1.0, 0.0)
    sel_ref[...] = jnp.concatenate(sel_parts, axis=0).astype(sel_ref.dtype)
    cnt_ref[...] = _dot_nt(jnp.ones((8, TQ), BF16), picked.astype(BF16))
    ocmp_ref[...] = jnp.concatenate(pieces, axis=0).T.astype(ocmp_ref.dtype)


def _overlap_t(S):
    n_cmp = (S - C_CMP_BLOCK) // C_CMP_STRIDE + 1
    n_sel = S // C_SEL_BLOCK
    NC = S // C_CMP_STRIDE
    cstart = np.arange(n_cmp) * C_CMP_STRIDE
    cend = cstart + C_CMP_BLOCK - 1
    sstart = np.arange(n_sel) * C_SEL_BLOCK
    send = sstart + C_SEL_BLOCK - 1
    ov = np.clip(np.minimum(cend[:, None], send[None, :]) - np.maximum(cstart[:, None], sstart[None, :]) + 1,
                 0, None).astype(np.float32) / C_CMP_STRIDE
    ovt = np.zeros((HEAD_DIM, NC), np.float32)
    ovt[:n_sel, :n_cmp] = ov.T
    cend_all = np.arange(NC) * C_CMP_STRIDE + C_CMP_BLOCK - 1
    cfeat = np.zeros((NC, HEAD_DIM), np.float32)
    cfeat[:, 0:3] = (cend_all // 256)[:, None]
    cfeat[:, 3:6] = (cend_all % 256)[:, None]
    return jnp.asarray(ovt, dtype=BF16), jnp.asarray(cfeat, dtype=BF16)


def _cmp_sel(P, kc, vct, B, S, slopes):
    T = B * S
    TQ = TQ_ATT
    nq = S // TQ
    NC = S // C_CMP_STRIDE
    NSEL = S // C_SEL_BLOCK
    assert NSEL <= HEAD_DIM and NSEL % 8 == 0 and S <= 256 * 256
    QW = C_HEADS * HEAD_DIM
    ovt, cfeat = _overlap_t(S)
    kern = functools.partial(_cmp_sel_kernel, TQ=TQ, NC=NC, NSEL=NSEL, NTOP=min(C_SEL_TOP, NSEL), slopes=slopes)
    return pl.pallas_call(
        kern,
        grid=(B, nq),
        in_specs=[pl.BlockSpec((TQ, QW), lambda b, i: (b * nq + i, P_CQ // QW)),
                  pl.BlockSpec((NC, LANES), lambda b, i: (b, 0)),
                  pl.BlockSpec((LANES, NC), lambda b, i: (b, 0)),
                  pl.BlockSpec((NC, HEAD_DIM), lambda b, i: (0, 0)),
                  pl.BlockSpec((HEAD_DIM, NC), lambda b, i: (0, 0)),
                  pl.BlockSpec((TQ, LANES), lambda b, i: (b * nq + i, P_GATE // LANES))],
        out_specs=[pl.BlockSpec((TQ, QW), lambda b, i: (b * nq + i, 0)),
                   pl.BlockSpec((LANES, TQ), lambda b, i: (0, b * nq + i)),
                   pl.BlockSpec((8, NSEL), lambda b, i: (b * nq + i, 0))],
        out_shape=[jax.ShapeDtypeStruct((T, QW), BF16), jax.ShapeDtypeStruct((LANES, T), BF16),
                   jax.ShapeDtypeStruct((B * nq * 8, NSEL), F32)],
        compiler_params=_params("parallel", "parallel"),
        name="nsa_cmp_select",
    )(P, kc, vct, cfeat, ovt, P)


def _selected_tile_lists(cnt, B, S):
    nq = S // TQ_SEL
    nkt = S // TK_SEL
    per = TK_SEL // C_SEL_BLOCK
    blk = jnp.any(cnt.reshape(B * nq, TQ_SEL // TQ_ATT, 8, -1)[:, :, 0, :] > 0.5, axis=1)
    need = jnp.any(blk.reshape(B * nq, nkt, per), axis=-1)
    own = ((jnp.arange(B * nq, dtype=jnp.int32) % nq) * TQ_SEL + TQ_SEL - 1) // TK_SEL
    jj = jnp.arange(nkt, dtype=jnp.int32)[None, :]
    need = (need | (jj == 0) | (jj == own[:, None])) & (jj <= own[:, None])
    order = jnp.argsort(jnp.where(need, jj, nkt + jj), axis=-1).astype(jnp.int32)
    return order.reshape(-1), jnp.sum(need, axis=-1).astype(jnp.int32)


def _outproj_kernel(x_ref, oa_ref, ob_ref, oc1_ref, oc2_ref, oc3_ref, ga_ref, gb_ref, gc_ref, wo_ref, gp_ref, o_ref):
    oa = _rms(oa_ref[...].astype(F32), ga_ref[...]).astype(BF16)
    ob = _rms(ob_ref[...].astype(F32), gb_ref[...]).astype(BF16)
    oc = oc1_ref[...].astype(F32) + oc2_ref[...].astype(F32) + oc3_ref[...].astype(F32)
    oc = _rms(oc, gc_ref[...]).astype(BF16)
    m = _dot(oa, wo_ref[0:256, :]) + _dot(ob, wo_ref[256:512, :]) + _dot(oc, wo_ref[512:1024, :])
    o_ref[...] = x_ref[...] + _rms(m, gp_ref[...])


def _outproj(x, o_a, o_b, o_cmp, o_slc, o_win, g_oa, g_ob, g_oc, w_o, g_post):
    T = x.shape[0]
    TM = TM_OUT
    row = lambda w: pl.BlockSpec((TM, w), lambda i: (i, 0))
    full = lambda r, c: pl.BlockSpec((r, c), lambda i: (0, 0))
    return pl.pallas_call(
        _outproj_kernel,
        grid=(T // TM,),
        in_specs=[row(D_MODEL), row(256), row(256), row(512), row(512), row(512),
                  full(1, 256), full(1, 256), full(1, 512), full(D_MODEL, D_MODEL), full(1, D_MODEL)],
        out_specs=row(D_MODEL),
        out_shape=jax.ShapeDtypeStruct((T, D_MODEL), F32),
        compiler_params=_params("parallel"),
        name="outproj",
    )(x, o_a, o_b, o_cmp, o_slc, o_win, g_oa.reshape(1, -1), g_ob.reshape(1, -1), g_oc.reshape(1, -1),
      w_o.astype(BF16), g_post.reshape(1, -1))


def _swiglu_step(h, wg_ref, wu_ref, wd_ref, acc_ref):
    a = _dot(h, wg_ref[...])
    b = _dot(h, wu_ref[...])
    z = (a * _sigmoid(a) * b).astype(BF16)
    acc_ref[...] += _dot(z, wd_ref[...])


def _ffn_kernel(x_ref, gpre_ref, wg_ref, wu_ref, wd_ref, gpost_ref, o_ref, h_sc, acc_sc, *, nf):
    f = pl.program_id(1)

    @pl.when(f == 0)
    def _():
        h_sc[...] = _rms(x_ref[...], gpre_ref[...]).astype(BF16)
        acc_sc[...] = jnp.zeros_like(acc_sc)

    _swiglu_step(h_sc[...], wg_ref, wu_ref, wd_ref, acc_sc)

    @pl.when(f == nf - 1)
    def _():
        o_ref[...] = x_ref[...] + _rms(acc_sc[...], gpost_ref[...])


def _ffn(x, g_pre, wg, wu, wd, g_post):
    T = x.shape[0]
    TM, TF = TM_FFN, TF_FFN
    FF = wg.shape[1]
    nf = FF // TF
    return pl.pallas_call(
        functools.partial(_ffn_kernel, nf=nf),
        grid=(T // TM, nf),
        in_specs=[pl.BlockSpec((TM, D_MODEL), lambda i, f: (i, 0)),
                  pl.BlockSpec((1, D_MODEL), lambda i, f: (0, 0)),
                  pl.BlockSpec((D_MODEL, TF), lambda i, f: (0, f)),
                  pl.BlockSpec((D_MODEL, TF), lambda i, f: (0, f)),
                  pl.BlockSpec((TF, D_MODEL), lambda i, f: (f, 0)),
                  pl.BlockSpec((1, D_MODEL), lambda i, f: (0, 0))],
        out_specs=pl.BlockSpec((TM, D_MODEL), lambda i, f: (i, 0)),
        out_shape=jax.ShapeDtypeStruct((T, D_MODEL), F32),
        scratch_shapes=[pltpu.VMEM((TM, D_MODEL), BF16), pltpu.VMEM((TM, D_MODEL), F32)],
        compiler_params=_params("parallel", "arbitrary"),
        name="dense_ffn",
    )(x, g_pre.reshape(1, -1), wg.astype(BF16), wu.astype(BF16), wd.astype(BF16), g_post.reshape(1, -1))


def _expert_kernel(te_ref, nt_ref, x_ref, wg_ref, wu_ref, wd_ref, o_ref, acc_sc, *, nf):
    j = pl.program_id(0)
    f = pl.program_id(1)
    live = j < nt_ref[0]

    @pl.when(f == 0)
    def _():
        acc_sc[...] = jnp.zeros_like(acc_sc)

    @pl.when(live)
    def _():
        _swiglu_step(x_ref[...], wg_ref.at[0], wu_ref.at[0], wd_ref.at[0], acc_sc)

    @pl.when(f == nf - 1)
    def _():
        o_ref[...] = acc_sc[...].astype(o_ref.dtype)


def _experts(xs, tile_expert, n_tiles, wg, wu, wd):
    NT = xs.shape[0] // TM_MOE
    TM, TF = TM_MOE, TF_MOE
    assert wg.shape[2] % TF == 0
    FF = wg.shape[2]
    nf = FF // TF
    fidx = lambda j, f, nt: jnp.where(j < nt[0], f, nf - 1)
    grid_spec = pltpu.PrefetchScalarGridSpec(
        num_scalar_prefetch=2,
        grid=(NT, nf),
        in_specs=[pl.BlockSpec((TM, D_MODEL), lambda j, f, te, nt: (j, 0)),
                  pl.BlockSpec((1, D_MODEL, TF), lambda j, f, te, nt: (te[j], 0, fidx(j, f, nt))),
                  pl.BlockSpec((1, D_MODEL, TF), lambda j, f, te, nt: (te[j], 0, fidx(j, f, nt))),
                  pl.BlockSpec((1, TF, D_MODEL), lambda j, f, te, nt: (te[j], fidx(j, f, nt), 0))],
        out_specs=pl.BlockSpec((TM, D_MODEL), lambda j, f, te, nt: (j, 0)),
        scratch_shapes=[pltpu.VMEM((TM, D_MODEL), F32)])
    return pl.pallas_call(
        functools.partial(_expert_kernel, nf=nf),
        grid_spec=grid_spec,
        out_shape=jax.ShapeDtypeStruct(xs.shape, BF16),
        compiler_params=_params("parallel", "arbitrary"),
        name="moe_experts",
    )(tile_expert, n_tiles, xs, wg, wu, wd)


def _router_kernel(x_ref, g_ref, wr_ref, h_out, meta_out, metat_out, start_out, cnt_out, run_sc, *, TS):
    s = pl.program_id(0)

    @pl.when(s == 0)
    def _():
        run_sc[...] = jnp.zeros_like(run_sc)

    h = _rms(x_ref[...], g_ref[...])
    hb = h.astype(BF16)
    h_out[...] = hb
    hl = (h - hb.astype(F32)).astype(BF16)
    w = wr_ref[...]
    wh = w.astype(BF16)
    wl = (w - wh.astype(F32)).astype(BF16)
    logits = _dot(hb, wh) + _dot(hl, wh) + _dot(hb, wl)
    lane = lax.broadcasted_iota(jnp.int32, (TS, LANES), 1).astype(F32)
    logits = jnp.where(lane < N_EXPERTS, logits, NEG_INF)
    m1 = jnp.max(logits, axis=-1, keepdims=True)
    i1 = jnp.min(jnp.where(logits == m1, lane, float(LANES)), axis=-1, keepdims=True)
    rest = jnp.where(lane == i1, NEG_INF, logits)
    m2 = jnp.max(rest, axis=-1, keepdims=True)
    i2 = jnp.min(jnp.where(rest == m2, lane, float(LANES)), axis=-1, keepdims=True)
    ex = jnp.exp(m2 - m1)
    w1 = 1.0 / (1.0 + ex)
    w2 = ex / (1.0 + ex)
    oh1 = lane == i1
    oh2 = lane == i2
    oh = jnp.where(oh1, 1.0, 0.0) + jnp.where(oh2, 1.0, 0.0)
    tri = jnp.where(lax.broadcasted_iota(jnp.int32, (TS, TS), 0) > lax.broadcasted_iota(jnp.int32, (TS, TS), 1),
                    1.0, 0.0).astype(BF16)
    run = run_sc[...]
    before = _dot(tri, oh.astype(BF16)) + run
    r1 = jnp.sum(jnp.where(oh1, before, 0.0), axis=-1, keepdims=True)
    r2 = jnp.sum(jnp.where(oh2, before, 0.0), axis=-1, keepdims=True)
    start_out[0] = run
    run = run + jnp.sum(oh, axis=0, keepdims=True)
    run_sc[...] = run
    cnt_out[...] = run
    meta = jnp.where(lane == 0, i1, jnp.where(lane == 1, i2, jnp.where(lane == 2, r1, jnp.where(
        lane == 3, r2, jnp.where(lane == 4, w1, jnp.where(lane == 5, w2, 0.0))))))
    meta_out[...] = meta
    metat_out[...] = meta.T[0:8, :]


def _router(x, g, w_router):
    T = x.shape[0]
    TS = TS_MOE
    nt = T // TS
    wr = jnp.zeros((D_MODEL, LANES), F32).at[:, :N_EXPERTS].set(w_router.astype(F32))
    return pl.pallas_call(
        functools.partial(_router_kernel, TS=TS),
        grid=(nt,),
        in_specs=[pl.BlockSpec((TS, D_MODEL), lambda s: (s, 0)),
                  pl.BlockSpec((1, D_MODEL), lambda s: (0, 0)),
                  pl.BlockSpec((D_MODEL, LANES), lambda s: (0, 0))],
        out_specs=[pl.BlockSpec((TS, D_MODEL), lambda s: (s, 0)),
                   pl.BlockSpec((TS, LANES), lambda s: (s, 0)),
                   pl.BlockSpec((8, TS), lambda s: (0, s)),
                   pl.BlockSpec((1, 1, LANES), lambda s: (s, 0, 0)),
                   pl.BlockSpec((1, LANES), lambda s: (0, 0))],
        out_shape=[jax.ShapeDtypeStruct((T, D_MODEL), BF16),
                   jax.ShapeDtypeStruct((T, LANES), F32),
                   jax.ShapeDtypeStruct((8, T), F32),
                   jax.ShapeDtypeStruct((nt, 1, LANES), F32),
                   jax.ShapeDtypeStruct((1, LANES), F32)],
        scratch_shapes=[pltpu.VMEM((1, LANES), F32)],
        compiler_params=_params("arbitrary"),
        name="moe_router",
    )(x, g.reshape(1, -1), wr)


def _gather_kernel(te_ref, slo_ref, shi_ref, base_ref, metat_ref, h_ref, xs_out, acc_sc, *, TM, TS, CH):
    j = pl.program_id(0)
    e = te_ref[j]
    ef = e.astype(F32)
    nsub = TM // CH
    nt = h_ref.shape[0] // TS

    def picked_rows(s, want):
        off = pl.multiple_of(s * TS, TS)
        e1 = metat_ref[0:1, pl.ds(off, TS)]
        e2 = metat_ref[1:2, pl.ds(off, TS)]
        r1 = metat_ref[2:3, pl.ds(off, TS)]
        r2 = metat_ref[3:4, pl.ds(off, TS)]
        key = jnp.where(e1 == ef, r1, jnp.where(e2 == ef, r2, -1.0))
        sel = jnp.where(key == want, 1.0, 0.0).astype(BF16)
        return _dot(sel, h_ref[pl.ds(off, TS), :])

    wants = []
    for sub in range(nsub):
        q = j * nsub + sub
        first = j * TM + sub * CH - base_ref[e]
        want = (first + lax.broadcasted_iota(jnp.int32, (CH, 1), 0)).astype(F32)
        lo = jnp.minimum(slo_ref[q], nt - 1)
        second = jnp.where(lo + 1 < nt, want, -2.0)
        acc_sc[sub] = picked_rows(lo, want) + picked_rows(jnp.minimum(lo + 1, nt - 1), second)
        wants.append(want)
    for sub in range(nsub):
        q = j * nsub + sub

        def body(s, _, sub=sub):
            acc_sc[sub] += picked_rows(s, wants[sub])
            return 0

        lax.fori_loop(slo_ref[q] + 2, shi_ref[q] + 1, body, 0)
    for sub in range(nsub):
        xs_out[sub * CH:(sub + 1) * CH, :] = acc_sc[sub].astype(xs_out.dtype)


def _gather(hb, metat, tile_expert, s_lo, s_hi, base, NT):
    TM, TS = TM_MOE, TS_MOE
    assert hb.shape[0] % TS == 0
    grid_spec = pltpu.PrefetchScalarGridSpec(
        num_scalar_prefetch=4,
        grid=(NT,),
        in_specs=[pl.BlockSpec(memory_space=pltpu.VMEM), pl.BlockSpec(memory_space=pltpu.VMEM)],
        out_specs=pl.BlockSpec((TM, D_MODEL), lambda j, *_: (j, 0)),
        scratch_shapes=[pltpu.VMEM((TM // CH_MOE, CH_MOE, D_MODEL), F32)])
    return pl.pallas_call(
        functools.partial(_gather_kernel, TM=TM, TS=TS, CH=CH_MOE),
        grid_spec=grid_spec,
        out_shape=jax.ShapeDtypeStruct((NT * TM, D_MODEL), BF16),
        compiler_params=_params("arbitrary"),
        name="moe_gather",
    )(tile_expert, s_lo, s_hi, base, metat, hb)


def _combine_kernel(blk_ref, row0_ref, base_ref, meta_ref, x_ref, gpost_ref, ys_hbm, o_ref, buf, sem, selw_sc,
                    *, TS, CH, U, nt):
    s = pl.program_id(0)
    slot = s % 2

    def chunk_copies(step, slot_):
        out = []
        for u in range(U):
            r0 = pl.multiple_of(blk_ref[step * U + u] * CH, CH)
            out.append(pltpu.make_async_copy(ys_hbm.at[pl.ds(r0, CH), :], buf.at[slot_, pl.ds(u * CH, CH), :],
                                             sem.at[slot_]))
        return out

    @pl.when(s == 0)
    def _():
        for c in chunk_copies(0, 0):
            c.start()

    @pl.when(s + 1 < nt)
    def _():
        for c in chunk_copies(s + 1, 1 - slot):
            c.start()

    meta = meta_ref[...]
    e1, e2 = meta[:, 0:1], meta[:, 1:2]
    w1, w2 = meta[:, 4:5], meta[:, 5:6]
    b1 = jnp.zeros((TS, 1), F32)
    b2 = jnp.zeros((TS, 1), F32)
    for e in range(N_EXPERTS):
        be = base_ref[e].astype(F32)
        b1 = jnp.where(e1 == float(e), be, b1)
        b2 = jnp.where(e2 == float(e), be, b2)
    pos1 = b1 + meta[:, 2:3]
    pos2 = b2 + meta[:, 3:4]

    lane = lax.broadcasted_iota(jnp.int32, (1, CH), 1)
    for u in range(U):
        rows = (row0_ref[s * U + u] + lane).astype(F32)
        selw = jnp.where(pos1 == rows, w1, 0.0) + jnp.where(pos2 == rows, w2, 0.0)
        selw_sc[:, u * CH:(u + 1) * CH] = selw.astype(BF16)
    for c in chunk_copies(s, slot):
        c.wait()
    y = _dot(selw_sc[...], buf[slot])
    o_ref[...] = x_ref[...] + _rms(y, gpost_ref[...])


def _combine(x, meta, ys, chunk_blk, chunk_row0, base, g_post):
    T = x.shape[0]
    TS, CH, U = TS_MOE, CH_MOE, U_MOE
    nt = T // TS
    grid_spec = pltpu.PrefetchScalarGridSpec(
        num_scalar_prefetch=3,
        grid=(nt,),
        in_specs=[pl.BlockSpec((TS, LANES), lambda s, *_: (s, 0)),
                  pl.BlockSpec((TS, D_MODEL), lambda s, *_: (s, 0)),
                  pl.BlockSpec((1, D_MODEL), lambda s, *_: (0, 0)),
                  pl.BlockSpec(memory_space=pl.ANY)],
        out_specs=pl.BlockSpec((TS, D_MODEL), lambda s, *_: (s, 0)),
        scratch_shapes=[pltpu.VMEM((2, U * CH, D_MODEL), BF16), pltpu.SemaphoreType.DMA((2,)),
                        pltpu.VMEM((TS, U * CH), BF16)])
    return pl.pallas_call(
        functools.partial(_combine_kernel, TS=TS, CH=CH, U=U, nt=nt),
        grid_spec=grid_spec,
        out_shape=jax.ShapeDtypeStruct((T, D_MODEL), F32),
        compiler_params=_params("arbitrary"),
        name="moe_combine",
    )(chunk_blk, chunk_row0, base, meta, x, g_post.reshape(1, -1), ys)


def _moe(x, g_pre, w_router, w_gate, w_up, w_down, g_post):
    T = x.shape[0]
    TS, TM, CH, U = TS_MOE, TM_MOE, CH_MOE, U_MOE
    nt = T // TS
    NT = (2 * T) // TM + N_EXPERTS
    hb, meta, metat, start, cnt = _router(x, g_pre, w_router)

    i32 = jnp.int32
    counts = cnt[0, :N_EXPERTS].astype(i32)
    start = start[:, 0, :N_EXPERTS].astype(i32)
    start_ext = jnp.concatenate([start, counts[None, :]], axis=0)
    tiles_e = (counts + TM - 1) // TM
    tiles_cum = jnp.cumsum(tiles_e)
    base = (tiles_cum - tiles_e) * TM
    n_tiles = tiles_cum[-1:]
    jj = jnp.arange(NT, dtype=i32)
    tile_expert = jnp.minimum(jnp.sum(jj[:, None] >= tiles_cum[None, :], axis=1), N_EXPERTS - 1).astype(i32)
    live = jj < n_tiles[0]
    qq = jnp.arange(NT * (TM // CH), dtype=i32)
    q_exp = tile_expert[qq // (TM // CH)]
    a = qq * CH - base[q_exp]
    st_e = start_ext[:, q_exp]
    s_lo = jnp.sum(st_e[1:] <= a[None, :], axis=0).astype(i32)
    s_hi = (jnp.sum(st_e[:-1] < (a + CH)[None, :], axis=0) - 1).astype(i32)
    q_live = live[qq // (TM // CH)]
    s_lo = jnp.where(q_live, s_lo, 1)
    s_hi = jnp.where(q_live, s_hi, 0)

    xs = _gather(hb, metat, tile_expert, s_lo, s_hi, base.astype(i32), NT)
    ys = _experts(xs, tile_expert, n_tiles.astype(i32), w_gate.astype(BF16), w_up.astype(BF16), w_down.astype(BF16))

    lo = base[None, :] + start_ext[:-1]
    hi = base[None, :] + start_ext[1:]
    n_e = jnp.where(hi > lo, (hi - 1) // CH - lo // CH + 1, 0)
    cum = jnp.cumsum(n_e, axis=1)
    uu = jnp.arange(U, dtype=i32)
    e_u = jnp.minimum(jnp.sum(uu[None, :, None] >= cum[:, None, :], axis=2), N_EXPERTS - 1)
    first = jnp.take_along_axis(lo // CH, e_u, axis=1)
    skipped = jnp.take_along_axis(cum - n_e, e_u, axis=1)
    chunk_blk = jnp.clip(first + (uu[None, :] - skipped), 0, (NT * TM) // CH - 1).astype(i32)
    used = uu[None, :] < cum[:, -1:]
    chunk_blk = jnp.where(used, chunk_blk, chunk_blk[:, :1])
    chunk_row0 = jnp.where(used, chunk_blk * CH, -(NT * TM)).astype(i32)
    return _combine(x, meta, ys, chunk_blk.reshape(-1), chunk_row0.reshape(-1), base.astype(i32), g_post)


def kernel(x, positions, w_in, a_sinks, g_cq, w_uq, g_ckv, w_ukv, c_w1_k, c_w2_k, c_pe_k, c_w1_v, c_w2_v, c_pe_v,
           g_oa, g_ob, g_oc, w_o, g_pre_mix, g_post_mix, g_pre_ffn, g_post_ffn, ffn_w_gate, ffn_w_up, ffn_w_down,
           moe_router, moe_w_gate, moe_w_up, moe_w_down):
    B, S, _ = x.shape
    T = B * S
    depth = w_in.shape[0]
    a_slopes = _alibi_slopes(A_HEADS)
    c_slopes = _alibi_slopes(C_HEADS)
    xf = x.reshape(T, D_MODEL).astype(F32)
    cos, sin = _rope_tables(positions)
    att_scale = HEAD_DIM ** -0.5
    flat = lambda w: w.reshape(-1, w.shape[-1])
    bf16_w = {}
    w_p_all, w_vt_all = jax.vmap(_build_w_in)(w_in)
    mla_w_all = jax.vmap(_build_mla_weights)(w_uq, w_ukv)
    cmp_k_all = jax.vmap(_build_compress_weights)(c_w1_k, c_w2_k, c_pe_k)
    cmp_v_all = jax.vmap(_build_compress_weights)(c_w1_v, c_w2_v, c_pe_v)
    for l in range(depth):
        P, VT, qB, kB, vtB, k_cmp, v_cmp = _inproj(xf, g_pre_mix[l].reshape(1, -1), w_p_all[l], w_vt_all[l], S, cos, sin,
                                     g_cq[l], g_ckv[l], tuple(w[l] for w in mla_w_all))
        o_a = _banded_attn(P, VT, B, S, name="swa_attn", NSUB=NSUB_A, G=A_KV_HEADS, R=A_HEADS // A_KV_HEADS,
                           KD=KD_A, q_col=P_AQ, k_col=P_KA, vt_slab=VT_A, W=A_WINDOW, slopes=a_slopes,
                           scale=att_scale, sinks=a_sinks[l])
        if l % 2 == 0:
            side = (ffn_w_gate[l // 2], ffn_w_up[l // 2], ffn_w_down[l // 2])
        else:
            side = (flat(moe_w_down[l // 2]),)
        o_b, *cast = _flash_attn(qB, kB, vtB, B, S, name="mla_attn", TQ=TQ_MLA, TK=TK_MLA, G=B_HEADS, R=1, KD=LANES,
                                 q_col=0, q_width=B_HEADS * LANES, k_col=0, vt_slab=0, side=side)
        if l % 2 == 0:
            bf16_w["ffn", l] = cast
        else:
            bf16_w["down", l] = cast[0]
        kc, vct = _compress(k_cmp, v_cmp, B, S, tuple(w[l] for w in cmp_k_all), tuple(w[l] for w in cmp_v_all))
        o_cmp, selb, pick_cnt = _cmp_sel(P, kc, vct, B, S, c_slopes)
        c_args = dict(TQ=TQ_ATT, G=C_KV_HEADS, R=C_HEADS // C_KV_HEADS, q_col=P_CQ, q_width=C_HEADS * HEAD_DIM,
                      slopes=c_slopes, scale=att_scale, gate=P)
        side = ()
        if l % 2 == 0 and l + 1 < depth:
            side = (flat(moe_w_gate[(l + 1) // 2]),)
        elif l % 2 == 1:
            side = (flat(moe_w_up[l // 2]),)
        o_slc, *cast = _flash_attn(P, P, VT, B, S, name="nsa_selected", TK=TK_SEL, KD=KD_SEL, k_col=P_KSEL,
                                   vt_slab=VT_SEL, selb=selb, tiles=_selected_tile_lists(pick_cnt, B, S),
                                   gate_branch=1, side=side, **dict(c_args, TQ=TQ_SEL))
        if l % 2 == 0 and l + 1 < depth:
            bf16_w["gate", l + 1] = cast[0]
        elif l % 2 == 1:
            bf16_w["up", l] = cast[0]
        o_win = _banded_attn(P, VT, B, S, name="nsa_window", NSUB=NSUB_WIN, G=C_KV_HEADS, R=C_HEADS // C_KV_HEADS,
                             KD=KD_WIN, q_col=P_CQ, k_col=P_KWIN, vt_slab=VT_WIN, W=C_WINDOW, slopes=c_slopes,
                             scale=att_scale, gate_branch=2)
        xf = _outproj(xf, o_a, o_b, o_cmp, o_slc, o_win, g_oa[l], g_ob[l], g_oc[l], w_o[l], g_post_mix[l])
        if l % 2 == 0:
            wg, wu, wd = bf16_w["ffn", l]
            xf = _ffn(xf, g_pre_ffn[l], wg, wu, wd, g_post_ffn[l])
        else:
            e = l // 2
            wg = bf16_w["gate", l].reshape(moe_w_gate[e].shape)
            wu = bf16_w["up", l].reshape(moe_w_up[e].shape)
            wd = bf16_w["down", l].reshape(moe_w_down[e].shape)
            xf = _moe(xf, g_pre_ffn[l], moe_router[e], wg, wu, wd, g_post_ffn[l])
    return xf.reshape(B, S, D_MODEL)
```
